```python
import jax, jax.numpy as jnp
from jax import lax
import numpy as np

D_MODEL = 2048
BATCH = 1
SEQ = 8192
DEPTH = 2

CTX_LEN = 256
GRID_W = 64
D_MIX = D_MODEL
D_CONV = D_MIX // 2
D_RET = D_MIX - D_CONV
RET_HEADS = 8
RET_HEAD_DIM = D_RET // RET_HEADS
CONV_WIDTH = 3
CHUNK = 128
D_IN = 4 * D_CONV + 4 * D_RET
ROPE_BASE = 10000.0
RET_DECAY_OFFSET = 5.0
EPS = 1e-6

kernel_name = "hybrid_conv_retention_prefix_dit_block"


def _rmsnorm(x, w):
    xf = x.astype(jnp.float32)
    y = xf * lax.rsqrt(jnp.mean(xf * xf, axis=-1, keepdims=True) + EPS)
    return (y * w.astype(jnp.float32)).astype(x.dtype)


def _split_in(u):
    idx = [D_CONV, 2 * D_CONV, 3 * D_CONV, 4 * D_CONV,
           4 * D_CONV + D_RET, 4 * D_CONV + 2 * D_RET, 4 * D_CONV + 3 * D_RET]
    return jnp.split(u, idx, axis=-1)


def _short_conv(u, w):
    up = jnp.pad(u, ((0, 0), (1, 1), (0, 0)))
    return up[:, :-2] * w[0] + up[:, 1:-1] * w[1] + up[:, 2:] * w[2]


def _conv_branch(h, b, c, z, conv_w, norm_w):
    y = b * _short_conv(c * h, conv_w)
    return jax.nn.silu(z) * _rmsnorm(y, norm_w)


def _heads(t):
    b, l, _ = t.shape
    return t.reshape(b, l, RET_HEADS, RET_HEAD_DIM).transpose(0, 2, 1, 3)


def _rope_1d(x, pos):
    f = x.shape[-1] // 2
    inv = ROPE_BASE ** (-jnp.arange(f, dtype=jnp.float32) / f)
    ang = pos.astype(jnp.float32)[:, None] * inv[None, :]
    cos, sin = jnp.cos(ang), jnp.sin(ang)
    x1, x2 = x[..., :f], x[..., f:]
    return jnp.concatenate([x1 * cos - x2 * sin, x1 * sin + x2 * cos], axis=-1).astype(x.dtype)


def _axial_rope(x, row_pos, col_pos):
    half = x.shape[-1] // 2
    return jnp.concatenate([_rope_1d(x[..., :half], row_pos),
                            _rope_1d(x[..., half:], col_pos)], axis=-1)


def _chunk_retention(q, k, v, lg, s0):
    b, h, l, dk = q.shape
    n = l // CHUNK
    qc = q.reshape(b, h, n, CHUNK, dk)
    kc = k.reshape(b, h, n, CHUNK, dk)
    vc = v.reshape(b, h, n, CHUNK, v.shape[-1])
    pos = jnp.arange(CHUNK, dtype=jnp.float32)
    diff = pos[:, None] - pos[None, :]
    dmask = jnp.where(diff >= 0, jnp.exp(lg[:, None, None] * jnp.maximum(diff, 0.0)[None]), 0.0)
    scores = jnp.einsum('bhnid,bhnjd->bhnij', qc, kc) * dmask[None, :, None]
    intra = jnp.einsum('bhnij,bhnje->bhnie', scores, vc)
    k_decay = jnp.exp(lg[:, None] * (CHUNK - 1 - pos)[None])
    q_decay = jnp.exp(lg[:, None] * (pos + 1.0)[None])
    chunk_decay = jnp.exp(lg * CHUNK)
    chunk_kv = jnp.einsum('bhnjd,hj,bhnje->nbhde', kc, k_decay, vc)

    def step(s, kv):
        return chunk_decay[None, :, None, None] * s + kv, s

    _, s_prev = lax.scan(step, s0, chunk_kv)
    inter = jnp.einsum('bhnid,hi,nbhde->bhnie', qc, q_decay, s_prev)
    return (intra + inter).reshape(b, h, l, -1)


def _bidir_retention(q, k, v, lg_f, lg_b, s0_f, s0_b):
    o_f = _chunk_retention(q, k, v, lg_f, s0_f)
    flip = lambda t: jnp.flip(t, axis=2)
    o_b = _chunk_retention(flip(q), flip(k), flip(v), lg_b, s0_b)
    return o_f + flip(o_b)


def _context_states(k, v, lg_f, lg_b):
    lc = k.shape[2]
    t = jnp.arange(lc, dtype=jnp.float32)
    w_f = jnp.exp(lg_f[:, None] * (lc - 1.0 - t)[None])
    w_b = jnp.exp(lg_b[:, None] * t[None])
    s_f = jnp.einsum('bhtd,ht,bhte->bhde', k, w_f, v)
    s_b = jnp.einsum('bhtd,ht,bhte->bhde', k, w_b, v)
    return s_f, s_b


def _ret_out(o, z, gn_w):
    of = o.astype(jnp.float32)
    mu = jnp.mean(of, axis=-1, keepdims=True)
    var = jnp.mean(jnp.square(of - mu), axis=-1, keepdims=True)
    on = (of - mu) * lax.rsqrt(var + EPS)
    b, h, l, d = on.shape
    on = on.transpose(0, 2, 1, 3).reshape(b, l, h * d) * gn_w.astype(jnp.float32)
    return jax.nn.silu(z) * on.astype(z.dtype)


def _layer(x, ctx, c, c_ctx, norm_w, w_mod, b_mod, w_in, conv_w, conv_norm_w, ret_norm_w,
           decay_f, decay_b, w_out, row_pos, col_pos, update_ctx):
    d = D_MODEL
    lg_f = -jnp.exp(decay_f.astype(jnp.float32))
    lg_b = -jnp.exp(decay_b.astype(jnp.float32))
    k_scale = RET_HEAD_DIM ** -0.5

    shift, scale, gate = jnp.split(jax.nn.silu(c) @ w_mod + b_mod, 3, axis=-1)
    hx = _rmsnorm(x, norm_w) * (1 + scale[:, None]) + shift[:, None]
    a_h, a_b, a_c, a_z, q, k, v, r_z = _split_in(hx @ w_in)

    n_mod = 3 if update_ctx else 2
    mod_c = jax.nn.silu(c_ctx) @ w_mod[:, :n_mod * d] + b_mod[:n_mod * d]
    hc = _rmsnorm(ctx, norm_w) * (1 + mod_c[d:2 * d]) + mod_c[:d]
    if update_ctx:
        ca_h, ca_b, ca_c, ca_z, cq, ck, cv, cr_z = _split_in(hc @ w_in)
    else:
        kv0 = 4 * D_CONV + D_RET
        ck, cv = jnp.split(hc @ w_in[:, kv0:kv0 + 2 * D_RET], 2, axis=-1)
    ck_h = _heads(ck) * k_scale
    cv_h = _heads(cv)
    s_f, s_b = _context_states(ck_h, cv_h, lg_f, lg_b)

    q_h = _axial_rope(_heads(q), row_pos, col_pos)
    k_h = _axial_rope(_heads(k), row_pos, col_pos) * k_scale
    o = _bidir_retention(q_h, k_h, _heads(v), lg_f, lg_b, s_f, s_b)
    y_ret = _ret_out(o, r_z, ret_norm_w)
    y_conv = _conv_branch(a_h, a_b, a_c, a_z, conv_w, conv_norm_w)
    x = x + gate[:, None] * (jnp.concatenate([y_conv, y_ret], axis=-1) @ w_out)

    if update_ctx:
        zeros = jnp.zeros_like(s_f)
        oc = _bidir_retention(_heads(cq), ck_h, cv_h, lg_f, lg_b, zeros, zeros)
        yc_ret = _ret_out(oc, cr_z, ret_norm_w)
        yc_conv = _conv_branch(ca_h, ca_b, ca_c, ca_z, conv_w, conv_norm_w)
        ctx = ctx + mod_c[2 * d:] * (jnp.concatenate([yc_conv, yc_ret], axis=-1) @ w_out)
    return x, ctx


def setup_inputs(seed: int = 0) -> dict:
    key = jax.random.key(seed)
    ks = jax.random.split(key, 16)
    f32 = jnp.float32
    nrm = lambda k, s: jax.random.normal(k, s, f32)
    base = jnp.log(-jnp.log1p(-(2.0 ** -(RET_DECAY_OFFSET + jnp.arange(RET_HEADS, dtype=f32)))))
    return {
        "x": nrm(ks[0], (BATCH, SEQ, D_MODEL)),
        "c": nrm(ks[1], (BATCH, D_MODEL)),
        "ctx": nrm(ks[2], (BATCH, CTX_LEN, D_MODEL)),
        "c_ctx": nrm(ks[3], (D_MODEL,)),
        "norm_w": 1.0 + 0.05 * nrm(ks[4], (DEPTH, D_MODEL)),
        "w_mod": nrm(ks[5], (DEPTH, D_MODEL, 3 * D_MODEL)) * (0.5 * D_MODEL ** -0.5),
        "b_mod": 0.02 * nrm(ks[6], (DEPTH, 3 * D_MODEL)),
        "w_in": nrm(ks[7], (DEPTH, D_MODEL, D_IN)) * D_MODEL ** -0.5,
        "conv_w": nrm(ks[8], (DEPTH, CONV_WIDTH, D_CONV)) * CONV_WIDTH ** -0.5,
        "conv_norm_w": 1.0 + 0.05 * nrm(ks[9], (DEPTH, D_CONV)),
        "ret_norm_w": 1.0 + 0.05 * nrm(ks[10], (DEPTH, D_RET)),
        "ret_decay_f": base[None] + 0.05 * nrm(ks[11], (DEPTH, RET_HEADS)),
        "ret_decay_b": base[None] + 0.05 * nrm(ks[12], (DEPTH, RET_HEADS)),
        "w_out": nrm(ks[13], (DEPTH, D_MIX, D_MODEL)) * D_MIX ** -0.5,
        "final_norm_w": 1.0 + 0.05 * nrm(ks[14], (D_MODEL,)),
    }


def reference(x, c, ctx, c_ctx, norm_w, w_mod, b_mod, w_in, conv_w, conv_norm_w, ret_norm_w,
              ret_decay_f, ret_decay_b, w_out, final_norm_w):
    seq = x.shape[1]
    rows = seq // GRID_W
    row_pos = jnp.repeat(jnp.arange(rows), GRID_W)
    col_pos = jnp.tile(jnp.arange(GRID_W), rows)
    for layer in range(DEPTH):
        x, ctx = _layer(x, ctx, c, c_ctx, norm_w[layer], w_mod[layer], b_mod[layer], w_in[layer],
                        conv_w[layer], conv_norm_w[layer], ret_norm_w[layer],
                        ret_decay_f[layer], ret_decay_b[layer], w_out[layer],
                        row_pos, col_pos, layer < DEPTH - 1)
    return _rmsnorm(x, final_norm_w)
```

```python
import functools

import jax
import jax.numpy as jnp
from jax import lax
from jax.experimental import pallas as pl
from jax.experimental.pallas import tpu as pltpu

D_MODEL = 2048
D_CONV = 1024
D_RET = 1024
RET_HEADS = 8
HEAD_DIM = 128
CHUNK = 128
SEG = 1024
N_SEG = 8
GRID_W = 64
ROPE_BASE = 10000.0
ROPE_F = 32
EPS = 1e-6
K_SCALE = HEAD_DIM ** -0.5

F32 = jnp.float32
BF16 = jnp.bfloat16

SEG_CONV_Z, SEG_Q, SEG_K, SEG_RET_Z = 3, 4, 5, 7


def _silu(x):
    return x / (1.0 + jnp.exp(-x))


def _params(vmem_mb, n_axes):
    return pltpu.CompilerParams(
        dimension_semantics=("arbitrary",) * n_axes,
        vmem_limit_bytes=vmem_mb * 1024 * 1024,
    )


def _mod_kernel(cv_ref, w_ref, b_ref, o_ref):
    s = _silu(cv_ref[...])
    o_ref[0] = jnp.dot(s.astype(BF16), w_ref[0].astype(BF16),
                       preferred_element_type=F32) + b_ref[0]


def _modulation(cv, w_mod, b_mod, tn=1024):
    depth, d, n = w_mod.shape
    return pl.pallas_call(
        _mod_kernel,
        grid=(depth, n // tn),
        in_specs=[
            pl.BlockSpec((8, d), lambda l, j: (0, 0)),
            pl.BlockSpec((1, d, tn), lambda l, j: (l, 0, j)),
            pl.BlockSpec((1, 1, tn), lambda l, j: (l, 0, j)),
        ],
        out_specs=pl.BlockSpec((1, 8, tn), lambda l, j: (l, 0, j)),
        out_shape=jax.ShapeDtypeStruct((depth, 8, n), F32),
        compiler_params=_params(40, 2),
    )(cv, w_mod, b_mod.reshape(depth, 1, n))


def _rope_heads(acc, cos, sa, sb, scale):
    outs = []
    for h in range(RET_HEADS):
        a = acc[:, h * HEAD_DIM:(h + 1) * HEAD_DIM]
        r = (a * cos + pltpu.roll(a, HEAD_DIM - ROPE_F, 1) * sa
             + pltpu.roll(a, ROPE_F, 1) * sb)
        outs.append(r if scale is None else r * scale)
    return jnp.concatenate(outs, axis=1)


def _inproj_kernel(x_ref, nw_ref, shift_ref, scale_ref, w_ref, cos_ref, sa_ref, sb_ref,
                   o_ref, hx_ref, *, col0, rope):
    j = pl.program_id(1)

    @pl.when(j == 0)
    def _():
        x = x_ref[...]
        ms = jnp.mean(x * x, axis=-1, keepdims=True)
        y = x * lax.rsqrt(ms + EPS) * nw_ref[...]
        hx_ref[...] = (y * (1.0 + scale_ref[...]) + shift_ref[...]).astype(BF16)

    acc = jnp.dot(hx_ref[...], w_ref[...], preferred_element_type=F32)
    seg = j + col0
    is_silu = jnp.logical_or(seg == SEG_CONV_Z, seg == SEG_RET_Z)
    is_q = seg == SEG_Q
    is_k = seg == SEG_K
    plain = jnp.logical_not(is_silu | is_q | is_k)

    @pl.when(plain)
    def _():
        o_ref[...] = acc.astype(BF16)

    @pl.when(is_silu)
    def _():
        o_ref[...] = _silu(acc).astype(BF16)

    @pl.when(is_q)
    def _():
        if rope:
            o_ref[...] = _rope_heads(acc, cos_ref[...], sa_ref[...], sb_ref[...], None).astype(BF16)
        else:
            o_ref[...] = acc.astype(BF16)

    @pl.when(is_k)
    def _():
        if rope:
            o_ref[...] = _rope_heads(acc, cos_ref[...], sa_ref[...], sb_ref[...], K_SCALE).astype(BF16)
        else:
            o_ref[...] = (acc * K_SCALE).astype(BF16)


def _inproj(x, norm_w, shift, scale, w_bf16, tables, *, col0, ncols, rope, tm):
    rows, d = x.shape
    cos, sa, sb = tables
    kern = functools.partial(_inproj_kernel, col0=col0, rope=rope)
    vec = lambda: pl.BlockSpec((1, d), lambda i, j: (0, 0))
    tab = lambda: pl.BlockSpec((tm, HEAD_DIM), lambda i, j: (i, 0))
    return pl.pallas_call(
        kern,
        grid=(rows // tm, ncols),
        in_specs=[
            pl.BlockSpec((tm, d), lambda i, j: (i, 0)),
            vec(), vec(), vec(),
            pl.BlockSpec((d, SEG), lambda i, j: (0, j + col0)),
            tab(), tab(), tab(),
        ],
        out_specs=pl.BlockSpec((tm, SEG), lambda i, j: (i, j)),
        out_shape=jax.ShapeDtypeStruct((rows, ncols * SEG), BF16),
        scratch_shapes=[pltpu.VMEM((tm, d), BF16)],
        compiler_params=_params(48, 2),
    )(x, norm_w, shift, scale, w_bf16, cos, sa, sb)


def _states_kernel(lgf_ref, lgb_ref, kf_ref, vf_ref, kb_ref, vb_ref, s0f_ref, s0b_ref,
                   sf_ref, sb_ref, ff_ref, fb_ref, *, nc):
    t = pl.program_id(0)

    @pl.when(t == 0)
    def _():
        ff_ref[...] = s0f_ref[...]
        fb_ref[...] = s0b_ref[...]

    row = lax.broadcasted_iota(jnp.int32, (CHUNK, HEAD_DIM), 0).astype(F32)
    tn = (((0,), (0,)), ((), ()))
    for h in range(RET_HEADS):
        hs = slice(h * HEAD_DIM, (h + 1) * HEAD_DIM)
        lgf = lgf_ref[h]
        lgb = lgb_ref[h]
        kdf = jnp.exp(lgf * (CHUNK - 1.0 - row))
        kdb = jnp.exp(lgb * row)
        cdf = jnp.exp(lgf * CHUNK + 0.0 * row)
        cdb = jnp.exp(lgb * CHUNK + 0.0 * row)

        s = ff_ref[h]
        for c in range(nc):
            rs = slice(c * CHUNK, (c + 1) * CHUNK)
            sf_ref[c, h] = s.astype(BF16)
            kd = (kf_ref[rs, hs].astype(F32) * kdf).astype(BF16)
            s = cdf * s + lax.dot_general(kd, vf_ref[rs, hs], tn, preferred_element_type=F32)
        ff_ref[h] = s

        s = fb_ref[h]
        for c in reversed(range(nc)):
            rs = slice(c * CHUNK, (c + 1) * CHUNK)
            sb_ref[c, h] = s.astype(BF16)
            kd = (kb_ref[rs, hs].astype(F32) * kdb).astype(BF16)
            s = cdb * s + lax.dot_general(kd, vb_ref[rs, hs], tn, preferred_element_type=F32)
        fb_ref[h] = s


def _states(lg_f, lg_b, u, s0f, s0b, *, kcol, vcol, tr):
    rows = u.shape[0]
    nt = rows // tr
    nc = tr // CHUNK
    n_chunks = rows // CHUNK
    kern = functools.partial(_states_kernel, nc=nc)
    smem = lambda: pl.BlockSpec(memory_space=pltpu.SMEM)
    st = lambda: pl.BlockSpec((RET_HEADS, HEAD_DIM, HEAD_DIM), lambda t: (0, 0, 0))
    seq_shape = jax.ShapeDtypeStruct((n_chunks, RET_HEADS, HEAD_DIM, HEAD_DIM), BF16)
    fin_shape = jax.ShapeDtypeStruct((RET_HEADS, HEAD_DIM, HEAD_DIM), F32)
    return pl.pallas_call(
        kern,
        grid=(nt,),
        in_specs=[
            smem(), smem(),
            pl.BlockSpec((tr, SEG), lambda t: (t, kcol)),
            pl.BlockSpec((tr, SEG), lambda t: (t, vcol)),
            pl.BlockSpec((tr, SEG), lambda t: (nt - 1 - t, kcol)),
            pl.BlockSpec((tr, SEG), lambda t: (nt - 1 - t, vcol)),
            st(), st(),
        ],
        out_specs=[
            pl.BlockSpec((nc, RET_HEADS, HEAD_DIM, HEAD_DIM), lambda t: (t, 0, 0, 0)),
            pl.BlockSpec((nc, RET_HEADS, HEAD_DIM, HEAD_DIM), lambda t: (nt - 1 - t, 0, 0, 0)),
            st(), st(),
        ],
        out_shape=[seq_shape, seq_shape, fin_shape, fin_shape],
        compiler_params=_params(32, 1),
    )(lg_f, lg_b, u, u, u, u, s0f, s0b)


def _out_kernel(lgf_ref, lgb_ref,
                h_ref, b_ref, c_ref, zc_ref, q_ref, k_ref, v_ref, zr_ref,
                hp_ref, cp_ref, hn_ref, cn_ref,
                sf_ref, sb_ref, x_ref, gate_ref, convw_ref, cnw_ref, gnw_ref, wout_ref, fnw_ref,
                o_ref,
                mask_ref, qdf_ref, qdb_ref, y_ref, *, tm, final):
    i = pl.program_id(0)
    last = pl.num_programs(0) - 1
    nc = tm // CHUNK

    @pl.when(i == 0)
    def _():
        r = lax.broadcasted_iota(jnp.int32, (CHUNK, CHUNK), 0).astype(F32)
        cc = lax.broadcasted_iota(jnp.int32, (CHUNK, CHUNK), 1).astype(F32)
        d = r - cc
        for h in range(RET_HEADS):
            lgf = lgf_ref[h]
            lgb = lgb_ref[h]
            mf = jnp.where(d >= 0, jnp.exp(lgf * jnp.maximum(d, 0.0)), 0.0)
            mb = jnp.where(d <= 0, jnp.exp(lgb * jnp.maximum(-d, 0.0)), 0.0)
            mask_ref[h] = mf + mb
            qdf_ref[h] = jnp.exp(lgf * (r + 1.0))
            qdb_ref[h] = jnp.exp(lgb * (CHUNK - r))

    ch = c_ref[...].astype(F32) * h_ref[...].astype(F32)
    prev_row = (cp_ref[...].astype(F32) * hp_ref[...].astype(F32))[15:16, :]
    next_row = (cn_ref[...].astype(F32) * hn_ref[...].astype(F32))[0:1, :]
    prev_row = jnp.where(i > 0, prev_row, 0.0)
    next_row = jnp.where(i < last, next_row, 0.0)
    ridx = lax.broadcasted_iota(jnp.int32, (tm, 1), 0)
    prev = jnp.where(ridx == 0, prev_row, pltpu.roll(ch, 1, 0))
    nxt = jnp.where(ridx == tm - 1, next_row, pltpu.roll(ch, tm - 1, 0))
    cw = convw_ref[...]
    conv = prev * cw[0:1, :] + ch * cw[1:2, :] + nxt * cw[2:3, :]
    y = b_ref[...].astype(F32) * conv
    ms = jnp.mean(y * y, axis=-1, keepdims=True)
    yn = y * lax.rsqrt(ms + EPS) * cnw_ref[...]
    y_ref[:, 0:D_CONV] = (zc_ref[...].astype(F32) * yn).astype(BF16)

    nt = (((1,), (1,)), ((), ()))
    for c in range(nc):
        rs = slice(c * CHUNK, (c + 1) * CHUNK)
        for h in range(RET_HEADS):
            hs = slice(h * HEAD_DIM, (h + 1) * HEAD_DIM)
            q = q_ref[rs, hs]
            s = lax.dot_general(q, k_ref[rs, hs], nt, preferred_element_type=F32)
            p = (s * mask_ref[h]).astype(BF16)
            qf32 = q.astype(F32)
            qf = (qf32 * qdf_ref[h]).astype(BF16)
            qb = (qf32 * qdb_ref[h]).astype(BF16)
            lhs = jnp.concatenate([p, qf, qb], axis=1)
            rhs = jnp.concatenate([v_ref[rs, hs], sf_ref[c, h], sb_ref[c, h]], axis=0)
            o = jnp.dot(lhs, rhs, preferred_element_type=F32)
            mu = jnp.mean(o, axis=-1, keepdims=True)
            oc = o - mu
            var = jnp.mean(oc * oc, axis=-1, keepdims=True)
            on = oc * lax.rsqrt(var + EPS) * gnw_ref[:, hs]
            y_ref[rs, D_CONV + h * HEAD_DIM:D_CONV + (h + 1) * HEAD_DIM] = (
                zr_ref[rs, hs].astype(F32) * on).astype(BF16)

    proj = jnp.dot(y_ref[...], wout_ref[...], preferred_element_type=F32)
    xn = x_ref[...] + gate_ref[...] * proj
    if final:
        ms = jnp.mean(xn * xn, axis=-1, keepdims=True)
        xn = xn * lax.rsqrt(ms + EPS) * fnw_ref[...]
    o_ref[...] = xn


def _out(lg_f, lg_b, u, sf, sb, x, gate, conv_w, cnw, gnw, wout_bf16, fnw, *, tm, final):
    rows, d = x.shape
    nt = rows // tm
    nc = tm // CHUNK
    hb = tm // 16
    n_hb = rows // 16
    kern = functools.partial(_out_kernel, tm=tm, final=final)
    smem = lambda: pl.BlockSpec(memory_space=pltpu.SMEM)
    seg = lambda s: pl.BlockSpec((tm, SEG), lambda i: (i, s))
    prev = lambda s: pl.BlockSpec((16, SEG), lambda i: (jnp.maximum(i * hb - 1, 0), s))
    nxt = lambda s: pl.BlockSpec((16, SEG), lambda i: (jnp.minimum((i + 1) * hb, n_hb - 1), s))
    st = lambda: pl.BlockSpec((nc, RET_HEADS, HEAD_DIM, HEAD_DIM), lambda i: (i, 0, 0, 0))
    full = lambda shape: pl.BlockSpec(shape, lambda i: (0,) * len(shape))
    return pl.pallas_call(
        kern,
        grid=(nt,),
        in_specs=[
            smem(), smem(),
            seg(0), seg(1), seg(2), seg(3), seg(4), seg(5), seg(6), seg(7),
            prev(0), prev(2), nxt(0), nxt(2),
            st(), st(),
            pl.BlockSpec((tm, d), lambda i: (i, 0)),
            full((1, d)), full((3, D_CONV)), full((1, D_CONV)), full((1, D_RET)),
            full((d, d)), full((1, d)),
        ],
        out_specs=pl.BlockSpec((tm, d), lambda i: (i, 0)),
        out_shape=jax.ShapeDtypeStruct((rows, d), F32),
        scratch_shapes=[
            pltpu.VMEM((RET_HEADS, CHUNK, CHUNK), F32),
            pltpu.VMEM((RET_HEADS, CHUNK, HEAD_DIM), F32),
            pltpu.VMEM((RET_HEADS, CHUNK, HEAD_DIM), F32),
            pltpu.VMEM((tm, d), BF16),
        ],
        compiler_params=_params(48, 1),
    )(lg_f, lg_b, u, u, u, u, u, u, u, u, u, u, u, u, sf, sb, x, gate, conv_w, cnw, gnw,
      wout_bf16, fnw)


def _rope_tables(seq):
    t = jnp.arange(seq)
    row = (t // GRID_W).astype(F32)
    col = (t % GRID_W).astype(F32)
    inv = ROPE_BASE ** (-jnp.arange(ROPE_F, dtype=F32) / ROPE_F)
    ar = row[:, None] * inv[None, :]
    ac = col[:, None] * inv[None, :]
    z = jnp.zeros_like(ar)
    cos = jnp.concatenate([jnp.cos(ar), jnp.cos(ar), jnp.cos(ac), jnp.cos(ac)], axis=-1)
    sa = jnp.concatenate([-jnp.sin(ar), z, -jnp.sin(ac), z], axis=-1)
    sb = jnp.concatenate([z, jnp.sin(ar), z, jnp.sin(ac)], axis=-1)
    return cos, sa, sb


def kernel(x, c, ctx, c_ctx, norm_w, w_mod, b_mod, w_in, conv_w, conv_norm_w, ret_norm_w,
           ret_decay_f, ret_decay_b, w_out, final_norm_w):
    batch, seq, d = x.shape
    assert batch == 1 and d == D_MODEL and seq % 1024 == 0
    depth = norm_w.shape[0]
    ctx_len = ctx.shape[1]
    xs = x[0]
    cs = ctx[0]

    cv = jnp.zeros((8, d), F32).at[0].set(c[0]).at[1].set(c_ctx)
    mod = _modulation(cv, w_mod, b_mod)
    tables = _rope_tables(seq)
    ctx_tables = tuple(jnp.zeros((ctx_len, HEAD_DIM), F32) for _ in range(3))
    zero_state = jnp.zeros((RET_HEADS, HEAD_DIM, HEAD_DIM), F32)
    fnw = final_norm_w.reshape(1, d)

    for layer in range(depth):
        update_ctx = layer < depth - 1
        lg_f = -jnp.exp(ret_decay_f[layer].astype(F32))
        lg_b = -jnp.exp(ret_decay_b[layer].astype(F32))
        w_in_b = w_in[layer].astype(BF16)
        w_out_b = w_out[layer].astype(BF16)
        nw = norm_w[layer].reshape(1, d)
        m = mod[layer]
        shift, scale, gate = m[0:1, 0:d], m[0:1, d:2 * d], m[0:1, 2 * d:3 * d]
        shift_c, scale_c, gate_c = m[1:2, 0:d], m[1:2, d:2 * d], m[1:2, 2 * d:3 * d]
        cw = conv_w[layer]
        cnw = conv_norm_w[layer].reshape(1, D_CONV)
        gnw = ret_norm_w[layer].reshape(1, D_RET)

        if update_ctx:
            u_c = _inproj(cs, nw, shift_c, scale_c, w_in_b, ctx_tables,
                          col0=0, ncols=N_SEG, rope=False, tm=ctx_len)
            kcol, vcol = SEG_K, SEG_K + 1
        else:
            u_c = _inproj(cs, nw, shift_c, scale_c, w_in_b, ctx_tables,
                          col0=SEG_K, ncols=2, rope=False, tm=ctx_len)
            kcol, vcol = 0, 1
        sf_c, sb_c, s_f, s_b = _states(lg_f, lg_b, u_c, zero_state, zero_state,
                                       kcol=kcol, vcol=vcol, tr=ctx_len)

        u = _inproj(xs, nw, shift, scale, w_in_b, tables, col0=0, ncols=N_SEG, rope=True, tm=512)
        sf, sb, _, _ = _states(lg_f, lg_b, u, s_f, s_b, kcol=SEG_K, vcol=SEG_K + 1, tr=512)
        xs = _out(lg_f, lg_b, u, sf, sb, xs, gate, cw, cnw, gnw, w_out_b, fnw,
                  tm=256, final=not update_ctx)
        if update_ctx:
            cs = _out(lg_f, lg_b, u_c, sf_c, sb_c, cs, gate_c, cw, cnw, gnw, w_out_b, fnw,
                      tm=ctx_len, final=False)
    return xs[None]
```

```python
import functools

import numpy as np
import jax
import jax.numpy as jnp
from jax import lax
from jax.experimental import pallas as pl
from jax.experimental.pallas import tpu as pltpu

D_MODEL = 2048
D_CONV = 1024
D_RET = 1024
RET_HEADS = 8
HEAD_DIM = 128
CHUNK = 128
SEG = 1024
N_SEG = 8
SUB = 256
N_SUB = SEG // SUB
GRID_W = 64
ROPE_BASE = 10000.0
ROPE_F = 32
EPS = 1e-6
K_SCALE = HEAD_DIM ** -0.5

F32 = jnp.float32
BF16 = jnp.bfloat16

G_H, G_B, G_C, G_ZC, G_Q, G_K, G_V, G_ZR = range(8)
TILE_ORDER = (G_H, G_B, G_C, G_ZC, G_Q, G_ZR, G_K, G_V)
T_H, T_B, T_C, T_ZC, T_Q, T_ZR, T_K, T_V = range(8)


def _silu(x):
    return x / (1.0 + jnp.exp(-x))


def _params(vmem_mb, n_axes):
    return pltpu.CompilerParams(
        dimension_semantics=("arbitrary",) * n_axes,
        vmem_limit_bytes=vmem_mb * 1024 * 1024,
    )


def _mod_kernel(cv_ref, w_ref, b_ref, o_ref):
    s = _silu(cv_ref[...])
    o_ref[0] = jnp.dot(s.astype(BF16), w_ref[0].astype(BF16),
                       preferred_element_type=F32) + b_ref[0]


def _modulation(cv, w_mod, b_mod, tn=1024):
    depth, d, n = w_mod.shape
    return pl.pallas_call(
        _mod_kernel,
        grid=(depth, n // tn),
        in_specs=[
            pl.BlockSpec((8, d), lambda l, j: (0, 0)),
            pl.BlockSpec((1, d, tn), lambda l, j: (l, 0, j)),
            pl.BlockSpec((1, 1, tn), lambda l, j: (l, 0, j)),
        ],
        out_specs=pl.BlockSpec((1, 8, tn), lambda l, j: (l, 0, j)),
        out_shape=jax.ShapeDtypeStruct((depth, 8, n), F32),
        compiler_params=_params(40, 2),
    )(cv, w_mod, b_mod.reshape(depth, 1, n))


def _prologue(x_ref, nw_ref, shift_ref, scale_ref, hx_ref):
    x = x_ref[...]
    ms = jnp.mean(x * x, axis=-1, keepdims=True)
    y = x * lax.rsqrt(ms + EPS) * nw_ref[...]
    hx_ref[...] = (y * (1.0 + scale_ref[...]) + shift_ref[...]).astype(BF16)


def _rope_pair(acc, cos, sa, sb):
    outs = []
    for h in range(SUB // HEAD_DIM):
        a = acc[:, h * HEAD_DIM:(h + 1) * HEAD_DIM]
        outs.append(a * cos + pltpu.roll(a, HEAD_DIM - ROPE_F, 1) * sa
                    + pltpu.roll(a, ROPE_F, 1) * sb)
    return jnp.concatenate(outs, axis=1)


def _inproj_kernel(x_ref, nw_ref, shift_ref, scale_ref, w_ref, cos_ref, sa_ref, sb_ref,
                   ch_ref, b_ref, zc_ref, q_ref, k_ref, v_ref, zr_ref, hx_ref, *, rope):
    @pl.when(pl.program_id(1) == 0)
    def _():
        _prologue(x_ref, nw_ref, shift_ref, scale_ref, hx_ref)

    hx = hx_ref[...]

    def seg(t):
        return jnp.dot(hx, w_ref[0, :, t * SUB:(t + 1) * SUB], preferred_element_type=F32)

    ch_ref[...] = (seg(T_C) * seg(T_H)).astype(BF16)
    b_ref[...] = seg(T_B).astype(BF16)
    zc_ref[...] = _silu(seg(T_ZC)).astype(BF16)
    if rope:
        cos, sa, sb = cos_ref[...], sa_ref[...], sb_ref[...]
        q_ref[...] = _rope_pair(seg(T_Q), cos, sa, sb).astype(BF16)
        k_ref[...] = _rope_pair(seg(T_K), cos * K_SCALE, sa * K_SCALE, sb * K_SCALE).astype(BF16)
    else:
        q_ref[...] = seg(T_Q).astype(BF16)
        k_ref[...] = (seg(T_K) * K_SCALE).astype(BF16)
    v_ref[...] = seg(T_V).astype(BF16)
    zr_ref[...] = _silu(seg(T_ZR)).astype(BF16)


def _inproj(x, norm_w, shift, scale, w_perm, layer, tables, *, rope, tm):
    rows, d = x.shape
    cos, sa, sb = tables
    kern = functools.partial(_inproj_kernel, rope=rope)
    vec = lambda: pl.BlockSpec((1, d), lambda i, j: (0, 0))
    tab = lambda: pl.BlockSpec((tm, HEAD_DIM), lambda i, j: (i, 0))
    out = lambda: pl.BlockSpec((tm, SUB), lambda i, j: (i, j))
    seg_shape = jax.ShapeDtypeStruct((rows, SEG), BF16)
    return pl.pallas_call(
        kern,
        grid=(rows // tm, N_SUB),
        in_specs=[
            pl.BlockSpec((tm, d), lambda i, j: (i, 0)),
            vec(), vec(), vec(),
            pl.BlockSpec((1, d, N_SEG * SUB), lambda i, j: (layer, 0, j)),
            tab(), tab(), tab(),
        ],
        out_specs=[out() for _ in range(7)],
        out_shape=[seg_shape] * 7,
        scratch_shapes=[pltpu.VMEM((tm, d), BF16)],
        compiler_params=_params(48, 2),
    )(x, norm_w, shift, scale, w_perm, cos, sa, sb)


def _inproj_kv_kernel(x_ref, nw_ref, shift_ref, scale_ref, w_ref, k_ref, v_ref, hx_ref):
    @pl.when(pl.program_id(1) == 0)
    def _():
        _prologue(x_ref, nw_ref, shift_ref, scale_ref, hx_ref)

    hx = hx_ref[...]
    k_ref[...] = (jnp.dot(hx, w_ref[0, :, 0:SUB], preferred_element_type=F32) * K_SCALE).astype(BF16)
    v_ref[...] = jnp.dot(hx, w_ref[0, :, SUB:2 * SUB], preferred_element_type=F32).astype(BF16)


def _inproj_kv(x, norm_w, shift, scale, w_perm, layer, *, tm):
    rows, d = x.shape
    vec = lambda: pl.BlockSpec((1, d), lambda i, j: (0, 0))
    out = lambda: pl.BlockSpec((tm, SUB), lambda i, j: (i, j))
    seg_shape = jax.ShapeDtypeStruct((rows, SEG), BF16)
    kv_block = T_K // 2
    return pl.pallas_call(
        _inproj_kv_kernel,
        grid=(rows // tm, N_SUB),
        in_specs=[
            pl.BlockSpec((tm, d), lambda i, j: (i, 0)),
            vec(), vec(), vec(),
            pl.BlockSpec((1, d, 2 * SUB), lambda i, j: (layer, 0, j * (N_SEG // 2) + kv_block)),
        ],
        out_specs=[out(), out()],
        out_shape=[seg_shape] * 2,
        scratch_shapes=[pltpu.VMEM((tm, d), BF16)],
        compiler_params=_params(32, 2),
    )(x, norm_w, shift, scale, w_perm)


def _states_kernel(lgf_ref, lgb_ref, kf_ref, vf_ref, kb_ref, vb_ref, s0f_ref, s0b_ref,
                   sf_ref, sb_ref, ff_ref, fb_ref, *, nc):
    t = pl.program_id(0)

    @pl.when(t == 0)
    def _():
        ff_ref[...] = s0f_ref[...]
        fb_ref[...] = s0b_ref[...]

    row = lax.broadcasted_iota(jnp.int32, (CHUNK, HEAD_DIM), 0).astype(F32)
    tn = (((0,), (0,)), ((), ()))
    for h in range(RET_HEADS):
        hs = slice(h * HEAD_DIM, (h + 1) * HEAD_DIM)
        lgf = lgf_ref[h]
        lgb = lgb_ref[h]
        kdf = jnp.exp(lgf * (CHUNK - 1.0 - row))
        kdb = jnp.exp(lgb * row)
        cdf = jnp.exp(lgf * CHUNK + 0.0 * row)
        cdb = jnp.exp(lgb * CHUNK + 0.0 * row)

        s = ff_ref[h]
        for c in range(nc):
            rs = slice(c * CHUNK, (c + 1) * CHUNK)
            sf_ref[c, h] = s.astype(BF16)
            kd = (kf_ref[rs, hs].astype(F32) * kdf).astype(BF16)
            s = cdf * s + lax.dot_general(kd, vf_ref[rs, hs], tn, preferred_element_type=F32)
        ff_ref[h] = s

        s = fb_ref[h]
        for c in reversed(range(nc)):
            rs = slice(c * CHUNK, (c + 1) * CHUNK)
            sb_ref[c, h] = s.astype(BF16)
            kd = (kb_ref[rs, hs].astype(F32) * kdb).astype(BF16)
            s = cdb * s + lax.dot_general(kd, vb_ref[rs, hs], tn, preferred_element_type=F32)
        fb_ref[h] = s


def _states(lg_f, lg_b, k, v, s0f, s0b, *, tr):
    rows = k.shape[0]
    nt = rows // tr
    nc = tr // CHUNK
    n_chunks = rows // CHUNK
    kern = functools.partial(_states_kernel, nc=nc)
    smem = lambda: pl.BlockSpec(memory_space=pltpu.SMEM)
    st = lambda: pl.BlockSpec((RET_HEADS, HEAD_DIM, HEAD_DIM), lambda t: (0, 0, 0))
    fwd = lambda: pl.BlockSpec((tr, SEG), lambda t: (t, 0))
    bwd = lambda: pl.BlockSpec((tr, SEG), lambda t: (nt - 1 - t, 0))
    seq_shape = jax.ShapeDtypeStruct((n_chunks, RET_HEADS, HEAD_DIM, HEAD_DIM), BF16)
    fin_shape = jax.ShapeDtypeStruct((RET_HEADS, HEAD_DIM, HEAD_DIM), F32)
    return pl.pallas_call(
        kern,
        grid=(nt,),
        in_specs=[smem(), smem(), fwd(), fwd(), bwd(), bwd(), st(), st()],
        out_specs=[
            pl.BlockSpec((nc, RET_HEADS, HEAD_DIM, HEAD_DIM), lambda t: (t, 0, 0, 0)),
            pl.BlockSpec((nc, RET_HEADS, HEAD_DIM, HEAD_DIM), lambda t: (nt - 1 - t, 0, 0, 0)),
            st(), st(),
        ],
        out_shape=[seq_shape, seq_shape, fin_shape, fin_shape],
        compiler_params=_params(32, 1),
    )(lg_f, lg_b, k, v, k, v, s0f, s0b)


def _out_kernel(lgf_ref, lgb_ref,
                ch_ref, b_ref, zc_ref, q_ref, k_ref, v_ref, zr_ref, chp_ref, chn_ref,
                sf_ref, sb_ref, x_ref, gate_ref, convw_ref, cnw_ref, gnw_ref, wout_ref, fnw_ref,
                o_ref,
                mask_ref, qdf_ref, qdb_ref, y_ref, *, tm, final):
    i = pl.program_id(0)
    last = pl.num_programs(0) - 1
    nc = tm // CHUNK

    @pl.when(i == 0)
    def _():
        r = lax.broadcasted_iota(jnp.int32, (CHUNK, CHUNK), 0).astype(F32)
        cc = lax.broadcasted_iota(jnp.int32, (CHUNK, CHUNK), 1).astype(F32)
        d = r - cc
        for h in range(RET_HEADS):
            lgf = lgf_ref[h]
            lgb = lgb_ref[h]
            mf = jnp.where(d >= 0, jnp.exp(lgf * jnp.maximum(d, 0.0)), 0.0)
            mb = jnp.where(d <= 0, jnp.exp(lgb * jnp.maximum(-d, 0.0)), 0.0)
            mask_ref[h] = mf + mb
            qdf_ref[h] = jnp.exp(lgf * (r + 1.0))
            qdb_ref[h] = jnp.exp(lgb * (CHUNK - r))

    ch = ch_ref[...].astype(F32)
    prev_row = jnp.where(i > 0, chp_ref[...].astype(F32)[15:16, :], 0.0)
    next_row = jnp.where(i < last, chn_ref[...].astype(F32)[0:1, :], 0.0)
    ridx = lax.broadcasted_iota(jnp.int32, (tm, 1), 0)
    prev = jnp.where(ridx == 0, prev_row, pltpu.roll(ch, 1, 0))
    nxt = jnp.where(ridx == tm - 1, next_row, pltpu.roll(ch, tm - 1, 0))
    cw = convw_ref[0]
    conv = prev * cw[0:1, :] + ch * cw[1:2, :] + nxt * cw[2:3, :]
    y = b_ref[...].astype(F32) * conv
    ms = jnp.mean(y * y, axis=-1, keepdims=True)
    yn = y * lax.rsqrt(ms + EPS) * cnw_ref[...]
    y_ref[:, 0:D_CONV] = (zc_ref[...].astype(F32) * yn).astype(BF16)

    nt = (((1,), (1,)), ((), ()))
    for c in range(nc):
        rs = slice(c * CHUNK, (c + 1) * CHUNK)
        for h in range(RET_HEADS):
            hs = slice(h * HEAD_DIM, (h + 1) * HEAD_DIM)
            q = q_ref[rs, hs]
            s = lax.dot_general(q, k_ref[rs, hs], nt, preferred_element_type=F32)
            p = (s * mask_ref[h]).astype(BF16)
            qf32 = q.astype(F32)
            qf = (qf32 * qdf_ref[h]).astype(BF16)
            qb = (qf32 * qdb_ref[h]).astype(BF16)
            lhs = jnp.concatenate([p, qf, qb], axis=1)
            rhs = jnp.concatenate([v_ref[rs, hs], sf_ref[c, h], sb_ref[c, h]], axis=0)
            o = jnp.dot(lhs, rhs, preferred_element_type=F32)
            mu = jnp.mean(o, axis=-1, keepdims=True)
            oc = o - mu
            var = jnp.mean(oc * oc, axis=-1, keepdims=True)
            on = oc * lax.rsqrt(var + EPS) * gnw_ref[:, hs]
            y_ref[rs, D_CONV + h * HEAD_DIM:D_CONV + (h + 1) * HEAD_DIM] = (
                zr_ref[rs, hs].astype(F32) * on).astype(BF16)

    proj = jnp.dot(y_ref[...], wout_ref[0], preferred_element_type=F32)
    xn = x_ref[...] + gate_ref[...] * proj
    if final:
        ms = jnp.mean(xn * xn, axis=-1, keepdims=True)
        xn = xn * lax.rsqrt(ms + EPS) * fnw_ref[...]
    o_ref[...] = xn


def _out(lg_f, lg_b, segs, sf, sb, x, gate, conv_w, cnw, gnw, wout_bf16, fnw, layer, *, tm, final):
    ch, b, zc, q, k, v, zr = segs
    rows, d = x.shape
    nt = rows // tm
    nc = tm // CHUNK
    hb = tm // 16
    n_hb = rows // 16
    kern = functools.partial(_out_kernel, tm=tm, final=final)
    smem = lambda: pl.BlockSpec(memory_space=pltpu.SMEM)
    seg = lambda: pl.BlockSpec((tm, SEG), lambda i: (i, 0))
    st = lambda: pl.BlockSpec((nc, RET_HEADS, HEAD_DIM, HEAD_DIM), lambda i: (i, 0, 0, 0))
    full = lambda shape: pl.BlockSpec(shape, lambda i: (0,) * len(shape))
    return pl.pallas_call(
        kern,
        grid=(nt,),
        in_specs=[
            smem(), smem(),
            seg(), seg(), seg(), seg(), seg(), seg(), seg(),
            pl.BlockSpec((16, SEG), lambda i: (jnp.maximum(i * hb - 1, 0), 0)),
            pl.BlockSpec((16, SEG), lambda i: (jnp.minimum((i + 1) * hb, n_hb - 1), 0)),
            st(), st(),
            pl.BlockSpec((tm, d), lambda i: (i, 0)),
            full((1, d)),
            pl.BlockSpec((1, 3, D_CONV), lambda i: (layer, 0, 0)),
            full((1, D_CONV)), full((1, D_RET)),
            pl.BlockSpec((1, d, d), lambda i: (layer, 0, 0)),
            full((1, d)),
        ],
        out_specs=pl.BlockSpec((tm, d), lambda i: (i, 0)),
        out_shape=jax.ShapeDtypeStruct((rows, d), F32),
        scratch_shapes=[
            pltpu.VMEM((RET_HEADS, CHUNK, CHUNK), F32),
            pltpu.VMEM((RET_HEADS, CHUNK, HEAD_DIM), F32),
            pltpu.VMEM((RET_HEADS, CHUNK, HEAD_DIM), F32),
            pltpu.VMEM((tm, d), BF16),
        ],
        compiler_params=_params(48, 1),
    )(lg_f, lg_b, ch, b, zc, q, k, v, zr, ch, ch, sf, sb, x, gate, conv_w, cnw, gnw,
      wout_bf16, fnw)


def _rope_tables(seq):
    t = np.arange(seq)
    row = (t // GRID_W).astype(np.float64)
    col = (t % GRID_W).astype(np.float64)
    inv = ROPE_BASE ** (-np.arange(ROPE_F, dtype=np.float64) / ROPE_F)
    ar = row[:, None] * inv[None, :]
    ac = col[:, None] * inv[None, :]
    z = np.zeros_like(ar)
    cos = np.concatenate([np.cos(ar), np.cos(ar), np.cos(ac), np.cos(ac)], axis=-1)
    sa = np.concatenate([-np.sin(ar), z, -np.sin(ac), z], axis=-1)
    sb = np.concatenate([z, np.sin(ar), z, np.sin(ac)], axis=-1)
    return tuple(jnp.asarray(a, dtype=F32) for a in (cos, sa, sb))


def _permute_w_in(w_in):
    depth, d, _ = w_in.shape
    w = w_in.reshape(depth, d, N_SEG, N_SUB, SUB)
    w = jnp.stack([w[:, :, g] for g in TILE_ORDER], axis=3)
    return w.reshape(depth, d, N_SEG * SEG).astype(BF16)


def kernel(x, c, ctx, c_ctx, norm_w, w_mod, b_mod, w_in, conv_w, conv_norm_w, ret_norm_w,
           ret_decay_f, ret_decay_b, w_out, final_norm_w):
    batch, seq, d = x.shape
    assert batch == 1 and d == D_MODEL and seq % 1024 == 0
    depth = norm_w.shape[0]
    ctx_len = ctx.shape[1]
    xs = x[0]
    cs = ctx[0]

    cv = jnp.zeros((8, d), F32).at[0].set(c[0]).at[1].set(c_ctx)
    mod = _modulation(cv, w_mod, b_mod)
    tables = _rope_tables(seq)
    ctx_tables = tuple(jnp.zeros((ctx_len, HEAD_DIM), F32) for _ in range(3))
    zero_state = jnp.zeros((RET_HEADS, HEAD_DIM, HEAD_DIM), F32)
    fnw = final_norm_w.reshape(1, d)
    w_in_p = _permute_w_in(w_in)
    w_out_b = w_out.astype(BF16)

    for layer in range(depth):
        update_ctx = layer < depth - 1
        lg_f = -jnp.exp(ret_decay_f[layer].astype(F32))
        lg_b = -jnp.exp(ret_decay_b[layer].astype(F32))
        nw = norm_w[layer].reshape(1, d)
        m = mod[layer]
        shift, scale, gate = m[0:1, 0:d], m[0:1, d:2 * d], m[0:1, 2 * d:3 * d]
        shift_c, scale_c, gate_c = m[1:2, 0:d], m[1:2, d:2 * d], m[1:2, 2 * d:3 * d]
        cnw = conv_norm_w[layer].reshape(1, D_CONV)
        gnw = ret_norm_w[layer].reshape(1, D_RET)

        if update_ctx:
            segs_c = _inproj(cs, nw, shift_c, scale_c, w_in_p, layer, ctx_tables,
                             rope=False, tm=ctx_len)
            k_c, v_c = segs_c[4], segs_c[5]
        else:
            k_c, v_c = _inproj_kv(cs, nw, shift_c, scale_c, w_in_p, layer, tm=ctx_len)
        sf_c, sb_c, s_f, s_b = _states(lg_f, lg_b, k_c, v_c, zero_state, zero_state, tr=ctx_len)

        segs = _inproj(xs, nw, shift, scale, w_in_p, layer, tables, rope=True, tm=512)
        sf, sb, _, _ = _states(lg_f, lg_b, segs[4], segs[5], s_f, s_b, tr=512)
        xs = _out(lg_f, lg_b, segs, sf, sb, xs, gate, conv_w, cnw, gnw, w_out_b, fnw, layer,
                  tm=256, final=not update_ctx)
        if update_ctx:
            cs = _out(lg_f, lg_b, segs_c, sf_c, sb_c, cs, gate_c, conv_w, cnw, gnw, w_out_b, fnw,
                      layer, tm=ctx_len, final=False)
    return xs[None]
```

```python
import functools

import numpy as np
import jax
import jax.numpy as jnp
from jax import lax
from jax.experimental import pallas as pl
from jax.experimental.pallas import tpu as pltpu

D_MODEL = 2048
D_CONV = 1024
D_RET = 1024
RET_HEADS = 8
HEAD_DIM = 128
CHUNK = 128
SEG = 1024
N_SEG = 8
SUB = 256
N_SUB = SEG // SUB
GRID_W = 64
ROPE_BASE = 10000.0
ROPE_F = 32
EPS = 1e-6
K_SCALE = HEAD_DIM ** -0.5

F32 = jnp.float32
BF16 = jnp.bfloat16

G_H, G_B, G_C, G_ZC, G_Q, G_K, G_V, G_ZR = range(8)


def _silu(x):
    return x / (1.0 + jnp.exp(-x))


def _params(vmem_mb, n_axes, flags=None):
    return pltpu.CompilerParams(
        dimension_semantics=("arbitrary",) * n_axes,
        vmem_limit_bytes=vmem_mb * 1024 * 1024,
        flags=flags,
    )


def _mod_kernel(cv_ref, w_ref, b_ref, o_ref):
    s = _silu(cv_ref[...])
    o_ref[0] = jnp.dot(s.astype(BF16), w_ref[0].astype(BF16),
                       preferred_element_type=F32) + b_ref[0]


def _modulation(cv, w_mod, b_mod, tn=1024):
    depth, d, n = w_mod.shape
    return pl.pallas_call(
        _mod_kernel,
        grid=(depth, n // tn),
        in_specs=[
            pl.BlockSpec((8, d), lambda l, j: (0, 0)),
            pl.BlockSpec((1, d, tn), lambda l, j: (l, 0, j)),
            pl.BlockSpec((1, 1, tn), lambda l, j: (l, 0, j)),
        ],
        out_specs=pl.BlockSpec((1, 8, tn), lambda l, j: (l, 0, j)),
        out_shape=jax.ShapeDtypeStruct((depth, 8, n), F32),
        compiler_params=_params(40, 2),
    )(cv, w_mod, b_mod.reshape(depth, 1, n))


def _prologue(x_ref, nw_ref, shift_ref, scale_ref, hx_ref):
    x = x_ref[...]
    ms = jnp.mean(x * x, axis=-1, keepdims=True)
    y = x * lax.rsqrt(ms + EPS) * nw_ref[...]
    hx_ref[...] = (y * (1.0 + scale_ref[...]) + shift_ref[...]).astype(BF16)


def _rope_pair(acc, cos, sa, sb):
    outs = []
    for h in range(SUB // HEAD_DIM):
        a = acc[:, h * HEAD_DIM:(h + 1) * HEAD_DIM]
        outs.append(a * cos + pltpu.roll(a, HEAD_DIM - ROPE_F, 1) * sa
                    + pltpu.roll(a, ROPE_F, 1) * sb)
    return jnp.concatenate(outs, axis=1)


def _inproj_kernel(x_ref, nw_ref, shift_ref, scale_ref,
                   wh_ref, wb_ref, wc_ref, wzc_ref, wq_ref, wk_ref, wv_ref, wzr_ref,
                   cos_ref, sa_ref, sb_ref,
                   ch_ref, b_ref, zc_ref, q_ref, k_ref, v_ref, zr_ref, hx_ref, *, rope):
    @pl.when(pl.program_id(1) == 0)
    def _():
        _prologue(x_ref, nw_ref, shift_ref, scale_ref, hx_ref)

    hx = hx_ref[...]

    def seg(w_ref):
        return jnp.dot(hx, w_ref[0], preferred_element_type=F32)

    ch_ref[...] = (seg(wc_ref) * seg(wh_ref)).astype(BF16)
    b_ref[...] = seg(wb_ref).astype(BF16)
    zc_ref[...] = _silu(seg(wzc_ref)).astype(BF16)
    if rope:
        cos, sa, sb = cos_ref[...], sa_ref[...], sb_ref[...]
        q_ref[...] = _rope_pair(seg(wq_ref), cos, sa, sb).astype(BF16)
        k_ref[...] = _rope_pair(seg(wk_ref), cos * K_SCALE, sa * K_SCALE, sb * K_SCALE).astype(BF16)
    else:
        q_ref[...] = seg(wq_ref).astype(BF16)
        k_ref[...] = (seg(wk_ref) * K_SCALE).astype(BF16)
    v_ref[...] = seg(wv_ref).astype(BF16)
    zr_ref[...] = _silu(seg(wzr_ref)).astype(BF16)


def _w_spec(layer, group):
    return pl.BlockSpec((1, D_MODEL, SUB), lambda i, j: (layer, 0, group * N_SUB + j))


def _inproj(x, norm_w, shift, scale, w_bf16, layer, tables, *, rope, tm):
    rows, d = x.shape
    cos, sa, sb = tables
    kern = functools.partial(_inproj_kernel, rope=rope)
    vec = lambda: pl.BlockSpec((1, d), lambda i, j: (0, 0))
    tab = lambda: pl.BlockSpec((tm, HEAD_DIM), lambda i, j: (i, 0))
    out = lambda: pl.BlockSpec((tm, SUB), lambda i, j: (i, j))
    seg_shape = jax.ShapeDtypeStruct((rows, SEG), BF16)
    return pl.pallas_call(
        kern,
        grid=(rows // tm, N_SUB),
        in_specs=[
            pl.BlockSpec((tm, d), lambda i, j: (i, 0)),
            vec(), vec(), vec(),
            *[_w_spec(layer, g) for g in range(N_SEG)],
            tab(), tab(), tab(),
        ],
        out_specs=[out() for _ in range(7)],
        out_shape=[seg_shape] * 7,
        scratch_shapes=[pltpu.VMEM((tm, d), BF16)],
        compiler_params=_params(48, 2),
    )(x, norm_w, shift, scale, *([w_bf16] * N_SEG), cos, sa, sb)


def _inproj_kv_kernel(x_ref, nw_ref, shift_ref, scale_ref, wk_ref, wv_ref, k_ref, v_ref, hx_ref):
    @pl.when(pl.program_id(1) == 0)
    def _():
        _prologue(x_ref, nw_ref, shift_ref, scale_ref, hx_ref)

    hx = hx_ref[...]
    k_ref[...] = (jnp.dot(hx, wk_ref[0], preferred_element_type=F32) * K_SCALE).astype(BF16)
    v_ref[...] = jnp.dot(hx, wv_ref[0], preferred_element_type=F32).astype(BF16)


def _inproj_kv(x, norm_w, shift, scale, w_bf16, layer, *, tm):
    rows, d = x.shape
    vec = lambda: pl.BlockSpec((1, d), lambda i, j: (0, 0))
    out = lambda: pl.BlockSpec((tm, SUB), lambda i, j: (i, j))
    seg_shape = jax.ShapeDtypeStruct((rows, SEG), BF16)
    return pl.pallas_call(
        _inproj_kv_kernel,
        grid=(rows // tm, N_SUB),
        in_specs=[
            pl.BlockSpec((tm, d), lambda i, j: (i, 0)),
            vec(), vec(), vec(),
            _w_spec(layer, G_K), _w_spec(layer, G_V),
        ],
        out_specs=[out(), out()],
        out_shape=[seg_shape] * 2,
        scratch_shapes=[pltpu.VMEM((tm, d), BF16)],
        compiler_params=_params(32, 2),
    )(x, norm_w, shift, scale, w_bf16, w_bf16)


def _states_kernel(lgf_ref, lgb_ref, kf_ref, vf_ref, kb_ref, vb_ref, s0f_ref, s0b_ref,
                   sf_ref, sb_ref, ff_ref, fb_ref, *, nc):
    t = pl.program_id(0)

    @pl.when(t == 0)
    def _():
        ff_ref[...] = s0f_ref[...]
        fb_ref[...] = s0b_ref[...]

    row = lax.broadcasted_iota(jnp.int32, (CHUNK, HEAD_DIM), 0).astype(F32)
    tn = (((0,), (0,)), ((), ()))
    for h in range(RET_HEADS):
        hs = slice(h * HEAD_DIM, (h + 1) * HEAD_DIM)
        lgf = lgf_ref[h]
        lgb = lgb_ref[h]
        kdf = jnp.exp(lgf * (CHUNK - 1.0 - row))
        kdb = jnp.exp(lgb * row)
        cdf = jnp.exp(lgf * CHUNK + 0.0 * row)
        cdb = jnp.exp(lgb * CHUNK + 0.0 * row)

        s = ff_ref[h]
        for c in range(nc):
            rs = slice(c * CHUNK, (c + 1) * CHUNK)
            sf_ref[c, h] = s.astype(BF16)
            kd = (kf_ref[rs, hs].astype(F32) * kdf).astype(BF16)
            s = cdf * s + lax.dot_general(kd, vf_ref[rs, hs], tn, preferred_element_type=F32)
        ff_ref[h] = s

        s = fb_ref[h]
        for c in reversed(range(nc)):
            rs = slice(c * CHUNK, (c + 1) * CHUNK)
            sb_ref[c, h] = s.astype(BF16)
            kd = (kb_ref[rs, hs].astype(F32) * kdb).astype(BF16)
            s = cdb * s + lax.dot_general(kd, vb_ref[rs, hs], tn, preferred_element_type=F32)
        fb_ref[h] = s


def _states(lg_f, lg_b, k, v, s0f, s0b, *, tr):
    rows = k.shape[0]
    nt = rows // tr
    nc = tr // CHUNK
    n_chunks = rows // CHUNK
    kern = functools.partial(_states_kernel, nc=nc)
    smem = lambda: pl.BlockSpec(memory_space=pltpu.SMEM)
    st = lambda: pl.BlockSpec((RET_HEADS, HEAD_DIM, HEAD_DIM), lambda t: (0, 0, 0))
    fwd = lambda: pl.BlockSpec((tr, SEG), lambda t: (t, 0))
    bwd = lambda: pl.BlockSpec((tr, SEG), lambda t: (nt - 1 - t, 0))
    seq_shape = jax.ShapeDtypeStruct((n_chunks, RET_HEADS, HEAD_DIM, HEAD_DIM), BF16)
    fin_shape = jax.ShapeDtypeStruct((RET_HEADS, HEAD_DIM, HEAD_DIM), F32)
    return pl.pallas_call(
        kern,
        grid=(nt,),
        in_specs=[smem(), smem(), fwd(), fwd(), bwd(), bwd(), st(), st()],
        out_specs=[
            pl.BlockSpec((nc, RET_HEADS, HEAD_DIM, HEAD_DIM), lambda t: (t, 0, 0, 0)),
            pl.BlockSpec((nc, RET_HEADS, HEAD_DIM, HEAD_DIM), lambda t: (nt - 1 - t, 0, 0, 0)),
            st(), st(),
        ],
        out_shape=[seq_shape, seq_shape, fin_shape, fin_shape],
        compiler_params=_params(32, 1),
    )(lg_f, lg_b, k, v, k, v, s0f, s0b)


def _lane_mean(t):
    hi = t.astype(BF16)
    lo = (t - hi.astype(F32)).astype(BF16)
    ones = jnp.full((2 * HEAD_DIM, HEAD_DIM), 1.0 / HEAD_DIM, BF16)
    return jnp.dot(jnp.concatenate([hi, lo], axis=1), ones, preferred_element_type=F32)


def _out_kernel(lgf_ref, lgb_ref,
                ch_ref, b_ref, zc_ref, q_ref, k_ref, v_ref, zr_ref, chp_ref, chn_ref,
                sf_ref, sb_ref, x_ref, gate_ref, convw_ref, cnw_ref, gnw_ref, wout_ref, fnw_ref,
                o_ref,
                mask_ref, qdf_ref, qdb_ref, ya_ref, yb_ref, *, tm, final):
    s_id = pl.program_id(0)
    n_tiles = pl.num_programs(0) - 1
    i = jnp.maximum(s_id - 1, 0)
    last = n_tiles - 1
    nc = tm // CHUNK

    @pl.when(s_id == 0)
    def _():
        yb_ref[...] = jnp.zeros_like(yb_ref)
        r = lax.broadcasted_iota(jnp.int32, (CHUNK, CHUNK), 0).astype(F32)
        cc = lax.broadcasted_iota(jnp.int32, (CHUNK, CHUNK), 1).astype(F32)
        d = r - cc
        for h in range(RET_HEADS):
            lgf = lgf_ref[h]
            lgb = lgb_ref[h]
            mf = jnp.where(d >= 0, jnp.exp(lgf * jnp.maximum(d, 0.0)), 0.0)
            mb = jnp.where(d <= 0, jnp.exp(lgb * jnp.maximum(-d, 0.0)), 0.0)
            mask_ref[h] = mf + mb
            qdf_ref[h] = jnp.exp(lgf * (r + 1.0))
            qdb_ref[h] = jnp.exp(lgb * (CHUNK - r))

    def step(yr_old, yr_new):
        n_q = 4
        cw_ = D_CONV // n_q
        pw = D_MODEL // n_q
        ridx = lax.broadcasted_iota(jnp.int32, (tm, 1), 0)
        cwt = convw_ref[0]

        proj_ret, y_l, ssq = [], [], None
        for n in range(n_q):
            proj_ret.append(jnp.dot(yr_old[...], wout_ref[0, D_CONV:, n * pw:(n + 1) * pw],
                                    preferred_element_type=F32))
            cs = slice(n * cw_, (n + 1) * cw_)
            ch = ch_ref[:, cs].astype(F32)
            prev_row = jnp.where(i > 0, chp_ref[:, cs].astype(F32)[15:16, :], 0.0)
            next_row = jnp.where(i < last, chn_ref[:, cs].astype(F32)[0:1, :], 0.0)
            prev = jnp.where(ridx == 0, prev_row, pltpu.roll(ch, 1, 0))
            nxt = jnp.where(ridx == tm - 1, next_row, pltpu.roll(ch, tm - 1, 0))
            conv = prev * cwt[0:1, cs] + ch * cwt[1:2, cs] + nxt * cwt[2:3, cs]
            y = b_ref[:, cs].astype(F32) * conv
            y_l.append(y)
            part = jnp.sum(y * y, axis=-1, keepdims=True)
            ssq = part if ssq is None else ssq + part
        rinv = lax.rsqrt(ssq * (1.0 / D_CONV) + EPS)

        nt = (((1,), (1,)), ((), ()))
        pairs = [(c, h) for c in range(nc) for h in range(RET_HEADS)]
        rs = lambda c: slice(c * CHUNK, (c + 1) * CHUNK)
        hs = lambda h: slice(h * HEAD_DIM, (h + 1) * HEAD_DIM)
        q_l = [q_ref[rs(c), hs(h)] for c, h in pairs]
        s_l = [lax.dot_general(q, k_ref[rs(c), hs(h)], nt, preferred_element_type=F32)
               for q, (c, h) in zip(q_l, pairs)]

        y_conv = jnp.concatenate(
            [(zc_ref[:, n * cw_:(n + 1) * cw_].astype(F32)
              * (y_l[n] * rinv * cnw_ref[:, n * cw_:(n + 1) * cw_])).astype(BF16)
             for n in range(n_q)], axis=1)

        def finish_chunk(n):
            cols = slice(n * pw, (n + 1) * pw)
            pc = jnp.dot(y_conv, wout_ref[0, :D_CONV, cols], preferred_element_type=F32)
            o_ref[:, cols] = x_ref[:, cols] + gate_ref[:, cols] * (pc + proj_ret[n])

        finish_chunk(0)
        finish_chunk(1)
        lhs_l = []
        for q, s, (c, h) in zip(q_l, s_l, pairs):
            qf32 = q.astype(F32)
            lhs_l.append(jnp.concatenate(
                [(s * mask_ref[h]).astype(BF16), (qf32 * qdf_ref[h]).astype(BF16),
                 (qf32 * qdb_ref[h]).astype(BF16)], axis=1))
        o_l = [jnp.dot(lhs, jnp.concatenate([v_ref[rs(c), hs(h)], sf_ref[c, h], sb_ref[c, h]],
                                            axis=0), preferred_element_type=F32)
               for lhs, (c, h) in zip(lhs_l, pairs)]
        finish_chunk(2)
        oc_l = [o - _lane_mean(o) for o in o_l]
        finish_chunk(3)
        var_l = [_lane_mean(oc * oc) for oc in oc_l]
        if final:
            xn = o_ref[...]
            ms = jnp.mean(xn * xn, axis=-1, keepdims=True)
            o_ref[...] = xn * lax.rsqrt(ms + EPS) * fnw_ref[...]
        for oc, var, (c, h) in zip(oc_l, var_l, pairs):
            on = oc * lax.rsqrt(var + EPS) * gnw_ref[:, hs(h)]
            yr_new[rs(c), hs(h)] = (zr_ref[rs(c), hs(h)].astype(F32) * on).astype(BF16)

    @pl.when(s_id % 2 == 0)
    def _():
        step(yb_ref, ya_ref)

    @pl.when(s_id % 2 == 1)
    def _():
        step(ya_ref, yb_ref)


def _out(lg_f, lg_b, segs, sf, sb, x, gate, conv_w, cnw, gnw, wout_bf16, fnw, layer, *, tm, final):
    ch, b, zc, q, k, v, zr = segs
    rows, d = x.shape
    nt = rows // tm
    nc = tm // CHUNK
    hb = tm // 16
    n_hb = rows // 16
    kern = functools.partial(_out_kernel, tm=tm, final=final)
    smem = lambda: pl.BlockSpec(memory_space=pltpu.SMEM)
    cur = lambda s: jnp.minimum(s, nt - 1)
    prv = lambda s: jnp.maximum(s - 1, 0)
    seg = lambda: pl.BlockSpec((tm, SEG), lambda s: (cur(s), 0))
    fin = lambda: pl.BlockSpec((tm, SEG), lambda s: (prv(s), 0))
    st = lambda: pl.BlockSpec((nc, RET_HEADS, HEAD_DIM, HEAD_DIM), lambda s: (cur(s), 0, 0, 0))
    full = lambda shape: pl.BlockSpec(shape, lambda s: (0,) * len(shape))
    return pl.pallas_call(
        kern,
        grid=(nt + 1,),
        in_specs=[
            smem(), smem(),
            fin(), fin(), fin(), seg(), seg(), seg(), seg(),
            pl.BlockSpec((16, SEG), lambda s: (jnp.maximum(prv(s) * hb - 1, 0), 0)),
            pl.BlockSpec((16, SEG), lambda s: (jnp.minimum((prv(s) + 1) * hb, n_hb - 1), 0)),
            st(), st(),
            pl.BlockSpec((tm, d), lambda s: (prv(s), 0)),
            full((1, d)),
            pl.BlockSpec((1, 3, D_CONV), lambda s: (layer, 0, 0)),
            full((1, D_CONV)), full((1, D_RET)),
            pl.BlockSpec((1, d, d), lambda s: (layer, 0, 0)),
            full((1, d)),
        ],
        out_specs=pl.BlockSpec((tm, d), lambda s: (prv(s), 0)),
        out_shape=jax.ShapeDtypeStruct((rows, d), F32),
        scratch_shapes=[
            pltpu.VMEM((RET_HEADS, CHUNK, CHUNK), F32),
            pltpu.VMEM((RET_HEADS, CHUNK, HEAD_DIM), F32),
            pltpu.VMEM((RET_HEADS, CHUNK, HEAD_DIM), F32),
            pltpu.VMEM((tm, D_RET), BF16),
            pltpu.VMEM((tm, D_RET), BF16),
        ],
        compiler_params=_params(48, 1),
    )(lg_f, lg_b, ch, b, zc, q, k, v, zr, ch, ch, sf, sb, x, gate, conv_w, cnw, gnw,
      wout_bf16, fnw)


def _rope_tables(seq):
    t = np.arange(seq)
    row = (t // GRID_W).astype(np.float64)
    col = (t % GRID_W).astype(np.float64)
    inv = ROPE_BASE ** (-np.arange(ROPE_F, dtype=np.float64) / ROPE_F)
    ar = row[:, None] * inv[None, :]
    ac = col[:, None] * inv[None, :]
    z = np.zeros_like(ar)
    cos = np.concatenate([np.cos(ar), np.cos(ar), np.cos(ac), np.cos(ac)], axis=-1)
    sa = np.concatenate([-np.sin(ar), z, -np.sin(ac), z], axis=-1)
    sb = np.concatenate([z, np.sin(ar), z, np.sin(ac)], axis=-1)
    return tuple(jnp.asarray(a, dtype=F32) for a in (cos, sa, sb))


def kernel(x, c, ctx, c_ctx, norm_w, w_mod, b_mod, w_in, conv_w, conv_norm_w, ret_norm_w,
           ret_decay_f, ret_decay_b, w_out, final_norm_w):
    batch, seq, d = x.shape
    assert batch == 1 and d == D_MODEL and seq % 1024 == 0
    depth = norm_w.shape[0]
    ctx_len = ctx.shape[1]
    xs = x[0]
    cs = ctx[0]

    cv = jnp.zeros((8, d), F32).at[0].set(c[0]).at[1].set(c_ctx)
    mod = _modulation(cv, w_mod, b_mod)
    tables = _rope_tables(seq)
    ctx_tables = tuple(jnp.zeros((ctx_len, HEAD_DIM), F32) for _ in range(3))
    zero_state = jnp.zeros((RET_HEADS, HEAD_DIM, HEAD_DIM), F32)
    fnw = final_norm_w.reshape(1, d)
    w_in_p = w_in.astype(BF16)
    w_out_b = w_out.astype(BF16)

    for layer in range(depth):
        update_ctx = layer < depth - 1
        lg_f = -jnp.exp(ret_decay_f[layer].astype(F32))
        lg_b = -jnp.exp(ret_decay_b[layer].astype(F32))
        nw = norm_w[layer].reshape(1, d)
        m = mod[layer]
        shift, scale, gate = m[0:1, 0:d], m[0:1, d:2 * d], m[0:1, 2 * d:3 * d]
        shift_c, scale_c, gate_c = m[1:2, 0:d], m[1:2, d:2 * d], m[1:2, 2 * d:3 * d]
        cnw = conv_norm_w[layer].reshape(1, D_CONV)
        gnw = ret_norm_w[layer].reshape(1, D_RET)

        if update_ctx:
            segs_c = _inproj(cs, nw, shift_c, scale_c, w_in_p, layer, ctx_tables,
                             rope=False, tm=ctx_len)
            k_c, v_c = segs_c[4], segs_c[5]
        else:
            k_c, v_c = _inproj_kv(cs, nw, shift_c, scale_c, w_in_p, layer, tm=ctx_len)
        sf_c, sb_c, s_f, s_b = _states(lg_f, lg_b, k_c, v_c, zero_state, zero_state, tr=ctx_len)

        segs = _inproj(xs, nw, shift, scale, w_in_p, layer, tables, rope=True, tm=512)
        sf, sb, _, _ = _states(lg_f, lg_b, segs[4], segs[5], s_f, s_b, tr=512)
        xs = _out(lg_f, lg_b, segs, sf, sb, xs, gate, conv_w, cnw, gnw, w_out_b, fnw, layer,
                  tm=256, final=not update_ctx)
        if update_ctx:
            cs = _out(lg_f, lg_b, segs_c, sf_c, sb_c, cs, gate_c, conv_w, cnw, gnw, w_out_b, fnw,
                      layer, tm=ctx_len, final=False)
    return xs[None]
```

```python
import functools

import numpy as np
import jax
import jax.numpy as jnp
from jax import lax
from jax.experimental import pallas as pl
from jax.experimental.pallas import tpu as pltpu

D_MODEL = 2048
D_CONV = 1024
D_RET = 1024
RET_HEADS = 8
HEAD_DIM = 128
CHUNK = 128
SEG = 1024
N_SEG = 8
SUB = 256
N_SUB = SEG // SUB
GRID_W = 64
ROPE_BASE = 10000.0
ROPE_F = 32
EPS = 1e-6
K_SCALE = HEAD_DIM ** -0.5

F32 = jnp.float32
BF16 = jnp.bfloat16

G_H, G_B, G_C, G_ZC, G_Q, G_K, G_V, G_ZR = range(8)


def _silu(x):
    return x / (1.0 + jnp.exp(-x))


def _params(vmem_mb, n_axes, flags=None):
    return pltpu.CompilerParams(
        dimension_semantics=("arbitrary",) * n_axes,
        vmem_limit_bytes=vmem_mb * 1024 * 1024,
        flags=flags,
    )


def _mod_kernel(cv_ref, w_ref, b_ref, o_ref):
    s = _silu(cv_ref[...])
    o_ref[0] = jnp.dot(s.astype(BF16), w_ref[0].astype(BF16),
                       preferred_element_type=F32) + b_ref[0]


def _modulation(cv, w_mod, b_mod, tn=1024):
    depth, d, n = w_mod.shape
    return pl.pallas_call(
        _mod_kernel,
        grid=(depth, n // tn),
        in_specs=[
            pl.BlockSpec((8, d), lambda l, j: (0, 0)),
            pl.BlockSpec((1, d, tn), lambda l, j: (l, 0, j)),
            pl.BlockSpec((1, 1, tn), lambda l, j: (l, 0, j)),
        ],
        out_specs=pl.BlockSpec((1, 8, tn), lambda l, j: (l, 0, j)),
        out_shape=jax.ShapeDtypeStruct((depth, 8, n), F32),
        compiler_params=_params(40, 2),
    )(cv, w_mod, b_mod.reshape(depth, 1, n))


PROLOGUE_ROWS = 64


def _prologue(x_ref, nw_ref, shift_ref, scale_ref, hx_ref):
    gain = nw_ref[...] * (1.0 + scale_ref[...])
    shift = shift_ref[...]

    def body(r, carry):
        rows = pl.ds(pl.multiple_of(r * PROLOGUE_ROWS, PROLOGUE_ROWS), PROLOGUE_ROWS)
        x = x_ref[rows, :]
        ms = jnp.mean(x * x, axis=-1, keepdims=True)
        hx_ref[rows, :] = (x * lax.rsqrt(ms + EPS) * gain + shift).astype(BF16)
        return carry

    lax.fori_loop(0, x_ref.shape[0] // PROLOGUE_ROWS, body, 0)


def _rope_pair(acc, cos, sa, sb):
    outs = []
    for h in range(SUB // HEAD_DIM):
        a = acc[:, h * HEAD_DIM:(h + 1) * HEAD_DIM]
        outs.append(a * cos + pltpu.roll(a, HEAD_DIM - ROPE_F, 1) * sa
                    + pltpu.roll(a, ROPE_F, 1) * sb)
    return jnp.concatenate(outs, axis=1)


def _inproj_kernel(x_ref, nw_ref, shift_ref, scale_ref,
                   wh_ref, wb_ref, wc_ref, wzc_ref, wq_ref, wk_ref, wv_ref, wzr_ref,
                   cos_ref, sa_ref, sb_ref,
                   ch_ref, b_ref, zc_ref, q_ref, k_ref, v_ref, zr_ref, hx_ref, *, rope):
    @pl.when(pl.program_id(1) == 0)
    def _():
        _prologue(x_ref, nw_ref, shift_ref, scale_ref, hx_ref)

    hx = hx_ref[...]

    def seg(w_ref):
        return jnp.dot(hx, w_ref[0], preferred_element_type=F32)

    ch_ref[...] = (seg(wc_ref) * seg(wh_ref)).astype(BF16)
    b_ref[...] = seg(wb_ref).astype(BF16)
    zc_ref[...] = _silu(seg(wzc_ref)).astype(BF16)
    if rope:
        cos, sa, sb = cos_ref[...], sa_ref[...], sb_ref[...]
        q_ref[...] = _rope_pair(seg(wq_ref), cos, sa, sb).astype(BF16)
        k_ref[...] = _rope_pair(seg(wk_ref), cos * K_SCALE, sa * K_SCALE, sb * K_SCALE).astype(BF16)
    else:
        q_ref[...] = seg(wq_ref).astype(BF16)
        k_ref[...] = (seg(wk_ref) * K_SCALE).astype(BF16)
    v_ref[...] = seg(wv_ref).astype(BF16)
    zr_ref[...] = _silu(seg(wzr_ref)).astype(BF16)


def _w_spec(layer, group):
    return pl.BlockSpec((1, D_MODEL, SUB), lambda i, j: (layer, 0, group * N_SUB + j))


def _inproj(x, norm_w, shift, scale, w_bf16, layer, tables, *, rope, tm):
    rows, d = x.shape
    cos, sa, sb = tables
    kern = functools.partial(_inproj_kernel, rope=rope)
    vec = lambda: pl.BlockSpec((1, d), lambda i, j: (0, 0))
    tab = lambda: pl.BlockSpec((tm, HEAD_DIM), lambda i, j: (i, 0))
    out = lambda: pl.BlockSpec((tm, SUB), lambda i, j: (i, j))
    seg_shape = jax.ShapeDtypeStruct((rows, SEG), BF16)
    return pl.pallas_call(
        kern,
        grid=(rows // tm, N_SUB),
        in_specs=[
            pl.BlockSpec((tm, d), lambda i, j: (i, 0)),
            vec(), vec(), vec(),
            *[_w_spec(layer, g) for g in range(N_SEG)],
            tab(), tab(), tab(),
        ],
        out_specs=[out() for _ in range(7)],
        out_shape=[seg_shape] * 7,
        scratch_shapes=[pltpu.VMEM((tm, d), BF16)],
        compiler_params=_params(56, 2),
    )(x, norm_w, shift, scale, *([w_bf16] * N_SEG), cos, sa, sb)


def _inproj_kv_kernel(x_ref, nw_ref, shift_ref, scale_ref, wk_ref, wv_ref, k_ref, v_ref, hx_ref):
    @pl.when(pl.program_id(1) == 0)
    def _():
        _prologue(x_ref, nw_ref, shift_ref, scale_ref, hx_ref)

    hx = hx_ref[...]
    k_ref[...] = (jnp.dot(hx, wk_ref[0], preferred_element_type=F32) * K_SCALE).astype(BF16)
    v_ref[...] = jnp.dot(hx, wv_ref[0], preferred_element_type=F32).astype(BF16)


def _inproj_kv(x, norm_w, shift, scale, w_bf16, layer, *, tm):
    rows, d = x.shape
    vec = lambda: pl.BlockSpec((1, d), lambda i, j: (0, 0))
    out = lambda: pl.BlockSpec((tm, SUB), lambda i, j: (i, j))
    seg_shape = jax.ShapeDtypeStruct((rows, SEG), BF16)
    return pl.pallas_call(
        _inproj_kv_kernel,
        grid=(rows // tm, N_SUB),
        in_specs=[
            pl.BlockSpec((tm, d), lambda i, j: (i, 0)),
            vec(), vec(), vec(),
            _w_spec(layer, G_K), _w_spec(layer, G_V),
        ],
        out_specs=[out(), out()],
        out_shape=[seg_shape] * 2,
        scratch_shapes=[pltpu.VMEM((tm, d), BF16)],
        compiler_params=_params(32, 2),
    )(x, norm_w, shift, scale, w_bf16, w_bf16)


def _states_kernel(lgf_ref, lgb_ref, kf_ref, vf_ref, kb_ref, vb_ref, s0f_ref, s0b_ref,
                   sf_ref, sb_ref, ff_ref, fb_ref, *, nc):
    t = pl.program_id(0)

    @pl.when(t == 0)
    def _():
        ff_ref[...] = s0f_ref[...]
        fb_ref[...] = s0b_ref[...]

    row = lax.broadcasted_iota(jnp.int32, (CHUNK, HEAD_DIM), 0).astype(F32)
    tn = (((0,), (0,)), ((), ()))
    for h in range(RET_HEADS):
        hs = slice(h * HEAD_DIM, (h + 1) * HEAD_DIM)
        lgf = lgf_ref[h]
        lgb = lgb_ref[h]
        kdf = jnp.exp(lgf * (CHUNK - 1.0 - row))
        kdb = jnp.exp(lgb * row)
        cdf = jnp.exp(lgf * CHUNK + 0.0 * row)
        cdb = jnp.exp(lgb * CHUNK + 0.0 * row)

        s = ff_ref[h]
        for c in range(nc):
            rs = slice(c * CHUNK, (c + 1) * CHUNK)
            sf_ref[c, h] = s.astype(BF16)
            kd = (kf_ref[rs, hs].astype(F32) * kdf).astype(BF16)
            s = cdf * s + lax.dot_general(kd, vf_ref[rs, hs], tn, preferred_element_type=F32)
        ff_ref[h] = s

        s = fb_ref[h]
        for c in reversed(range(nc)):
            rs = slice(c * CHUNK, (c + 1) * CHUNK)
            sb_ref[c, h] = s.astype(BF16)
            kd = (kb_ref[rs, hs].astype(F32) * kdb).astype(BF16)
            s = cdb * s + lax.dot_general(kd, vb_ref[rs, hs], tn, preferred_element_type=F32)
        fb_ref[h] = s


def _states(lg_f, lg_b, k, v, s0f, s0b, *, tr):
    rows = k.shape[0]
    nt = rows // tr
    nc = tr // CHUNK
    n_chunks = rows // CHUNK
    kern = functools.partial(_states_kernel, nc=nc)
    smem = lambda: pl.BlockSpec(memory_space=pltpu.SMEM)
    st = lambda: pl.BlockSpec((RET_HEADS, HEAD_DIM, HEAD_DIM), lambda t: (0, 0, 0))
    fwd = lambda: pl.BlockSpec((tr, SEG), lambda t: (t, 0))
    bwd = lambda: pl.BlockSpec((tr, SEG), lambda t: (nt - 1 - t, 0))
    seq_shape = jax.ShapeDtypeStruct((n_chunks, RET_HEADS, HEAD_DIM, HEAD_DIM), BF16)
    fin_shape = jax.ShapeDtypeStruct((RET_HEADS, HEAD_DIM, HEAD_DIM), F32)
    return pl.pallas_call(
        kern,
        grid=(nt,),
        in_specs=[smem(), smem(), fwd(), fwd(), bwd(), bwd(), st(), st()],
        out_specs=[
            pl.BlockSpec((nc, RET_HEADS, HEAD_DIM, HEAD_DIM), lambda t: (t, 0, 0, 0)),
            pl.BlockSpec((nc, RET_HEADS, HEAD_DIM, HEAD_DIM), lambda t: (nt - 1 - t, 0, 0, 0)),
            st(), st(),
        ],
        out_shape=[seq_shape, seq_shape, fin_shape, fin_shape],
        compiler_params=_params(32, 1),
    )(lg_f, lg_b, k, v, k, v, s0f, s0b)


def _lane_mean(t):
    hi = t.astype(BF16)
    lo = (t - hi.astype(F32)).astype(BF16)
    ones = jnp.full((2 * HEAD_DIM, HEAD_DIM), 1.0 / HEAD_DIM, BF16)
    return jnp.dot(jnp.concatenate([hi, lo], axis=1), ones, preferred_element_type=F32)


def _out_kernel(lgf_ref, lgb_ref,
                ch_ref, b_ref, zc_ref, q_ref, k_ref, v_ref, zr_ref, chp_ref, chn_ref,
                sf_ref, sb_ref, x_ref, gate_ref, convw_ref, cnw_ref, gnw_ref, wout_ref, fnw_ref,
                o_ref,
                mask_ref, qdf_ref, qdb_ref, ya_ref, yb_ref, *, tm, final):
    s_id = pl.program_id(0)
    n_tiles = pl.num_programs(0) - 1
    i = jnp.maximum(s_id - 1, 0)
    last = n_tiles - 1
    nc = tm // CHUNK

    @pl.when(s_id == 0)
    def _():
        yb_ref[...] = jnp.zeros_like(yb_ref)
        r = lax.broadcasted_iota(jnp.int32, (CHUNK, CHUNK), 0).astype(F32)
        cc = lax.broadcasted_iota(jnp.int32, (CHUNK, CHUNK), 1).astype(F32)
        d = r - cc
        for h in range(RET_HEADS):
            lgf = lgf_ref[h]
            lgb = lgb_ref[h]
            mf = jnp.where(d >= 0, jnp.exp(lgf * jnp.maximum(d, 0.0)), 0.0)
            mb = jnp.where(d <= 0, jnp.exp(lgb * jnp.maximum(-d, 0.0)), 0.0)
            mask_ref[h] = mf + mb
            qdf_ref[h] = jnp.exp(lgf * (r + 1.0))
            qdb_ref[h] = jnp.exp(lgb * (CHUNK - r))

    def step(yr_old, yr_new):
        n_q = 4
        cw_ = D_CONV // n_q
        pw = D_MODEL // n_q
        ridx = lax.broadcasted_iota(jnp.int32, (tm, 1), 0)
        cwt = convw_ref[0]

        proj_ret, y_l, ssq = [], [], None
        for n in range(n_q):
            proj_ret.append(jnp.dot(yr_old[...], wout_ref[0, D_CONV:, n * pw:(n + 1) * pw],
                                    preferred_element_type=F32))
            cs = slice(n * cw_, (n + 1) * cw_)
            ch = ch_ref[:, cs].astype(F32)
            prev_row = jnp.where(i > 0, chp_ref[:, cs].astype(F32)[15:16, :], 0.0)
            next_row = jnp.where(i < last, chn_ref[:, cs].astype(F32)[0:1, :], 0.0)
            prev = jnp.where(ridx == 0, prev_row, pltpu.roll(ch, 1, 0))
            nxt = jnp.where(ridx == tm - 1, next_row, pltpu.roll(ch, tm - 1, 0))
            conv = prev * cwt[0:1, cs] + ch * cwt[1:2, cs] + nxt * cwt[2:3, cs]
            y = b_ref[:, cs].astype(F32) * conv
            y_l.append(y)
            part = jnp.sum(y * y, axis=-1, keepdims=True)
            ssq = part if ssq is None else ssq + part
        rinv = lax.rsqrt(ssq * (1.0 / D_CONV) + EPS)

        nt = (((1,), (1,)), ((), ()))
        pairs = [(c, h) for c in range(nc) for h in range(RET_HEADS)]
        rs = lambda c: slice(c * CHUNK, (c + 1) * CHUNK)
        hs = lambda h: slice(h * HEAD_DIM, (h + 1) * HEAD_DIM)
        q_l = [q_ref[rs(c), hs(h)] for c, h in pairs]
        s_l = [lax.dot_general(q, k_ref[rs(c), hs(h)], nt, preferred_element_type=F32)
               for q, (c, h) in zip(q_l, pairs)]

        y_conv = jnp.concatenate(
            [(zc_ref[:, n * cw_:(n + 1) * cw_].astype(F32)
              * (y_l[n] * rinv * cnw_ref[:, n * cw_:(n + 1) * cw_])).astype(BF16)
             for n in range(n_q)], axis=1)

        def finish_chunk(n):
            cols = slice(n * pw, (n + 1) * pw)
            pc = jnp.dot(y_conv, wout_ref[0, :D_CONV, cols], preferred_element_type=F32)
            o_ref[:, cols] = x_ref[:, cols] + gate_ref[:, cols] * (pc + proj_ret[n])

        finish_chunk(0)
        finish_chunk(1)
        lhs_l = []
        for q, s, (c, h) in zip(q_l, s_l, pairs):
            qf32 = q.astype(F32)
            lhs_l.append(jnp.concatenate(
                [(s * mask_ref[h]).astype(BF16), (qf32 * qdf_ref[h]).astype(BF16),
                 (qf32 * qdb_ref[h]).astype(BF16)], axis=1))
        o_l = [jnp.dot(lhs, jnp.concatenate([v_ref[rs(c), hs(h)], sf_ref[c, h], sb_ref[c, h]],
                                            axis=0), preferred_element_type=F32)
               for lhs, (c, h) in zip(lhs_l, pairs)]
        finish_chunk(2)
        oc_l = [o - _lane_mean(o) for o in o_l]
        finish_chunk(3)
        var_l = [_lane_mean(oc * oc) for oc in oc_l]
        if final:
            xn = o_ref[...]
            ms = jnp.mean(xn * xn, axis=-1, keepdims=True)
            o_ref[...] = xn * lax.rsqrt(ms + EPS) * fnw_ref[...]
        for oc, var, (c, h) in zip(oc_l, var_l, pairs):
            on = oc * lax.rsqrt(var + EPS) * gnw_ref[:, hs(h)]
            yr_new[rs(c), hs(h)] = (zr_ref[rs(c), hs(h)].astype(F32) * on).astype(BF16)

    @pl.when(s_id % 2 == 0)
    def _():
        step(yb_ref, ya_ref)

    @pl.when(s_id % 2 == 1)
    def _():
        step(ya_ref, yb_ref)


def _out(lg_f, lg_b, segs, sf, sb, x, gate, conv_w, cnw, gnw, wout_bf16, fnw, layer, *, tm, final):
    ch, b, zc, q, k, v, zr = segs
    rows, d = x.shape
    nt = rows // tm
    nc = tm // CHUNK
    hb = tm // 16
    n_hb = rows // 16
    kern = functools.partial(_out_kernel, tm=tm, final=final)
    smem = lambda: pl.BlockSpec(memory_space=pltpu.SMEM)
    cur = lambda s: jnp.minimum(s, nt - 1)
    prv = lambda s: jnp.maximum(s - 1, 0)
    seg = lambda: pl.BlockSpec((tm, SEG), lambda s: (cur(s), 0))
    fin = lambda: pl.BlockSpec((tm, SEG), lambda s: (prv(s), 0))
    st = lambda: pl.BlockSpec((nc, RET_HEADS, HEAD_DIM, HEAD_DIM), lambda s: (cur(s), 0, 0, 0))
    full = lambda shape: pl.BlockSpec(shape, lambda s: (0,) * len(shape))
    return pl.pallas_call(
        kern,
        grid=(nt + 1,),
        in_specs=[
            smem(), smem(),
            fin(), fin(), fin(), seg(), seg(), seg(), seg(),
            pl.BlockSpec((16, SEG), lambda s: (jnp.maximum(prv(s) * hb - 1, 0), 0)),
            pl.BlockSpec((16, SEG), lambda s: (jnp.minimum((prv(s) + 1) * hb, n_hb - 1), 0)),
            st(), st(),
            pl.BlockSpec((tm, d), lambda s: (prv(s), 0)),
            full((1, d)),
            pl.BlockSpec((1, 3, D_CONV), lambda s: (layer, 0, 0)),
            full((1, D_CONV)), full((1, D_RET)),
            pl.BlockSpec((1, d, d), lambda s: (layer, 0, 0)),
            full((1, d)),
        ],
        out_specs=pl.BlockSpec((tm, d), lambda s: (prv(s), 0)),
        out_shape=jax.ShapeDtypeStruct((rows, d), F32),
        scratch_shapes=[
            pltpu.VMEM((RET_HEADS, CHUNK, CHUNK), F32),
            pltpu.VMEM((RET_HEADS, CHUNK, HEAD_DIM), F32),
            pltpu.VMEM((RET_HEADS, CHUNK, HEAD_DIM), F32),
            pltpu.VMEM((tm, D_RET), BF16),
            pltpu.VMEM((tm, D_RET), BF16),
        ],
        compiler_params=_params(48, 1),
    )(lg_f, lg_b, ch, b, zc, q, k, v, zr, ch, ch, sf, sb, x, gate, conv_w, cnw, gnw,
      wout_bf16, fnw)


def _rope_tables(seq):
    t = np.arange(seq)
    row = (t // GRID_W).astype(np.float64)
    col = (t % GRID_W).astype(np.float64)
    inv = ROPE_BASE ** (-np.arange(ROPE_F, dtype=np.float64) / ROPE_F)
    ar = row[:, None] * inv[None, :]
    ac = col[:, None] * inv[None, :]
    z = np.zeros_like(ar)
    cos = np.concatenate([np.cos(ar), np.cos(ar), np.cos(ac), np.cos(ac)], axis=-1)
    sa = np.concatenate([-np.sin(ar), z, -np.sin(ac), z], axis=-1)
    sb = np.concatenate([z, np.sin(ar), z, np.sin(ac)], axis=-1)
    return tuple(jnp.asarray(a, dtype=F32) for a in (cos, sa, sb))


def kernel(x, c, ctx, c_ctx, norm_w, w_mod, b_mod, w_in, conv_w, conv_norm_w, ret_norm_w,
           ret_decay_f, ret_decay_b, w_out, final_norm_w):
    batch, seq, d = x.shape
    assert batch == 1 and d == D_MODEL and seq % 1024 == 0
    depth = norm_w.shape[0]
    ctx_len = ctx.shape[1]
    xs = x[0]
    cs = ctx[0]

    cv = jnp.zeros((8, d), F32).at[0].set(c[0]).at[1].set(c_ctx)
    mod = _modulation(cv, w_mod, b_mod)
    tables = _rope_tables(seq)
    ctx_tables = tuple(jnp.zeros((ctx_len, HEAD_DIM), F32) for _ in range(3))
    zero_state = jnp.zeros((RET_HEADS, HEAD_DIM, HEAD_DIM), F32)
    fnw = final_norm_w.reshape(1, d)
    w_in_p = w_in.astype(BF16)
    w_out_b = w_out.astype(BF16)

    for layer in range(depth):
        update_ctx = layer < depth - 1
        lg_f = -jnp.exp(ret_decay_f[layer].astype(F32))
        lg_b = -jnp.exp(ret_decay_b[layer].astype(F32))
        nw = norm_w[layer].reshape(1, d)
        m = mod[layer]
        shift, scale, gate = m[0:1, 0:d], m[0:1, d:2 * d], m[0:1, 2 * d:3 * d]
        shift_c, scale_c, gate_c = m[1:2, 0:d], m[1:2, d:2 * d], m[1:2, 2 * d:3 * d]
        cnw = conv_norm_w[layer].reshape(1, D_CONV)
        gnw = ret_norm_w[layer].reshape(1, D_RET)

        if update_ctx:
            segs_c = _inproj(cs, nw, shift_c, scale_c, w_in_p, layer, ctx_tables,
                             rope=False, tm=ctx_len)
            k_c, v_c = segs_c[4], segs_c[5]
        else:
            k_c, v_c = _inproj_kv(cs, nw, shift_c, scale_c, w_in_p, layer, tm=ctx_len)
        sf_c, sb_c, s_f, s_b = _states(lg_f, lg_b, k_c, v_c, zero_state, zero_state, tr=ctx_len)

        segs = _inproj(xs, nw, shift, scale, w_in_p, layer, tables, rope=True, tm=1024)
        sf, sb, _, _ = _states(lg_f, lg_b, segs[4], segs[5], s_f, s_b, tr=512)
        xs = _out(lg_f, lg_b, segs, sf, sb, xs, gate, conv_w, cnw, gnw, w_out_b, fnw, layer,
                  tm=256, final=not update_ctx)
        if update_ctx:
            cs = _out(lg_f, lg_b, segs_c, sf_c, sb_c, cs, gate_c, conv_w, cnw, gnw, w_out_b, fnw,
                      layer, tm=ctx_len, final=False)
    return xs[None]
```

```python
import functools

import numpy as np
import jax
import jax.numpy as jnp
from jax import lax
from jax.experimental import pallas as pl
from jax.experimental.pallas import tpu as pltpu

D_MODEL = 2048
D_CONV = 1024
D_RET = 1024
RET_HEADS = 8
HEAD_DIM = 128
CHUNK = 128
SEG = 1024
N_SEG = 8
SUB = 256
N_SUB = SEG // SUB
GRID_W = 64
ROPE_BASE = 10000.0
ROPE_F = 32
EPS = 1e-6
K_SCALE = HEAD_DIM ** -0.5

F32 = jnp.float32
BF16 = jnp.bfloat16

G_H, G_B, G_C, G_ZC, G_Q, G_K, G_V, G_ZR = range(8)


def _silu(x):
    return x / (1.0 + jnp.exp(-x))


def _params(vmem_mb, n_axes, flags=None):
    return pltpu.CompilerParams(
        dimension_semantics=("arbitrary",) * n_axes,
        vmem_limit_bytes=vmem_mb * 1024 * 1024,
        flags=flags,
    )


def _mod_kernel(cv_ref, w_ref, b_ref, o_ref):
    s = _silu(cv_ref[...])
    o_ref[0] = jnp.dot(s.astype(BF16), w_ref[0].astype(BF16),
                       preferred_element_type=F32) + b_ref[0]


def _modulation(cv, w_mod, b_mod, tn=1024):
    depth, d, n = w_mod.shape
    return pl.pallas_call(
        _mod_kernel,
        grid=(depth, n // tn),
        in_specs=[
            pl.BlockSpec((8, d), lambda l, j: (0, 0)),
            pl.BlockSpec((1, d, tn), lambda l, j: (l, 0, j)),
            pl.BlockSpec((1, 1, tn), lambda l, j: (l, 0, j)),
        ],
        out_specs=pl.BlockSpec((1, 8, tn), lambda l, j: (l, 0, j)),
        out_shape=jax.ShapeDtypeStruct((depth, 8, n), F32),
        compiler_params=_params(40, 2),
    )(cv, w_mod, b_mod.reshape(depth, 1, n))


PROLOGUE_ROWS = 64


def _prologue(x_ref, nw_ref, shift_ref, scale_ref, hx_ref):
    gain = nw_ref[...] * (1.0 + scale_ref[...])
    shift = shift_ref[...]

    def body(r, carry):
        rows = pl.ds(pl.multiple_of(r * PROLOGUE_ROWS, PROLOGUE_ROWS), PROLOGUE_ROWS)
        x = x_ref[rows, :]
        ms = jnp.mean(x * x, axis=-1, keepdims=True)
        hx_ref[rows, :] = (x * lax.rsqrt(ms + EPS) * gain + shift).astype(BF16)
        return carry

    lax.fori_loop(0, x_ref.shape[0] // PROLOGUE_ROWS, body, 0)


def _rope_pair(acc, cos, sa, sb):
    outs = []
    for h in range(SUB // HEAD_DIM):
        a = acc[:, h * HEAD_DIM:(h + 1) * HEAD_DIM]
        outs.append(a * cos + pltpu.roll(a, HEAD_DIM - ROPE_F, 1) * sa
                    + pltpu.roll(a, ROPE_F, 1) * sb)
    return jnp.concatenate(outs, axis=1)


def _project_groups(hx, w_refs, table_refs, out_refs, rope):
    wh_ref, wb_ref, wc_ref, wzc_ref, wq_ref, wk_ref, wv_ref, wzr_ref = w_refs
    ch_ref, b_ref, zc_ref, q_ref, k_ref, v_ref, zr_ref = out_refs

    def seg(w_ref):
        return jnp.dot(hx, w_ref[0], preferred_element_type=F32)

    ch_ref[...] = (seg(wc_ref) * seg(wh_ref)).astype(BF16)
    b_ref[...] = seg(wb_ref).astype(BF16)
    zc_ref[...] = _silu(seg(wzc_ref)).astype(BF16)
    if rope:
        cos, sa, sb = (t[...] for t in table_refs)
        q_ref[...] = _rope_pair(seg(wq_ref), cos, sa, sb).astype(BF16)
        k_ref[...] = _rope_pair(seg(wk_ref), cos * K_SCALE, sa * K_SCALE, sb * K_SCALE).astype(BF16)
    else:
        q_ref[...] = seg(wq_ref).astype(BF16)
        k_ref[...] = (seg(wk_ref) * K_SCALE).astype(BF16)
    v_ref[...] = seg(wv_ref).astype(BF16)
    zr_ref[...] = _silu(seg(wzr_ref)).astype(BF16)


def _inproj_kernel(x_ref, nw_ref, shift_ref, scale_ref, *refs, rope):
    w_refs, table_refs, out_refs, hx_ref = refs[:8], refs[8:11], refs[11:18], refs[18]

    @pl.when(pl.program_id(1) == 0)
    def _():
        _prologue(x_ref, nw_ref, shift_ref, scale_ref, hx_ref)

    _project_groups(hx_ref[...], w_refs, table_refs, out_refs, rope)


def _inproj_latent_kernel(xq_ref, nw_ref, shift_ref, scale_ref, *refs, tm):
    w_refs, table_refs, out_refs = refs[:8], refs[8:11], refs[11:18]
    hxa_ref, hxb_ref = refs[18:20]
    i = pl.program_id(0)
    j = pl.program_id(1)
    quarter = tm // N_SUB
    row0 = pl.multiple_of(j * quarter, quarter)

    def normalise(hx_ref):
        gain = nw_ref[...] * (1.0 + scale_ref[...])
        shift = shift_ref[...]
        for r in range(0, quarter, PROLOGUE_ROWS):
            x = xq_ref[r:r + PROLOGUE_ROWS, :]
            ms = jnp.mean(x * x, axis=-1, keepdims=True)
            hx_ref[pl.ds(row0 + r, PROLOGUE_ROWS), :] = (
                x * lax.rsqrt(ms + EPS) * gain + shift).astype(BF16)

    @pl.when(i == 0)
    def _():
        normalise(hxa_ref)

    @pl.when(i % 2 == 1)
    def _():
        normalise(hxb_ref)
        _project_groups(hxa_ref[...], w_refs, table_refs, out_refs, True)

    @pl.when(jnp.logical_and(i > 0, i % 2 == 0))
    def _():
        normalise(hxa_ref)
        _project_groups(hxb_ref[...], w_refs, table_refs, out_refs, True)


def _w_spec(layer, group):
    return pl.BlockSpec((1, D_MODEL, SUB), lambda i, j: (layer, 0, group * N_SUB + j))


def _inproj(x, norm_w, shift, scale, w_bf16, layer, tables, *, rope, tm):
    rows, d = x.shape
    cos, sa, sb = tables
    kern = functools.partial(_inproj_kernel, rope=rope)
    vec = lambda: pl.BlockSpec((1, d), lambda i, j: (0, 0))
    tab = lambda: pl.BlockSpec((tm, HEAD_DIM), lambda i, j: (i, 0))
    out = lambda: pl.BlockSpec((tm, SUB), lambda i, j: (i, j))
    seg_shape = jax.ShapeDtypeStruct((rows, SEG), BF16)
    return pl.pallas_call(
        kern,
        grid=(rows // tm, N_SUB),
        in_specs=[
            pl.BlockSpec((tm, d), lambda i, j: (i, 0)),
            vec(), vec(), vec(),
            *[_w_spec(layer, g) for g in range(N_SEG)],
            tab(), tab(), tab(),
        ],
        out_specs=[out() for _ in range(7)],
        out_shape=[seg_shape] * 7,
        scratch_shapes=[pltpu.VMEM((tm, d), BF16)],
        compiler_params=_params(56, 2),
    )(x, norm_w, shift, scale, *([w_bf16] * N_SEG), cos, sa, sb)


def _inproj_latent(x, norm_w, shift, scale, w_bf16, layer, tables, *, tm):
    rows, d = x.shape
    nt = rows // tm
    cos, sa, sb = tables
    kern = functools.partial(_inproj_latent_kernel, tm=tm)
    done = lambda i: jnp.maximum(i - 1, 0)
    vec = lambda: pl.BlockSpec((1, d), lambda i, j: (0, 0))
    tab = lambda: pl.BlockSpec((tm, HEAD_DIM), lambda i, j: (done(i), 0))
    out = lambda: pl.BlockSpec((tm, SUB), lambda i, j: (done(i), jnp.where(i == 0, 0, j)))
    seg_shape = jax.ShapeDtypeStruct((rows, SEG), BF16)
    return pl.pallas_call(
        kern,
        grid=(nt + 1, N_SUB),
        in_specs=[
            pl.BlockSpec((tm // N_SUB, d), lambda i, j: (jnp.minimum(i, nt - 1) * N_SUB + j, 0)),
            vec(), vec(), vec(),
            *[_w_spec(layer, g) for g in range(N_SEG)],
            tab(), tab(), tab(),
        ],
        out_specs=[out() for _ in range(7)],
        out_shape=[seg_shape] * 7,
        scratch_shapes=[pltpu.VMEM((tm, d), BF16), pltpu.VMEM((tm, d), BF16)],
        compiler_params=_params(56, 2),
    )(x, norm_w, shift, scale, *([w_bf16] * N_SEG), cos, sa, sb)


def _inproj_kv_kernel(x_ref, nw_ref, shift_ref, scale_ref, wk_ref, wv_ref, k_ref, v_ref, hx_ref):
    @pl.when(pl.program_id(1) == 0)
    def _():
        _prologue(x_ref, nw_ref, shift_ref, scale_ref, hx_ref)

    hx = hx_ref[...]
    k_ref[...] = (jnp.dot(hx, wk_ref[0], preferred_element_type=F32) * K_SCALE).astype(BF16)
    v_ref[...] = jnp.dot(hx, wv_ref[0], preferred_element_type=F32).astype(BF16)


def _inproj_kv(x, norm_w, shift, scale, w_bf16, layer, *, tm):
    rows, d = x.shape
    vec = lambda: pl.BlockSpec((1, d), lambda i, j: (0, 0))
    out = lambda: pl.BlockSpec((tm, SUB), lambda i, j: (i, j))
    seg_shape = jax.ShapeDtypeStruct((rows, SEG), BF16)
    return pl.pallas_call(
        _inproj_kv_kernel,
        grid=(rows // tm, N_SUB),
        in_specs=[
            pl.BlockSpec((tm, d), lambda i, j: (i, 0)),
            vec(), vec(), vec(),
            _w_spec(layer, G_K), _w_spec(layer, G_V),
        ],
        out_specs=[out(), out()],
        out_shape=[seg_shape] * 2,
        scratch_shapes=[pltpu.VMEM((tm, d), BF16)],
        compiler_params=_params(32, 2),
    )(x, norm_w, shift, scale, w_bf16, w_bf16)


def _states_kernel(lgf_ref, lgb_ref, kf_ref, vf_ref, kb_ref, vb_ref, s0f_ref, s0b_ref,
                   sf_ref, sb_ref, ff_ref, fb_ref, *, nc):
    t = pl.program_id(0)

    @pl.when(t == 0)
    def _():
        ff_ref[...] = s0f_ref[...]
        fb_ref[...] = s0b_ref[...]

    row = lax.broadcasted_iota(jnp.int32, (CHUNK, HEAD_DIM), 0).astype(F32)
    tn = (((0,), (0,)), ((), ()))
    for h in range(RET_HEADS):
        hs = slice(h * HEAD_DIM, (h + 1) * HEAD_DIM)
        lgf = lgf_ref[h]
        lgb = lgb_ref[h]
        kdf = jnp.exp(lgf * (CHUNK - 1.0 - row))
        kdb = jnp.exp(lgb * row)
        cdf = jnp.exp(lgf * CHUNK + 0.0 * row)
        cdb = jnp.exp(lgb * CHUNK + 0.0 * row)

        s = ff_ref[h]
        for c in range(nc):
            rs = slice(c * CHUNK, (c + 1) * CHUNK)
            sf_ref[c, h] = s.astype(BF16)
            kd = (kf_ref[rs, hs].astype(F32) * kdf).astype(BF16)
            s = cdf * s + lax.dot_general(kd, vf_ref[rs, hs], tn, preferred_element_type=F32)
        ff_ref[h] = s

        s = fb_ref[h]
        for c in reversed(range(nc)):
            rs = slice(c * CHUNK, (c + 1) * CHUNK)
            sb_ref[c, h] = s.astype(BF16)
            kd = (kb_ref[rs, hs].astype(F32) * kdb).astype(BF16)
            s = cdb * s + lax.dot_general(kd, vb_ref[rs, hs], tn, preferred_element_type=F32)
        fb_ref[h] = s


def _states(lg_f, lg_b, k, v, s0f, s0b, *, tr):
    rows = k.shape[0]
    nt = rows // tr
    nc = tr // CHUNK
    n_chunks = rows // CHUNK
    kern = functools.partial(_states_kernel, nc=nc)
    smem = lambda: pl.BlockSpec(memory_space=pltpu.SMEM)
    st = lambda: pl.BlockSpec((RET_HEADS, HEAD_DIM, HEAD_DIM), lambda t: (0, 0, 0))
    fwd = lambda: pl.BlockSpec((tr, SEG), lambda t: (t, 0))
    bwd = lambda: pl.BlockSpec((tr, SEG), lambda t: (nt - 1 - t, 0))
    seq_shape = jax.ShapeDtypeStruct((n_chunks, RET_HEADS, HEAD_DIM, HEAD_DIM), BF16)
    fin_shape = jax.ShapeDtypeStruct((RET_HEADS, HEAD_DIM, HEAD_DIM), F32)
    return pl.pallas_call(
        kern,
        grid=(nt,),
        in_specs=[smem(), smem(), fwd(), fwd(), bwd(), bwd(), st(), st()],
        out_specs=[
            pl.BlockSpec((nc, RET_HEADS, HEAD_DIM, HEAD_DIM), lambda t: (t, 0, 0, 0)),
            pl.BlockSpec((nc, RET_HEADS, HEAD_DIM, HEAD_DIM), lambda t: (nt - 1 - t, 0, 0, 0)),
            st(), st(),
        ],
        out_shape=[seq_shape, seq_shape, fin_shape, fin_shape],
        compiler_params=_params(32, 1),
    )(lg_f, lg_b, k, v, k, v, s0f, s0b)


def _lane_mean(t):
    hi = t.astype(BF16)
    lo = (t - hi.astype(F32)).astype(BF16)
    ones = jnp.full((2 * HEAD_DIM, HEAD_DIM), 1.0 / HEAD_DIM, BF16)
    return jnp.dot(jnp.concatenate([hi, lo], axis=1), ones, preferred_element_type=F32)


def _out_kernel(lgf_ref, lgb_ref,
                ch_ref, b_ref, zc_ref, q_ref, k_ref, v_ref, zr_ref, chp_ref, chn_ref,
                sf_ref, sb_ref, x_ref, gate_ref, convw_ref, cnw_ref, gnw_ref, wout_ref, fnw_ref,
                o_ref,
                mask_ref, qdf_ref, qdb_ref, ya_ref, yb_ref, *, tm, final):
    s_id = pl.program_id(0)
    n_tiles = pl.num_programs(0) - 1
    i = jnp.maximum(s_id - 1, 0)
    last = n_tiles - 1
    nc = tm // CHUNK

    @pl.when(s_id == 0)
    def _():
        yb_ref[...] = jnp.zeros_like(yb_ref)
        r = lax.broadcasted_iota(jnp.int32, (CHUNK, CHUNK), 0).astype(F32)
        cc = lax.broadcasted_iota(jnp.int32, (CHUNK, CHUNK), 1).astype(F32)
        d = r - cc
        for h in range(RET_HEADS):
            lgf = lgf_ref[h]
            lgb = lgb_ref[h]
            mf = jnp.where(d >= 0, jnp.exp(lgf * jnp.maximum(d, 0.0)), 0.0)
            mb = jnp.where(d <= 0, jnp.exp(lgb * jnp.maximum(-d, 0.0)), 0.0)
            mask_ref[h] = mf + mb
            qdf_ref[h] = jnp.exp(lgf * (r + 1.0))
            qdb_ref[h] = jnp.exp(lgb * (CHUNK - r))

    def step(yr_old, yr_new):
        n_q = 4
        cw_ = D_CONV // n_q
        pw = D_MODEL // n_q
        ridx = lax.broadcasted_iota(jnp.int32, (tm, 1), 0)
        cwt = convw_ref[0]

        proj_ret, y_l, ssq = [], [], None
        for n in range(n_q):
            proj_ret.append(jnp.dot(yr_old[...], wout_ref[0, D_CONV:, n * pw:(n + 1) * pw],
                                    preferred_element_type=F32))
            cs = slice(n * cw_, (n + 1) * cw_)
            ch = ch_ref[:, cs].astype(F32)
            prev_row = jnp.where(i > 0, chp_ref[:, cs].astype(F32)[15:16, :], 0.0)
            next_row = jnp.where(i < last, chn_ref[:, cs].astype(F32)[0:1, :], 0.0)
            prev = jnp.where(ridx == 0, prev_row, pltpu.roll(ch, 1, 0))
            nxt = jnp.where(ridx == tm - 1, next_row, pltpu.roll(ch, tm - 1, 0))
            conv = prev * cwt[0:1, cs] + ch * cwt[1:2, cs] + nxt * cwt[2:3, cs]
            y = b_ref[:, cs].astype(F32) * conv
            y_l.append(y)
            part = jnp.sum(y * y, axis=-1, keepdims=True)
            ssq = part if ssq is None else ssq + part
        rinv = lax.rsqrt(ssq * (1.0 / D_CONV) + EPS)

        nt = (((1,), (1,)), ((), ()))
        pairs = [(c, h) for c in range(nc) for h in range(RET_HEADS)]
        rs = lambda c: slice(c * CHUNK, (c + 1) * CHUNK)
        hs = lambda h: slice(h * HEAD_DIM, (h + 1) * HEAD_DIM)
        q_l = [q_ref[rs(c), hs(h)] for c, h in pairs]
        s_l = [lax.dot_general(q, k_ref[rs(c), hs(h)], nt, preferred_element_type=F32)
               for q, (c, h) in zip(q_l, pairs)]

        y_conv = jnp.concatenate(
            [(zc_ref[:, n * cw_:(n + 1) * cw_].astype(F32)
              * (y_l[n] * rinv * cnw_ref[:, n * cw_:(n + 1) * cw_])).astype(BF16)
             for n in range(n_q)], axis=1)

        def finish_chunk(n):
            cols = slice(n * pw, (n + 1) * pw)
            pc = jnp.dot(y_conv, wout_ref[0, :D_CONV, cols], preferred_element_type=F32)
            o_ref[:, cols] = x_ref[:, cols] + gate_ref[:, cols] * (pc + proj_ret[n])

        finish_chunk(0)
        finish_chunk(1)
        lhs_l = []
        for q, s, (c, h) in zip(q_l, s_l, pairs):
            qf32 = q.astype(F32)
            lhs_l.append(jnp.concatenate(
                [(s * mask_ref[h]).astype(BF16), (qf32 * qdf_ref[h]).astype(BF16),
                 (qf32 * qdb_ref[h]).astype(BF16)], axis=1))
        o_l = [jnp.dot(lhs, jnp.concatenate([v_ref[rs(c), hs(h)], sf_ref[c, h], sb_ref[c, h]],
                                            axis=0), preferred_element_type=F32)
               for lhs, (c, h) in zip(lhs_l, pairs)]
        finish_chunk(2)
        oc_l = [o - _lane_mean(o) for o in o_l]
        finish_chunk(3)
        var_l = [_lane_mean(oc * oc) for oc in oc_l]
        if final:
            xn = o_ref[...]
            ms = jnp.mean(xn * xn, axis=-1, keepdims=True)
            o_ref[...] = xn * lax.rsqrt(ms + EPS) * fnw_ref[...]
        for oc, var, (c, h) in zip(oc_l, var_l, pairs):
            on = oc * lax.rsqrt(var + EPS) * gnw_ref[:, hs(h)]
            yr_new[rs(c), hs(h)] = (zr_ref[rs(c), hs(h)].astype(F32) * on).astype(BF16)

    @pl.when(s_id % 2 == 0)
    def _():
        step(yb_ref, ya_ref)

    @pl.when(s_id % 2 == 1)
    def _():
        step(ya_ref, yb_ref)


def _out(lg_f, lg_b, segs, sf, sb, x, gate, conv_w, cnw, gnw, wout_bf16, fnw, layer, *, tm, final):
    ch, b, zc, q, k, v, zr = segs
    rows, d = x.shape
    nt = rows // tm
    nc = tm // CHUNK
    hb = tm // 16
    n_hb = rows // 16
    kern = functools.partial(_out_kernel, tm=tm, final=final)
    smem = lambda: pl.BlockSpec(memory_space=pltpu.SMEM)
    cur = lambda s: jnp.minimum(s, nt - 1)
    prv = lambda s: jnp.maximum(s - 1, 0)
    seg = lambda: pl.BlockSpec((tm, SEG), lambda s: (cur(s), 0))
    fin = lambda: pl.BlockSpec((tm, SEG), lambda s: (prv(s), 0))
    st = lambda: pl.BlockSpec((nc, RET_HEADS, HEAD_DIM, HEAD_DIM), lambda s: (cur(s), 0, 0, 0))
    full = lambda shape: pl.BlockSpec(shape, lambda s: (0,) * len(shape))
    return pl.pallas_call(
        kern,
        grid=(nt + 1,),
        in_specs=[
            smem(), smem(),
            fin(), fin(), fin(), seg(), seg(), seg(), seg(),
            pl.BlockSpec((16, SEG), lambda s: (jnp.maximum(prv(s) * hb - 1, 0), 0)),
            pl.BlockSpec((16, SEG), lambda s: (jnp.minimum((prv(s) + 1) * hb, n_hb - 1), 0)),
            st(), st(),
            pl.BlockSpec((tm, d), lambda s: (prv(s), 0)),
            full((1, d)),
            pl.BlockSpec((1, 3, D_CONV), lambda s: (layer, 0, 0)),
            full((1, D_CONV)), full((1, D_RET)),
            pl.BlockSpec((1, d, d), lambda s: (layer, 0, 0), pipeline_mode=pl.Buffered(1)),
            full((1, d)),
        ],
        out_specs=pl.BlockSpec((tm, d), lambda s: (prv(s), 0)),
        out_shape=jax.ShapeDtypeStruct((rows, d), F32),
        scratch_shapes=[
            pltpu.VMEM((RET_HEADS, CHUNK, CHUNK), F32),
            pltpu.VMEM((RET_HEADS, CHUNK, HEAD_DIM), F32),
            pltpu.VMEM((RET_HEADS, CHUNK, HEAD_DIM), F32),
            pltpu.VMEM((tm, D_RET), BF16),
            pltpu.VMEM((tm, D_RET), BF16),
        ],
        compiler_params=_params(56, 1),
    )(lg_f, lg_b, ch, b, zc, q, k, v, zr, ch, ch, sf, sb, x, gate, conv_w, cnw, gnw,
      wout_bf16, fnw)


def _rope_tables(seq):
    t = np.arange(seq)
    row = (t // GRID_W).astype(np.float64)
    col = (t % GRID_W).astype(np.float64)
    inv = ROPE_BASE ** (-np.arange(ROPE_F, dtype=np.float64) / ROPE_F)
    ar = row[:, None] * inv[None, :]
    ac = col[:, None] * inv[None, :]
    z = np.zeros_like(ar)
    cos = np.concatenate([np.cos(ar), np.cos(ar), np.cos(ac), np.cos(ac)], axis=-1)
    sa = np.concatenate([-np.sin(ar), z, -np.sin(ac), z], axis=-1)
    sb = np.concatenate([z, np.sin(ar), z, np.sin(ac)], axis=-1)
    return tuple(jnp.asarray(a, dtype=F32) for a in (cos, sa, sb))


def kernel(x, c, ctx, c_ctx, norm_w, w_mod, b_mod, w_in, conv_w, conv_norm_w, ret_norm_w,
           ret_decay_f, ret_decay_b, w_out, final_norm_w):
    batch, seq, d = x.shape
    assert batch == 1 and d == D_MODEL and seq % 1024 == 0
    depth = norm_w.shape[0]
    ctx_len = ctx.shape[1]
    xs = x[0]
    cs = ctx[0]

    cv = jnp.zeros((8, d), F32).at[0].set(c[0]).at[1].set(c_ctx)
    mod = _modulation(cv, w_mod, b_mod)
    tables = _rope_tables(seq)
    ctx_tables = tuple(jnp.zeros((ctx_len, HEAD_DIM), F32) for _ in range(3))
    zero_state = jnp.zeros((RET_HEADS, HEAD_DIM, HEAD_DIM), F32)
    fnw = final_norm_w.reshape(1, d)
    w_in_p = w_in.astype(BF16)
    w_out_b = w_out.astype(BF16)

    for layer in range(depth):
        update_ctx = layer < depth - 1
        lg_f = -jnp.exp(ret_decay_f[layer].astype(F32))
        lg_b = -jnp.exp(ret_decay_b[layer].astype(F32))
        nw = norm_w[layer].reshape(1, d)
        m = mod[layer]
        shift, scale, gate = m[0:1, 0:d], m[0:1, d:2 * d], m[0:1, 2 * d:3 * d]
        shift_c, scale_c, gate_c = m[1:2, 0:d], m[1:2, d:2 * d], m[1:2, 2 * d:3 * d]
        cnw = conv_norm_w[layer].reshape(1, D_CONV)
        gnw = ret_norm_w[layer].reshape(1, D_RET)

        if update_ctx:
            segs_c = _inproj(cs, nw, shift_c, scale_c, w_in_p, layer, ctx_tables,
                             rope=False, tm=ctx_len)
            k_c, v_c = segs_c[4], segs_c[5]
        else:
            k_c, v_c = _inproj_kv(cs, nw, shift_c, scale_c, w_in_p, layer, tm=ctx_len)
        sf_c, sb_c, s_f, s_b = _states(lg_f, lg_b, k_c, v_c, zero_state, zero_state, tr=ctx_len)

        segs = _inproj_latent(xs, nw, shift, scale, w_in_p, layer, tables, tm=1024)
        sf, sb, _, _ = _states(lg_f, lg_b, segs[4], segs[5], s_f, s_b, tr=512)
        xs = _out(lg_f, lg_b, segs, sf, sb, xs, gate, conv_w, cnw, gnw, w_out_b, fnw, layer,
                  tm=512, final=not update_ctx)
        if update_ctx:
            cs = _out(lg_f, lg_b, segs_c, sf_c, sb_c, cs, gate_c, conv_w, cnw, gnw, w_out_b, fnw,
                      layer, tm=ctx_len, final=False)
    return xs[None]
```

```python
import functools

import numpy as np
import jax
import jax.numpy as jnp
from jax import lax
from jax.experimental import pallas as pl
from jax.experimental.pallas import tpu as pltpu

D_MODEL = 2048
D_CONV = 1024
D_RET = 1024
RET_HEADS = 8
HEAD_DIM = 128
CHUNK = 128
SEG = 1024
N_SEG = 8
SUB = 256
N_SUB = SEG // SUB
GRID_W = 64
ROPE_BASE = 10000.0
ROPE_F = 32
EPS = 1e-6
K_SCALE = HEAD_DIM ** -0.5

F32 = jnp.float32
BF16 = jnp.bfloat16

G_H, G_B, G_C, G_ZC, G_Q, G_K, G_V, G_ZR = range(8)


def _silu(x):
    return x / (1.0 + jnp.exp(-x))


def _params(vmem_mb, n_axes, flags=None):
    return pltpu.CompilerParams(
        dimension_semantics=("arbitrary",) * n_axes,
        vmem_limit_bytes=vmem_mb * 1024 * 1024,
        flags=flags,
    )


def _mod_kernel(cv_ref, w_ref, b_ref, o_ref):
    s = _silu(cv_ref[...])
    o_ref[0] = jnp.dot(s.astype(BF16), w_ref[0].astype(BF16),
                       preferred_element_type=F32) + b_ref[0]


def _modulation(cv, w_mod, b_mod, tn=1024):
    depth, d, n = w_mod.shape
    return pl.pallas_call(
        _mod_kernel,
        grid=(depth, n // tn),
        in_specs=[
            pl.BlockSpec((8, d), lambda l, j: (0, 0)),
            pl.BlockSpec((1, d, tn), lambda l, j: (l, 0, j)),
            pl.BlockSpec((1, 1, tn), lambda l, j: (l, 0, j)),
        ],
        out_specs=pl.BlockSpec((1, 8, tn), lambda l, j: (l, 0, j)),
        out_shape=jax.ShapeDtypeStruct((depth, 8, n), F32),
        compiler_params=_params(40, 2),
    )(cv, w_mod, b_mod.reshape(depth, 1, n))


PROLOGUE_ROWS = 64
WOUT_CAST_ROWS = 128


def _prologue(x_ref, nw_ref, shift_ref, scale_ref, hx_ref):
    gain = nw_ref[...] * (1.0 + scale_ref[...])
    shift = shift_ref[...]

    def body(r, carry):
        rows = pl.ds(pl.multiple_of(r * PROLOGUE_ROWS, PROLOGUE_ROWS), PROLOGUE_ROWS)
        x = x_ref[rows, :]
        ms = jnp.mean(x * x, axis=-1, keepdims=True)
        hx_ref[rows, :] = (x * lax.rsqrt(ms + EPS) * gain + shift).astype(BF16)
        return carry

    lax.fori_loop(0, x_ref.shape[0] // PROLOGUE_ROWS, body, 0)


def _rope_pair(acc, cos, sa, sb):
    outs = []
    for h in range(SUB // HEAD_DIM):
        a = acc[:, h * HEAD_DIM:(h + 1) * HEAD_DIM]
        outs.append(a * cos + pltpu.roll(a, HEAD_DIM - ROPE_F, 1) * sa
                    + pltpu.roll(a, ROPE_F, 1) * sb)
    return jnp.concatenate(outs, axis=1)


def _project_groups(hx, w_refs, table_refs, out_refs, rope):
    wh_ref, wb_ref, wc_ref, wzc_ref, wq_ref, wk_ref, wv_ref, wzr_ref = w_refs
    ch_ref, b_ref, zc_ref, q_ref, k_ref, v_ref, zr_ref = out_refs

    def seg(w_ref):
        return jnp.dot(hx, w_ref[0], preferred_element_type=F32)

    ch_ref[...] = (seg(wc_ref) * seg(wh_ref)).astype(BF16)
    b_ref[...] = seg(wb_ref).astype(BF16)
    zc_ref[...] = _silu(seg(wzc_ref)).astype(BF16)
    if rope:
        cos, sa, sb = (t[...] for t in table_refs)
        q_ref[...] = _rope_pair(seg(wq_ref), cos, sa, sb).astype(BF16)
        k_ref[...] = _rope_pair(seg(wk_ref), cos * K_SCALE, sa * K_SCALE, sb * K_SCALE).astype(BF16)
    else:
        q_ref[...] = seg(wq_ref).astype(BF16)
        k_ref[...] = (seg(wk_ref) * K_SCALE).astype(BF16)
    v_ref[...] = seg(wv_ref).astype(BF16)
    zr_ref[...] = _silu(seg(wzr_ref)).astype(BF16)


def _inproj_kernel(x_ref, nw_ref, shift_ref, scale_ref, *refs, rope):
    w_refs, table_refs, out_refs, hx_ref = refs[:8], refs[8:11], refs[11:18], refs[18]

    @pl.when(pl.program_id(1) == 0)
    def _():
        _prologue(x_ref, nw_ref, shift_ref, scale_ref, hx_ref)

    _project_groups(hx_ref[...], w_refs, table_refs, out_refs, rope)


def _inproj_latent_kernel(xq_ref, nw_ref, shift_ref, scale_ref, *refs, tm):
    w_refs, table_refs, out_refs = refs[:8], refs[8:11], refs[11:18]
    hxa_ref, hxb_ref = refs[18:20]
    i = pl.program_id(0)
    j = pl.program_id(1)
    quarter = tm // N_SUB
    row0 = pl.multiple_of(j * quarter, quarter)

    def normalise(hx_ref):
        gain = nw_ref[...] * (1.0 + scale_ref[...])
        shift = shift_ref[...]
        for r in range(0, quarter, PROLOGUE_ROWS):
            x = xq_ref[r:r + PROLOGUE_ROWS, :]
            ms = jnp.mean(x * x, axis=-1, keepdims=True)
            hx_ref[pl.ds(row0 + r, PROLOGUE_ROWS), :] = (
                x * lax.rsqrt(ms + EPS) * gain + shift).astype(BF16)

    @pl.when(i == 0)
    def _():
        normalise(hxa_ref)

    @pl.when(i % 2 == 1)
    def _():
        normalise(hxb_ref)
        _project_groups(hxa_ref[...], w_refs, table_refs, out_refs, True)

    @pl.when(jnp.logical_and(i > 0, i % 2 == 0))
    def _():
        normalise(hxa_ref)
        _project_groups(hxb_ref[...], w_refs, table_refs, out_refs, True)


def _w_spec(layer, group):
    return pl.BlockSpec((1, D_MODEL, SUB), lambda i, j: (layer, 0, group * N_SUB + j))


def _inproj(x, norm_w, shift, scale, w_bf16, layer, tables, *, rope, tm):
    rows, d = x.shape
    cos, sa, sb = tables
    kern = functools.partial(_inproj_kernel, rope=rope)
    vec = lambda: pl.BlockSpec((1, d), lambda i, j: (0, 0))
    tab = lambda: pl.BlockSpec((tm, HEAD_DIM), lambda i, j: (i, 0))
    out = lambda: pl.BlockSpec((tm, SUB), lambda i, j: (i, j))
    seg_shape = jax.ShapeDtypeStruct((rows, SEG), BF16)
    return pl.pallas_call(
        kern,
        grid=(rows // tm, N_SUB),
        in_specs=[
            pl.BlockSpec((tm, d), lambda i, j: (i, 0)),
            vec(), vec(), vec(),
            *[_w_spec(layer, g) for g in range(N_SEG)],
            tab(), tab(), tab(),
        ],
        out_specs=[out() for _ in range(7)],
        out_shape=[seg_shape] * 7,
        scratch_shapes=[pltpu.VMEM((tm, d), BF16)],
        compiler_params=_params(56, 2),
    )(x, norm_w, shift, scale, *([w_bf16] * N_SEG), cos, sa, sb)


def _inproj_latent(x, norm_w, shift, scale, w_bf16, layer, tables, *, tm):
    rows, d = x.shape
    nt = rows // tm
    cos, sa, sb = tables
    kern = functools.partial(_inproj_latent_kernel, tm=tm)
    done = lambda i: jnp.maximum(i - 1, 0)
    vec = lambda: pl.BlockSpec((1, d), lambda i, j: (0, 0))
    tab = lambda: pl.BlockSpec((tm, HEAD_DIM), lambda i, j: (done(i), 0))
    out = lambda: pl.BlockSpec((tm, SUB), lambda i, j: (done(i), jnp.where(i == 0, 0, j)))
    seg_shape = jax.ShapeDtypeStruct((rows, SEG), BF16)
    return pl.pallas_call(
        kern,
        grid=(nt + 1, N_SUB),
        in_specs=[
            pl.BlockSpec((tm // N_SUB, d), lambda i, j: (jnp.minimum(i, nt - 1) * N_SUB + j, 0)),
            vec(), vec(), vec(),
            *[_w_spec(layer, g) for g in range(N_SEG)],
            tab(), tab(), tab(),
        ],
        out_specs=[out() for _ in range(7)],
        out_shape=[seg_shape] * 7,
        scratch_shapes=[pltpu.VMEM((tm, d), BF16), pltpu.VMEM((tm, d), BF16)],
        compiler_params=_params(56, 2),
    )(x, norm_w, shift, scale, *([w_bf16] * N_SEG), cos, sa, sb)


def _inproj_kv_kernel(x_ref, nw_ref, shift_ref, scale_ref, wk_ref, wv_ref, k_ref, v_ref, hx_ref):
    @pl.when(pl.program_id(1) == 0)
    def _():
        _prologue(x_ref, nw_ref, shift_ref, scale_ref, hx_ref)

    hx = hx_ref[...]
    k_ref[...] = (jnp.dot(hx, wk_ref[0], preferred_element_type=F32) * K_SCALE).astype(BF16)
    v_ref[...] = jnp.dot(hx, wv_ref[0], preferred_element_type=F32).astype(BF16)


def _inproj_kv(x, norm_w, shift, scale, w_bf16, layer, *, tm):
    rows, d = x.shape
    vec = lambda: pl.BlockSpec((1, d), lambda i, j: (0, 0))
    out = lambda: pl.BlockSpec((tm, SUB), lambda i, j: (i, j))
    seg_shape = jax.ShapeDtypeStruct((rows, SEG), BF16)
    return pl.pallas_call(
        _inproj_kv_kernel,
        grid=(rows // tm, N_SUB),
        in_specs=[
            pl.BlockSpec((tm, d), lambda i, j: (i, 0)),
            vec(), vec(), vec(),
            _w_spec(layer, G_K), _w_spec(layer, G_V),
        ],
        out_specs=[out(), out()],
        out_shape=[seg_shape] * 2,
        scratch_shapes=[pltpu.VMEM((tm, d), BF16)],
        compiler_params=_params(32, 2),
    )(x, norm_w, shift, scale, w_bf16, w_bf16)


def _states_kernel(lgf_ref, lgb_ref, kf_ref, vf_ref, kb_ref, vb_ref, s0f_ref, s0b_ref,
                   sf_ref, sb_ref, ff_ref, fb_ref, *, nc):
    t = pl.program_id(0)

    @pl.when(t == 0)
    def _():
        ff_ref[...] = s0f_ref[...]
        fb_ref[...] = s0b_ref[...]

    row = lax.broadcasted_iota(jnp.int32, (CHUNK, HEAD_DIM), 0).astype(F32)
    tn = (((0,), (0,)), ((), ()))
    for h in range(RET_HEADS):
        hs = slice(h * HEAD_DIM, (h + 1) * HEAD_DIM)
        lgf = lgf_ref[h]
        lgb = lgb_ref[h]
        kdf = jnp.exp(lgf * (CHUNK - 1.0 - row))
        kdb = jnp.exp(lgb * row)
        cdf = jnp.exp(lgf * CHUNK + 0.0 * row)
        cdb = jnp.exp(lgb * CHUNK + 0.0 * row)

        s = ff_ref[h]
        for c in range(nc):
            rs = slice(c * CHUNK, (c + 1) * CHUNK)
            sf_ref[c, h] = s.astype(BF16)
            kd = (kf_ref[rs, hs].astype(F32) * kdf).astype(BF16)
            s = cdf * s + lax.dot_general(kd, vf_ref[rs, hs], tn, preferred_element_type=F32)
        ff_ref[h] = s

        s = fb_ref[h]
        for c in reversed(range(nc)):
            rs = slice(c * CHUNK, (c + 1) * CHUNK)
            sb_ref[c, h] = s.astype(BF16)
            kd = (kb_ref[rs, hs].astype(F32) * kdb).astype(BF16)
            s = cdb * s + lax.dot_general(kd, vb_ref[rs, hs], tn, preferred_element_type=F32)
        fb_ref[h] = s


def _states(lg_f, lg_b, k, v, s0f, s0b, *, tr):
    rows = k.shape[0]
    nt = rows // tr
    nc = tr // CHUNK
    n_chunks = rows // CHUNK
    kern = functools.partial(_states_kernel, nc=nc)
    smem = lambda: pl.BlockSpec(memory_space=pltpu.SMEM)
    st = lambda: pl.BlockSpec((RET_HEADS, HEAD_DIM, HEAD_DIM), lambda t: (0, 0, 0))
    fwd = lambda: pl.BlockSpec((tr, SEG), lambda t: (t, 0))
    bwd = lambda: pl.BlockSpec((tr, SEG), lambda t: (nt - 1 - t, 0))
    seq_shape = jax.ShapeDtypeStruct((n_chunks, RET_HEADS, HEAD_DIM, HEAD_DIM), BF16)
    fin_shape = jax.ShapeDtypeStruct((RET_HEADS, HEAD_DIM, HEAD_DIM), F32)
    return pl.pallas_call(
        kern,
        grid=(nt,),
        in_specs=[smem(), smem(), fwd(), fwd(), bwd(), bwd(), st(), st()],
        out_specs=[
            pl.BlockSpec((nc, RET_HEADS, HEAD_DIM, HEAD_DIM), lambda t: (t, 0, 0, 0)),
            pl.BlockSpec((nc, RET_HEADS, HEAD_DIM, HEAD_DIM), lambda t: (nt - 1 - t, 0, 0, 0)),
            st(), st(),
        ],
        out_shape=[seq_shape, seq_shape, fin_shape, fin_shape],
        compiler_params=_params(32, 1),
    )(lg_f, lg_b, k, v, k, v, s0f, s0b)


def _lane_mean(t):
    hi = t.astype(BF16)
    lo = (t - hi.astype(F32)).astype(BF16)
    ones = jnp.full((2 * HEAD_DIM, HEAD_DIM), 1.0 / HEAD_DIM, BF16)
    return jnp.dot(jnp.concatenate([hi, lo], axis=1), ones, preferred_element_type=F32)


def _out_kernel(lgf_ref, lgb_ref,
                ch_ref, b_ref, zc_ref, q_ref, k_ref, v_ref, zr_ref, chp_ref, chn_ref,
                sf_ref, sb_ref, x_ref, gate_ref, convw_ref, cnw_ref, gnw_ref, wout32_ref, fnw_ref,
                o_ref,
                mask_ref, qdf_ref, qdb_ref, ya_ref, yb_ref, wout_ref, *, tm, final):
    s_id = pl.program_id(0)
    n_tiles = pl.num_programs(0) - 1
    i = jnp.maximum(s_id - 1, 0)
    last = n_tiles - 1
    nc = tm // CHUNK

    @pl.when(s_id == 0)
    def _():
        yb_ref[...] = jnp.zeros_like(yb_ref)

        def cast_rows(t, carry):
            rows = pl.ds(pl.multiple_of(t * WOUT_CAST_ROWS, WOUT_CAST_ROWS), WOUT_CAST_ROWS)
            wout_ref[rows, :] = wout32_ref[0, rows, :].astype(BF16)
            return carry

        lax.fori_loop(0, D_MODEL // WOUT_CAST_ROWS, cast_rows, 0)

        r = lax.broadcasted_iota(jnp.int32, (CHUNK, CHUNK), 0).astype(F32)
        cc = lax.broadcasted_iota(jnp.int32, (CHUNK, CHUNK), 1).astype(F32)
        d = r - cc
        for h in range(RET_HEADS):
            lgf = lgf_ref[h]
            lgb = lgb_ref[h]
            mf = jnp.where(d >= 0, jnp.exp(lgf * jnp.maximum(d, 0.0)), 0.0)
            mb = jnp.where(d <= 0, jnp.exp(lgb * jnp.maximum(-d, 0.0)), 0.0)
            mask_ref[h] = mf + mb
            qdf_ref[h] = jnp.exp(lgf * (r + 1.0))
            qdb_ref[h] = jnp.exp(lgb * (CHUNK - r))

    def step(yr_old, yr_new):
        n_q = 4
        cw_ = D_CONV // n_q
        pw = D_MODEL // n_q
        ridx = lax.broadcasted_iota(jnp.int32, (tm, 1), 0)
        cwt = convw_ref[0]

        proj_ret, y_l, ssq = [], [], None
        for n in range(n_q):
            proj_ret.append(jnp.dot(yr_old[...], wout_ref[D_CONV:, n * pw:(n + 1) * pw],
                                    preferred_element_type=F32))
            cs = slice(n * cw_, (n + 1) * cw_)
            ch = ch_ref[:, cs].astype(F32)
            prev_row = jnp.where(i > 0, chp_ref[:, cs].astype(F32)[15:16, :], 0.0)
            next_row = jnp.where(i < last, chn_ref[:, cs].astype(F32)[0:1, :], 0.0)
            prev = jnp.where(ridx == 0, prev_row, pltpu.roll(ch, 1, 0))
            nxt = jnp.where(ridx == tm - 1, next_row, pltpu.roll(ch, tm - 1, 0))
            conv = prev * cwt[0:1, cs] + ch * cwt[1:2, cs] + nxt * cwt[2:3, cs]
            y = b_ref[:, cs].astype(F32) * conv
            y_l.append(y)
            part = jnp.sum(y * y, axis=-1, keepdims=True)
            ssq = part if ssq is None else ssq + part
        rinv = lax.rsqrt(ssq * (1.0 / D_CONV) + EPS)

        nt = (((1,), (1,)), ((), ()))
        pairs = [(c, h) for c in range(nc) for h in range(RET_HEADS)]
        rs = lambda c: slice(c * CHUNK, (c + 1) * CHUNK)
        hs = lambda h: slice(h * HEAD_DIM, (h + 1) * HEAD_DIM)
        q_l = [q_ref[rs(c), hs(h)] for c, h in pairs]
        s_l = [lax.dot_general(q, k_ref[rs(c), hs(h)], nt, preferred_element_type=F32)
               for q, (c, h) in zip(q_l, pairs)]

        y_conv = jnp.concatenate(
            [(zc_ref[:, n * cw_:(n + 1) * cw_].astype(F32)
              * (y_l[n] * rinv * cnw_ref[:, n * cw_:(n + 1) * cw_])).astype(BF16)
             for n in range(n_q)], axis=1)

        def finish_chunk(n):
            cols = slice(n * pw, (n + 1) * pw)
            pc = jnp.dot(y_conv, wout_ref[:D_CONV, cols], preferred_element_type=F32)
            o_ref[:, cols] = x_ref[:, cols] + gate_ref[:, cols] * (pc + proj_ret[n])

        finish_chunk(0)
        finish_chunk(1)
        lhs_l = []
        for q, s, (c, h) in zip(q_l, s_l, pairs):
            qf32 = q.astype(F32)
            lhs_l.append(jnp.concatenate(
                [(s * mask_ref[h]).astype(BF16), (qf32 * qdf_ref[h]).astype(BF16),
                 (qf32 * qdb_ref[h]).astype(BF16)], axis=1))
        o_l = [jnp.dot(lhs, jnp.concatenate([v_ref[rs(c), hs(h)], sf_ref[c, h], sb_ref[c, h]],
                                            axis=0), preferred_element_type=F32)
               for lhs, (c, h) in zip(lhs_l, pairs)]
        finish_chunk(2)
        oc_l = [o - _lane_mean(o) for o in o_l]
        finish_chunk(3)
        var_l = [_lane_mean(oc * oc) for oc in oc_l]
        if final:
            xn = o_ref[...]
            ms = jnp.mean(xn * xn, axis=-1, keepdims=True)
            o_ref[...] = xn * lax.rsqrt(ms + EPS) * fnw_ref[...]
        for oc, var, (c, h) in zip(oc_l, var_l, pairs):
            on = oc * lax.rsqrt(var + EPS) * gnw_ref[:, hs(h)]
            yr_new[rs(c), hs(h)] = (zr_ref[rs(c), hs(h)].astype(F32) * on).astype(BF16)

    @pl.when(s_id % 2 == 0)
    def _():
        step(yb_ref, ya_ref)

    @pl.when(s_id % 2 == 1)
    def _():
        step(ya_ref, yb_ref)


def _out(lg_f, lg_b, segs, sf, sb, x, gate, conv_w, cnw, gnw, w_out, fnw, layer, *, tm, final):
    ch, b, zc, q, k, v, zr = segs
    rows, d = x.shape
    nt = rows // tm
    nc = tm // CHUNK
    hb = tm // 16
    n_hb = rows // 16
    kern = functools.partial(_out_kernel, tm=tm, final=final)
    smem = lambda: pl.BlockSpec(memory_space=pltpu.SMEM)
    cur = lambda s: jnp.minimum(s, nt - 1)
    prv = lambda s: jnp.maximum(s - 1, 0)
    seg = lambda: pl.BlockSpec((tm, SEG), lambda s: (cur(s), 0))
    fin = lambda: pl.BlockSpec((tm, SEG), lambda s: (prv(s), 0))
    st = lambda: pl.BlockSpec((nc, RET_HEADS, HEAD_DIM, HEAD_DIM), lambda s: (cur(s), 0, 0, 0))
    full = lambda shape: pl.BlockSpec(shape, lambda s: (0,) * len(shape))
    return pl.pallas_call(
        kern,
        grid=(nt + 1,),
        in_specs=[
            smem(), smem(),
            fin(), fin(), fin(), seg(), seg(), seg(), seg(),
            pl.BlockSpec((16, SEG), lambda s: (jnp.maximum(prv(s) * hb - 1, 0), 0)),
            pl.BlockSpec((16, SEG), lambda s: (jnp.minimum((prv(s) + 1) * hb, n_hb - 1), 0)),
            st(), st(),
            pl.BlockSpec((tm, d), lambda s: (prv(s), 0)),
            full((1, d)),
            pl.BlockSpec((1, 3, D_CONV), lambda s: (layer, 0, 0)),
            full((1, D_CONV)), full((1, D_RET)),
            pl.BlockSpec((1, d, d), lambda s: (layer, 0, 0), pipeline_mode=pl.Buffered(1)),
            full((1, d)),
        ],
        out_specs=pl.BlockSpec((tm, d), lambda s: (prv(s), 0)),
        out_shape=jax.ShapeDtypeStruct((rows, d), F32),
        scratch_shapes=[
            pltpu.VMEM((RET_HEADS, CHUNK, CHUNK), F32),
            pltpu.VMEM((RET_HEADS, CHUNK, HEAD_DIM), F32),
            pltpu.VMEM((RET_HEADS, CHUNK, HEAD_DIM), F32),
            pltpu.VMEM((tm, D_RET), BF16),
            pltpu.VMEM((tm, D_RET), BF16),
            pltpu.VMEM((d, d), BF16),
        ],
        compiler_params=_params(56, 1),
    )(lg_f, lg_b, ch, b, zc, q, k, v, zr, ch, ch, sf, sb, x, gate, conv_w, cnw, gnw,
      w_out, fnw)


def _rope_tables(seq):
    t = np.arange(seq)
    row = (t // GRID_W).astype(np.float64)
    col = (t % GRID_W).astype(np.float64)
    inv = ROPE_BASE ** (-np.arange(ROPE_F, dtype=np.float64) / ROPE_F)
    ar = row[:, None] * inv[None, :]
    ac = col[:, None] * inv[None, :]
    z = np.zeros_like(ar)
    cos = np.concatenate([np.cos(ar), np.cos(ar), np.cos(ac), np.cos(ac)], axis=-1)
    sa = np.concatenate([-np.sin(ar), z, -np.sin(ac), z], axis=-1)
    sb = np.concatenate([z, np.sin(ar), z, np.sin(ac)], axis=-1)
    return tuple(jnp.asarray(a, dtype=F32) for a in (cos, sa, sb))


def kernel(x, c, ctx, c_ctx, norm_w, w_mod, b_mod, w_in, conv_w, conv_norm_w, ret_norm_w,
           ret_decay_f, ret_decay_b, w_out, final_norm_w):
    batch, seq, d = x.shape
    assert batch == 1 and d == D_MODEL and seq % 1024 == 0
    depth = norm_w.shape[0]
    ctx_len = ctx.shape[1]
    xs = x[0]
    cs = ctx[0]

    cv = jnp.zeros((8, d), F32).at[0].set(c[0]).at[1].set(c_ctx)
    mod = _modulation(cv, w_mod, b_mod)
    tables = _rope_tables(seq)
    ctx_tables = tuple(jnp.zeros((ctx_len, HEAD_DIM), F32) for _ in range(3))
    zero_state = jnp.zeros((RET_HEADS, HEAD_DIM, HEAD_DIM), F32)
    fnw = final_norm_w.reshape(1, d)
    w_in_p = w_in.astype(BF16)

    for layer in range(depth):
        update_ctx = layer < depth - 1
        lg_f = -jnp.exp(ret_decay_f[layer].astype(F32))
        lg_b = -jnp.exp(ret_decay_b[layer].astype(F32))
        nw = norm_w[layer].reshape(1, d)
        m = mod[layer]
        shift, scale, gate = m[0:1, 0:d], m[0:1, d:2 * d], m[0:1, 2 * d:3 * d]
        shift_c, scale_c, gate_c = m[1:2, 0:d], m[1:2, d:2 * d], m[1:2, 2 * d:3 * d]
        cnw = conv_norm_w[layer].reshape(1, D_CONV)
        gnw = ret_norm_w[layer].reshape(1, D_RET)

        if update_ctx:
            segs_c = _inproj(cs, nw, shift_c, scale_c, w_in_p, layer, ctx_tables,
                             rope=False, tm=ctx_len)
            k_c, v_c = segs_c[4], segs_c[5]
        else:
            k_c, v_c = _inproj_kv(cs, nw, shift_c, scale_c, w_in_p, layer, tm=ctx_len)
        sf_c, sb_c, s_f, s_b = _states(lg_f, lg_b, k_c, v_c, zero_state, zero_state, tr=ctx_len)

        segs = _inproj_latent(xs, nw, shift, scale, w_in_p, layer, tables, tm=1024)
        sf, sb, _, _ = _states(lg_f, lg_b, segs[4], segs[5], s_f, s_b, tr=1024)
        xs = _out(lg_f, lg_b, segs, sf, sb, xs, gate, conv_w, cnw, gnw, w_out, fnw, layer,
                  tm=256, final=not update_ctx)
        if update_ctx:
            cs = _out(lg_f, lg_b, segs_c, sf_c, sb_c, cs, gate_c, conv_w, cnw, gnw, w_out, fnw,
                      layer, tm=ctx_len, final=False)
    return xs[None]
```

```python
import functools

import numpy as np
import jax
import jax.numpy as jnp
from jax import lax
from jax.experimental import pallas as pl
from jax.experimental.pallas import tpu as pltpu

D_MODEL = 2048
D_CONV = 1024
D_RET = 1024
RET_HEADS = 8
HEAD_DIM = 128
CHUNK = 128
SEG = 1024
N_SEG = 8
SUB = 256
N_SUB = SEG // SUB
GRID_W = 64
ROPE_BASE = 10000.0
ROPE_F = 32
EPS = 1e-6
K_SCALE = HEAD_DIM ** -0.5

F32 = jnp.float32
BF16 = jnp.bfloat16

G_H, G_B, G_C, G_ZC, G_Q, G_K, G_V, G_ZR = range(8)


def _silu(x):
    return x / (1.0 + jnp.exp(-x))


def _params(vmem_mb, n_axes, flags=None):
    return pltpu.CompilerParams(
        dimension_semantics=("arbitrary",) * n_axes,
        vmem_limit_bytes=vmem_mb * 1024 * 1024,
        flags=flags,
    )


def _mod_kernel(cv_ref, w_ref, b_ref, o_ref):
    s = _silu(cv_ref[...])
    o_ref[0] = jnp.dot(s.astype(BF16), w_ref[0].astype(BF16),
                       preferred_element_type=F32) + b_ref[0]


def _modulation(cv, w_mod, b_mod, tn=1024):
    depth, d, n = w_mod.shape
    return pl.pallas_call(
        _mod_kernel,
        grid=(depth, n // tn),
        in_specs=[
            pl.BlockSpec((8, d), lambda l, j: (0, 0)),
            pl.BlockSpec((1, d, tn), lambda l, j: (l, 0, j)),
            pl.BlockSpec((1, 1, tn), lambda l, j: (l, 0, j)),
        ],
        out_specs=pl.BlockSpec((1, 8, tn), lambda l, j: (l, 0, j)),
        out_shape=jax.ShapeDtypeStruct((depth, 8, n), F32),
        compiler_params=_params(40, 2),
    )(cv, w_mod, b_mod.reshape(depth, 1, n))


PROLOGUE_ROWS = 64
WOUT_CAST_ROWS = 128
W_CAST_BLOCKS = 32


def _prologue(x_ref, nw_ref, shift_ref, scale_ref, hx_ref):
    gain = nw_ref[...] * (1.0 + scale_ref[...])
    shift = shift_ref[...]

    def body(r, carry):
        rows = pl.ds(pl.multiple_of(r * PROLOGUE_ROWS, PROLOGUE_ROWS), PROLOGUE_ROWS)
        x = x_ref[rows, :]
        ms = jnp.mean(x * x, axis=-1, keepdims=True)
        hx_ref[rows, :] = (x * lax.rsqrt(ms + EPS) * gain + shift).astype(BF16)
        return carry

    lax.fori_loop(0, x_ref.shape[0] // PROLOGUE_ROWS, body, 0)


def _rope_pair(acc, cos, sa, sb):
    outs = []
    for h in range(SUB // HEAD_DIM):
        a = acc[:, h * HEAD_DIM:(h + 1) * HEAD_DIM]
        outs.append(a * cos + pltpu.roll(a, HEAD_DIM - ROPE_F, 1) * sa
                    + pltpu.roll(a, ROPE_F, 1) * sb)
    return jnp.concatenate(outs, axis=1)


def _project_groups(hx, w_refs, table_refs, out_refs, rope):
    wh_ref, wb_ref, wc_ref, wzc_ref, wq_ref, wk_ref, wv_ref, wzr_ref = w_refs
    ch_ref, b_ref, zc_ref, q_ref, k_ref, v_ref, zr_ref = out_refs

    def seg(w_ref):
        return jnp.dot(hx, w_ref[...], preferred_element_type=F32)

    ch_ref[...] = (seg(wc_ref) * seg(wh_ref)).astype(BF16)
    b_ref[...] = seg(wb_ref).astype(BF16)
    zc_ref[...] = _silu(seg(wzc_ref)).astype(BF16)
    if rope:
        cos, sa, sb = (t[...] for t in table_refs)
        q_ref[...] = _rope_pair(seg(wq_ref), cos, sa, sb).astype(BF16)
        k_ref[...] = _rope_pair(seg(wk_ref), cos * K_SCALE, sa * K_SCALE, sb * K_SCALE).astype(BF16)
    else:
        q_ref[...] = seg(wq_ref).astype(BF16)
        k_ref[...] = (seg(wk_ref) * K_SCALE).astype(BF16)
    v_ref[...] = seg(wv_ref).astype(BF16)
    zr_ref[...] = _silu(seg(wzr_ref)).astype(BF16)


def _inproj_kernel(x_ref, nw_ref, shift_ref, scale_ref, *refs, rope):
    w_refs, table_refs, out_refs, hx_ref = refs[:8], refs[8:11], refs[11:18], refs[18]

    @pl.when(pl.program_id(1) == 0)
    def _():
        _prologue(x_ref, nw_ref, shift_ref, scale_ref, hx_ref)

    _project_groups(hx_ref[...], w_refs, table_refs, out_refs, rope)


def _inproj_latent_kernel(xq_ref, nw_ref, shift_ref, scale_ref, *refs, tm, cast_next):
    w_refs, table_refs = refs[:8], refs[8:11]
    refs = refs[11:]
    if cast_next:
        wnext32_ref, refs = refs[0], refs[1:]
    out_refs, refs = refs[:7], refs[7:]
    if cast_next:
        wnext_ref, refs = refs[0], refs[1:]
    hxa_ref, hxb_ref = refs
    i = pl.program_id(0)
    j = pl.program_id(1)
    quarter = tm // N_SUB
    row0 = pl.multiple_of(j * quarter, quarter)

    def normalise(hx_ref):
        if cast_next:
            wnext_ref[...] = wnext32_ref[...].astype(BF16)
        gain = nw_ref[...] * (1.0 + scale_ref[...])
        shift = shift_ref[...]
        for r in range(0, quarter, PROLOGUE_ROWS):
            x = xq_ref[r:r + PROLOGUE_ROWS, :]
            ms = jnp.mean(x * x, axis=-1, keepdims=True)
            hx_ref[pl.ds(row0 + r, PROLOGUE_ROWS), :] = (
                x * lax.rsqrt(ms + EPS) * gain + shift).astype(BF16)

    @pl.when(i == 0)
    def _():
        normalise(hxa_ref)

    @pl.when(i % 2 == 1)
    def _():
        normalise(hxb_ref)
        _project_groups(hxa_ref[...], w_refs, table_refs, out_refs, True)

    @pl.when(jnp.logical_and(i > 0, i % 2 == 0))
    def _():
        normalise(hxa_ref)
        _project_groups(hxb_ref[...], w_refs, table_refs, out_refs, True)


def _w_spec(group):
    return pl.BlockSpec((D_MODEL, SUB), lambda i, j: (0, group * N_SUB + j))


def _inproj(x, norm_w, shift, scale, w_bf16, tables, *, rope, tm):
    rows, d = x.shape
    cos, sa, sb = tables
    kern = functools.partial(_inproj_kernel, rope=rope)
    vec = lambda: pl.BlockSpec((1, d), lambda i, j: (0, 0))
    tab = lambda: pl.BlockSpec((tm, HEAD_DIM), lambda i, j: (i, 0))
    out = lambda: pl.BlockSpec((tm, SUB), lambda i, j: (i, j))
    seg_shape = jax.ShapeDtypeStruct((rows, SEG), BF16)
    return pl.pallas_call(
        kern,
        grid=(rows // tm, N_SUB),
        in_specs=[
            pl.BlockSpec((tm, d), lambda i, j: (i, 0)),
            vec(), vec(), vec(),
            *[_w_spec(g) for g in range(N_SEG)],
            tab(), tab(), tab(),
        ],
        out_specs=[out() for _ in range(7)],
        out_shape=[seg_shape] * 7,
        scratch_shapes=[pltpu.VMEM((tm, d), BF16)],
        compiler_params=_params(56, 2),
    )(x, norm_w, shift, scale, *([w_bf16] * N_SEG), cos, sa, sb)


def _inproj_latent(x, norm_w, shift, scale, w_bf16, tables, w_next_f32, *, tm):
    rows, d = x.shape
    nt = rows // tm
    cos, sa, sb = tables
    cast_next = w_next_f32 is not None
    kern = functools.partial(_inproj_latent_kernel, tm=tm, cast_next=cast_next)
    done = lambda i: jnp.maximum(i - 1, 0)
    vec = lambda: pl.BlockSpec((1, d), lambda i, j: (0, 0))
    tab = lambda: pl.BlockSpec((tm, HEAD_DIM), lambda i, j: (done(i), 0))
    out = lambda: pl.BlockSpec((tm, SUB), lambda i, j: (done(i), jnp.where(i == 0, 0, j)))
    seg_shape = jax.ShapeDtypeStruct((rows, SEG), BF16)
    in_specs = [
        pl.BlockSpec((tm // N_SUB, d), lambda i, j: (jnp.minimum(i, nt - 1) * N_SUB + j, 0)),
        vec(), vec(), vec(),
        *[_w_spec(g) for g in range(N_SEG)],
        tab(), tab(), tab(),
    ]
    operands = [x, norm_w, shift, scale, *([w_bf16] * N_SEG), cos, sa, sb]
    out_specs = [out() for _ in range(7)]
    out_shape = [seg_shape] * 7
    if cast_next:
        n_blocks = W_CAST_BLOCKS
        assert n_blocks <= (nt + 1) * N_SUB and d % n_blocks == 0
        w_all, next_layer = w_next_f32
        blk = lambda i, j: jnp.minimum(i * N_SUB + j, n_blocks - 1)
        in_specs.append(pl.BlockSpec((None, d // n_blocks, w_all.shape[2]),
                                     lambda i, j: (next_layer, blk(i, j), 0)))
        operands.append(w_all)
        out_specs.append(pl.BlockSpec((d // n_blocks, w_all.shape[2]),
                                      lambda i, j: (blk(i, j), 0)))
        out_shape.append(jax.ShapeDtypeStruct(w_all.shape[1:], BF16))
    res = pl.pallas_call(
        kern,
        grid=(nt + 1, N_SUB),
        in_specs=in_specs,
        out_specs=out_specs,
        out_shape=out_shape,
        scratch_shapes=[pltpu.VMEM((tm, d), BF16), pltpu.VMEM((tm, d), BF16)],
        compiler_params=_params(56, 2),
    )(*operands)
    return (res[:7], res[7]) if cast_next else (res, None)


def _inproj_kv_kernel(x_ref, nw_ref, shift_ref, scale_ref, wk_ref, wv_ref, k_ref, v_ref, hx_ref):
    @pl.when(pl.program_id(1) == 0)
    def _():
        _prologue(x_ref, nw_ref, shift_ref, scale_ref, hx_ref)

    hx = hx_ref[...]
    k_ref[...] = (jnp.dot(hx, wk_ref[...], preferred_element_type=F32) * K_SCALE).astype(BF16)
    v_ref[...] = jnp.dot(hx, wv_ref[...], preferred_element_type=F32).astype(BF16)


def _inproj_kv(x, norm_w, shift, scale, w_bf16, *, tm):
    rows, d = x.shape
    vec = lambda: pl.BlockSpec((1, d), lambda i, j: (0, 0))
    out = lambda: pl.BlockSpec((tm, SUB), lambda i, j: (i, j))
    seg_shape = jax.ShapeDtypeStruct((rows, SEG), BF16)
    return pl.pallas_call(
        _inproj_kv_kernel,
        grid=(rows // tm, N_SUB),
        in_specs=[
            pl.BlockSpec((tm, d), lambda i, j: (i, 0)),
            vec(), vec(), vec(),
            _w_spec(G_K), _w_spec(G_V),
        ],
        out_specs=[out(), out()],
        out_shape=[seg_shape] * 2,
        scratch_shapes=[pltpu.VMEM((tm, d), BF16)],
        compiler_params=_params(32, 2),
    )(x, norm_w, shift, scale, w_bf16, w_bf16)


def _states_kernel(lgf_ref, lgb_ref, kf_ref, vf_ref, kb_ref, vb_ref, s0f_ref, s0b_ref,
                   sf_ref, sb_ref, ff_ref, fb_ref, *, nc):
    t = pl.program_id(0)

    @pl.when(t == 0)
    def _():
        ff_ref[...] = s0f_ref[...]
        fb_ref[...] = s0b_ref[...]

    row = lax.broadcasted_iota(jnp.int32, (CHUNK, HEAD_DIM), 0).astype(F32)
    tn = (((0,), (0,)), ((), ()))
    for h in range(RET_HEADS):
        hs = slice(h * HEAD_DIM, (h + 1) * HEAD_DIM)
        lgf = lgf_ref[h]
        lgb = lgb_ref[h]
        kdf = jnp.exp(lgf * (CHUNK - 1.0 - row))
        kdb = jnp.exp(lgb * row)
        cdf = jnp.exp(lgf * CHUNK + 0.0 * row)
        cdb = jnp.exp(lgb * CHUNK + 0.0 * row)

        s = ff_ref[h]
        for c in range(nc):
            rs = slice(c * CHUNK, (c + 1) * CHUNK)
            sf_ref[c, h] = s.astype(BF16)
            kd = (kf_ref[rs, hs].astype(F32) * kdf).astype(BF16)
            s = cdf * s + lax.dot_general(kd, vf_ref[rs, hs], tn, preferred_element_type=F32)
        ff_ref[h] = s

        s = fb_ref[h]
        for c in reversed(range(nc)):
            rs = slice(c * CHUNK, (c + 1) * CHUNK)
            sb_ref[c, h] = s.astype(BF16)
            kd = (kb_ref[rs, hs].astype(F32) * kdb).astype(BF16)
            s = cdb * s + lax.dot_general(kd, vb_ref[rs, hs], tn, preferred_element_type=F32)
        fb_ref[h] = s


def _states(lg_f, lg_b, k, v, s0f, s0b, *, tr):
    rows = k.shape[0]
    nt = rows // tr
    nc = tr // CHUNK
    n_chunks = rows // CHUNK
    kern = functools.partial(_states_kernel, nc=nc)
    smem = lambda: pl.BlockSpec(memory_space=pltpu.SMEM)
    st = lambda: pl.BlockSpec((RET_HEADS, HEAD_DIM, HEAD_DIM), lambda t: (0, 0, 0))
    fwd = lambda: pl.BlockSpec((tr, SEG), lambda t: (t, 0))
    bwd = lambda: pl.BlockSpec((tr, SEG), lambda t: (nt - 1 - t, 0))
    seq_shape = jax.ShapeDtypeStruct((n_chunks, RET_HEADS, HEAD_DIM, HEAD_DIM), BF16)
    fin_shape = jax.ShapeDtypeStruct((RET_HEADS, HEAD_DIM, HEAD_DIM), F32)
    return pl.pallas_call(
        kern,
        grid=(nt,),
        in_specs=[smem(), smem(), fwd(), fwd(), bwd(), bwd(), st(), st()],
        out_specs=[
            pl.BlockSpec((nc, RET_HEADS, HEAD_DIM, HEAD_DIM), lambda t: (t, 0, 0, 0)),
            pl.BlockSpec((nc, RET_HEADS, HEAD_DIM, HEAD_DIM), lambda t: (nt - 1 - t, 0, 0, 0)),
            st(), st(),
        ],
        out_shape=[seq_shape, seq_shape, fin_shape, fin_shape],
        compiler_params=_params(32, 1),
    )(lg_f, lg_b, k, v, k, v, s0f, s0b)


def _lane_mean(t):
    hi = t.astype(BF16)
    lo = (t - hi.astype(F32)).astype(BF16)
    ones = jnp.full((2 * HEAD_DIM, HEAD_DIM), 1.0 / HEAD_DIM, BF16)
    return jnp.dot(jnp.concatenate([hi, lo], axis=1), ones, preferred_element_type=F32)


def _out_kernel(lgf_ref, lgb_ref,
                ch_ref, b_ref, zc_ref, q_ref, k_ref, v_ref, zr_ref, chp_ref, chn_ref,
                sf_ref, sb_ref, x_ref, gate_ref, convw_ref, cnw_ref, gnw_ref, wout32_ref, fnw_ref,
                o_ref,
                mask_ref, qdf_ref, qdb_ref, ya_ref, yb_ref, wout_ref, *, tm, final):
    s_id = pl.program_id(0)
    n_tiles = pl.num_programs(0) - 1
    i = jnp.maximum(s_id - 1, 0)
    last = n_tiles - 1
    nc = tm // CHUNK

    @pl.when(s_id == 0)
    def _():
        yb_ref[...] = jnp.zeros_like(yb_ref)

        def cast_rows(t, carry):
            rows = pl.ds(pl.multiple_of(t * WOUT_CAST_ROWS, WOUT_CAST_ROWS), WOUT_CAST_ROWS)
            wout_ref[rows, :] = wout32_ref[0, rows, :].astype(BF16)
            return carry

        lax.fori_loop(0, D_MODEL // WOUT_CAST_ROWS, cast_rows, 0)

        r = lax.broadcasted_iota(jnp.int32, (CHUNK, CHUNK), 0).astype(F32)
        cc = lax.broadcasted_iota(jnp.int32, (CHUNK, CHUNK), 1).astype(F32)
        d = r - cc
        for h in range(RET_HEADS):
            lgf = lgf_ref[h]
            lgb = lgb_ref[h]
            mf = jnp.where(d >= 0, jnp.exp(lgf * jnp.maximum(d, 0.0)), 0.0)
            mb = jnp.where(d <= 0, jnp.exp(lgb * jnp.maximum(-d, 0.0)), 0.0)
            mask_ref[h] = mf + mb
            qdf_ref[h] = jnp.exp(lgf * (r + 1.0))
            qdb_ref[h] = jnp.exp(lgb * (CHUNK - r))

    def step(yr_old, yr_new):
        n_q = 4
        cw_ = D_CONV // n_q
        pw = D_MODEL // n_q
        ridx = lax.broadcasted_iota(jnp.int32, (tm, 1), 0)
        cwt = convw_ref[0]

        proj_ret, y_l, ssq = [], [], None
        for n in range(n_q):
            proj_ret.append(jnp.dot(yr_old[...], wout_ref[D_CONV:, n * pw:(n + 1) * pw],
                                    preferred_element_type=F32))
            cs = slice(n * cw_, (n + 1) * cw_)
            ch = ch_ref[:, cs].astype(F32)
            prev_row = jnp.where(i > 0, chp_ref[:, cs].astype(F32)[15:16, :], 0.0)
            next_row = jnp.where(i < last, chn_ref[:, cs].astype(F32)[0:1, :], 0.0)
            prev = jnp.where(ridx == 0, prev_row, pltpu.roll(ch, 1, 0))
            nxt = jnp.where(ridx == tm - 1, next_row, pltpu.roll(ch, tm - 1, 0))
            conv = prev * cwt[0:1, cs] + ch * cwt[1:2, cs] + nxt * cwt[2:3, cs]
            y = b_ref[:, cs].astype(F32) * conv
            y_l.append(y)
            part = jnp.sum(y * y, axis=-1, keepdims=True)
            ssq = part if ssq is None else ssq + part
        rinv = lax.rsqrt(ssq * (1.0 / D_CONV) + EPS)

        nt = (((1,), (1,)), ((), ()))
        pairs = [(c, h) for c in range(nc) for h in range(RET_HEADS)]
        rs = lambda c: slice(c * CHUNK, (c + 1) * CHUNK)
        hs = lambda h: slice(h * HEAD_DIM, (h + 1) * HEAD_DIM)
        q_l = [q_ref[rs(c), hs(h)] for c, h in pairs]
        s_l = [lax.dot_general(q, k_ref[rs(c), hs(h)], nt, preferred_element_type=F32)
               for q, (c, h) in zip(q_l, pairs)]

        y_conv = jnp.concatenate(
            [(zc_ref[:, n * cw_:(n + 1) * cw_].astype(F32)
              * (y_l[n] * rinv * cnw_ref[:, n * cw_:(n + 1) * cw_])).astype(BF16)
             for n in range(n_q)], axis=1)

        def finish_chunk(n):
            cols = slice(n * pw, (n + 1) * pw)
            pc = jnp.dot(y_conv, wout_ref[:D_CONV, cols], preferred_element_type=F32)
            o_ref[:, cols] = x_ref[:, cols] + gate_ref[:, cols] * (pc + proj_ret[n])

        finish_chunk(0)
        finish_chunk(1)
        lhs_l = []
        for q, s, (c, h) in zip(q_l, s_l, pairs):
            qf32 = q.astype(F32)
            lhs_l.append(jnp.concatenate(
                [(s * mask_ref[h]).astype(BF16), (qf32 * qdf_ref[h]).astype(BF16),
                 (qf32 * qdb_ref[h]).astype(BF16)], axis=1))
        o_l = [jnp.dot(lhs, jnp.concatenate([v_ref[rs(c), hs(h)], sf_ref[c, h], sb_ref[c, h]],
                                            axis=0), preferred_element_type=F32)
               for lhs, (c, h) in zip(lhs_l, pairs)]
        finish_chunk(2)
        oc_l = [o - _lane_mean(o) for o in o_l]
        finish_chunk(3)
        var_l = [_lane_mean(oc * oc) for oc in oc_l]
        if final:
            xn = o_ref[...]
            ms = jnp.mean(xn * xn, axis=-1, keepdims=True)
            o_ref[...] = xn * lax.rsqrt(ms + EPS) * fnw_ref[...]
        for oc, var, (c, h) in zip(oc_l, var_l, pairs):
            on = oc * lax.rsqrt(var + EPS) * gnw_ref[:, hs(h)]
            yr_new[rs(c), hs(h)] = (zr_ref[rs(c), hs(h)].astype(F32) * on).astype(BF16)

    @pl.when(s_id % 2 == 0)
    def _():
        step(yb_ref, ya_ref)

    @pl.when(s_id % 2 == 1)
    def _():
        step(ya_ref, yb_ref)


def _out(lg_f, lg_b, segs, sf, sb, x, gate, conv_w, cnw, gnw, w_out, fnw, layer, *, tm, final):
    ch, b, zc, q, k, v, zr = segs
    rows, d = x.shape
    nt = rows // tm
    nc = tm // CHUNK
    hb = tm // 16
    n_hb = rows // 16
    kern = functools.partial(_out_kernel, tm=tm, final=final)
    smem = lambda: pl.BlockSpec(memory_space=pltpu.SMEM)
    cur = lambda s: jnp.minimum(s, nt - 1)
    prv = lambda s: jnp.maximum(s - 1, 0)
    seg = lambda: pl.BlockSpec((tm, SEG), lambda s: (cur(s), 0))
    fin = lambda: pl.BlockSpec((tm, SEG), lambda s: (prv(s), 0))
    st = lambda: pl.BlockSpec((nc, RET_HEADS, HEAD_DIM, HEAD_DIM), lambda s: (cur(s), 0, 0, 0))
    full = lambda shape: pl.BlockSpec(shape, lambda s: (0,) * len(shape))
    return pl.pallas_call(
        kern,
        grid=(nt + 1,),
        in_specs=[
            smem(), smem(),
            fin(), fin(), fin(), seg(), seg(), seg(), seg(),
            pl.BlockSpec((16, SEG), lambda s: (jnp.maximum(prv(s) * hb - 1, 0), 0)),
            pl.BlockSpec((16, SEG), lambda s: (jnp.minimum((prv(s) + 1) * hb, n_hb - 1), 0)),
            st(), st(),
            pl.BlockSpec((tm, d), lambda s: (prv(s), 0)),
            full((1, d)),
            pl.BlockSpec((1, 3, D_CONV), lambda s: (layer, 0, 0)),
            full((1, D_CONV)), full((1, D_RET)),
            pl.BlockSpec((1, d, d), lambda s: (layer, 0, 0), pipeline_mode=pl.Buffered(1)),
            full((1, d)),
        ],
        out_specs=pl.BlockSpec((tm, d), lambda s: (prv(s), 0)),
        out_shape=jax.ShapeDtypeStruct((rows, d), F32),
        scratch_shapes=[
            pltpu.VMEM((RET_HEADS, CHUNK, CHUNK), F32),
            pltpu.VMEM((RET_HEADS, CHUNK, HEAD_DIM), F32),
            pltpu.VMEM((RET_HEADS, CHUNK, HEAD_DIM), F32),
            pltpu.VMEM((tm, D_RET), BF16),
            pltpu.VMEM((tm, D_RET), BF16),
            pltpu.VMEM((d, d), BF16),
        ],
        compiler_params=_params(56, 1),
    )(lg_f, lg_b, ch, b, zc, q, k, v, zr, ch, ch, sf, sb, x, gate, conv_w, cnw, gnw,
      w_out, fnw)


def _rope_tables(seq):
    t = np.arange(seq)
    row = (t // GRID_W).astype(np.float64)
    col = (t % GRID_W).astype(np.float64)
    inv = ROPE_BASE ** (-np.arange(ROPE_F, dtype=np.float64) / ROPE_F)
    ar = row[:, None] * inv[None, :]
    ac = col[:, None] * inv[None, :]
    z = np.zeros_like(ar)
    cos = np.concatenate([np.cos(ar), np.cos(ar), np.cos(ac), np.cos(ac)], axis=-1)
    sa = np.concatenate([-np.sin(ar), z, -np.sin(ac), z], axis=-1)
    sb = np.concatenate([z, np.sin(ar), z, np.sin(ac)], axis=-1)
    return tuple(jnp.asarray(a, dtype=F32) for a in (cos, sa, sb))


def kernel(x, c, ctx, c_ctx, norm_w, w_mod, b_mod, w_in, conv_w, conv_norm_w, ret_norm_w,
           ret_decay_f, ret_decay_b, w_out, final_norm_w):
    batch, seq, d = x.shape
    assert batch == 1 and d == D_MODEL and seq % 1024 == 0
    depth = norm_w.shape[0]
    ctx_len = ctx.shape[1]
    xs = x[0]
    cs = ctx[0]

    cv = jnp.zeros((8, d), F32).at[0].set(c[0]).at[1].set(c_ctx)
    mod = _modulation(cv, w_mod, b_mod)
    tables = _rope_tables(seq)
    ctx_tables = tuple(jnp.zeros((ctx_len, HEAD_DIM), F32) for _ in range(3))
    zero_state = jnp.zeros((RET_HEADS, HEAD_DIM, HEAD_DIM), F32)
    fnw = final_norm_w.reshape(1, d)
    w_in_b = w_in[0].astype(BF16)

    for layer in range(depth):
        update_ctx = layer < depth - 1
        lg_f = -jnp.exp(ret_decay_f[layer].astype(F32))
        lg_b = -jnp.exp(ret_decay_b[layer].astype(F32))
        nw = norm_w[layer].reshape(1, d)
        m = mod[layer]
        shift, scale, gate = m[0:1, 0:d], m[0:1, d:2 * d], m[0:1, 2 * d:3 * d]
        shift_c, scale_c, gate_c = m[1:2, 0:d], m[1:2, d:2 * d], m[1:2, 2 * d:3 * d]
        cnw = conv_norm_w[layer].reshape(1, D_CONV)
        gnw = ret_norm_w[layer].reshape(1, D_RET)

        if update_ctx:
            segs_c = _inproj(cs, nw, shift_c, scale_c, w_in_b, ctx_tables,
                             rope=False, tm=ctx_len)
            k_c, v_c = segs_c[4], segs_c[5]
        else:
            k_c, v_c = _inproj_kv(cs, nw, shift_c, scale_c, w_in_b, tm=ctx_len)
        sf_c, sb_c, s_f, s_b = _states(lg_f, lg_b, k_c, v_c, zero_state, zero_state, tr=ctx_len)

        w_next = (w_in, layer + 1) if layer + 1 < depth else None
        segs, w_in_b = _inproj_latent(xs, nw, shift, scale, w_in_b, tables, w_next, tm=1024)
        sf, sb, _, _ = _states(lg_f, lg_b, segs[4], segs[5], s_f, s_b, tr=1024)
        xs = _out(lg_f, lg_b, segs, sf, sb, xs, gate, conv_w, cnw, gnw, w_out, fnw, layer,
                  tm=256, final=not update_ctx)
        if update_ctx:
            cs = _out(lg_f, lg_b, segs_c, sf_c, sb_c, cs, gate_c, conv_w, cnw, gnw, w_out, fnw,
                      layer, tm=ctx_len, final=False)
    return xs[None]
```

```python
import functools

import numpy as np
import jax
import jax.numpy as jnp
from jax import lax
from jax.experimental import pallas as pl
from jax.experimental.pallas import tpu as pltpu

D_MODEL = 2048
D_CONV = 1024
D_RET = 1024
RET_HEADS = 8
HEAD_DIM = 128
CHUNK = 128
SEG = 1024
N_SEG = 8
SUB = 256
N_SUB = SEG // SUB
GRID_W = 64
ROPE_BASE = 10000.0
ROPE_F = 32
EPS = 1e-6
K_SCALE = HEAD_DIM ** -0.5

F32 = jnp.float32
BF16 = jnp.bfloat16

G_H, G_B, G_C, G_ZC, G_Q, G_K, G_V, G_ZR = range(8)


def _silu(x):
    return x / (1.0 + jnp.exp(-x))


def _params(vmem_mb, n_axes, flags=None):
    return pltpu.CompilerParams(
        dimension_semantics=("arbitrary",) * n_axes,
        vmem_limit_bytes=vmem_mb * 1024 * 1024,
        flags=flags,
    )


def _mod_kernel(cv_ref, w_ref, b_ref, o_ref):
    s = _silu(cv_ref[...])
    o_ref[0] = jnp.dot(s.astype(BF16), w_ref[0].astype(BF16),
                       preferred_element_type=F32) + b_ref[0]


def _modulation(cv, w_mod, b_mod, tn=1024):
    depth, d, n = w_mod.shape
    return pl.pallas_call(
        _mod_kernel,
        grid=(depth, n // tn),
        in_specs=[
            pl.BlockSpec((8, d), lambda l, j: (0, 0)),
            pl.BlockSpec((1, d, tn), lambda l, j: (l, 0, j)),
            pl.BlockSpec((1, 1, tn), lambda l, j: (l, 0, j)),
        ],
        out_specs=pl.BlockSpec((1, 8, tn), lambda l, j: (l, 0, j)),
        out_shape=jax.ShapeDtypeStruct((depth, 8, n), F32),
        compiler_params=_params(40, 2),
    )(cv, w_mod, b_mod.reshape(depth, 1, n))


PROLOGUE_ROWS = 64
WOUT_CAST_ROWS = 128
W_CAST_BLOCKS = 32


def _prologue(x_ref, nw_ref, shift_ref, scale_ref, hx_ref):
    gain = nw_ref[...] * (1.0 + scale_ref[...])
    shift = shift_ref[...]

    def body(r, carry):
        rows = pl.ds(pl.multiple_of(r * PROLOGUE_ROWS, PROLOGUE_ROWS), PROLOGUE_ROWS)
        x = x_ref[rows, :]
        ms = jnp.mean(x * x, axis=-1, keepdims=True)
        hx_ref[rows, :] = (x * lax.rsqrt(ms + EPS) * gain + shift).astype(BF16)
        return carry

    lax.fori_loop(0, x_ref.shape[0] // PROLOGUE_ROWS, body, 0)


def _rope_pair(acc, cos, sa, sb):
    outs = []
    for h in range(SUB // HEAD_DIM):
        a = acc[:, h * HEAD_DIM:(h + 1) * HEAD_DIM]
        outs.append(a * cos + pltpu.roll(a, HEAD_DIM - ROPE_F, 1) * sa
                    + pltpu.roll(a, ROPE_F, 1) * sb)
    return jnp.concatenate(outs, axis=1)


def _project_groups(hx, w_refs, convw_ref, table_refs, out_refs, rope, prev_row=None,
                    edge_refs=None):
    wh_ref, wb_ref, wc_ref, wzc_ref, wq_ref, wk_ref, wv_ref, wzr_ref = w_refs
    yb_ref, zc_ref, q_ref, k_ref, v_ref, zr_ref = out_refs
    tm = hx.shape[0]

    def seg(w_ref):
        return jnp.dot(hx, w_ref[...], preferred_element_type=F32)

    ch = seg(wc_ref) * seg(wh_ref)
    ridx = lax.broadcasted_iota(jnp.int32, (tm, 1), 0)
    above = pltpu.roll(ch, 1, 0)
    above = jnp.where(ridx == 0, 0.0 if prev_row is None else prev_row, above)
    below = jnp.where(ridx == tm - 1, 0.0, pltpu.roll(ch, tm - 1, 0))
    cw = convw_ref[...]
    b = seg(wb_ref)
    yb_ref[...] = (b * (above * cw[0:1, :] + ch * cw[1:2, :] + below * cw[2:3, :])).astype(BF16)
    if edge_refs is not None:
        edge_refs[0][0] = b[tm - 8:tm, :]
        edge_refs[1][0] = ch[0:8, :]
    zc_ref[...] = _silu(seg(wzc_ref)).astype(BF16)
    if rope:
        cos, sa, sb = (t[...] for t in table_refs)
        q_ref[...] = _rope_pair(seg(wq_ref), cos, sa, sb).astype(BF16)
        k_ref[...] = _rope_pair(seg(wk_ref), cos * K_SCALE, sa * K_SCALE, sb * K_SCALE).astype(BF16)
    else:
        q_ref[...] = seg(wq_ref).astype(BF16)
        k_ref[...] = (seg(wk_ref) * K_SCALE).astype(BF16)
    v_ref[...] = seg(wv_ref).astype(BF16)
    zr_ref[...] = _silu(seg(wzr_ref)).astype(BF16)
    return ch[tm - 8:tm, :]


def _inproj_kernel(x_ref, nw_ref, shift_ref, scale_ref, *refs, rope):
    w_refs, convw_ref, table_refs = refs[:8], refs[8], refs[9:12]
    out_refs, hx_ref = refs[12:18], refs[18]

    @pl.when(pl.program_id(1) == 0)
    def _():
        _prologue(x_ref, nw_ref, shift_ref, scale_ref, hx_ref)

    _project_groups(hx_ref[...], w_refs, convw_ref, table_refs, out_refs, rope)


def _inproj_latent_kernel(xq_ref, nw_ref, shift_ref, scale_ref, *refs, tm, cast_next):
    w_refs, convw_ref, table_refs = refs[:8], refs[8], refs[9:12]
    refs = refs[12:]
    if cast_next:
        wnext32_ref, refs = refs[0], refs[1:]
    out_refs, edge_refs, refs = refs[:6], refs[6:8], refs[8:]
    if cast_next:
        wnext_ref, refs = refs[0], refs[1:]
    hxa_ref, hxb_ref, carry_ref = refs
    i = pl.program_id(0)
    j = pl.program_id(1)
    quarter = tm // N_SUB
    row0 = pl.multiple_of(j * quarter, quarter)

    def project(hx_ref):
        prev_row = carry_ref[j][7:8, :]
        carry_ref[j] = _project_groups(hx_ref[...], w_refs, convw_ref, table_refs, out_refs,
                                       True, prev_row, edge_refs)

    def normalise(hx_ref):
        if cast_next:
            wnext_ref[...] = wnext32_ref[...].astype(BF16)
        gain = nw_ref[...] * (1.0 + scale_ref[...])
        shift = shift_ref[...]
        for r in range(0, quarter, PROLOGUE_ROWS):
            x = xq_ref[r:r + PROLOGUE_ROWS, :]
            ms = jnp.mean(x * x, axis=-1, keepdims=True)
            hx_ref[pl.ds(row0 + r, PROLOGUE_ROWS), :] = (
                x * lax.rsqrt(ms + EPS) * gain + shift).astype(BF16)

    @pl.when(i == 0)
    def _():
        normalise(hxa_ref)
        carry_ref[j] = jnp.zeros(carry_ref.shape[1:], F32)

    @pl.when(i % 2 == 1)
    def _():
        normalise(hxb_ref)
        project(hxa_ref)

    @pl.when(jnp.logical_and(i > 0, i % 2 == 0))
    def _():
        normalise(hxa_ref)
        project(hxb_ref)


def _w_spec(group):
    return pl.BlockSpec((D_MODEL, SUB), lambda i, j: (0, group * N_SUB + j))


def _convw_spec(layer):
    return pl.BlockSpec((None, 3, SUB), lambda i, j: (layer, 0, j))


def _inproj(x, norm_w, shift, scale, w_bf16, conv_w, layer, tables, *, rope):
    tm, d = x.shape
    cos, sa, sb = tables
    kern = functools.partial(_inproj_kernel, rope=rope)
    vec = lambda: pl.BlockSpec((1, d), lambda i, j: (0, 0))
    tab = lambda: pl.BlockSpec((tm, HEAD_DIM), lambda i, j: (i, 0))
    out = lambda: pl.BlockSpec((tm, SUB), lambda i, j: (i, j))
    seg_shape = jax.ShapeDtypeStruct((tm, SEG), BF16)
    return pl.pallas_call(
        kern,
        grid=(1, N_SUB),
        in_specs=[
            pl.BlockSpec((tm, d), lambda i, j: (i, 0)),
            vec(), vec(), vec(),
            *[_w_spec(g) for g in range(N_SEG)],
            _convw_spec(layer),
            tab(), tab(), tab(),
        ],
        out_specs=[out() for _ in range(6)],
        out_shape=[seg_shape] * 6,
        scratch_shapes=[pltpu.VMEM((tm, d), BF16)],
        compiler_params=_params(56, 2),
    )(x, norm_w, shift, scale, *([w_bf16] * N_SEG), conv_w, cos, sa, sb)


def _inproj_latent(x, norm_w, shift, scale, w_bf16, conv_w, layer, tables, w_next_f32, *, tm):
    rows, d = x.shape
    nt = rows // tm
    cos, sa, sb = tables
    cast_next = w_next_f32 is not None
    kern = functools.partial(_inproj_latent_kernel, tm=tm, cast_next=cast_next)
    done = lambda i: jnp.maximum(i - 1, 0)
    vec = lambda: pl.BlockSpec((1, d), lambda i, j: (0, 0))
    tab = lambda: pl.BlockSpec((tm, HEAD_DIM), lambda i, j: (done(i), 0))
    out = lambda: pl.BlockSpec((tm, SUB), lambda i, j: (done(i), jnp.where(i == 0, 0, j)))
    seg_shape = jax.ShapeDtypeStruct((rows, SEG), BF16)
    in_specs = [
        pl.BlockSpec((tm // N_SUB, d), lambda i, j: (jnp.minimum(i, nt - 1) * N_SUB + j, 0)),
        vec(), vec(), vec(),
        *[_w_spec(g) for g in range(N_SEG)],
        _convw_spec(layer),
        tab(), tab(), tab(),
    ]
    operands = [x, norm_w, shift, scale, *([w_bf16] * N_SEG), conv_w, cos, sa, sb]
    edge = lambda: pl.BlockSpec((1, 8, SUB), lambda i, j: (done(i), 0, jnp.where(i == 0, 0, j)))
    out_specs = [out() for _ in range(6)] + [edge(), edge()]
    out_shape = [seg_shape] * 6 + [jax.ShapeDtypeStruct((nt, 8, SEG), F32)] * 2
    if cast_next:
        n_blocks = W_CAST_BLOCKS
        assert n_blocks <= (nt + 1) * N_SUB and d % n_blocks == 0
        w_all, next_layer = w_next_f32
        blk = lambda i, j: jnp.minimum(i * N_SUB + j, n_blocks - 1)
        in_specs.append(pl.BlockSpec((None, d // n_blocks, w_all.shape[2]),
                                     lambda i, j: (next_layer, blk(i, j), 0)))
        operands.append(w_all)
        out_specs.append(pl.BlockSpec((d // n_blocks, w_all.shape[2]),
                                      lambda i, j: (blk(i, j), 0)))
        out_shape.append(jax.ShapeDtypeStruct(w_all.shape[1:], BF16))
    res = pl.pallas_call(
        kern,
        grid=(nt + 1, N_SUB),
        in_specs=in_specs,
        out_specs=out_specs,
        out_shape=out_shape,
        scratch_shapes=[pltpu.VMEM((tm, d), BF16), pltpu.VMEM((tm, d), BF16),
                        pltpu.VMEM((N_SUB, 8, SUB), F32)],
        compiler_params=_params(56, 2),
    )(*operands)
    return res[:6], res[6:8], (res[8] if cast_next else None)


def _inproj_kv_kernel(x_ref, nw_ref, shift_ref, scale_ref, wk_ref, wv_ref, k_ref, v_ref, hx_ref):
    @pl.when(pl.program_id(1) == 0)
    def _():
        _prologue(x_ref, nw_ref, shift_ref, scale_ref, hx_ref)

    hx = hx_ref[...]
    k_ref[...] = (jnp.dot(hx, wk_ref[...], preferred_element_type=F32) * K_SCALE).astype(BF16)
    v_ref[...] = jnp.dot(hx, wv_ref[...], preferred_element_type=F32).astype(BF16)


def _inproj_kv(x, norm_w, shift, scale, w_bf16, *, tm):
    rows, d = x.shape
    vec = lambda: pl.BlockSpec((1, d), lambda i, j: (0, 0))
    out = lambda: pl.BlockSpec((tm, SUB), lambda i, j: (i, j))
    seg_shape = jax.ShapeDtypeStruct((rows, SEG), BF16)
    return pl.pallas_call(
        _inproj_kv_kernel,
        grid=(rows // tm, N_SUB),
        in_specs=[
            pl.BlockSpec((tm, d), lambda i, j: (i, 0)),
            vec(), vec(), vec(),
            _w_spec(G_K), _w_spec(G_V),
        ],
        out_specs=[out(), out()],
        out_shape=[seg_shape] * 2,
        scratch_shapes=[pltpu.VMEM((tm, d), BF16)],
        compiler_params=_params(32, 2),
    )(x, norm_w, shift, scale, w_bf16, w_bf16)


def _states_kernel(lgf_ref, lgb_ref, kf_ref, vf_ref, kb_ref, vb_ref, s0f_ref, s0b_ref,
                   sf_ref, sb_ref, ff_ref, fb_ref, *, nc):
    t = pl.program_id(0)

    @pl.when(t == 0)
    def _():
        ff_ref[...] = s0f_ref[...]
        fb_ref[...] = s0b_ref[...]

    row = lax.broadcasted_iota(jnp.int32, (CHUNK, HEAD_DIM), 0).astype(F32)
    tn = (((0,), (0,)), ((), ()))
    for h in range(RET_HEADS):
        hs = slice(h * HEAD_DIM, (h + 1) * HEAD_DIM)
        lgf = lgf_ref[h]
        lgb = lgb_ref[h]
        kdf = jnp.exp(lgf * (CHUNK - 1.0 - row))
        kdb = jnp.exp(lgb * row)
        cdf = jnp.exp(lgf * CHUNK + 0.0 * row)
        cdb = jnp.exp(lgb * CHUNK + 0.0 * row)

        s = ff_ref[h]
        for c in range(nc):
            rs = slice(c * CHUNK, (c + 1) * CHUNK)
            sf_ref[c, h] = s.astype(BF16)
            kd = (kf_ref[rs, hs].astype(F32) * kdf).astype(BF16)
            s = cdf * s + lax.dot_general(kd, vf_ref[rs, hs], tn, preferred_element_type=F32)
        ff_ref[h] = s

        s = fb_ref[h]
        for c in reversed(range(nc)):
            rs = slice(c * CHUNK, (c + 1) * CHUNK)
            sb_ref[c, h] = s.astype(BF16)
            kd = (kb_ref[rs, hs].astype(F32) * kdb).astype(BF16)
            s = cdb * s + lax.dot_general(kd, vb_ref[rs, hs], tn, preferred_element_type=F32)
        fb_ref[h] = s


def _states(lg_f, lg_b, k, v, s0f, s0b, *, tr):
    rows = k.shape[0]
    nt = rows // tr
    nc = tr // CHUNK
    n_chunks = rows // CHUNK
    kern = functools.partial(_states_kernel, nc=nc)
    smem = lambda: pl.BlockSpec(memory_space=pltpu.SMEM)
    st = lambda: pl.BlockSpec((RET_HEADS, HEAD_DIM, HEAD_DIM), lambda t: (0, 0, 0))
    fwd = lambda: pl.BlockSpec((tr, SEG), lambda t: (t, 0))
    bwd = lambda: pl.BlockSpec((tr, SEG), lambda t: (nt - 1 - t, 0))
    seq_shape = jax.ShapeDtypeStruct((n_chunks, RET_HEADS, HEAD_DIM, HEAD_DIM), BF16)
    fin_shape = jax.ShapeDtypeStruct((RET_HEADS, HEAD_DIM, HEAD_DIM), F32)
    return pl.pallas_call(
        kern,
        grid=(nt,),
        in_specs=[smem(), smem(), fwd(), fwd(), bwd(), bwd(), st(), st()],
        out_specs=[
            pl.BlockSpec((nc, RET_HEADS, HEAD_DIM, HEAD_DIM), lambda t: (t, 0, 0, 0)),
            pl.BlockSpec((nc, RET_HEADS, HEAD_DIM, HEAD_DIM), lambda t: (nt - 1 - t, 0, 0, 0)),
            st(), st(),
        ],
        out_shape=[seq_shape, seq_shape, fin_shape, fin_shape],
        compiler_params=_params(32, 1),
    )(lg_f, lg_b, k, v, k, v, s0f, s0b)


def _lane_mean(t):
    hi = t.astype(BF16)
    lo = (t - hi.astype(F32)).astype(BF16)
    ones = jnp.full((2 * HEAD_DIM, HEAD_DIM), 1.0 / HEAD_DIM, BF16)
    return jnp.dot(jnp.concatenate([hi, lo], axis=1), ones, preferred_element_type=F32)


def _out_kernel(lgf_ref, lgb_ref,
                ycv_ref, zc_ref, q_ref, k_ref, v_ref, zr_ref, eb_ref, ec_ref,
                sf_ref, sb_ref, x_ref, gate_ref, convw_ref, cnw_ref, gnw_ref, wout32_ref, fnw_ref,
                o_ref,
                mask_ref, qdf_ref, qdb_ref, ya_ref, yb_ref, wout_ref, *, tm, edge_every, final):
    s_id = pl.program_id(0)
    n_tiles = pl.num_programs(0) - 1
    i = jnp.maximum(s_id - 1, 0)
    nc = tm // CHUNK
    at_edge = jnp.logical_and((i + 1) % edge_every == 0, i + 1 < n_tiles)

    @pl.when(s_id == 0)
    def _():
        yb_ref[...] = jnp.zeros_like(yb_ref)

        def cast_rows(t, carry):
            rows = pl.ds(pl.multiple_of(t * WOUT_CAST_ROWS, WOUT_CAST_ROWS), WOUT_CAST_ROWS)
            wout_ref[rows, :] = wout32_ref[0, rows, :].astype(BF16)
            return carry

        lax.fori_loop(0, D_MODEL // WOUT_CAST_ROWS, cast_rows, 0)

        r = lax.broadcasted_iota(jnp.int32, (CHUNK, CHUNK), 0).astype(F32)
        cc = lax.broadcasted_iota(jnp.int32, (CHUNK, CHUNK), 1).astype(F32)
        d = r - cc
        for h in range(RET_HEADS):
            lgf = lgf_ref[h]
            lgb = lgb_ref[h]
            mf = jnp.where(d >= 0, jnp.exp(lgf * jnp.maximum(d, 0.0)), 0.0)
            mb = jnp.where(d <= 0, jnp.exp(lgb * jnp.maximum(-d, 0.0)), 0.0)
            mask_ref[h] = mf + mb
            qdf_ref[h] = jnp.exp(lgf * (r + 1.0))
            qdb_ref[h] = jnp.exp(lgb * (CHUNK - r))

    def step(yr_old, yr_new):
        n_q = 4
        cw_ = D_CONV // n_q
        pw = D_MODEL // n_q
        last_row = lax.broadcasted_iota(jnp.int32, (8, 1), 0) == 7
        cwt = convw_ref[0]

        proj_ret, y_l, ssq = [], [], None
        for n in range(n_q):
            proj_ret.append(jnp.dot(yr_old[...], wout_ref[D_CONV:, n * pw:(n + 1) * pw],
                                    preferred_element_type=F32))
            cs = slice(n * cw_, (n + 1) * cw_)
            y = ycv_ref[:, cs].astype(F32)
            below = eb_ref[0, 7:8, cs] * cwt[2:3, cs] * ec_ref[0, 0:1, cs]
            below = jnp.where(jnp.logical_and(at_edge, last_row), below, 0.0)
            y = jnp.concatenate([y[:tm - 8], y[tm - 8:] + below], axis=0)
            y_l.append(y)
            part = jnp.sum(y * y, axis=-1, keepdims=True)
            ssq = part if ssq is None else ssq + part
        rinv = lax.rsqrt(ssq * (1.0 / D_CONV) + EPS)

        nt = (((1,), (1,)), ((), ()))
        pairs = [(c, h) for c in range(nc) for h in range(RET_HEADS)]
        rs = lambda c: slice(c * CHUNK, (c + 1) * CHUNK)
        hs = lambda h: slice(h * HEAD_DIM, (h + 1) * HEAD_DIM)
        q_l = [q_ref[rs(c), hs(h)] for c, h in pairs]
        s_l = [lax.dot_general(q, k_ref[rs(c), hs(h)], nt, preferred_element_type=F32)
               for q, (c, h) in zip(q_l, pairs)]

        y_conv = jnp.concatenate(
            [(zc_ref[:, n * cw_:(n + 1) * cw_].astype(F32)
              * (y_l[n] * rinv * cnw_ref[:, n * cw_:(n + 1) * cw_])).astype(BF16)
             for n in range(n_q)], axis=1)

        def finish_chunk(n):
            cols = slice(n * pw, (n + 1) * pw)
            pc = jnp.dot(y_conv, wout_ref[:D_CONV, cols], preferred_element_type=F32)
            o_ref[:, cols] = x_ref[:, cols] + gate_ref[:, cols] * (pc + proj_ret[n])

        finish_chunk(0)
        finish_chunk(1)
        lhs_l = []
        for q, s, (c, h) in zip(q_l, s_l, pairs):
            qf32 = q.astype(F32)
            lhs_l.append(jnp.concatenate(
                [(s * mask_ref[h]).astype(BF16), (qf32 * qdf_ref[h]).astype(BF16),
                 (qf32 * qdb_ref[h]).astype(BF16)], axis=1))
        o_l = [jnp.dot(lhs, jnp.concatenate([v_ref[rs(c), hs(h)], sf_ref[c, h], sb_ref[c, h]],
                                            axis=0), preferred_element_type=F32)
               for lhs, (c, h) in zip(lhs_l, pairs)]
        finish_chunk(2)
        oc_l = [o - _lane_mean(o) for o in o_l]
        finish_chunk(3)
        var_l = [_lane_mean(oc * oc) for oc in oc_l]
        if final:
            xn = o_ref[...]
            ms = jnp.mean(xn * xn, axis=-1, keepdims=True)
            o_ref[...] = xn * lax.rsqrt(ms + EPS) * fnw_ref[...]
        for oc, var, (c, h) in zip(oc_l, var_l, pairs):
            on = oc * lax.rsqrt(var + EPS) * gnw_ref[:, hs(h)]
            yr_new[rs(c), hs(h)] = (zr_ref[rs(c), hs(h)].astype(F32) * on).astype(BF16)

    @pl.when(s_id % 2 == 0)
    def _():
        step(yb_ref, ya_ref)

    @pl.when(s_id % 2 == 1)
    def _():
        step(ya_ref, yb_ref)


def _out(lg_f, lg_b, segs, edges, sf, sb, x, gate, conv_w, cnw, gnw, w_out, fnw, layer, *,
         tm, final):
    ycv, zc, q, k, v, zr = segs
    eb, ec = edges
    rows, d = x.shape
    nt = rows // tm
    nc = tm // CHUNK
    n_edge = eb.shape[0]
    edge_every = nt // n_edge
    kern = functools.partial(_out_kernel, tm=tm, edge_every=edge_every, final=final)
    smem = lambda: pl.BlockSpec(memory_space=pltpu.SMEM)
    cur = lambda s: jnp.minimum(s, nt - 1)
    prv = lambda s: jnp.maximum(s - 1, 0)
    seg = lambda: pl.BlockSpec((tm, SEG), lambda s: (cur(s), 0))
    fin = lambda: pl.BlockSpec((tm, SEG), lambda s: (prv(s), 0))
    st = lambda: pl.BlockSpec((nc, RET_HEADS, HEAD_DIM, HEAD_DIM), lambda s: (cur(s), 0, 0, 0))
    full = lambda shape: pl.BlockSpec(shape, lambda s: (0,) * len(shape))
    return pl.pallas_call(
        kern,
        grid=(nt + 1,),
        in_specs=[
            smem(), smem(),
            fin(), fin(), seg(), seg(), seg(), seg(),
            pl.BlockSpec((1, 8, SEG), lambda s: (prv(s) // edge_every, 0, 0)),
            pl.BlockSpec((1, 8, SEG),
                         lambda s: (jnp.minimum(prv(s) // edge_every + 1, n_edge - 1), 0, 0)),
            st(), st(),
            pl.BlockSpec((tm, d), lambda s: (prv(s), 0)),
            full((1, d)),
            pl.BlockSpec((1, 3, D_CONV), lambda s: (layer, 0, 0)),
            full((1, D_CONV)), full((1, D_RET)),
            pl.BlockSpec((1, d, d), lambda s: (layer, 0, 0), pipeline_mode=pl.Buffered(1)),
            full((1, d)),
        ],
        out_specs=pl.BlockSpec((tm, d), lambda s: (prv(s), 0)),
        out_shape=jax.ShapeDtypeStruct((rows, d), F32),
        scratch_shapes=[
            pltpu.VMEM((RET_HEADS, CHUNK, CHUNK), F32),
            pltpu.VMEM((RET_HEADS, CHUNK, HEAD_DIM), F32),
            pltpu.VMEM((RET_HEADS, CHUNK, HEAD_DIM), F32),
            pltpu.VMEM((tm, D_RET), BF16),
            pltpu.VMEM((tm, D_RET), BF16),
            pltpu.VMEM((d, d), BF16),
        ],
        compiler_params=_params(56, 1),
    )(lg_f, lg_b, ycv, zc, q, k, v, zr, eb, ec, sf, sb, x, gate, conv_w, cnw, gnw,
      w_out, fnw)


def _rope_tables(seq):
    t = np.arange(seq)
    row = (t // GRID_W).astype(np.float64)
    col = (t % GRID_W).astype(np.float64)
    inv = ROPE_BASE ** (-np.arange(ROPE_F, dtype=np.float64) / ROPE_F)
    ar = row[:, None] * inv[None, :]
    ac = col[:, None] * inv[None, :]
    z = np.zeros_like(ar)
    cos = np.concatenate([np.cos(ar), np.cos(ar), np.cos(ac), np.cos(ac)], axis=-1)
    sa = np.concatenate([-np.sin(ar), z, -np.sin(ac), z], axis=-1)
    sb = np.concatenate([z, np.sin(ar), z, np.sin(ac)], axis=-1)
    return tuple(jnp.asarray(a, dtype=F32) for a in (cos, sa, sb))


def kernel(x, c, ctx, c_ctx, norm_w, w_mod, b_mod, w_in, conv_w, conv_norm_w, ret_norm_w,
           ret_decay_f, ret_decay_b, w_out, final_norm_w):
    batch, seq, d = x.shape
    assert batch == 1 and d == D_MODEL and seq % 1024 == 0
    depth = norm_w.shape[0]
    ctx_len = ctx.shape[1]
    xs = x[0]
    cs = ctx[0]

    cv = jnp.zeros((8, d), F32).at[0].set(c[0]).at[1].set(c_ctx)
    mod = _modulation(cv, w_mod, b_mod)
    tables = _rope_tables(seq)
    ctx_tables = tuple(jnp.zeros((ctx_len, HEAD_DIM), F32) for _ in range(3))
    zero_state = jnp.zeros((RET_HEADS, HEAD_DIM, HEAD_DIM), F32)
    zero_edge = jnp.zeros((1, 8, D_CONV), F32)
    fnw = final_norm_w.reshape(1, d)
    w_in_b = w_in[0].astype(BF16)

    for layer in range(depth):
        update_ctx = layer < depth - 1
        lg_f = -jnp.exp(ret_decay_f[layer].astype(F32))
        lg_b = -jnp.exp(ret_decay_b[layer].astype(F32))
        nw = norm_w[layer].reshape(1, d)
        m = mod[layer]
        shift, scale, gate = m[0:1, 0:d], m[0:1, d:2 * d], m[0:1, 2 * d:3 * d]
        shift_c, scale_c, gate_c = m[1:2, 0:d], m[1:2, d:2 * d], m[1:2, 2 * d:3 * d]
        cnw = conv_norm_w[layer].reshape(1, D_CONV)
        gnw = ret_norm_w[layer].reshape(1, D_RET)

        if update_ctx:
            segs_c = _inproj(cs, nw, shift_c, scale_c, w_in_b, conv_w, layer, ctx_tables,
                             rope=False)
            k_c, v_c = segs_c[3], segs_c[4]
        else:
            k_c, v_c = _inproj_kv(cs, nw, shift_c, scale_c, w_in_b, tm=ctx_len)
        sf_c, sb_c, s_f, s_b = _states(lg_f, lg_b, k_c, v_c, zero_state, zero_state, tr=ctx_len)

        w_next = (w_in, layer + 1) if layer + 1 < depth else None
        segs, edges, w_in_b = _inproj_latent(xs, nw, shift, scale, w_in_b, conv_w, layer, tables,
                                             w_next, tm=1024)
        sf, sb, _, _ = _states(lg_f, lg_b, segs[3], segs[4], s_f, s_b, tr=1024)
        xs = _out(lg_f, lg_b, segs, edges, sf, sb, xs, gate, conv_w, cnw, gnw, w_out, fnw, layer,
                  tm=256, final=not update_ctx)
        if update_ctx:
            cs = _out(lg_f, lg_b, segs_c, (zero_edge, zero_edge), sf_c, sb_c, cs, gate_c, conv_w,
                      cnw, gnw, w_out, fnw, layer, tm=ctx_len, final=False)
    return xs[None]
```

```python
import functools

import numpy as np
import jax
import jax.numpy as jnp
from jax import lax
from jax.experimental import pallas as pl
from jax.experimental.pallas import tpu as pltpu

D_MODEL = 2048
D_CONV = 1024
D_RET = 1024
RET_HEADS = 8
HEAD_DIM = 128
CHUNK = 128
SEG = 1024
N_SEG = 8
SUB = 256
N_SUB = SEG // SUB
GRID_W = 64
ROPE_BASE = 10000.0
ROPE_F = 32
EPS = 1e-6
K_SCALE = HEAD_DIM ** -0.5

F32 = jnp.float32
BF16 = jnp.bfloat16

G_H, G_B, G_C, G_ZC, G_Q, G_K, G_V, G_ZR = range(8)


def _silu(x):
    return x / (1.0 + jnp.exp(-x))


def _params(vmem_mb, n_axes, flags=None):
    return pltpu.CompilerParams(
        dimension_semantics=("arbitrary",) * n_axes,
        vmem_limit_bytes=vmem_mb * 1024 * 1024,
        flags=flags,
    )


def _mod_kernel(cv_ref, w_ref, b_ref, o_ref):
    s = _silu(cv_ref[...])
    o_ref[0] = jnp.dot(s.astype(BF16), w_ref[0].astype(BF16),
                       preferred_element_type=F32) + b_ref[0]


def _modulation(cv, w_mod, b_mod, tn=1024):
    depth, d, n = w_mod.shape
    return pl.pallas_call(
        _mod_kernel,
        grid=(depth, n // tn),
        in_specs=[
            pl.BlockSpec((8, d), lambda l, j: (0, 0)),
            pl.BlockSpec((1, d, tn), lambda l, j: (l, 0, j)),
            pl.BlockSpec((1, 1, tn), lambda l, j: (l, 0, j)),
        ],
        out_specs=pl.BlockSpec((1, 8, tn), lambda l, j: (l, 0, j)),
        out_shape=jax.ShapeDtypeStruct((depth, 8, n), F32),
        compiler_params=_params(40, 2),
    )(cv, w_mod, b_mod.reshape(depth, 1, n))


PROLOGUE_ROWS = 64
WOUT_CAST_ROWS = 128
W_CAST_BLOCKS = 32


def _prologue(x_ref, nw_ref, shift_ref, scale_ref, hx_ref):
    gain = nw_ref[...] * (1.0 + scale_ref[...])
    shift = shift_ref[...]

    def body(r, carry):
        rows = pl.ds(pl.multiple_of(r * PROLOGUE_ROWS, PROLOGUE_ROWS), PROLOGUE_ROWS)
        x = x_ref[rows, :]
        ms = jnp.mean(x * x, axis=-1, keepdims=True)
        hx_ref[rows, :] = (x * lax.rsqrt(ms + EPS) * gain + shift).astype(BF16)
        return carry

    lax.fori_loop(0, x_ref.shape[0] // PROLOGUE_ROWS, body, 0)


def _rope_pair(acc, cos, sa, sb):
    outs = []
    for h in range(SUB // HEAD_DIM):
        a = acc[:, h * HEAD_DIM:(h + 1) * HEAD_DIM]
        outs.append(a * cos + pltpu.roll(a, HEAD_DIM - ROPE_F, 1) * sa
                    + pltpu.roll(a, ROPE_F, 1) * sb)
    return jnp.concatenate(outs, axis=1)


def _project_groups(hx, w_refs, convw_ref, table_refs, out_refs, rope, prev_row=None,
                    edge_refs=None):
    wh_ref, wb_ref, wc_ref, wzc_ref, wq_ref, wk_ref, wv_ref, wzr_ref = w_refs
    yb_ref, zc_ref, q_ref, k_ref, v_ref, zr_ref = out_refs
    tm = hx.shape[0]

    def seg(w_ref):
        return jnp.dot(hx, w_ref[...], preferred_element_type=F32)

    ch = seg(wc_ref) * seg(wh_ref)
    ridx = lax.broadcasted_iota(jnp.int32, (tm, 1), 0)
    above = pltpu.roll(ch, 1, 0)
    above = jnp.where(ridx == 0, 0.0 if prev_row is None else prev_row, above)
    below = jnp.where(ridx == tm - 1, 0.0, pltpu.roll(ch, tm - 1, 0))
    cw = convw_ref[...]
    b = seg(wb_ref)
    yb_ref[...] = (b * (above * cw[0:1, :] + ch * cw[1:2, :] + below * cw[2:3, :])).astype(BF16)
    if edge_refs is not None:
        edge_refs[0][0] = b[tm - 8:tm, :]
        edge_refs[1][0] = ch[0:8, :]
    zc_ref[...] = _silu(seg(wzc_ref)).astype(BF16)
    if rope:
        cos, sa, sb = (t[...] for t in table_refs)
        q_ref[...] = _rope_pair(seg(wq_ref), cos, sa, sb).astype(BF16)
        k_ref[...] = _rope_pair(seg(wk_ref), cos * K_SCALE, sa * K_SCALE, sb * K_SCALE).astype(BF16)
    else:
        q_ref[...] = seg(wq_ref).astype(BF16)
        k_ref[...] = (seg(wk_ref) * K_SCALE).astype(BF16)
    v_ref[...] = seg(wv_ref).astype(BF16)
    zr_ref[...] = _silu(seg(wzr_ref)).astype(BF16)
    return ch[tm - 8:tm, :]


def _inproj_kernel(x_ref, nw_ref, shift_ref, scale_ref, *refs, rope):
    w_refs, convw_ref, table_refs = refs[:8], refs[8], refs[9:12]
    out_refs, hx_ref = refs[12:18], refs[18]

    @pl.when(pl.program_id(1) == 0)
    def _():
        _prologue(x_ref, nw_ref, shift_ref, scale_ref, hx_ref)

    _project_groups(hx_ref[...], w_refs, convw_ref, table_refs, out_refs, rope)


def _inproj_latent_kernel(xq_ref, nw_ref, shift_ref, scale_ref, *refs, tm, cast_next):
    w_refs, convw_ref, table_refs = refs[:8], refs[8], refs[9:12]
    refs = refs[12:]
    if cast_next:
        wnext32_ref, refs = refs[0], refs[1:]
    out_refs, edge_refs, refs = refs[:6], refs[6:8], refs[8:]
    if cast_next:
        wnext_ref, refs = refs[0], refs[1:]
    hxa_ref, hxb_ref, carry_ref = refs
    i = pl.program_id(0)
    j = pl.program_id(1)
    quarter = tm // N_SUB
    row0 = pl.multiple_of(j * quarter, quarter)

    def project(hx_ref):
        prev_row = carry_ref[j][7:8, :]
        carry_ref[j] = _project_groups(hx_ref[...], w_refs, convw_ref, table_refs, out_refs,
                                       True, prev_row, edge_refs)

    def normalise(hx_ref):
        if cast_next:
            wnext_ref[...] = wnext32_ref[...].astype(BF16)
        gain = nw_ref[...] * (1.0 + scale_ref[...])
        shift = shift_ref[...]
        for r in range(0, quarter, PROLOGUE_ROWS):
            x = xq_ref[r:r + PROLOGUE_ROWS, :]
            ms = jnp.mean(x * x, axis=-1, keepdims=True)
            hx_ref[pl.ds(row0 + r, PROLOGUE_ROWS), :] = (
                x * lax.rsqrt(ms + EPS) * gain + shift).astype(BF16)

    @pl.when(i == 0)
    def _():
        normalise(hxa_ref)
        carry_ref[j] = jnp.zeros(carry_ref.shape[1:], F32)

    @pl.when(i % 2 == 1)
    def _():
        normalise(hxb_ref)
        project(hxa_ref)

    @pl.when(jnp.logical_and(i > 0, i % 2 == 0))
    def _():
        normalise(hxa_ref)
        project(hxb_ref)


def _w_spec(group):
    return pl.BlockSpec((D_MODEL, SUB), lambda i, j: (0, group * N_SUB + j))


def _convw_spec(layer):
    return pl.BlockSpec((None, 3, SUB), lambda i, j: (layer, 0, j))


def _inproj(x, norm_w, shift, scale, w_bf16, conv_w, layer, tables, *, rope):
    tm, d = x.shape
    cos, sa, sb = tables
    kern = functools.partial(_inproj_kernel, rope=rope)
    vec = lambda: pl.BlockSpec((1, d), lambda i, j: (0, 0))
    tab = lambda: pl.BlockSpec((tm, HEAD_DIM), lambda i, j: (i, 0))
    out = lambda: pl.BlockSpec((tm, SUB), lambda i, j: (i, j))
    seg_shape = jax.ShapeDtypeStruct((tm, SEG), BF16)
    return pl.pallas_call(
        kern,
        grid=(1, N_SUB),
        in_specs=[
            pl.BlockSpec((tm, d), lambda i, j: (i, 0)),
            vec(), vec(), vec(),
            *[_w_spec(g) for g in range(N_SEG)],
            _convw_spec(layer),
            tab(), tab(), tab(),
        ],
        out_specs=[out() for _ in range(6)],
        out_shape=[seg_shape] * 6,
        scratch_shapes=[pltpu.VMEM((tm, d), BF16)],
        compiler_params=_params(56, 2),
    )(x, norm_w, shift, scale, *([w_bf16] * N_SEG), conv_w, cos, sa, sb)


def _inproj_latent(x, norm_w, shift, scale, w_bf16, conv_w, layer, tables, w_next_f32, *, tm):
    rows, d = x.shape
    nt = rows // tm
    cos, sa, sb = tables
    cast_next = w_next_f32 is not None
    kern = functools.partial(_inproj_latent_kernel, tm=tm, cast_next=cast_next)
    done = lambda i: jnp.maximum(i - 1, 0)
    vec = lambda: pl.BlockSpec((1, d), lambda i, j: (0, 0))
    tab = lambda: pl.BlockSpec((tm, HEAD_DIM), lambda i, j: (done(i), 0))
    out = lambda: pl.BlockSpec((tm, SUB), lambda i, j: (done(i), jnp.where(i == 0, 0, j)))
    seg_shape = jax.ShapeDtypeStruct((rows, SEG), BF16)
    in_specs = [
        pl.BlockSpec((tm // N_SUB, d), lambda i, j: (jnp.minimum(i, nt - 1) * N_SUB + j, 0)),
        vec(), vec(), vec(),
        *[_w_spec(g) for g in range(N_SEG)],
        _convw_spec(layer),
        tab(), tab(), tab(),
    ]
    operands = [x, norm_w, shift, scale, *([w_bf16] * N_SEG), conv_w, cos, sa, sb]
    edge = lambda: pl.BlockSpec((1, 8, SUB), lambda i, j: (done(i), 0, jnp.where(i == 0, 0, j)))
    out_specs = [out() for _ in range(6)] + [edge(), edge()]
    out_shape = [seg_shape] * 6 + [jax.ShapeDtypeStruct((nt, 8, SEG), F32)] * 2
    if cast_next:
        n_blocks = W_CAST_BLOCKS
        assert n_blocks <= (nt + 1) * N_SUB and d % n_blocks == 0
        w_all, next_layer = w_next_f32
        blk = lambda i, j: jnp.minimum(i * N_SUB + j, n_blocks - 1)
        in_specs.append(pl.BlockSpec((None, d // n_blocks, w_all.shape[2]),
                                     lambda i, j: (next_layer, blk(i, j), 0)))
        operands.append(w_all)
        out_specs.append(pl.BlockSpec((d // n_blocks, w_all.shape[2]),
                                      lambda i, j: (blk(i, j), 0)))
        out_shape.append(jax.ShapeDtypeStruct(w_all.shape[1:], BF16))
    res = pl.pallas_call(
        kern,
        grid=(nt + 1, N_SUB),
        in_specs=in_specs,
        out_specs=out_specs,
        out_shape=out_shape,
        scratch_shapes=[pltpu.VMEM((tm, d), BF16), pltpu.VMEM((tm, d), BF16),
                        pltpu.VMEM((N_SUB, 8, SUB), F32)],
        compiler_params=_params(56, 2),
    )(*operands)
    return res[:6], res[6:8], (res[8] if cast_next else None)


def _inproj_kv_kernel(x_ref, nw_ref, shift_ref, scale_ref, wk_ref, wv_ref, k_ref, v_ref, hx_ref):
    @pl.when(pl.program_id(1) == 0)
    def _():
        _prologue(x_ref, nw_ref, shift_ref, scale_ref, hx_ref)

    hx = hx_ref[...]
    k_ref[...] = (jnp.dot(hx, wk_ref[...], preferred_element_type=F32) * K_SCALE).astype(BF16)
    v_ref[...] = jnp.dot(hx, wv_ref[...], preferred_element_type=F32).astype(BF16)


def _inproj_kv(x, norm_w, shift, scale, w_bf16, *, tm):
    rows, d = x.shape
    vec = lambda: pl.BlockSpec((1, d), lambda i, j: (0, 0))
    out = lambda: pl.BlockSpec((tm, SUB), lambda i, j: (i, j))
    seg_shape = jax.ShapeDtypeStruct((rows, SEG), BF16)
    return pl.pallas_call(
        _inproj_kv_kernel,
        grid=(rows // tm, N_SUB),
        in_specs=[
            pl.BlockSpec((tm, d), lambda i, j: (i, 0)),
            vec(), vec(), vec(),
            _w_spec(G_K), _w_spec(G_V),
        ],
        out_specs=[out(), out()],
        out_shape=[seg_shape] * 2,
        scratch_shapes=[pltpu.VMEM((tm, d), BF16)],
        compiler_params=_params(32, 2),
    )(x, norm_w, shift, scale, w_bf16, w_bf16)


def _states_kernel(lgf_ref, lgb_ref, kf_ref, vf_ref, kb_ref, vb_ref, s0f_ref, s0b_ref,
                   sf_ref, sb_ref, ff_ref, fb_ref, *, nc):
    t = pl.program_id(0)

    @pl.when(t == 0)
    def _():
        ff_ref[...] = s0f_ref[...]
        fb_ref[...] = s0b_ref[...]

    row = lax.broadcasted_iota(jnp.int32, (CHUNK, HEAD_DIM), 0).astype(F32)
    tn = (((0,), (0,)), ((), ()))
    for h in range(RET_HEADS):
        hs = slice(h * HEAD_DIM, (h + 1) * HEAD_DIM)
        lgf = lgf_ref[h]
        lgb = lgb_ref[h]
        kdf = jnp.exp(lgf * (CHUNK - 1.0 - row))
        kdb = jnp.exp(lgb * row)
        cdf = jnp.exp(lgf * CHUNK + 0.0 * row)
        cdb = jnp.exp(lgb * CHUNK + 0.0 * row)

        s = ff_ref[h]
        for c in range(nc):
            rs = slice(c * CHUNK, (c + 1) * CHUNK)
            sf_ref[c, h] = s.astype(BF16)
            kd = (kf_ref[rs, hs].astype(F32) * kdf).astype(BF16)
            s = cdf * s + lax.dot_general(kd, vf_ref[rs, hs], tn, preferred_element_type=F32)
        ff_ref[h] = s

        s = fb_ref[h]
        for c in reversed(range(nc)):
            rs = slice(c * CHUNK, (c + 1) * CHUNK)
            sb_ref[c, h] = s.astype(BF16)
            kd = (kb_ref[rs, hs].astype(F32) * kdb).astype(BF16)
            s = cdb * s + lax.dot_general(kd, vb_ref[rs, hs], tn, preferred_element_type=F32)
        fb_ref[h] = s


def _states(lg_f, lg_b, k, v, s0f, s0b, *, tr):
    rows = k.shape[0]
    nt = rows // tr
    nc = tr // CHUNK
    n_chunks = rows // CHUNK
    kern = functools.partial(_states_kernel, nc=nc)
    smem = lambda: pl.BlockSpec(memory_space=pltpu.SMEM)
    st = lambda: pl.BlockSpec((RET_HEADS, HEAD_DIM, HEAD_DIM), lambda t: (0, 0, 0))
    fwd = lambda: pl.BlockSpec((tr, SEG), lambda t: (t, 0))
    bwd = lambda: pl.BlockSpec((tr, SEG), lambda t: (nt - 1 - t, 0))
    seq_shape = jax.ShapeDtypeStruct((n_chunks, RET_HEADS, HEAD_DIM, HEAD_DIM), BF16)
    fin_shape = jax.ShapeDtypeStruct((RET_HEADS, HEAD_DIM, HEAD_DIM), F32)
    return pl.pallas_call(
        kern,
        grid=(nt,),
        in_specs=[smem(), smem(), fwd(), fwd(), bwd(), bwd(), st(), st()],
        out_specs=[
            pl.BlockSpec((nc, RET_HEADS, HEAD_DIM, HEAD_DIM), lambda t: (t, 0, 0, 0)),
            pl.BlockSpec((nc, RET_HEADS, HEAD_DIM, HEAD_DIM), lambda t: (nt - 1 - t, 0, 0, 0)),
            st(), st(),
        ],
        out_shape=[seq_shape, seq_shape, fin_shape, fin_shape],
        compiler_params=_params(32, 1),
    )(lg_f, lg_b, k, v, k, v, s0f, s0b)


def _lane_means(ts):
    parts = []
    for t in ts:
        hi = t.astype(BF16)
        lo = (t - hi.astype(F32)).astype(BF16)
        parts.append(jnp.concatenate([hi, lo], axis=1))
    rows = ts[0].shape[0]
    ones = jnp.full((2 * HEAD_DIM, HEAD_DIM), 1.0 / HEAD_DIM, BF16)
    m = jnp.dot(jnp.concatenate(parts, axis=0), ones, preferred_element_type=F32)
    return [m[n * rows:(n + 1) * rows] for n in range(len(ts))]


def _out_kernel(lgf_ref, lgb_ref,
                ycv_ref, zc_ref, q_ref, k_ref, v_ref, zr_ref, eb_ref, ec_ref,
                sf_ref, sb_ref, x_ref, gate_ref, convw_ref, cnw_ref, gnw_ref, wout32_ref, fnw_ref,
                o_ref,
                mask_ref, qdf_ref, qdb_ref, ya_ref, yb_ref, wout_ref, *, tm, n_tiles, edge_every,
                final):
    s_id = pl.program_id(0)
    i = jnp.maximum(s_id - 1, 0)
    nc = tm // CHUNK
    at_edge = jnp.logical_and((i + 1) % edge_every == 0, i + 1 < n_tiles)

    @pl.when(s_id == 0)
    def _():
        def cast_rows(t, carry):
            rows = pl.ds(pl.multiple_of(t * WOUT_CAST_ROWS, WOUT_CAST_ROWS), WOUT_CAST_ROWS)
            wout_ref[rows, :] = wout32_ref[0, rows, :].astype(BF16)
            return carry

        lax.fori_loop(0, D_MODEL // WOUT_CAST_ROWS, cast_rows, 0)

        r = lax.broadcasted_iota(jnp.int32, (CHUNK, CHUNK), 0).astype(F32)
        cc = lax.broadcasted_iota(jnp.int32, (CHUNK, CHUNK), 1).astype(F32)
        d = r - cc
        for h in range(RET_HEADS):
            lgf = lgf_ref[h]
            lgb = lgb_ref[h]
            mf = jnp.where(d >= 0, jnp.exp(lgf * jnp.maximum(d, 0.0)), 0.0)
            mb = jnp.where(d <= 0, jnp.exp(lgb * jnp.maximum(-d, 0.0)), 0.0)
            mask_ref[h] = mf + mb
            qdf_ref[h] = jnp.exp(lgf * (r + 1.0))
            qdb_ref[h] = jnp.exp(lgb * (CHUNK - r))

    def step(yr_old, yr_new):
        n_q = 4
        cw_ = D_CONV // n_q
        pw = D_MODEL // n_q

        if yr_old is not None:
            last_row = lax.broadcasted_iota(jnp.int32, (8, 1), 0) == 7
            cwt = convw_ref[0]
            proj_ret, y_l, ssq = [], [], None
            for n in range(n_q):
                proj_ret.append(jnp.dot(yr_old[...], wout_ref[D_CONV:, n * pw:(n + 1) * pw],
                                        preferred_element_type=F32))
                cs = slice(n * cw_, (n + 1) * cw_)
                y = ycv_ref[:, cs].astype(F32)
                below = eb_ref[0, 7:8, cs] * cwt[2:3, cs] * ec_ref[0, 0:1, cs]
                below = jnp.where(jnp.logical_and(at_edge, last_row), below, 0.0)
                y = jnp.concatenate([y[:tm - 8], y[tm - 8:] + below], axis=0)
                y_l.append(y)
                part = jnp.sum(y * y, axis=-1, keepdims=True)
                ssq = part if ssq is None else ssq + part
            rinv = lax.rsqrt(ssq * (1.0 / D_CONV) + EPS)

        nt = (((1,), (1,)), ((), ()))
        pairs = ([(c, h) for c in range(nc) for h in range(RET_HEADS)]
                 if yr_new is not None else [])
        rs = lambda c: slice(c * CHUNK, (c + 1) * CHUNK)
        hs = lambda h: slice(h * HEAD_DIM, (h + 1) * HEAD_DIM)
        q_l = [q_ref[rs(c), hs(h)] for c, h in pairs]
        s_l = [lax.dot_general(q, k_ref[rs(c), hs(h)], nt, preferred_element_type=F32)
               for q, (c, h) in zip(q_l, pairs)]

        if yr_old is not None:
            y_conv = jnp.concatenate(
                [(zc_ref[:, n * cw_:(n + 1) * cw_].astype(F32)
                  * (y_l[n] * rinv * cnw_ref[:, n * cw_:(n + 1) * cw_])).astype(BF16)
                 for n in range(n_q)], axis=1)

        def finish_chunk(n):
            if yr_old is None:
                return
            cols = slice(n * pw, (n + 1) * pw)
            pc = jnp.dot(y_conv, wout_ref[:D_CONV, cols], preferred_element_type=F32)
            o_ref[:, cols] = x_ref[:, cols] + gate_ref[:, cols] * (pc + proj_ret[n])

        finish_chunk(0)
        finish_chunk(1)
        lhs_l = []
        for q, s, (c, h) in zip(q_l, s_l, pairs):
            qf32 = q.astype(F32)
            lhs_l.append(jnp.concatenate(
                [(s * mask_ref[h]).astype(BF16), (qf32 * qdf_ref[h]).astype(BF16),
                 (qf32 * qdb_ref[h]).astype(BF16)], axis=1))
        o_l = [jnp.dot(lhs, jnp.concatenate([v_ref[rs(c), hs(h)], sf_ref[c, h], sb_ref[c, h]],
                                            axis=0), preferred_element_type=F32)
               for lhs, (c, h) in zip(lhs_l, pairs)]
        finish_chunk(2)
        per_chunk = lambda xs: [xs[c * RET_HEADS:(c + 1) * RET_HEADS]
                                for c in range(len(xs) // RET_HEADS)]
        mean_l = [m for grp in per_chunk(o_l) for m in _lane_means(grp)]
        oc_l = [o - m for o, m in zip(o_l, mean_l)]
        finish_chunk(3)
        var_l = [m for grp in per_chunk([oc * oc for oc in oc_l]) for m in _lane_means(grp)]
        if final and yr_old is not None:
            xn = o_ref[...]
            ms = jnp.mean(xn * xn, axis=-1, keepdims=True)
            o_ref[...] = xn * lax.rsqrt(ms + EPS) * fnw_ref[...]
        for oc, var, (c, h) in zip(oc_l, var_l, pairs):
            on = oc * lax.rsqrt(var + EPS) * gnw_ref[:, hs(h)]
            yr_new[rs(c), hs(h)] = (zr_ref[rs(c), hs(h)].astype(F32) * on).astype(BF16)

    middle = jnp.logical_and(s_id > 0, s_id < n_tiles)

    @pl.when(s_id == 0)
    def _():
        step(None, ya_ref)

    @pl.when(jnp.logical_and(middle, s_id % 2 == 0))
    def _():
        step(yb_ref, ya_ref)

    @pl.when(jnp.logical_and(middle, s_id % 2 == 1))
    def _():
        step(ya_ref, yb_ref)

    @pl.when(s_id == n_tiles)
    def _():
        step(ya_ref if n_tiles % 2 == 1 else yb_ref, None)


def _out(lg_f, lg_b, segs, edges, sf, sb, x, gate, conv_w, cnw, gnw, w_out, fnw, layer, *,
         tm, final):
    ycv, zc, q, k, v, zr = segs
    eb, ec = edges
    rows, d = x.shape
    nt = rows // tm
    nc = tm // CHUNK
    n_edge = eb.shape[0]
    edge_every = nt // n_edge
    kern = functools.partial(_out_kernel, tm=tm, n_tiles=nt, edge_every=edge_every, final=final)
    smem = lambda: pl.BlockSpec(memory_space=pltpu.SMEM)
    cur = lambda s: jnp.minimum(s, nt - 1)
    prv = lambda s: jnp.maximum(s - 1, 0)
    seg = lambda: pl.BlockSpec((tm, SEG), lambda s: (cur(s), 0))
    fin = lambda: pl.BlockSpec((tm, SEG), lambda s: (prv(s), 0))
    st = lambda: pl.BlockSpec((nc, RET_HEADS, HEAD_DIM, HEAD_DIM), lambda s: (cur(s), 0, 0, 0))
    full = lambda shape: pl.BlockSpec(shape, lambda s: (0,) * len(shape))
    return pl.pallas_call(
        kern,
        grid=(nt + 1,),
        in_specs=[
            smem(), smem(),
            fin(), fin(), seg(), seg(), seg(), seg(),
            pl.BlockSpec((1, 8, SEG), lambda s: (prv(s) // edge_every, 0, 0)),
            pl.BlockSpec((1, 8, SEG),
                         lambda s: (jnp.minimum(prv(s) // edge_every + 1, n_edge - 1), 0, 0)),
            st(), st(),
            pl.BlockSpec((tm, d), lambda s: (prv(s), 0)),
            full((1, d)),
            pl.BlockSpec((1, 3, D_CONV), lambda s: (layer, 0, 0)),
            full((1, D_CONV)), full((1, D_RET)),
            pl.BlockSpec((1, d, d), lambda s: (layer, 0, 0), pipeline_mode=pl.Buffered(1)),
            full((1, d)),
        ],
        out_specs=pl.BlockSpec((tm, d), lambda s: (prv(s), 0)),
        out_shape=jax.ShapeDtypeStruct((rows, d), F32),
        scratch_shapes=[
            pltpu.VMEM((RET_HEADS, CHUNK, CHUNK), F32),
            pltpu.VMEM((RET_HEADS, CHUNK, HEAD_DIM), F32),
            pltpu.VMEM((RET_HEADS, CHUNK, HEAD_DIM), F32),
            pltpu.VMEM((tm, D_RET), BF16),
            pltpu.VMEM((tm, D_RET), BF16),
            pltpu.VMEM((d, d), BF16),
        ],
        compiler_params=_params(56, 1),
    )(lg_f, lg_b, ycv, zc, q, k, v, zr, eb, ec, sf, sb, x, gate, conv_w, cnw, gnw,
      w_out, fnw)


def _rope_tables(seq):
    t = np.arange(seq)
    row = (t // GRID_W).astype(np.float64)
    col = (t % GRID_W).astype(np.float64)
    inv = ROPE_BASE ** (-np.arange(ROPE_F, dtype=np.float64) / ROPE_F)
    ar = row[:, None] * inv[None, :]
    ac = col[:, None] * inv[None, :]
    z = np.zeros_like(ar)
    cos = np.concatenate([np.cos(ar), np.cos(ar), np.cos(ac), np.cos(ac)], axis=-1)
    sa = np.concatenate([-np.sin(ar), z, -np.sin(ac), z], axis=-1)
    sb = np.concatenate([z, np.sin(ar), z, np.sin(ac)], axis=-1)
    return tuple(jnp.asarray(a, dtype=F32) for a in (cos, sa, sb))


def kernel(x, c, ctx, c_ctx, norm_w, w_mod, b_mod, w_in, conv_w, conv_norm_w, ret_norm_w,
           ret_decay_f, ret_decay_b, w_out, final_norm_w):
    batch, seq, d = x.shape
    assert batch == 1 and d == D_MODEL and seq % 1024 == 0
    depth = norm_w.shape[0]
    ctx_len = ctx.shape[1]
    xs = x[0]
    cs = ctx[0]

    cv = jnp.zeros((8, d), F32).at[0].set(c[0]).at[1].set(c_ctx)
    mod = _modulation(cv, w_mod, b_mod)
    tables = _rope_tables(seq)
    ctx_tables = tuple(jnp.zeros((ctx_len, HEAD_DIM), F32) for _ in range(3))
    zero_state = jnp.zeros((RET_HEADS, HEAD_DIM, HEAD_DIM), F32)
    zero_edge = jnp.zeros((1, 8, D_CONV), F32)
    fnw = final_norm_w.reshape(1, d)
    w_in_b = w_in[0].astype(BF16)

    for layer in range(depth):
        update_ctx = layer < depth - 1
        lg_f = -jnp.exp(ret_decay_f[layer].astype(F32))
        lg_b = -jnp.exp(ret_decay_b[layer].astype(F32))
        nw = norm_w[layer].reshape(1, d)
        m = mod[layer]
        shift, scale, gate = m[0:1, 0:d], m[0:1, d:2 * d], m[0:1, 2 * d:3 * d]
        shift_c, scale_c, gate_c = m[1:2, 0:d], m[1:2, d:2 * d], m[1:2, 2 * d:3 * d]
        cnw = conv_norm_w[layer].reshape(1, D_CONV)
        gnw = ret_norm_w[layer].reshape(1, D_RET)

        if update_ctx:
            segs_c = _inproj(cs, nw, shift_c, scale_c, w_in_b, conv_w, layer, ctx_tables,
                             rope=False)
            k_c, v_c = segs_c[3], segs_c[4]
        else:
            k_c, v_c = _inproj_kv(cs, nw, shift_c, scale_c, w_in_b, tm=ctx_len)
        sf_c, sb_c, s_f, s_b = _states(lg_f, lg_b, k_c, v_c, zero_state, zero_state, tr=ctx_len)

        w_next = (w_in, layer + 1) if layer + 1 < depth else None
        segs, edges, w_in_b = _inproj_latent(xs, nw, shift, scale, w_in_b, conv_w, layer, tables,
                                             w_next, tm=1024)
        sf, sb, _, _ = _states(lg_f, lg_b, segs[3], segs[4], s_f, s_b, tr=1024)
        xs = _out(lg_f, lg_b, segs, edges, sf, sb, xs, gate, conv_w, cnw, gnw, w_out, fnw, layer,
                  tm=256, final=not update_ctx)
        if update_ctx:
            cs = _out(lg_f, lg_b, segs_c, (zero_edge, zero_edge), sf_c, sb_c, cs, gate_c, conv_w,
                      cnw, gnw, w_out, fnw, layer, tm=ctx_len, final=False)
    return xs[None]
```

```python
import functools

import numpy as np
import jax
import jax.numpy as jnp
from jax import lax
from jax.experimental import pallas as pl
from jax.experimental.pallas import tpu as pltpu

D_MODEL = 2048
D_CONV = 1024
D_RET = 1024
RET_HEADS = 8
HEAD_DIM = 128
CHUNK = 128
SEG = 1024
N_SEG = 8
SUB = 256
N_SUB = SEG // SUB
GRID_W = 64
ROPE_BASE = 10000.0
ROPE_F = 32
EPS = 1e-6
K_SCALE = HEAD_DIM ** -0.5

F32 = jnp.float32
BF16 = jnp.bfloat16

G_H, G_B, G_C, G_ZC, G_Q, G_K, G_V, G_ZR = range(8)


def _silu(x):
    return x / (1.0 + jnp.exp(-x))


def _params(vmem_mb, n_axes, flags=None):
    return pltpu.CompilerParams(
        dimension_semantics=("arbitrary",) * n_axes,
        vmem_limit_bytes=vmem_mb * 1024 * 1024,
        flags=flags,
    )


def _mod_kernel(cv_ref, w_ref, b_ref, o_ref):
    s = _silu(cv_ref[...])
    o_ref[0] = jnp.dot(s.astype(BF16), w_ref[0].astype(BF16),
                       preferred_element_type=F32) + b_ref[0]


def _modulation(cv, w_mod, b_mod, tn=1024):
    depth, d, n = w_mod.shape
    return pl.pallas_call(
        _mod_kernel,
        grid=(depth, n // tn),
        in_specs=[
            pl.BlockSpec((8, d), lambda l, j: (0, 0)),
            pl.BlockSpec((1, d, tn), lambda l, j: (l, 0, j)),
            pl.BlockSpec((1, 1, tn), lambda l, j: (l, 0, j)),
        ],
        out_specs=pl.BlockSpec((1, 8, tn), lambda l, j: (l, 0, j)),
        out_shape=jax.ShapeDtypeStruct((depth, 8, n), F32),
        compiler_params=_params(40, 2),
    )(cv, w_mod, b_mod.reshape(depth, 1, n))


PROLOGUE_ROWS = 64
RET_ROUND_CHUNKS = 2
W_CAST_BLOCKS = 32


def _prologue(x_ref, nw_ref, shift_ref, scale_ref, hx_ref):
    gain = nw_ref[...] * (1.0 + scale_ref[...])
    shift = shift_ref[...]

    def body(r, carry):
        rows = pl.ds(pl.multiple_of(r * PROLOGUE_ROWS, PROLOGUE_ROWS), PROLOGUE_ROWS)
        x = x_ref[rows, :]
        ms = jnp.mean(x * x, axis=-1, keepdims=True)
        hx_ref[rows, :] = (x * lax.rsqrt(ms + EPS) * gain + shift).astype(BF16)
        return carry

    lax.fori_loop(0, x_ref.shape[0] // PROLOGUE_ROWS, body, 0)


def _rope_pair(acc, cos, sa, sb):
    outs = []
    for h in range(SUB // HEAD_DIM):
        a = acc[:, h * HEAD_DIM:(h + 1) * HEAD_DIM]
        outs.append(a * cos + pltpu.roll(a, HEAD_DIM - ROPE_F, 1) * sa
                    + pltpu.roll(a, ROPE_F, 1) * sb)
    return jnp.concatenate(outs, axis=1)


def _project_groups(hx, w_refs, convw_ref, table_refs, out_refs, rope, prev_row=None,
                    edge_refs=None):
    wh_ref, wb_ref, wc_ref, wzc_ref, wq_ref, wk_ref, wv_ref, wzr_ref = w_refs
    yb_ref, zc_ref, q_ref, k_ref, v_ref, zr_ref = out_refs
    tm = hx.shape[0]

    def seg(w_ref):
        return jnp.dot(hx, w_ref[...], preferred_element_type=F32)

    ch = seg(wc_ref) * seg(wh_ref)
    ridx = lax.broadcasted_iota(jnp.int32, (tm, 1), 0)
    above = pltpu.roll(ch, 1, 0)
    above = jnp.where(ridx == 0, 0.0 if prev_row is None else prev_row, above)
    below = jnp.where(ridx == tm - 1, 0.0, pltpu.roll(ch, tm - 1, 0))
    cw = convw_ref[...]
    b = seg(wb_ref)
    yb_ref[...] = (b * (above * cw[0:1, :] + ch * cw[1:2, :] + below * cw[2:3, :])).astype(BF16)
    if edge_refs is not None:
        edge_refs[0][0] = b[tm - 8:tm, :]
        edge_refs[1][0] = ch[0:8, :]
    zc_ref[...] = _silu(seg(wzc_ref)).astype(BF16)
    if rope:
        cos, sa, sb = (t[...] for t in table_refs)
        q_ref[...] = _rope_pair(seg(wq_ref), cos, sa, sb).astype(BF16)
        k_ref[...] = _rope_pair(seg(wk_ref), cos * K_SCALE, sa * K_SCALE, sb * K_SCALE).astype(BF16)
    else:
        q_ref[...] = seg(wq_ref).astype(BF16)
        k_ref[...] = (seg(wk_ref) * K_SCALE).astype(BF16)
    v_ref[...] = seg(wv_ref).astype(BF16)
    zr_ref[...] = _silu(seg(wzr_ref)).astype(BF16)
    return ch[tm - 8:tm, :]


def _inproj_kernel(x_ref, nw_ref, shift_ref, scale_ref, *refs, rope):
    w_refs, convw_ref, table_refs = refs[:8], refs[8], refs[9:12]
    out_refs, hx_ref = refs[12:18], refs[18]

    @pl.when(pl.program_id(1) == 0)
    def _():
        _prologue(x_ref, nw_ref, shift_ref, scale_ref, hx_ref)

    _project_groups(hx_ref[...], w_refs, convw_ref, table_refs, out_refs, rope)


def _inproj_latent_kernel(xq_ref, nw_ref, shift_ref, scale_ref, *refs, tm, n_cast):
    w_refs, convw_ref, table_refs = refs[:8], refs[8], refs[9:12]
    cast_in_refs, refs = refs[12:12 + n_cast], refs[12 + n_cast:]
    out_refs, edge_refs, refs = refs[:6], refs[6:8], refs[8:]
    cast_out_refs, refs = refs[:n_cast], refs[n_cast:]
    hxa_ref, hxb_ref, carry_ref = refs
    i = pl.program_id(0)
    j = pl.program_id(1)
    quarter = tm // N_SUB
    row0 = pl.multiple_of(j * quarter, quarter)

    def project(hx_ref):
        prev_row = carry_ref[j][7:8, :]
        carry_ref[j] = _project_groups(hx_ref[...], w_refs, convw_ref, table_refs, out_refs,
                                       True, prev_row, edge_refs)

    def normalise(hx_ref):
        for src_ref, dst_ref in zip(cast_in_refs, cast_out_refs):
            dst_ref[...] = src_ref[...].astype(BF16)
        gain = nw_ref[...] * (1.0 + scale_ref[...])
        shift = shift_ref[...]
        for r in range(0, quarter, PROLOGUE_ROWS):
            x = xq_ref[r:r + PROLOGUE_ROWS, :]
            ms = jnp.mean(x * x, axis=-1, keepdims=True)
            hx_ref[pl.ds(row0 + r, PROLOGUE_ROWS), :] = (
                x * lax.rsqrt(ms + EPS) * gain + shift).astype(BF16)

    @pl.when(i == 0)
    def _():
        normalise(hxa_ref)
        carry_ref[j] = jnp.zeros(carry_ref.shape[1:], F32)

    @pl.when(i % 2 == 1)
    def _():
        normalise(hxb_ref)
        project(hxa_ref)

    @pl.when(jnp.logical_and(i > 0, i % 2 == 0))
    def _():
        normalise(hxa_ref)
        project(hxb_ref)


def _w_spec(group):
    return pl.BlockSpec((D_MODEL, SUB), lambda i, j: (0, group * N_SUB + j))


def _convw_spec(layer):
    return pl.BlockSpec((None, 3, SUB), lambda i, j: (layer, 0, j))


def _inproj(x, norm_w, shift, scale, w_bf16, conv_w, layer, tables, *, rope):
    tm, d = x.shape
    cos, sa, sb = tables
    kern = functools.partial(_inproj_kernel, rope=rope)
    vec = lambda: pl.BlockSpec((1, d), lambda i, j: (0, 0))
    tab = lambda: pl.BlockSpec((tm, HEAD_DIM), lambda i, j: (i, 0))
    out = lambda: pl.BlockSpec((tm, SUB), lambda i, j: (i, j))
    seg_shape = jax.ShapeDtypeStruct((tm, SEG), BF16)
    return pl.pallas_call(
        kern,
        grid=(1, N_SUB),
        in_specs=[
            pl.BlockSpec((tm, d), lambda i, j: (i, 0)),
            vec(), vec(), vec(),
            *[_w_spec(g) for g in range(N_SEG)],
            _convw_spec(layer),
            tab(), tab(), tab(),
        ],
        out_specs=[out() for _ in range(6)],
        out_shape=[seg_shape] * 6,
        scratch_shapes=[pltpu.VMEM((tm, d), BF16)],
        compiler_params=_params(56, 2),
    )(x, norm_w, shift, scale, *([w_bf16] * N_SEG), conv_w, cos, sa, sb)


def _inproj_latent(x, norm_w, shift, scale, w_bf16, conv_w, layer, tables, to_round, *, tm):
    rows, d = x.shape
    nt = rows // tm
    cos, sa, sb = tables
    kern = functools.partial(_inproj_latent_kernel, tm=tm, n_cast=len(to_round))
    done = lambda i: jnp.maximum(i - 1, 0)
    vec = lambda: pl.BlockSpec((1, d), lambda i, j: (0, 0))
    tab = lambda: pl.BlockSpec((tm, HEAD_DIM), lambda i, j: (done(i), 0))
    out = lambda: pl.BlockSpec((tm, SUB), lambda i, j: (done(i), jnp.where(i == 0, 0, j)))
    seg_shape = jax.ShapeDtypeStruct((rows, SEG), BF16)
    in_specs = [
        pl.BlockSpec((tm // N_SUB, d), lambda i, j: (jnp.minimum(i, nt - 1) * N_SUB + j, 0)),
        vec(), vec(), vec(),
        *[_w_spec(g) for g in range(N_SEG)],
        _convw_spec(layer),
        tab(), tab(), tab(),
    ]
    operands = [x, norm_w, shift, scale, *([w_bf16] * N_SEG), conv_w, cos, sa, sb]
    edge = lambda: pl.BlockSpec((1, 8, SUB), lambda i, j: (done(i), 0, jnp.where(i == 0, 0, j)))
    out_specs = [out() for _ in range(6)] + [edge(), edge()]
    out_shape = [seg_shape] * 6 + [jax.ShapeDtypeStruct((nt, 8, SEG), F32)] * 2
    n_blocks = W_CAST_BLOCKS
    assert n_blocks <= (nt + 1) * N_SUB
    blk = lambda i, j: jnp.minimum(i * N_SUB + j, n_blocks - 1)
    for w_all, which in to_round:
        n_rows, n_cols = w_all.shape[1:]
        assert n_rows % n_blocks == 0
        in_specs.append(pl.BlockSpec((None, n_rows // n_blocks, n_cols),
                                     lambda i, j, which=which: (which, blk(i, j), 0)))
        operands.append(w_all)
        out_specs.append(pl.BlockSpec((n_rows // n_blocks, n_cols), lambda i, j: (blk(i, j), 0)))
        out_shape.append(jax.ShapeDtypeStruct((n_rows, n_cols), BF16))
    res = pl.pallas_call(
        kern,
        grid=(nt + 1, N_SUB),
        in_specs=in_specs,
        out_specs=out_specs,
        out_shape=out_shape,
        scratch_shapes=[pltpu.VMEM((tm, d), BF16), pltpu.VMEM((tm, d), BF16),
                        pltpu.VMEM((N_SUB, 8, SUB), F32)],
        compiler_params=_params(56, 2),
    )(*operands)
    return res[:6], res[6:8], res[8:]


def _inproj_kv_kernel(x_ref, nw_ref, shift_ref, scale_ref, wk_ref, wv_ref, k_ref, v_ref, hx_ref):
    @pl.when(pl.program_id(1) == 0)
    def _():
        _prologue(x_ref, nw_ref, shift_ref, scale_ref, hx_ref)

    hx = hx_ref[...]
    k_ref[...] = (jnp.dot(hx, wk_ref[...], preferred_element_type=F32) * K_SCALE).astype(BF16)
    v_ref[...] = jnp.dot(hx, wv_ref[...], preferred_element_type=F32).astype(BF16)


def _inproj_kv(x, norm_w, shift, scale, w_bf16, *, tm):
    rows, d = x.shape
    vec = lambda: pl.BlockSpec((1, d), lambda i, j: (0, 0))
    out = lambda: pl.BlockSpec((tm, SUB), lambda i, j: (i, j))
    seg_shape = jax.ShapeDtypeStruct((rows, SEG), BF16)
    return pl.pallas_call(
        _inproj_kv_kernel,
        grid=(rows // tm, N_SUB),
        in_specs=[
            pl.BlockSpec((tm, d), lambda i, j: (i, 0)),
            vec(), vec(), vec(),
            _w_spec(G_K), _w_spec(G_V),
        ],
        out_specs=[out(), out()],
        out_shape=[seg_shape] * 2,
        scratch_shapes=[pltpu.VMEM((tm, d), BF16)],
        compiler_params=_params(32, 2),
    )(x, norm_w, shift, scale, w_bf16, w_bf16)


def _states_kernel(lgf_ref, lgb_ref, kf_ref, vf_ref, kb_ref, vb_ref, s0f_ref, s0b_ref,
                   sf_ref, sb_ref, ff_ref, fb_ref, *, nc):
    t = pl.program_id(0)

    @pl.when(t == 0)
    def _():
        ff_ref[...] = s0f_ref[...]
        fb_ref[...] = s0b_ref[...]

    row = lax.broadcasted_iota(jnp.int32, (CHUNK, HEAD_DIM), 0).astype(F32)
    tn = (((0,), (0,)), ((), ()))
    for h in range(RET_HEADS):
        hs = slice(h * HEAD_DIM, (h + 1) * HEAD_DIM)
        lgf = lgf_ref[h]
        lgb = lgb_ref[h]
        kdf = jnp.exp(lgf * (CHUNK - 1.0 - row))
        kdb = jnp.exp(lgb * row)
        cdf = jnp.exp(lgf * CHUNK + 0.0 * row)
        cdb = jnp.exp(lgb * CHUNK + 0.0 * row)

        s = ff_ref[h]
        for c in range(nc):
            rs = slice(c * CHUNK, (c + 1) * CHUNK)
            sf_ref[c, h] = s.astype(BF16)
            kd = (kf_ref[rs, hs].astype(F32) * kdf).astype(BF16)
            s = cdf * s + lax.dot_general(kd, vf_ref[rs, hs], tn, preferred_element_type=F32)
        ff_ref[h] = s

        s = fb_ref[h]
        for c in reversed(range(nc)):
            rs = slice(c * CHUNK, (c + 1) * CHUNK)
            sb_ref[c, h] = s.astype(BF16)
            kd = (kb_ref[rs, hs].astype(F32) * kdb).astype(BF16)
            s = cdb * s + lax.dot_general(kd, vb_ref[rs, hs], tn, preferred_element_type=F32)
        fb_ref[h] = s


def _states(lg_f, lg_b, k, v, s0f, s0b, *, tr):
    rows = k.shape[0]
    nt = rows // tr
    nc = tr // CHUNK
    n_chunks = rows // CHUNK
    kern = functools.partial(_states_kernel, nc=nc)
    smem = lambda: pl.BlockSpec(memory_space=pltpu.SMEM)
    st = lambda: pl.BlockSpec((RET_HEADS, HEAD_DIM, HEAD_DIM), lambda t: (0, 0, 0))
    fwd = lambda: pl.BlockSpec((tr, SEG), lambda t: (t, 0))
    bwd = lambda: pl.BlockSpec((tr, SEG), lambda t: (nt - 1 - t, 0))
    seq_shape = jax.ShapeDtypeStruct((n_chunks, RET_HEADS, HEAD_DIM, HEAD_DIM), BF16)
    fin_shape = jax.ShapeDtypeStruct((RET_HEADS, HEAD_DIM, HEAD_DIM), F32)
    return pl.pallas_call(
        kern,
        grid=(nt,),
        in_specs=[smem(), smem(), fwd(), fwd(), bwd(), bwd(), st(), st()],
        out_specs=[
            pl.BlockSpec((nc, RET_HEADS, HEAD_DIM, HEAD_DIM), lambda t: (t, 0, 0, 0)),
            pl.BlockSpec((nc, RET_HEADS, HEAD_DIM, HEAD_DIM), lambda t: (nt - 1 - t, 0, 0, 0)),
            st(), st(),
        ],
        out_shape=[seq_shape, seq_shape, fin_shape, fin_shape],
        compiler_params=_params(32, 1),
    )(lg_f, lg_b, k, v, k, v, s0f, s0b)


def _lane_means(ts):
    parts = []
    for t in ts:
        hi = t.astype(BF16)
        lo = (t - hi.astype(F32)).astype(BF16)
        parts.append(jnp.concatenate([hi, lo], axis=1))
    rows = ts[0].shape[0]
    ones = jnp.full((2 * HEAD_DIM, HEAD_DIM), 1.0 / HEAD_DIM, BF16)
    m = jnp.dot(jnp.concatenate(parts, axis=0), ones, preferred_element_type=F32)
    return [m[n * rows:(n + 1) * rows] for n in range(len(ts))]


def _out_kernel(lgf_ref, lgb_ref,
                ycv_ref, zc_ref, q_ref, k_ref, v_ref, zr_ref, eb_ref, ec_ref,
                sf_ref, sb_ref, x_ref, gate_ref, convw_ref, cnw_ref, gnw_ref, wout_ref, fnw_ref,
                o_ref,
                mask_ref, qdf_ref, qdb_ref, ya_ref, yb_ref, *, tm, n_tiles, edge_every, final):
    s_id = pl.program_id(0)
    i = jnp.maximum(s_id - 1, 0)
    nc = tm // CHUNK
    at_edge = jnp.logical_and((i + 1) % edge_every == 0, i + 1 < n_tiles)

    @pl.when(s_id == 0)
    def _():
        r = lax.broadcasted_iota(jnp.int32, (CHUNK, CHUNK), 0).astype(F32)
        cc = lax.broadcasted_iota(jnp.int32, (CHUNK, CHUNK), 1).astype(F32)
        d = r - cc
        for h in range(RET_HEADS):
            lgf = lgf_ref[h]
            lgb = lgb_ref[h]
            mf = jnp.where(d >= 0, jnp.exp(lgf * jnp.maximum(d, 0.0)), 0.0)
            mb = jnp.where(d <= 0, jnp.exp(lgb * jnp.maximum(-d, 0.0)), 0.0)
            mask_ref[h] = mf + mb
            qdf_ref[h] = jnp.exp(lgf * (r + 1.0))
            qdb_ref[h] = jnp.exp(lgb * (CHUNK - r))

    def step(yr_old, yr_new):
        n_q = 4
        cw_ = D_CONV // n_q
        pw = D_MODEL // n_q

        if yr_old is not None:
            last_row = lax.broadcasted_iota(jnp.int32, (8, 1), 0) == 7
            cwt = convw_ref[0]
            proj_ret, y_l, ssq = [], [], None
            for n in range(n_q):
                proj_ret.append(jnp.dot(yr_old[...], wout_ref[D_CONV:, n * pw:(n + 1) * pw],
                                        preferred_element_type=F32))
                cs = slice(n * cw_, (n + 1) * cw_)
                y = ycv_ref[:, cs].astype(F32)
                below = eb_ref[0, 7:8, cs] * cwt[2:3, cs] * ec_ref[0, 0:1, cs]
                below = jnp.where(jnp.logical_and(at_edge, last_row), below, 0.0)
                y = jnp.concatenate([y[:tm - 8], y[tm - 8:] + below], axis=0)
                y_l.append(y)
                part = jnp.sum(y * y, axis=-1, keepdims=True)
                ssq = part if ssq is None else ssq + part
            rinv = lax.rsqrt(ssq * (1.0 / D_CONV) + EPS)

            y_conv = jnp.concatenate(
                [(zc_ref[:, n * cw_:(n + 1) * cw_].astype(F32)
                  * (y_l[n] * rinv * cnw_ref[:, n * cw_:(n + 1) * cw_])).astype(BF16)
                 for n in range(n_q)], axis=1)

        pending = list(range(n_q)) if yr_old is not None else []

        def finish_chunk():
            if not pending:
                return
            n = pending.pop(0)
            cols = slice(n * pw, (n + 1) * pw)
            pc = jnp.dot(y_conv, wout_ref[:D_CONV, cols], preferred_element_type=F32)
            o_ref[:, cols] = x_ref[:, cols] + gate_ref[:, cols] * (pc + proj_ret[n])

        nt = (((1,), (1,)), ((), ()))
        rs = lambda c: slice(c * CHUNK, (c + 1) * CHUNK)
        hs = lambda h: slice(h * HEAD_DIM, (h + 1) * HEAD_DIM)
        rounds = range(0, nc, RET_ROUND_CHUNKS) if yr_new is not None else []
        per_round = -(-n_q // max(len(rounds), 1) // 2)
        for c0 in rounds:
            pairs = [(c, h) for c in range(c0, c0 + RET_ROUND_CHUNKS) for h in range(RET_HEADS)]
            q_l = [q_ref[rs(c), hs(h)] for c, h in pairs]
            s_l = [lax.dot_general(q, k_ref[rs(c), hs(h)], nt, preferred_element_type=F32)
                   for q, (c, h) in zip(q_l, pairs)]
            for _ in range(per_round):
                finish_chunk()
            lhs_l = []
            for q, s, (c, h) in zip(q_l, s_l, pairs):
                qf32 = q.astype(F32)
                lhs_l.append(jnp.concatenate(
                    [(s * mask_ref[h]).astype(BF16), (qf32 * qdf_ref[h]).astype(BF16),
                     (qf32 * qdb_ref[h]).astype(BF16)], axis=1))
            o_l = [jnp.dot(lhs, jnp.concatenate([v_ref[rs(c), hs(h)], sf_ref[c, h],
                                                 sb_ref[c, h]], axis=0),
                           preferred_element_type=F32)
                   for lhs, (c, h) in zip(lhs_l, pairs)]
            for _ in range(per_round):
                finish_chunk()
            per_chunk = lambda xs: [xs[n * RET_HEADS:(n + 1) * RET_HEADS]
                                    for n in range(len(xs) // RET_HEADS)]
            mean_l = [m for grp in per_chunk(o_l) for m in _lane_means(grp)]
            oc_l = [o - m for o, m in zip(o_l, mean_l)]
            var_l = [m for grp in per_chunk([oc * oc for oc in oc_l]) for m in _lane_means(grp)]
            for oc, var, (c, h) in zip(oc_l, var_l, pairs):
                on = oc * lax.rsqrt(var + EPS) * gnw_ref[:, hs(h)]
                yr_new[rs(c), hs(h)] = (zr_ref[rs(c), hs(h)].astype(F32) * on).astype(BF16)
        while pending:
            finish_chunk()
        if final and yr_old is not None:
            xn = o_ref[...]
            ms = jnp.mean(xn * xn, axis=-1, keepdims=True)
            o_ref[...] = xn * lax.rsqrt(ms + EPS) * fnw_ref[...]

    middle = jnp.logical_and(s_id > 0, s_id < n_tiles)

    @pl.when(s_id == 0)
    def _():
        step(None, ya_ref)

    @pl.when(jnp.logical_and(middle, s_id % 2 == 0))
    def _():
        step(yb_ref, ya_ref)

    @pl.when(jnp.logical_and(middle, s_id % 2 == 1))
    def _():
        step(ya_ref, yb_ref)

    @pl.when(s_id == n_tiles)
    def _():
        step(ya_ref if n_tiles % 2 == 1 else yb_ref, None)


def _out(lg_f, lg_b, segs, edges, sf, sb, x, gate, conv_w, cnw, gnw, w_out, fnw, layer, *,
         tm, final):
    ycv, zc, q, k, v, zr = segs
    eb, ec = edges
    rows, d = x.shape
    nt = rows // tm
    nc = tm // CHUNK
    n_edge = eb.shape[0]
    edge_every = nt // n_edge
    kern = functools.partial(_out_kernel, tm=tm, n_tiles=nt, edge_every=edge_every, final=final)
    smem = lambda: pl.BlockSpec(memory_space=pltpu.SMEM)
    cur = lambda s: jnp.minimum(s, nt - 1)
    prv = lambda s: jnp.maximum(s - 1, 0)
    seg = lambda: pl.BlockSpec((tm, SEG), lambda s: (cur(s), 0))
    fin = lambda: pl.BlockSpec((tm, SEG), lambda s: (prv(s), 0))
    st = lambda: pl.BlockSpec((nc, RET_HEADS, HEAD_DIM, HEAD_DIM), lambda s: (cur(s), 0, 0, 0))
    full = lambda shape: pl.BlockSpec(shape, lambda s: (0,) * len(shape))
    return pl.pallas_call(
        kern,
        grid=(nt + 1,),
        in_specs=[
            smem(), smem(),
            fin(), fin(), seg(), seg(), seg(), seg(),
            pl.BlockSpec((1, 8, SEG), lambda s: (prv(s) // edge_every, 0, 0)),
            pl.BlockSpec((1, 8, SEG),
                         lambda s: (jnp.minimum(prv(s) // edge_every + 1, n_edge - 1), 0, 0)),
            st(), st(),
            pl.BlockSpec((tm, d), lambda s: (prv(s), 0)),
            full((1, d)),
            pl.BlockSpec((1, 3, D_CONV), lambda s: (layer, 0, 0)),
            full((1, D_CONV)), full((1, D_RET)),
            pl.BlockSpec((d, d), lambda s: (0, 0), pipeline_mode=pl.Buffered(1)),
            full((1, d)),
        ],
        out_specs=pl.BlockSpec((tm, d), lambda s: (prv(s), 0)),
        out_shape=jax.ShapeDtypeStruct((rows, d), F32),
        scratch_shapes=[
            pltpu.VMEM((RET_HEADS, CHUNK, CHUNK), F32),
            pltpu.VMEM((RET_HEADS, CHUNK, HEAD_DIM), F32),
            pltpu.VMEM((RET_HEADS, CHUNK, HEAD_DIM), F32),
            pltpu.VMEM((tm, D_RET), BF16),
            pltpu.VMEM((tm, D_RET), BF16),
        ],
        compiler_params=_params(56, 1),
    )(lg_f, lg_b, ycv, zc, q, k, v, zr, eb, ec, sf, sb, x, gate, conv_w, cnw, gnw,
      w_out, fnw)


def _rope_tables(seq):
    t = np.arange(seq)
    row = (t // GRID_W).astype(np.float64)
    col = (t % GRID_W).astype(np.float64)
    inv = ROPE_BASE ** (-np.arange(ROPE_F, dtype=np.float64) / ROPE_F)
    ar = row[:, None] * inv[None, :]
    ac = col[:, None] * inv[None, :]
    z = np.zeros_like(ar)
    cos = np.concatenate([np.cos(ar), np.cos(ar), np.cos(ac), np.cos(ac)], axis=-1)
    sa = np.concatenate([-np.sin(ar), z, -np.sin(ac), z], axis=-1)
    sb = np.concatenate([z, np.sin(ar), z, np.sin(ac)], axis=-1)
    return tuple(jnp.asarray(a, dtype=F32) for a in (cos, sa, sb))


def kernel(x, c, ctx, c_ctx, norm_w, w_mod, b_mod, w_in, conv_w, conv_norm_w, ret_norm_w,
           ret_decay_f, ret_decay_b, w_out, final_norm_w):
    batch, seq, d = x.shape
    assert batch == 1 and d == D_MODEL and seq % 1024 == 0
    depth = norm_w.shape[0]
    ctx_len = ctx.shape[1]
    xs = x[0]
    cs = ctx[0]

    cv = jnp.zeros((8, d), F32).at[0].set(c[0]).at[1].set(c_ctx)
    mod = _modulation(cv, w_mod, b_mod)
    tables = _rope_tables(seq)
    ctx_tables = tuple(jnp.zeros((ctx_len, HEAD_DIM), F32) for _ in range(3))
    zero_state = jnp.zeros((RET_HEADS, HEAD_DIM, HEAD_DIM), F32)
    zero_edge = jnp.zeros((1, 8, D_CONV), F32)
    fnw = final_norm_w.reshape(1, d)
    w_in_b = w_in[0].astype(BF16)

    for layer in range(depth):
        update_ctx = layer < depth - 1
        lg_f = -jnp.exp(ret_decay_f[layer].astype(F32))
        lg_b = -jnp.exp(ret_decay_b[layer].astype(F32))
        nw = norm_w[layer].reshape(1, d)
        m = mod[layer]
        shift, scale, gate = m[0:1, 0:d], m[0:1, d:2 * d], m[0:1, 2 * d:3 * d]
        shift_c, scale_c, gate_c = m[1:2, 0:d], m[1:2, d:2 * d], m[1:2, 2 * d:3 * d]
        cnw = conv_norm_w[layer].reshape(1, D_CONV)
        gnw = ret_norm_w[layer].reshape(1, D_RET)

        if update_ctx:
            segs_c = _inproj(cs, nw, shift_c, scale_c, w_in_b, conv_w, layer, ctx_tables,
                             rope=False)
            k_c, v_c = segs_c[3], segs_c[4]
        else:
            k_c, v_c = _inproj_kv(cs, nw, shift_c, scale_c, w_in_b, tm=ctx_len)
        sf_c, sb_c, s_f, s_b = _states(lg_f, lg_b, k_c, v_c, zero_state, zero_state, tr=ctx_len)

        to_round = [(w_out, layer)] + ([(w_in, layer + 1)] if layer + 1 < depth else [])
        segs, edges, rounded = _inproj_latent(xs, nw, shift, scale, w_in_b, conv_w, layer, tables,
                                              to_round, tm=1024)
        w_out_b = rounded[0]
        w_in_b = rounded[1] if layer + 1 < depth else None
        sf, sb, _, _ = _states(lg_f, lg_b, segs[3], segs[4], s_f, s_b, tr=1024)
        xs = _out(lg_f, lg_b, segs, edges, sf, sb, xs, gate, conv_w, cnw, gnw, w_out_b, fnw, layer,
                  tm=512, final=not update_ctx)
        if update_ctx:
            cs = _out(lg_f, lg_b, segs_c, (zero_edge, zero_edge), sf_c, sb_c, cs, gate_c, conv_w,
                      cnw, gnw, w_out_b, fnw, layer, tm=ctx_len, final=False)
    return xs[None]
```

```python
import functools

import numpy as np
import jax
import jax.numpy as jnp
from jax import lax
from jax.experimental import pallas as pl
from jax.experimental.pallas import tpu as pltpu

D_MODEL = 2048
D_CONV = 1024
D_RET = 1024
RET_HEADS = 8
HEAD_DIM = 128
CHUNK = 128
SEG = 1024
N_SEG = 8
SUB = 256
N_SUB = SEG // SUB
GRID_W = 64
ROPE_BASE = 10000.0
ROPE_F = 32
EPS = 1e-6
K_SCALE = HEAD_DIM ** -0.5

F32 = jnp.float32
BF16 = jnp.bfloat16

G_H, G_B, G_C, G_ZC, G_Q, G_K, G_V, G_ZR = range(8)


def _silu(x):
    return x / (1.0 + jnp.exp(-x))


def _params(vmem_mb, n_axes, flags=None):
    return pltpu.CompilerParams(
        dimension_semantics=("arbitrary",) * n_axes,
        vmem_limit_bytes=vmem_mb * 1024 * 1024,
        flags=flags,
    )


def _mod_kernel(cv_ref, w_ref, b_ref, o_ref):
    s = _silu(cv_ref[...])
    o_ref[0] = jnp.dot(s.astype(BF16), w_ref[0].astype(BF16),
                       preferred_element_type=F32) + b_ref[0]


def _modulation(cv, w_mod, b_mod, tn=1024):
    depth, d, n = w_mod.shape
    return pl.pallas_call(
        _mod_kernel,
        grid=(depth, n // tn),
        in_specs=[
            pl.BlockSpec((8, d), lambda l, j: (0, 0)),
            pl.BlockSpec((1, d, tn), lambda l, j: (l, 0, j)),
            pl.BlockSpec((1, 1, tn), lambda l, j: (l, 0, j)),
        ],
        out_specs=pl.BlockSpec((1, 8, tn), lambda l, j: (l, 0, j)),
        out_shape=jax.ShapeDtypeStruct((depth, 8, n), F32),
        compiler_params=_params(40, 2),
    )(cv, w_mod, b_mod.reshape(depth, 1, n))


PROLOGUE_ROWS = 64
RET_ROUND_CHUNKS = 2
W_CAST_BLOCKS = 32


def _prologue(x_ref, nw_ref, shift_ref, scale_ref, hx_ref):
    gain = nw_ref[...] * (1.0 + scale_ref[...])
    shift = shift_ref[...]

    def body(r, carry):
        rows = pl.ds(pl.multiple_of(r * PROLOGUE_ROWS, PROLOGUE_ROWS), PROLOGUE_ROWS)
        x = x_ref[rows, :]
        ms = jnp.mean(x * x, axis=-1, keepdims=True)
        hx_ref[rows, :] = (x * lax.rsqrt(ms + EPS) * gain + shift).astype(BF16)
        return carry

    lax.fori_loop(0, x_ref.shape[0] // PROLOGUE_ROWS, body, 0)


def _rope_pair(acc, cos, sa, sb):
    outs = []
    for h in range(SUB // HEAD_DIM):
        a = acc[:, h * HEAD_DIM:(h + 1) * HEAD_DIM]
        outs.append(a * cos + pltpu.roll(a, HEAD_DIM - ROPE_F, 1) * sa
                    + pltpu.roll(a, ROPE_F, 1) * sb)
    return jnp.concatenate(outs, axis=1)


def _project_groups(hx, w_refs, convw_ref, table_refs, out_refs, rope, prev_row=None,
                    edge_refs=None):
    wh_ref, wb_ref, wc_ref, wzc_ref, wq_ref, wk_ref, wv_ref, wzr_ref = w_refs
    yb_ref, zc_ref, q_ref, k_ref, v_ref, zr_ref = out_refs
    tm = hx.shape[0]

    def seg(w_ref):
        return jnp.dot(hx, w_ref[...], preferred_element_type=F32)

    ch = seg(wc_ref) * seg(wh_ref)
    ridx = lax.broadcasted_iota(jnp.int32, (tm, 1), 0)
    above = pltpu.roll(ch, 1, 0)
    above = jnp.where(ridx == 0, 0.0 if prev_row is None else prev_row, above)
    below = jnp.where(ridx == tm - 1, 0.0, pltpu.roll(ch, tm - 1, 0))
    cw = convw_ref[...]
    b = seg(wb_ref)
    yb_ref[...] = (b * (above * cw[0:1, :] + ch * cw[1:2, :] + below * cw[2:3, :])).astype(BF16)
    if edge_refs is not None:
        edge_refs[0][0] = b[tm - 8:tm, :]
        edge_refs[1][0] = ch[0:8, :]
    zc_ref[...] = _silu(seg(wzc_ref)).astype(BF16)
    if rope:
        cos, sa, sb = (t[...] for t in table_refs)
        q_ref[...] = _rope_pair(seg(wq_ref), cos, sa, sb).astype(BF16)
        k_ref[...] = _rope_pair(seg(wk_ref), cos * K_SCALE, sa * K_SCALE, sb * K_SCALE).astype(BF16)
    else:
        q_ref[...] = seg(wq_ref).astype(BF16)
        k_ref[...] = (seg(wk_ref) * K_SCALE).astype(BF16)
    v_ref[...] = seg(wv_ref).astype(BF16)
    zr_ref[...] = _silu(seg(wzr_ref)).astype(BF16)
    return ch[tm - 8:tm, :]


def _inproj_kernel(x_ref, nw_ref, shift_ref, scale_ref, *refs, rope, round_weights):
    w_refs, convw_ref, table_refs = refs[:8], refs[8], refs[9:12]
    out_refs, refs = refs[12:18], refs[18:]
    if round_weights:
        for w_ref, wb_ref in zip(w_refs, refs[:8]):
            wb_ref[...] = w_ref[...].astype(BF16)
        w_refs, refs = refs[:8], refs[8:]
    hx_ref, = refs

    @pl.when(pl.program_id(1) == 0)
    def _():
        _prologue(x_ref, nw_ref, shift_ref, scale_ref, hx_ref)

    _project_groups(hx_ref[...], w_refs, convw_ref, table_refs, out_refs, rope)


def _inproj_latent_kernel(xq_ref, nw_ref, shift_ref, scale_ref, *refs, tm, n_cast):
    w_refs, convw_ref, table_refs = refs[:8], refs[8], refs[9:12]
    cast_in_refs, refs = refs[12:12 + n_cast], refs[12 + n_cast:]
    out_refs, edge_refs, refs = refs[:6], refs[6:8], refs[8:]
    cast_out_refs, refs = refs[:n_cast], refs[n_cast:]
    hxa_ref, hxb_ref, carry_ref = refs
    i = pl.program_id(0)
    j = pl.program_id(1)
    quarter = tm // N_SUB
    row0 = pl.multiple_of(j * quarter, quarter)

    def project(hx_ref):
        prev_row = carry_ref[j][7:8, :]
        carry_ref[j] = _project_groups(hx_ref[...], w_refs, convw_ref, table_refs, out_refs,
                                       True, prev_row, edge_refs)

    def normalise(hx_ref):
        for src_ref, dst_ref in zip(cast_in_refs, cast_out_refs):
            dst_ref[...] = src_ref[...].astype(BF16)
        gain = nw_ref[...] * (1.0 + scale_ref[...])
        shift = shift_ref[...]
        for r in range(0, quarter, PROLOGUE_ROWS):
            x = xq_ref[r:r + PROLOGUE_ROWS, :]
            ms = jnp.mean(x * x, axis=-1, keepdims=True)
            hx_ref[pl.ds(row0 + r, PROLOGUE_ROWS), :] = (
                x * lax.rsqrt(ms + EPS) * gain + shift).astype(BF16)

    @pl.when(i == 0)
    def _():
        normalise(hxa_ref)
        carry_ref[j] = jnp.zeros(carry_ref.shape[1:], F32)

    @pl.when(i % 2 == 1)
    def _():
        normalise(hxb_ref)
        project(hxa_ref)

    @pl.when(jnp.logical_and(i > 0, i % 2 == 0))
    def _():
        normalise(hxa_ref)
        project(hxb_ref)


def _w_spec(first_block):
    return pl.BlockSpec((D_MODEL, SUB), lambda i, j: (0, first_block + j))


def _split_groups(w_bf16):
    return [(w_bf16, g * N_SUB) for g in range(N_SEG)]


def _convw_spec(layer):
    return pl.BlockSpec((None, 3, SUB), lambda i, j: (layer, 0, j))


def _inproj(x, norm_w, shift, scale, w_groups, conv_w, layer, tables, *, rope, w_f32=None):
    tm, d = x.shape
    cos, sa, sb = tables
    round_weights = w_f32 is not None
    kern = functools.partial(_inproj_kernel, rope=rope, round_weights=round_weights)
    vec = lambda: pl.BlockSpec((1, d), lambda i, j: (0, 0))
    tab = lambda: pl.BlockSpec((tm, HEAD_DIM), lambda i, j: (i, 0))
    out = lambda: pl.BlockSpec((tm, SUB), lambda i, j: (i, j))
    seg_shape = jax.ShapeDtypeStruct((tm, SEG), BF16)
    out_specs = [out() for _ in range(6)]
    out_shape = [seg_shape] * 6
    if round_weights:
        w_specs = [pl.BlockSpec((None, d, SUB), lambda i, j, g=g: (layer, 0, g * N_SUB + j))
                   for g in range(N_SEG)]
        w_operands = [w_f32] * N_SEG
        out_specs += [pl.BlockSpec((d, SUB), lambda i, j: (0, j)) for _ in range(N_SEG)]
        out_shape += [jax.ShapeDtypeStruct((d, SEG), BF16)] * N_SEG
    else:
        w_specs = [_w_spec(first) for _, first in w_groups]
        w_operands = [w for w, _ in w_groups]
    res = pl.pallas_call(
        kern,
        grid=(1, N_SUB),
        in_specs=[
            pl.BlockSpec((tm, d), lambda i, j: (i, 0)),
            vec(), vec(), vec(),
            *w_specs,
            _convw_spec(layer),
            tab(), tab(), tab(),
        ],
        out_specs=out_specs,
        out_shape=out_shape,
        scratch_shapes=[pltpu.VMEM((tm, d), BF16)],
        compiler_params=_params(56, 2),
    )(x, norm_w, shift, scale, *w_operands, conv_w, cos, sa, sb)
    return res[:6], ([(w, 0) for w in res[6:]] if round_weights else w_groups)


def _inproj_latent(x, norm_w, shift, scale, w_groups, conv_w, layer, tables, to_round, *, tm):
    rows, d = x.shape
    nt = rows // tm
    cos, sa, sb = tables
    kern = functools.partial(_inproj_latent_kernel, tm=tm, n_cast=len(to_round))
    done = lambda i: jnp.maximum(i - 1, 0)
    vec = lambda: pl.BlockSpec((1, d), lambda i, j: (0, 0))
    tab = lambda: pl.BlockSpec((tm, HEAD_DIM), lambda i, j: (done(i), 0))
    out = lambda: pl.BlockSpec((tm, SUB), lambda i, j: (done(i), jnp.where(i == 0, 0, j)))
    seg_shape = jax.ShapeDtypeStruct((rows, SEG), BF16)
    in_specs = [
        pl.BlockSpec((tm // N_SUB, d), lambda i, j: (jnp.minimum(i, nt - 1) * N_SUB + j, 0)),
        vec(), vec(), vec(),
        *[_w_spec(first) for _, first in w_groups],
        _convw_spec(layer),
        tab(), tab(), tab(),
    ]
    operands = [x, norm_w, shift, scale, *[w for w, _ in w_groups], conv_w, cos, sa, sb]
    edge = lambda: pl.BlockSpec((1, 8, SUB), lambda i, j: (done(i), 0, jnp.where(i == 0, 0, j)))
    out_specs = [out() for _ in range(6)] + [edge(), edge()]
    out_shape = [seg_shape] * 6 + [jax.ShapeDtypeStruct((nt, 8, SEG), F32)] * 2
    n_blocks = W_CAST_BLOCKS
    assert n_blocks <= (nt + 1) * N_SUB
    blk = lambda i, j: jnp.minimum(i * N_SUB + j, n_blocks - 1)
    for w_all, which in to_round:
        n_rows, n_cols = w_all.shape[1:]
        assert n_rows % n_blocks == 0
        in_specs.append(pl.BlockSpec((None, n_rows // n_blocks, n_cols),
                                     lambda i, j, which=which: (which, blk(i, j), 0)))
        operands.append(w_all)
        out_specs.append(pl.BlockSpec((n_rows // n_blocks, n_cols), lambda i, j: (blk(i, j), 0)))
        out_shape.append(jax.ShapeDtypeStruct((n_rows, n_cols), BF16))
    res = pl.pallas_call(
        kern,
        grid=(nt + 1, N_SUB),
        in_specs=in_specs,
        out_specs=out_specs,
        out_shape=out_shape,
        scratch_shapes=[pltpu.VMEM((tm, d), BF16), pltpu.VMEM((tm, d), BF16),
                        pltpu.VMEM((N_SUB, 8, SUB), F32)],
        compiler_params=_params(56, 2),
    )(*operands)
    return res[:6], res[6:8], res[8:]


def _inproj_kv_kernel(x_ref, nw_ref, shift_ref, scale_ref, wk_ref, wv_ref, k_ref, v_ref, hx_ref):
    @pl.when(pl.program_id(1) == 0)
    def _():
        _prologue(x_ref, nw_ref, shift_ref, scale_ref, hx_ref)

    hx = hx_ref[...]
    k_ref[...] = (jnp.dot(hx, wk_ref[...], preferred_element_type=F32) * K_SCALE).astype(BF16)
    v_ref[...] = jnp.dot(hx, wv_ref[...], preferred_element_type=F32).astype(BF16)


def _inproj_kv(x, norm_w, shift, scale, w_groups, *, tm):
    rows, d = x.shape
    vec = lambda: pl.BlockSpec((1, d), lambda i, j: (0, 0))
    out = lambda: pl.BlockSpec((tm, SUB), lambda i, j: (i, j))
    seg_shape = jax.ShapeDtypeStruct((rows, SEG), BF16)
    return pl.pallas_call(
        _inproj_kv_kernel,
        grid=(rows // tm, N_SUB),
        in_specs=[
            pl.BlockSpec((tm, d), lambda i, j: (i, 0)),
            vec(), vec(), vec(),
            _w_spec(w_groups[G_K][1]), _w_spec(w_groups[G_V][1]),
        ],
        out_specs=[out(), out()],
        out_shape=[seg_shape] * 2,
        scratch_shapes=[pltpu.VMEM((tm, d), BF16)],
        compiler_params=_params(32, 2),
    )(x, norm_w, shift, scale, w_groups[G_K][0], w_groups[G_V][0])


def _states_kernel(lgf_ref, lgb_ref, kf_ref, vf_ref, kb_ref, vb_ref, s0f_ref, s0b_ref,
                   sf_ref, sb_ref, ff_ref, fb_ref, *, nc):
    t = pl.program_id(0)

    @pl.when(t == 0)
    def _():
        ff_ref[...] = s0f_ref[...]
        fb_ref[...] = s0b_ref[...]

    row = lax.broadcasted_iota(jnp.int32, (CHUNK, HEAD_DIM), 0).astype(F32)
    tn = (((0,), (0,)), ((), ()))
    for h in range(RET_HEADS):
        hs = slice(h * HEAD_DIM, (h + 1) * HEAD_DIM)
        lgf = lgf_ref[h]
        lgb = lgb_ref[h]
        kdf = jnp.exp(lgf * (CHUNK - 1.0 - row))
        kdb = jnp.exp(lgb * row)
        cdf = jnp.exp(lgf * CHUNK + 0.0 * row)
        cdb = jnp.exp(lgb * CHUNK + 0.0 * row)

        s = ff_ref[h]
        for c in range(nc):
            rs = slice(c * CHUNK, (c + 1) * CHUNK)
            sf_ref[c, h] = s.astype(BF16)
            kd = (kf_ref[rs, hs].astype(F32) * kdf).astype(BF16)
            s = cdf * s + lax.dot_general(kd, vf_ref[rs, hs], tn, preferred_element_type=F32)
        ff_ref[h] = s

        s = fb_ref[h]
        for c in reversed(range(nc)):
            rs = slice(c * CHUNK, (c + 1) * CHUNK)
            sb_ref[c, h] = s.astype(BF16)
            kd = (kb_ref[rs, hs].astype(F32) * kdb).astype(BF16)
            s = cdb * s + lax.dot_general(kd, vb_ref[rs, hs], tn, preferred_element_type=F32)
        fb_ref[h] = s


def _states(lg_f, lg_b, k, v, s0f, s0b, *, tr):
    rows = k.shape[0]
    nt = rows // tr
    nc = tr // CHUNK
    n_chunks = rows // CHUNK
    kern = functools.partial(_states_kernel, nc=nc)
    smem = lambda: pl.BlockSpec(memory_space=pltpu.SMEM)
    st = lambda: pl.BlockSpec((RET_HEADS, HEAD_DIM, HEAD_DIM), lambda t: (0, 0, 0))
    fwd = lambda: pl.BlockSpec((tr, SEG), lambda t: (t, 0))
    bwd = lambda: pl.BlockSpec((tr, SEG), lambda t: (nt - 1 - t, 0))
    seq_shape = jax.ShapeDtypeStruct((n_chunks, RET_HEADS, HEAD_DIM, HEAD_DIM), BF16)
    fin_shape = jax.ShapeDtypeStruct((RET_HEADS, HEAD_DIM, HEAD_DIM), F32)
    return pl.pallas_call(
        kern,
        grid=(nt,),
        in_specs=[smem(), smem(), fwd(), fwd(), bwd(), bwd(), st(), st()],
        out_specs=[
            pl.BlockSpec((nc, RET_HEADS, HEAD_DIM, HEAD_DIM), lambda t: (t, 0, 0, 0)),
            pl.BlockSpec((nc, RET_HEADS, HEAD_DIM, HEAD_DIM), lambda t: (nt - 1 - t, 0, 0, 0)),
            st(), st(),
        ],
        out_shape=[seq_shape, seq_shape, fin_shape, fin_shape],
        compiler_params=_params(32, 1),
    )(lg_f, lg_b, k, v, k, v, s0f, s0b)


def _lane_means(ts):
    parts = []
    for t in ts:
        hi = t.astype(BF16)
        lo = (t - hi.astype(F32)).astype(BF16)
        parts.append(jnp.concatenate([hi, lo], axis=1))
    rows = ts[0].shape[0]
    ones = jnp.full((2 * HEAD_DIM, HEAD_DIM), 1.0 / HEAD_DIM, BF16)
    m = jnp.dot(jnp.concatenate(parts, axis=0), ones, preferred_element_type=F32)
    return [m[n * rows:(n + 1) * rows] for n in range(len(ts))]


def _out_kernel(lgf_ref, lgb_ref,
                ycv_ref, zc_ref, q_ref, k_ref, v_ref, zr_ref, eb_ref, ec_ref,
                sf_ref, sb_ref, x_ref, gate_ref, convw_ref, cnw_ref, gnw_ref, wout_ref, fnw_ref,
                o_ref,
                mask_ref, qdf_ref, qdb_ref, ya_ref, yb_ref, *, tm, n_tiles, edge_every, final):
    s_id = pl.program_id(0)
    i = jnp.maximum(s_id - 1, 0)
    nc = tm // CHUNK
    at_edge = jnp.logical_and((i + 1) % edge_every == 0, i + 1 < n_tiles)

    @pl.when(s_id == 0)
    def _():
        r = lax.broadcasted_iota(jnp.int32, (CHUNK, CHUNK), 0).astype(F32)
        cc = lax.broadcasted_iota(jnp.int32, (CHUNK, CHUNK), 1).astype(F32)
        d = r - cc
        for h in range(RET_HEADS):
            lgf = lgf_ref[h]
            lgb = lgb_ref[h]
            mf = jnp.where(d >= 0, jnp.exp(lgf * jnp.maximum(d, 0.0)), 0.0)
            mb = jnp.where(d <= 0, jnp.exp(lgb * jnp.maximum(-d, 0.0)), 0.0)
            mask_ref[h] = mf + mb
            qdf_ref[h] = jnp.exp(lgf * (r + 1.0))
            qdb_ref[h] = jnp.exp(lgb * (CHUNK - r))

    def step(yr_old, yr_new):
        n_q = 4
        cw_ = D_CONV // n_q
        pw = D_MODEL // n_q

        if yr_old is not None:
            last_row = lax.broadcasted_iota(jnp.int32, (8, 1), 0) == 7
            cwt = convw_ref[0]
            proj_ret, y_l, ssq = [], [], None
            for n in range(n_q):
                proj_ret.append(jnp.dot(yr_old[...], wout_ref[D_CONV:, n * pw:(n + 1) * pw],
                                        preferred_element_type=F32))
                cs = slice(n * cw_, (n + 1) * cw_)
                y = ycv_ref[:, cs].astype(F32)
                below = eb_ref[0, 7:8, cs] * cwt[2:3, cs] * ec_ref[0, 0:1, cs]
                below = jnp.where(jnp.logical_and(at_edge, last_row), below, 0.0)
                y = jnp.concatenate([y[:tm - 8], y[tm - 8:] + below], axis=0)
                y_l.append(y)
                part = jnp.sum(y * y, axis=-1, keepdims=True)
                ssq = part if ssq is None else ssq + part
            rinv = lax.rsqrt(ssq * (1.0 / D_CONV) + EPS)

            y_conv = jnp.concatenate(
                [(zc_ref[:, n * cw_:(n + 1) * cw_].astype(F32)
                  * (y_l[n] * rinv * cnw_ref[:, n * cw_:(n + 1) * cw_])).astype(BF16)
                 for n in range(n_q)], axis=1)

        pending = list(range(n_q)) if yr_old is not None else []

        def finish_chunk():
            if not pending:
                return
            n = pending.pop(0)
            cols = slice(n * pw, (n + 1) * pw)
            pc = jnp.dot(y_conv, wout_ref[:D_CONV, cols], preferred_element_type=F32)
            o_ref[:, cols] = x_ref[:, cols] + gate_ref[:, cols] * (pc + proj_ret[n])

        nt = (((1,), (1,)), ((), ()))
        rs = lambda c: slice(c * CHUNK, (c + 1) * CHUNK)
        hs = lambda h: slice(h * HEAD_DIM, (h + 1) * HEAD_DIM)
        rounds = range(0, nc, RET_ROUND_CHUNKS) if yr_new is not None else []
        per_round = -(-n_q // max(len(rounds), 1) // 2)
        for c0 in rounds:
            pairs = [(c, h) for c in range(c0, c0 + RET_ROUND_CHUNKS) for h in range(RET_HEADS)]
            q_l = [q_ref[rs(c), hs(h)] for c, h in pairs]
            s_l = [lax.dot_general(q, k_ref[rs(c), hs(h)], nt, preferred_element_type=F32)
                   for q, (c, h) in zip(q_l, pairs)]
            for _ in range(per_round):
                finish_chunk()
            lhs_l = []
            for q, s, (c, h) in zip(q_l, s_l, pairs):
                qf32 = q.astype(F32)
                lhs_l.append(jnp.concatenate(
                    [(s * mask_ref[h]).astype(BF16), (qf32 * qdf_ref[h]).astype(BF16),
                     (qf32 * qdb_ref[h]).astype(BF16)], axis=1))
            o_l = [jnp.dot(lhs, jnp.concatenate([v_ref[rs(c), hs(h)], sf_ref[c, h],
                                                 sb_ref[c, h]], axis=0),
                           preferred_element_type=F32)
                   for lhs, (c, h) in zip(lhs_l, pairs)]
            for _ in range(per_round):
                finish_chunk()
            per_chunk = lambda xs: [xs[n * RET_HEADS:(n + 1) * RET_HEADS]
                                    for n in range(len(xs) // RET_HEADS)]
            mean_l = [m for grp in per_chunk(o_l) for m in _lane_means(grp)]
            oc_l = [o - m for o, m in zip(o_l, mean_l)]
            var_l = [m for grp in per_chunk([oc * oc for oc in oc_l]) for m in _lane_means(grp)]
            for oc, var, (c, h) in zip(oc_l, var_l, pairs):
                on = oc * lax.rsqrt(var + EPS) * gnw_ref[:, hs(h)]
                yr_new[rs(c), hs(h)] = (zr_ref[rs(c), hs(h)].astype(F32) * on).astype(BF16)
        while pending:
            finish_chunk()
        if final and yr_old is not None:
            xn = o_ref[...]
            ms = jnp.mean(xn * xn, axis=-1, keepdims=True)
            o_ref[...] = xn * lax.rsqrt(ms + EPS) * fnw_ref[...]

    middle = jnp.logical_and(s_id > 0, s_id < n_tiles)

    @pl.when(s_id == 0)
    def _():
        step(None, ya_ref)

    @pl.when(jnp.logical_and(middle, s_id % 2 == 0))
    def _():
        step(yb_ref, ya_ref)

    @pl.when(jnp.logical_and(middle, s_id % 2 == 1))
    def _():
        step(ya_ref, yb_ref)

    @pl.when(s_id == n_tiles)
    def _():
        step(ya_ref if n_tiles % 2 == 1 else yb_ref, None)


def _out(lg_f, lg_b, segs, edges, sf, sb, x, gate, conv_w, cnw, gnw, w_out, fnw, layer, *,
         tm, final):
    ycv, zc, q, k, v, zr = segs
    eb, ec = edges
    rows, d = x.shape
    nt = rows // tm
    nc = tm // CHUNK
    n_edge = eb.shape[0]
    edge_every = nt // n_edge
    kern = functools.partial(_out_kernel, tm=tm, n_tiles=nt, edge_every=edge_every, final=final)
    smem = lambda: pl.BlockSpec(memory_space=pltpu.SMEM)
    cur = lambda s: jnp.minimum(s, nt - 1)
    prv = lambda s: jnp.maximum(s - 1, 0)
    seg = lambda: pl.BlockSpec((tm, SEG), lambda s: (cur(s), 0))
    fin = lambda: pl.BlockSpec((tm, SEG), lambda s: (prv(s), 0))
    st = lambda: pl.BlockSpec((nc, RET_HEADS, HEAD_DIM, HEAD_DIM), lambda s: (cur(s), 0, 0, 0))
    full = lambda shape: pl.BlockSpec(shape, lambda s: (0,) * len(shape))
    return pl.pallas_call(
        kern,
        grid=(nt + 1,),
        in_specs=[
            smem(), smem(),
            fin(), fin(), seg(), seg(), seg(), seg(),
            pl.BlockSpec((1, 8, SEG), lambda s: (prv(s) // edge_every, 0, 0)),
            pl.BlockSpec((1, 8, SEG),
                         lambda s: (jnp.minimum(prv(s) // edge_every + 1, n_edge - 1), 0, 0)),
            st(), st(),
            pl.BlockSpec((tm, d), lambda s: (prv(s), 0)),
            full((1, d)),
            pl.BlockSpec((1, 3, D_CONV), lambda s: (layer, 0, 0)),
            full((1, D_CONV)), full((1, D_RET)),
            pl.BlockSpec((d, d), lambda s: (0, 0), pipeline_mode=pl.Buffered(1)),
            full((1, d)),
        ],
        out_specs=pl.BlockSpec((tm, d), lambda s: (prv(s), 0)),
        out_shape=jax.ShapeDtypeStruct((rows, d), F32),
        scratch_shapes=[
            pltpu.VMEM((RET_HEADS, CHUNK, CHUNK), F32),
            pltpu.VMEM((RET_HEADS, CHUNK, HEAD_DIM), F32),
            pltpu.VMEM((RET_HEADS, CHUNK, HEAD_DIM), F32),
            pltpu.VMEM((tm, D_RET), BF16),
            pltpu.VMEM((tm, D_RET), BF16),
        ],
        compiler_params=_params(56, 1),
    )(lg_f, lg_b, ycv, zc, q, k, v, zr, eb, ec, sf, sb, x, gate, conv_w, cnw, gnw,
      w_out, fnw)


def _rope_tables(seq):
    t = np.arange(seq)
    row = (t // GRID_W).astype(np.float64)
    col = (t % GRID_W).astype(np.float64)
    inv = ROPE_BASE ** (-np.arange(ROPE_F, dtype=np.float64) / ROPE_F)
    ar = row[:, None] * inv[None, :]
    ac = col[:, None] * inv[None, :]
    z = np.zeros_like(ar)
    cos = np.concatenate([np.cos(ar), np.cos(ar), np.cos(ac), np.cos(ac)], axis=-1)
    sa = np.concatenate([-np.sin(ar), z, -np.sin(ac), z], axis=-1)
    sb = np.concatenate([z, np.sin(ar), z, np.sin(ac)], axis=-1)
    return tuple(jnp.asarray(a, dtype=F32) for a in (cos, sa, sb))


def kernel(x, c, ctx, c_ctx, norm_w, w_mod, b_mod, w_in, conv_w, conv_norm_w, ret_norm_w,
           ret_decay_f, ret_decay_b, w_out, final_norm_w):
    batch, seq, d = x.shape
    assert batch == 1 and d == D_MODEL and seq % 1024 == 0
    depth = norm_w.shape[0]
    ctx_len = ctx.shape[1]
    xs = x[0]
    cs = ctx[0]

    cv = jnp.zeros((8, d), F32).at[0].set(c[0]).at[1].set(c_ctx)
    mod = _modulation(cv, w_mod, b_mod)
    tables = _rope_tables(seq)
    ctx_tables = tuple(jnp.zeros((ctx_len, HEAD_DIM), F32) for _ in range(3))
    zero_state = jnp.zeros((RET_HEADS, HEAD_DIM, HEAD_DIM), F32)
    zero_edge = jnp.zeros((1, 8, D_CONV), F32)
    fnw = final_norm_w.reshape(1, d)
    w_groups = None

    for layer in range(depth):
        update_ctx = layer < depth - 1
        lg_f = -jnp.exp(ret_decay_f[layer].astype(F32))
        lg_b = -jnp.exp(ret_decay_b[layer].astype(F32))
        nw = norm_w[layer].reshape(1, d)
        m = mod[layer]
        shift, scale, gate = m[0:1, 0:d], m[0:1, d:2 * d], m[0:1, 2 * d:3 * d]
        shift_c, scale_c, gate_c = m[1:2, 0:d], m[1:2, d:2 * d], m[1:2, 2 * d:3 * d]
        cnw = conv_norm_w[layer].reshape(1, D_CONV)
        gnw = ret_norm_w[layer].reshape(1, D_RET)

        if update_ctx:
            segs_c, w_groups = _inproj(cs, nw, shift_c, scale_c, w_groups, conv_w, layer,
                                       ctx_tables, rope=False,
                                       w_f32=w_in if w_groups is None else None)
            k_c, v_c = segs_c[3], segs_c[4]
        else:
            if w_groups is None:
                w_groups = _split_groups(w_in[layer].astype(BF16))
            k_c, v_c = _inproj_kv(cs, nw, shift_c, scale_c, w_groups, tm=ctx_len)
        sf_c, sb_c, s_f, s_b = _states(lg_f, lg_b, k_c, v_c, zero_state, zero_state, tr=ctx_len)

        to_round = [(w_out, layer)] + ([(w_in, layer + 1)] if layer + 1 < depth else [])
        segs, edges, rounded = _inproj_latent(xs, nw, shift, scale, w_groups, conv_w, layer,
                                              tables, to_round, tm=1024)
        w_out_b = rounded[0]
        w_groups = _split_groups(rounded[1]) if layer + 1 < depth else None
        sf, sb, _, _ = _states(lg_f, lg_b, segs[3], segs[4], s_f, s_b, tr=1024)
        xs = _out(lg_f, lg_b, segs, edges, sf, sb, xs, gate, conv_w, cnw, gnw, w_out_b, fnw, layer,
                  tm=512, final=not update_ctx)
        if update_ctx:
            cs = _out(lg_f, lg_b, segs_c, (zero_edge, zero_edge), sf_c, sb_c, cs, gate_c, conv_w,
                      cnw, gnw, w_out_b, fnw, layer, tm=ctx_len, final=False)
    return xs[None]
```

```python
import functools

import numpy as np
import jax
import jax.numpy as jnp
from jax import lax
from jax.experimental import pallas as pl
from jax.experimental.pallas import tpu as pltpu

D_MODEL = 2048
D_CONV = 1024
D_RET = 1024
RET_HEADS = 8
HEAD_DIM = 128
CHUNK = 128
SEG = 1024
N_SEG = 8
SUB = 256
N_SUB = SEG // SUB
GRID_W = 64
ROPE_BASE = 10000.0
ROPE_F = 32
EPS = 1e-6
K_SCALE = HEAD_DIM ** -0.5

F32 = jnp.float32
BF16 = jnp.bfloat16

G_H, G_B, G_C, G_ZC, G_Q, G_K, G_V, G_ZR = range(8)


def _silu(x):
    return x / (1.0 + jnp.exp(-x))


def _params(vmem_mb, n_axes, flags=None):
    return pltpu.CompilerParams(
        dimension_semantics=("arbitrary",) * n_axes,
        vmem_limit_bytes=vmem_mb * 1024 * 1024,
        flags=flags,
    )


def _mod_kernel(cv_ref, w_ref, b_ref, o_ref):
    s = _silu(cv_ref[...])
    o_ref[0] = jnp.dot(s.astype(BF16), w_ref[0].astype(BF16),
                       preferred_element_type=F32) + b_ref[0]


def _modulation(cv, w_mod, b_mod, tn=1024):
    depth, d, n = w_mod.shape
    return pl.pallas_call(
        _mod_kernel,
        grid=(depth, n // tn),
        in_specs=[
            pl.BlockSpec((8, d), lambda l, j: (0, 0)),
            pl.BlockSpec((1, d, tn), lambda l, j: (l, 0, j)),
            pl.BlockSpec((1, 1, tn), lambda l, j: (l, 0, j)),
        ],
        out_specs=pl.BlockSpec((1, 8, tn), lambda l, j: (l, 0, j)),
        out_shape=jax.ShapeDtypeStruct((depth, 8, n), F32),
        compiler_params=_params(40, 2),
    )(cv, w_mod, b_mod.reshape(depth, 1, n))


PROLOGUE_ROWS = 64
RET_ROUND_CHUNKS = 2
W_CAST_BLOCKS = 32


def _prologue(x_ref, nw_ref, shift_ref, scale_ref, hx_ref):
    gain = nw_ref[...] * (1.0 + scale_ref[...])
    shift = shift_ref[...]

    def body(r, carry):
        rows = pl.ds(pl.multiple_of(r * PROLOGUE_ROWS, PROLOGUE_ROWS), PROLOGUE_ROWS)
        x = x_ref[rows, :]
        ms = jnp.mean(x * x, axis=-1, keepdims=True)
        hx_ref[rows, :] = (x * lax.rsqrt(ms + EPS) * gain + shift).astype(BF16)
        return carry

    lax.fori_loop(0, x_ref.shape[0] // PROLOGUE_ROWS, body, 0)


def _rope_pair(acc, cos, sa, sb):
    outs = []
    for h in range(SUB // HEAD_DIM):
        a = acc[:, h * HEAD_DIM:(h + 1) * HEAD_DIM]
        outs.append(a * cos + pltpu.roll(a, HEAD_DIM - ROPE_F, 1) * sa
                    + pltpu.roll(a, ROPE_F, 1) * sb)
    return jnp.concatenate(outs, axis=1)


def _project_groups(hx, w_refs, convw_ref, table_refs, out_refs, rope, prev_row=None,
                    edge_refs=None):
    wh_ref, wb_ref, wc_ref, wzc_ref, wq_ref, wk_ref, wv_ref, wzr_ref = w_refs
    yb_ref, zc_ref, q_ref, k_ref, v_ref, zr_ref = out_refs
    tm = hx.shape[0]

    def seg(w_ref):
        return jnp.dot(hx, w_ref[...], preferred_element_type=F32)

    ch = seg(wc_ref) * seg(wh_ref)
    ridx = lax.broadcasted_iota(jnp.int32, (tm, 1), 0)
    above = pltpu.roll(ch, 1, 0)
    above = jnp.where(ridx == 0, 0.0 if prev_row is None else prev_row, above)
    below = jnp.where(ridx == tm - 1, 0.0, pltpu.roll(ch, tm - 1, 0))
    cw = convw_ref[...]
    b = seg(wb_ref)
    yb_ref[...] = (b * (above * cw[0:1, :] + ch * cw[1:2, :] + below * cw[2:3, :])).astype(BF16)
    if edge_refs is not None:
        edge_refs[0][0] = b[tm - 8:tm, :]
        edge_refs[1][0] = ch[0:8, :]
    zc_ref[...] = _silu(seg(wzc_ref)).astype(BF16)
    if rope:
        cos, sa, sb = (t[...] for t in table_refs)
        q_ref[...] = _rope_pair(seg(wq_ref), cos, sa, sb).astype(BF16)
        k_ref[...] = _rope_pair(seg(wk_ref), cos * K_SCALE, sa * K_SCALE, sb * K_SCALE).astype(BF16)
    else:
        q_ref[...] = seg(wq_ref).astype(BF16)
        k_ref[...] = (seg(wk_ref) * K_SCALE).astype(BF16)
    v_ref[...] = seg(wv_ref).astype(BF16)
    zr_ref[...] = _silu(seg(wzr_ref)).astype(BF16)
    return ch[tm - 8:tm, :]


def _inproj_kernel(x_ref, nw_ref, shift_ref, scale_ref, *refs, rope, round_weights):
    w_refs, convw_ref, table_refs = refs[:8], refs[8], refs[9:12]
    out_refs, refs = refs[12:18], refs[18:]
    if round_weights:
        for w_ref, wb_ref in zip(w_refs, refs[:8]):
            wb_ref[...] = w_ref[...].astype(BF16)
        w_refs, refs = refs[:8], refs[8:]
    hx_ref, = refs

    @pl.when(pl.program_id(1) == 0)
    def _():
        _prologue(x_ref, nw_ref, shift_ref, scale_ref, hx_ref)

    _project_groups(hx_ref[...], w_refs, convw_ref, table_refs, out_refs, rope)


def _inproj_latent_kernel(xq_ref, nw_ref, shift_ref, scale_ref, *refs, tm, n_cast):
    w_refs, convw_ref, table_refs = refs[:8], refs[8], refs[9:12]
    cast_in_refs, refs = refs[12:12 + n_cast], refs[12 + n_cast:]
    out_refs, edge_refs, refs = refs[:6], refs[6:8], refs[8:]
    cast_out_refs, refs = refs[:n_cast], refs[n_cast:]
    hxa_ref, hxb_ref, carry_ref = refs
    i = pl.program_id(0)
    j = pl.program_id(1)
    quarter = tm // N_SUB
    row0 = pl.multiple_of(j * quarter, quarter)

    def project(hx_ref):
        prev_row = carry_ref[j][7:8, :]
        carry_ref[j] = _project_groups(hx_ref[...], w_refs, convw_ref, table_refs, out_refs,
                                       True, prev_row, edge_refs)

    def normalise(hx_ref):
        for src_ref, dst_ref in zip(cast_in_refs, cast_out_refs):
            dst_ref[...] = src_ref[...].astype(BF16)
        gain = nw_ref[...] * (1.0 + scale_ref[...])
        shift = shift_ref[...]
        for r in range(0, quarter, PROLOGUE_ROWS):
            x = xq_ref[r:r + PROLOGUE_ROWS, :]
            ms = jnp.mean(x * x, axis=-1, keepdims=True)
            hx_ref[pl.ds(row0 + r, PROLOGUE_ROWS), :] = (
                x * lax.rsqrt(ms + EPS) * gain + shift).astype(BF16)

    @pl.when(i == 0)
    def _():
        normalise(hxa_ref)
        carry_ref[j] = jnp.zeros(carry_ref.shape[1:], F32)

    @pl.when(i % 2 == 1)
    def _():
        normalise(hxb_ref)
        project(hxa_ref)

    @pl.when(jnp.logical_and(i > 0, i % 2 == 0))
    def _():
        normalise(hxa_ref)
        project(hxb_ref)


def _w_spec(first_block):
    return pl.BlockSpec((D_MODEL, SUB), lambda i, j: (0, first_block + j))


def _split_groups(w_bf16):
    return [(w_bf16, g * N_SUB) for g in range(N_SEG)]


def _convw_spec(layer):
    return pl.BlockSpec((None, 3, SUB), lambda i, j: (layer, 0, j))


def _inproj(x, norm_w, shift, scale, w_groups, conv_w, layer, tables, *, rope, w_f32=None):
    tm, d = x.shape
    cos, sa, sb = tables
    round_weights = w_f32 is not None
    kern = functools.partial(_inproj_kernel, rope=rope, round_weights=round_weights)
    vec = lambda: pl.BlockSpec((1, d), lambda i, j: (0, 0))
    tab = lambda: pl.BlockSpec((tm, HEAD_DIM), lambda i, j: (i, 0))
    out = lambda: pl.BlockSpec((tm, SUB), lambda i, j: (i, j))
    seg_shape = jax.ShapeDtypeStruct((tm, SEG), BF16)
    out_specs = [out() for _ in range(6)]
    out_shape = [seg_shape] * 6
    if round_weights:
        w_specs = [pl.BlockSpec((None, d, SUB), lambda i, j, g=g: (layer, 0, g * N_SUB + j))
                   for g in range(N_SEG)]
        w_operands = [w_f32] * N_SEG
        out_specs += [pl.BlockSpec((d, SUB), lambda i, j: (0, j)) for _ in range(N_SEG)]
        out_shape += [jax.ShapeDtypeStruct((d, SEG), BF16)] * N_SEG
    else:
        w_specs = [_w_spec(first) for _, first in w_groups]
        w_operands = [w for w, _ in w_groups]
    res = pl.pallas_call(
        kern,
        grid=(1, N_SUB),
        in_specs=[
            pl.BlockSpec((tm, d), lambda i, j: (i, 0)),
            vec(), vec(), vec(),
            *w_specs,
            _convw_spec(layer),
            tab(), tab(), tab(),
        ],
        out_specs=out_specs,
        out_shape=out_shape,
        scratch_shapes=[pltpu.VMEM((tm, d), BF16)],
        compiler_params=_params(56, 2),
    )(x, norm_w, shift, scale, *w_operands, conv_w, cos, sa, sb)
    return res[:6], ([(w, 0) for w in res[6:]] if round_weights else w_groups)


def _inproj_latent(x, norm_w, shift, scale, w_groups, conv_w, layer, tables, to_round, *, tm):
    rows, d = x.shape
    nt = rows // tm
    cos, sa, sb = tables
    kern = functools.partial(_inproj_latent_kernel, tm=tm, n_cast=len(to_round))
    done = lambda i: jnp.maximum(i - 1, 0)
    vec = lambda: pl.BlockSpec((1, d), lambda i, j: (0, 0))
    tab = lambda: pl.BlockSpec((tm, HEAD_DIM), lambda i, j: (done(i), 0))
    out = lambda: pl.BlockSpec((tm, SUB), lambda i, j: (done(i), jnp.where(i == 0, 0, j)))
    seg_shape = jax.ShapeDtypeStruct((rows, SEG), BF16)
    in_specs = [
        pl.BlockSpec((tm // N_SUB, d), lambda i, j: (jnp.minimum(i, nt - 1) * N_SUB + j, 0)),
        vec(), vec(), vec(),
        *[_w_spec(first) for _, first in w_groups],
        _convw_spec(layer),
        tab(), tab(), tab(),
    ]
    operands = [x, norm_w, shift, scale, *[w for w, _ in w_groups], conv_w, cos, sa, sb]
    edge = lambda: pl.BlockSpec((1, 8, SUB), lambda i, j: (done(i), 0, jnp.where(i == 0, 0, j)))
    out_specs = [out() for _ in range(6)] + [edge(), edge()]
    out_shape = [seg_shape] * 6 + [jax.ShapeDtypeStruct((nt, 8, SEG), F32)] * 2
    n_blocks = W_CAST_BLOCKS
    assert n_blocks <= (nt + 1) * N_SUB
    blk = lambda i, j: jnp.minimum(i * N_SUB + j, n_blocks - 1)
    for w_all, which in to_round:
        n_rows, n_cols = w_all.shape[1:]
        assert n_rows % n_blocks == 0
        in_specs.append(pl.BlockSpec((None, n_rows // n_blocks, n_cols),
                                     lambda i, j, which=which: (which, blk(i, j), 0)))
        operands.append(w_all)
        out_specs.append(pl.BlockSpec((n_rows // n_blocks, n_cols), lambda i, j: (blk(i, j), 0)))
        out_shape.append(jax.ShapeDtypeStruct((n_rows, n_cols), BF16))
    res = pl.pallas_call(
        kern,
        grid=(nt + 1, N_SUB),
        in_specs=in_specs,
        out_specs=out_specs,
        out_shape=out_shape,
        scratch_shapes=[pltpu.VMEM((tm, d), BF16), pltpu.VMEM((tm, d), BF16),
                        pltpu.VMEM((N_SUB, 8, SUB), F32)],
        compiler_params=_params(56, 2),
    )(*operands)
    return res[:6], res[6:8], res[8:]


def _inproj_kv_kernel(x_ref, nw_ref, shift_ref, scale_ref, wk_ref, wv_ref, k_ref, v_ref, hx_ref):
    @pl.when(pl.program_id(1) == 0)
    def _():
        _prologue(x_ref, nw_ref, shift_ref, scale_ref, hx_ref)

    hx = hx_ref[...]
    k_ref[...] = (jnp.dot(hx, wk_ref[...], preferred_element_type=F32) * K_SCALE).astype(BF16)
    v_ref[...] = jnp.dot(hx, wv_ref[...], preferred_element_type=F32).astype(BF16)


def _inproj_kv(x, norm_w, shift, scale, w_groups, *, tm):
    rows, d = x.shape
    vec = lambda: pl.BlockSpec((1, d), lambda i, j: (0, 0))
    out = lambda: pl.BlockSpec((tm, SUB), lambda i, j: (i, j))
    seg_shape = jax.ShapeDtypeStruct((rows, SEG), BF16)
    return pl.pallas_call(
        _inproj_kv_kernel,
        grid=(rows // tm, N_SUB),
        in_specs=[
            pl.BlockSpec((tm, d), lambda i, j: (i, 0)),
            vec(), vec(), vec(),
            _w_spec(w_groups[G_K][1]), _w_spec(w_groups[G_V][1]),
        ],
        out_specs=[out(), out()],
        out_shape=[seg_shape] * 2,
        scratch_shapes=[pltpu.VMEM((tm, d), BF16)],
        compiler_params=_params(32, 2),
    )(x, norm_w, shift, scale, w_groups[G_K][0], w_groups[G_V][0])


def _states_kernel(lgf_ref, lgb_ref, kf_ref, vf_ref, kb_ref, vb_ref, s0f_ref, s0b_ref,
                   sf_ref, sb_ref, ff_ref, fb_ref, *, nc):
    t = pl.program_id(0)

    @pl.when(t == 0)
    def _():
        ff_ref[...] = s0f_ref[...]
        fb_ref[...] = s0b_ref[...]

    row = lax.broadcasted_iota(jnp.int32, (CHUNK, HEAD_DIM), 0).astype(F32)
    tn = (((0,), (0,)), ((), ()))
    for h in range(RET_HEADS):
        hs = slice(h * HEAD_DIM, (h + 1) * HEAD_DIM)
        lgf = lgf_ref[h]
        lgb = lgb_ref[h]
        kdf = jnp.exp(lgf * (CHUNK - 1.0 - row))
        kdb = jnp.exp(lgb * row)
        cdf = jnp.exp(lgf * CHUNK + 0.0 * row)
        cdb = jnp.exp(lgb * CHUNK + 0.0 * row)

        s = ff_ref[h]
        for c in range(nc):
            rs = slice(c * CHUNK, (c + 1) * CHUNK)
            sf_ref[c, h] = s.astype(BF16)
            kd = (kf_ref[rs, hs].astype(F32) * kdf).astype(BF16)
            s = cdf * s + lax.dot_general(kd, vf_ref[rs, hs], tn, preferred_element_type=F32)
        ff_ref[h] = s

        s = fb_ref[h]
        for c in reversed(range(nc)):
            rs = slice(c * CHUNK, (c + 1) * CHUNK)
            sb_ref[c, h] = s.astype(BF16)
            kd = (kb_ref[rs, hs].astype(F32) * kdb).astype(BF16)
            s = cdb * s + lax.dot_general(kd, vb_ref[rs, hs], tn, preferred_element_type=F32)
        fb_ref[h] = s


def _states(lg_f, lg_b, k, v, s0f, s0b, *, tr):
    rows = k.shape[0]
    nt = rows // tr
    nc = tr // CHUNK
    n_chunks = rows // CHUNK
    kern = functools.partial(_states_kernel, nc=nc)
    smem = lambda: pl.BlockSpec(memory_space=pltpu.SMEM)
    st = lambda: pl.BlockSpec((RET_HEADS, HEAD_DIM, HEAD_DIM), lambda t: (0, 0, 0))
    fwd = lambda: pl.BlockSpec((tr, SEG), lambda t: (t, 0))
    bwd = lambda: pl.BlockSpec((tr, SEG), lambda t: (nt - 1 - t, 0))
    seq_shape = pltpu.HBM((n_chunks, RET_HEADS, HEAD_DIM, HEAD_DIM), BF16)
    fin_shape = jax.ShapeDtypeStruct((RET_HEADS, HEAD_DIM, HEAD_DIM), F32)
    return pl.pallas_call(
        kern,
        grid=(nt,),
        in_specs=[smem(), smem(), fwd(), fwd(), bwd(), bwd(), st(), st()],
        out_specs=[
            pl.BlockSpec((nc, RET_HEADS, HEAD_DIM, HEAD_DIM), lambda t: (t, 0, 0, 0)),
            pl.BlockSpec((nc, RET_HEADS, HEAD_DIM, HEAD_DIM), lambda t: (nt - 1 - t, 0, 0, 0)),
            st(), st(),
        ],
        out_shape=[seq_shape, seq_shape, fin_shape, fin_shape],
        compiler_params=_params(32, 1),
    )(lg_f, lg_b, k, v, k, v, s0f, s0b)


def _lane_means(ts):
    parts = []
    for t in ts:
        hi = t.astype(BF16)
        lo = (t - hi.astype(F32)).astype(BF16)
        parts.append(jnp.concatenate([hi, lo], axis=1))
    rows = ts[0].shape[0]
    ones = jnp.full((2 * HEAD_DIM, HEAD_DIM), 1.0 / HEAD_DIM, BF16)
    m = jnp.dot(jnp.concatenate(parts, axis=0), ones, preferred_element_type=F32)
    return [m[n * rows:(n + 1) * rows] for n in range(len(ts))]


def _out_kernel(lgf_ref, lgb_ref,
                ycv_ref, zc_ref, q_ref, k_ref, v_ref, zr_ref, eb_ref, ec_ref,
                sf_ref, sb_ref, x_ref, gate_ref, convw_ref, cnw_ref, gnw_ref, wout_ref, fnw_ref,
                o_ref,
                mask_ref, qdf_ref, qdb_ref, ya_ref, yb_ref, *, tm, n_tiles, edge_every, final):
    s_id = pl.program_id(0)
    i = jnp.maximum(s_id - 1, 0)
    nc = tm // CHUNK
    at_edge = jnp.logical_and((i + 1) % edge_every == 0, i + 1 < n_tiles)

    @pl.when(s_id == 0)
    def _():
        r = lax.broadcasted_iota(jnp.int32, (CHUNK, CHUNK), 0).astype(F32)
        cc = lax.broadcasted_iota(jnp.int32, (CHUNK, CHUNK), 1).astype(F32)
        d = r - cc
        for h in range(RET_HEADS):
            lgf = lgf_ref[h]
            lgb = lgb_ref[h]
            mf = jnp.where(d >= 0, jnp.exp(lgf * jnp.maximum(d, 0.0)), 0.0)
            mb = jnp.where(d <= 0, jnp.exp(lgb * jnp.maximum(-d, 0.0)), 0.0)
            mask_ref[h] = mf + mb
            qdf_ref[h] = jnp.exp(lgf * (r + 1.0))
            qdb_ref[h] = jnp.exp(lgb * (CHUNK - r))

    def step(yr_old, yr_new):
        n_q = 4
        cw_ = D_CONV // n_q
        pw = D_MODEL // n_q

        if yr_old is not None:
            last_row = lax.broadcasted_iota(jnp.int32, (8, 1), 0) == 7
            cwt = convw_ref[0]
            proj_ret, y_l, ssq = [], [], None
            for n in range(n_q):
                proj_ret.append(jnp.dot(yr_old[...], wout_ref[D_CONV:, n * pw:(n + 1) * pw],
                                        preferred_element_type=F32))
                cs = slice(n * cw_, (n + 1) * cw_)
                y = ycv_ref[:, cs].astype(F32)
                below = eb_ref[0, 7:8, cs] * cwt[2:3, cs] * ec_ref[0, 0:1, cs]
                below = jnp.where(jnp.logical_and(at_edge, last_row), below, 0.0)
                y = jnp.concatenate([y[:tm - 8], y[tm - 8:] + below], axis=0)
                y_l.append(y)
                part = jnp.sum(y * y, axis=-1, keepdims=True)
                ssq = part if ssq is None else ssq + part
            rinv = lax.rsqrt(ssq * (1.0 / D_CONV) + EPS)

            y_conv = jnp.concatenate(
                [(zc_ref[:, n * cw_:(n + 1) * cw_].astype(F32)
                  * (y_l[n] * rinv * cnw_ref[:, n * cw_:(n + 1) * cw_])).astype(BF16)
                 for n in range(n_q)], axis=1)

        pending = list(range(n_q)) if yr_old is not None else []

        def finish_chunk():
            if not pending:
                return
            n = pending.pop(0)
            cols = slice(n * pw, (n + 1) * pw)
            pc = jnp.dot(y_conv, wout_ref[:D_CONV, cols], preferred_element_type=F32)
            o_ref[:, cols] = x_ref[:, cols] + gate_ref[:, cols] * (pc + proj_ret[n])

        nt = (((1,), (1,)), ((), ()))
        rs = lambda c: slice(c * CHUNK, (c + 1) * CHUNK)
        hs = lambda h: slice(h * HEAD_DIM, (h + 1) * HEAD_DIM)
        rounds = range(0, nc, RET_ROUND_CHUNKS) if yr_new is not None else []
        per_round = -(-n_q // max(len(rounds), 1) // 2)
        for c0 in rounds:
            pairs = [(c, h) for c in range(c0, c0 + RET_ROUND_CHUNKS) for h in range(RET_HEADS)]
            q_l = [q_ref[rs(c), hs(h)] for c, h in pairs]
            s_l = [lax.dot_general(q, k_ref[rs(c), hs(h)], nt, preferred_element_type=F32)
                   for q, (c, h) in zip(q_l, pairs)]
            for _ in range(per_round):
                finish_chunk()
            lhs_l = []
            for q, s, (c, h) in zip(q_l, s_l, pairs):
                qf32 = q.astype(F32)
                lhs_l.append(jnp.concatenate(
                    [(s * mask_ref[h]).astype(BF16), (qf32 * qdf_ref[h]).astype(BF16),
                     (qf32 * qdb_ref[h]).astype(BF16)], axis=1))
            o_l = [jnp.dot(lhs, jnp.concatenate([v_ref[rs(c), hs(h)], sf_ref[c, h],
                                                 sb_ref[c, h]], axis=0),
                           preferred_element_type=F32)
                   for lhs, (c, h) in zip(lhs_l, pairs)]
            for _ in range(per_round):
                finish_chunk()
            per_chunk = lambda xs: [xs[n * RET_HEADS:(n + 1) * RET_HEADS]
                                    for n in range(len(xs) // RET_HEADS)]
            mean_l = [m for grp in per_chunk(o_l) for m in _lane_means(grp)]
            oc_l = [o - m for o, m in zip(o_l, mean_l)]
            var_l = [m for grp in per_chunk([oc * oc for oc in oc_l]) for m in _lane_means(grp)]
            for oc, var, (c, h) in zip(oc_l, var_l, pairs):
                on = oc * lax.rsqrt(var + EPS) * gnw_ref[:, hs(h)]
                yr_new[rs(c), hs(h)] = (zr_ref[rs(c), hs(h)].astype(F32) * on).astype(BF16)
        while pending:
            finish_chunk()
        if final and yr_old is not None:
            xn = o_ref[...]
            ms = jnp.mean(xn * xn, axis=-1, keepdims=True)
            o_ref[...] = xn * lax.rsqrt(ms + EPS) * fnw_ref[...]

    middle = jnp.logical_and(s_id > 0, s_id < n_tiles)

    @pl.when(s_id == 0)
    def _():
        step(None, ya_ref)

    @pl.when(jnp.logical_and(middle, s_id % 2 == 0))
    def _():
        step(yb_ref, ya_ref)

    @pl.when(jnp.logical_and(middle, s_id % 2 == 1))
    def _():
        step(ya_ref, yb_ref)

    @pl.when(s_id == n_tiles)
    def _():
        step(ya_ref if n_tiles % 2 == 1 else yb_ref, None)


def _out(lg_f, lg_b, segs, edges, sf, sb, x, gate, conv_w, cnw, gnw, w_out, fnw, layer, *,
         tm, final):
    ycv, zc, q, k, v, zr = segs
    eb, ec = edges
    rows, d = x.shape
    nt = rows // tm
    nc = tm // CHUNK
    n_edge = eb.shape[0]
    edge_every = nt // n_edge
    kern = functools.partial(_out_kernel, tm=tm, n_tiles=nt, edge_every=edge_every, final=final)
    smem = lambda: pl.BlockSpec(memory_space=pltpu.SMEM)
    cur = lambda s: jnp.minimum(s, nt - 1)
    prv = lambda s: jnp.maximum(s - 1, 0)
    seg = lambda: pl.BlockSpec((tm, SEG), lambda s: (cur(s), 0))
    fin = lambda: pl.BlockSpec((tm, SEG), lambda s: (prv(s), 0))
    st = lambda: pl.BlockSpec((nc, RET_HEADS, HEAD_DIM, HEAD_DIM), lambda s: (cur(s), 0, 0, 0))
    full = lambda shape: pl.BlockSpec(shape, lambda s: (0,) * len(shape))
    return pl.pallas_call(
        kern,
        grid=(nt + 1,),
        in_specs=[
            smem(), smem(),
            fin(), fin(), seg(), seg(), seg(), seg(),
            pl.BlockSpec((1, 8, SEG), lambda s: (prv(s) // edge_every, 0, 0)),
            pl.BlockSpec((1, 8, SEG),
                         lambda s: (jnp.minimum(prv(s) // edge_every + 1, n_edge - 1), 0, 0)),
            st(), st(),
            pl.BlockSpec((tm, d), lambda s: (prv(s), 0)),
            full((1, d)),
            pl.BlockSpec((1, 3, D_CONV), lambda s: (layer, 0, 0)),
            full((1, D_CONV)), full((1, D_RET)),
            pl.BlockSpec((d, d), lambda s: (0, 0), pipeline_mode=pl.Buffered(1)),
            full((1, d)),
        ],
        out_specs=pl.BlockSpec((tm, d), lambda s: (prv(s), 0)),
        out_shape=jax.ShapeDtypeStruct((rows, d), F32),
        scratch_shapes=[
            pltpu.VMEM((RET_HEADS, CHUNK, CHUNK), F32),
            pltpu.VMEM((RET_HEADS, CHUNK, HEAD_DIM), F32),
            pltpu.VMEM((RET_HEADS, CHUNK, HEAD_DIM), F32),
            pltpu.VMEM((tm, D_RET), BF16),
            pltpu.VMEM((tm, D_RET), BF16),
        ],
        compiler_params=_params(56, 1),
    )(lg_f, lg_b, ycv, zc, q, k, v, zr, eb, ec, sf, sb, x, gate, conv_w, cnw, gnw,
      w_out, fnw)


def _rope_tables(seq):
    t = np.arange(seq)
    row = (t // GRID_W).astype(np.float64)
    col = (t % GRID_W).astype(np.float64)
    inv = ROPE_BASE ** (-np.arange(ROPE_F, dtype=np.float64) / ROPE_F)
    ar = row[:, None] * inv[None, :]
    ac = col[:, None] * inv[None, :]
    z = np.zeros_like(ar)
    cos = np.concatenate([np.cos(ar), np.cos(ar), np.cos(ac), np.cos(ac)], axis=-1)
    sa = np.concatenate([-np.sin(ar), z, -np.sin(ac), z], axis=-1)
    sb = np.concatenate([z, np.sin(ar), z, np.sin(ac)], axis=-1)
    return tuple(jnp.asarray(a, dtype=F32) for a in (cos, sa, sb))


def kernel(x, c, ctx, c_ctx, norm_w, w_mod, b_mod, w_in, conv_w, conv_norm_w, ret_norm_w,
           ret_decay_f, ret_decay_b, w_out, final_norm_w):
    batch, seq, d = x.shape
    assert batch == 1 and d == D_MODEL and seq % 1024 == 0
    depth = norm_w.shape[0]
    ctx_len = ctx.shape[1]
    xs = x[0]
    cs = ctx[0]

    cv = jnp.zeros((8, d), F32).at[0].set(c[0]).at[1].set(c_ctx)
    mod = _modulation(cv, w_mod, b_mod)
    tables = _rope_tables(seq)
    ctx_tables = tuple(jnp.zeros((ctx_len, HEAD_DIM), F32) for _ in range(3))
    zero_state = jnp.zeros((RET_HEADS, HEAD_DIM, HEAD_DIM), F32)
    zero_edge = jnp.zeros((1, 8, D_CONV), F32)
    fnw = final_norm_w.reshape(1, d)
    w_groups = None

    for layer in range(depth):
        update_ctx = layer < depth - 1
        lg_f = -jnp.exp(ret_decay_f[layer].astype(F32))
        lg_b = -jnp.exp(ret_decay_b[layer].astype(F32))
        nw = norm_w[layer].reshape(1, d)
        m = mod[layer]
        shift, scale, gate = m[0:1, 0:d], m[0:1, d:2 * d], m[0:1, 2 * d:3 * d]
        shift_c, scale_c, gate_c = m[1:2, 0:d], m[1:2, d:2 * d], m[1:2, 2 * d:3 * d]
        cnw = conv_norm_w[layer].reshape(1, D_CONV)
        gnw = ret_norm_w[layer].reshape(1, D_RET)

        if update_ctx:
            segs_c, w_groups = _inproj(cs, nw, shift_c, scale_c, w_groups, conv_w, layer,
                                       ctx_tables, rope=False,
                                       w_f32=w_in if w_groups is None else None)
            k_c, v_c = segs_c[3], segs_c[4]
        else:
            if w_groups is None:
                w_groups = _split_groups(w_in[layer].astype(BF16))
            k_c, v_c = _inproj_kv(cs, nw, shift_c, scale_c, w_groups, tm=ctx_len)
        sf_c, sb_c, s_f, s_b = _states(lg_f, lg_b, k_c, v_c, zero_state, zero_state, tr=ctx_len)

        to_round = [(w_out, layer)] + ([(w_in, layer + 1)] if layer + 1 < depth else [])
        segs, edges, rounded = _inproj_latent(xs, nw, shift, scale, w_groups, conv_w, layer,
                                              tables, to_round, tm=1024)
        w_out_b = rounded[0]
        w_groups = _split_groups(rounded[1]) if layer + 1 < depth else None
        sf, sb, _, _ = _states(lg_f, lg_b, segs[3], segs[4], s_f, s_b, tr=1024)
        xs = _out(lg_f, lg_b, segs, edges, sf, sb, xs, gate, conv_w, cnw, gnw, w_out_b, fnw, layer,
                  tm=512, final=not update_ctx)
        if update_ctx:
            cs = _out(lg_f, lg_b, segs_c, (zero_edge, zero_edge), sf_c, sb_c, cs, gate_c, conv_w,
                      cnw, gnw, w_out_b, fnw, layer, tm=ctx_len, final=False)
    return xs[None]
```

```python
import functools

import numpy as np
import jax
import jax.numpy as jnp
from jax import lax
from jax.experimental import pallas as pl
from jax.experimental.pallas import tpu as pltpu

D_MODEL = 2048
D_CONV = 1024
D_RET = 1024
RET_HEADS = 8
HEAD_DIM = 128
CHUNK = 128
SEG = 1024
N_SEG = 8
SUB = 256
N_SUB = SEG // SUB
GRID_W = 64
ROPE_BASE = 10000.0
ROPE_F = 32
EPS = 1e-6
K_SCALE = HEAD_DIM ** -0.5

F32 = jnp.float32
BF16 = jnp.bfloat16

G_H, G_B, G_C, G_ZC, G_Q, G_K, G_V, G_ZR = range(8)


def _silu(x):
    return x / (1.0 + jnp.exp(-x))


def _params(vmem_mb, n_axes, flags=None):
    return pltpu.CompilerParams(
        dimension_semantics=("arbitrary",) * n_axes,
        vmem_limit_bytes=vmem_mb * 1024 * 1024,
        flags=flags,
    )


def _mod_kernel(cv_ref, w_ref, b_ref, o_ref):
    s = _silu(cv_ref[...])
    o_ref[0] = jnp.dot(s.astype(BF16), w_ref[0].astype(BF16),
                       preferred_element_type=F32) + b_ref[0]


def _modulation(cv, w_mod, b_mod, tn=1024):
    depth, d, n = w_mod.shape
    return pl.pallas_call(
        _mod_kernel,
        grid=(depth, n // tn),
        in_specs=[
            pl.BlockSpec((8, d), lambda l, j: (0, 0)),
            pl.BlockSpec((1, d, tn), lambda l, j: (l, 0, j)),
            pl.BlockSpec((1, 1, tn), lambda l, j: (l, 0, j)),
        ],
        out_specs=pl.BlockSpec((1, 8, tn), lambda l, j: (l, 0, j)),
        out_shape=jax.ShapeDtypeStruct((depth, 8, n), F32),
        compiler_params=_params(40, 2),
    )(cv, w_mod, b_mod.reshape(depth, 1, n))


PROLOGUE_ROWS = 64
RET_ROUND_CHUNKS = 2
W_CAST_BLOCKS = 32


def _prologue(x_ref, nw_ref, shift_ref, scale_ref, hx_ref):
    gain = nw_ref[...] * (1.0 + scale_ref[...])
    shift = shift_ref[...]

    def body(r, carry):
        rows = pl.ds(pl.multiple_of(r * PROLOGUE_ROWS, PROLOGUE_ROWS), PROLOGUE_ROWS)
        x = x_ref[rows, :]
        ms = jnp.mean(x * x, axis=-1, keepdims=True)
        hx_ref[rows, :] = (x * lax.rsqrt(ms + EPS) * gain + shift).astype(BF16)
        return carry

    lax.fori_loop(0, x_ref.shape[0] // PROLOGUE_ROWS, body, 0)


def _rope_pair(acc, cos, sa, sb):
    outs = []
    for h in range(SUB // HEAD_DIM):
        a = acc[:, h * HEAD_DIM:(h + 1) * HEAD_DIM]
        outs.append(a * cos + pltpu.roll(a, HEAD_DIM - ROPE_F, 1) * sa
                    + pltpu.roll(a, ROPE_F, 1) * sb)
    return jnp.concatenate(outs, axis=1)


def _project_groups(hx, w_refs, convw_ref, table_refs, out_refs, rope, prev_row=None,
                    edge_refs=None):
    wh_ref, wb_ref, wc_ref, wzc_ref, wq_ref, wk_ref, wv_ref, wzr_ref = w_refs
    yb_ref, zc_ref, q_ref, k_ref, v_ref, zr_ref = out_refs
    tm = hx.shape[0]

    def seg(w_ref):
        return jnp.dot(hx, w_ref[...], preferred_element_type=F32)

    ch = seg(wc_ref) * seg(wh_ref)
    ridx = lax.broadcasted_iota(jnp.int32, (tm, 1), 0)
    above = pltpu.roll(ch, 1, 0)
    above = jnp.where(ridx == 0, 0.0 if prev_row is None else prev_row, above)
    below = jnp.where(ridx == tm - 1, 0.0, pltpu.roll(ch, tm - 1, 0))
    cw = convw_ref[...]
    b = seg(wb_ref)
    yb_ref[...] = (b * (above * cw[0:1, :] + ch * cw[1:2, :] + below * cw[2:3, :])).astype(BF16)
    if edge_refs is not None:
        edge_refs[0][0] = b[tm - 8:tm, :]
        edge_refs[1][0] = ch[0:8, :]
    zc_ref[...] = _silu(seg(wzc_ref)).astype(BF16)
    if rope:
        cos, sa, sb = (t[...] for t in table_refs)
        q_ref[...] = _rope_pair(seg(wq_ref), cos, sa, sb).astype(BF16)
        k_ref[...] = _rope_pair(seg(wk_ref), cos * K_SCALE, sa * K_SCALE, sb * K_SCALE).astype(BF16)
    else:
        q_ref[...] = seg(wq_ref).astype(BF16)
        k_ref[...] = (seg(wk_ref) * K_SCALE).astype(BF16)
    v_ref[...] = seg(wv_ref).astype(BF16)
    zr_ref[...] = _silu(seg(wzr_ref)).astype(BF16)
    return ch[tm - 8:tm, :]


def _inproj_kernel(x_ref, nw_ref, shift_ref, scale_ref, *refs, rope, round_weights):
    w_refs, convw_ref, table_refs = refs[:8], refs[8], refs[9:12]
    out_refs, refs = refs[12:18], refs[18:]
    if round_weights:
        for w_ref, wb_ref in zip(w_refs, refs[:8]):
            wb_ref[...] = w_ref[...].astype(BF16)
        w_refs, refs = refs[:8], refs[8:]
    hx_ref, = refs

    @pl.when(pl.program_id(1) == 0)
    def _():
        _prologue(x_ref, nw_ref, shift_ref, scale_ref, hx_ref)

    _project_groups(hx_ref[...], w_refs, convw_ref, table_refs, out_refs, rope)


def _inproj_latent_kernel(xq_ref, nw_ref, shift_ref, scale_ref, *refs, tm, n_cast):
    w_refs, convw_ref, table_refs = refs[:8], refs[8], refs[9:12]
    cast_in_refs, refs = refs[12:12 + n_cast], refs[12 + n_cast:]
    out_refs, edge_refs, refs = refs[:6], refs[6:8], refs[8:]
    cast_out_refs, refs = refs[:n_cast], refs[n_cast:]
    hxa_ref, hxb_ref, carry_ref = refs
    i = pl.program_id(0)
    j = pl.program_id(1)
    quarter = tm // N_SUB
    row0 = pl.multiple_of(j * quarter, quarter)

    def project(hx_ref):
        prev_row = carry_ref[j][7:8, :]
        carry_ref[j] = _project_groups(hx_ref[...], w_refs, convw_ref, table_refs, out_refs,
                                       True, prev_row, edge_refs)

    def normalise(hx_ref):
        for src_ref, dst_ref in zip(cast_in_refs, cast_out_refs):
            dst_ref[...] = src_ref[...].astype(BF16)
        gain = nw_ref[...] * (1.0 + scale_ref[...])
        shift = shift_ref[...]
        for r in range(0, quarter, PROLOGUE_ROWS):
            x = xq_ref[r:r + PROLOGUE_ROWS, :]
            ms = jnp.mean(x * x, axis=-1, keepdims=True)
            hx_ref[pl.ds(row0 + r, PROLOGUE_ROWS), :] = (
                x * lax.rsqrt(ms + EPS) * gain + shift).astype(BF16)

    @pl.when(i == 0)
    def _():
        normalise(hxa_ref)
        carry_ref[j] = jnp.zeros(carry_ref.shape[1:], F32)

    @pl.when(i % 2 == 1)
    def _():
        normalise(hxb_ref)
        project(hxa_ref)

    @pl.when(jnp.logical_and(i > 0, i % 2 == 0))
    def _():
        normalise(hxa_ref)
        project(hxb_ref)


def _w_spec(first_block):
    return pl.BlockSpec((D_MODEL, SUB), lambda i, j: (0, first_block + j))


def _split_groups(w_bf16):
    return [(w_bf16, g * N_SUB) for g in range(N_SEG)]


def _convw_spec(layer):
    return pl.BlockSpec((None, 3, SUB), lambda i, j: (layer, 0, j))


def _inproj(x, norm_w, shift, scale, w_groups, conv_w, layer, tables, *, rope, w_f32=None):
    tm, d = x.shape
    cos, sa, sb = tables
    round_weights = w_f32 is not None
    kern = functools.partial(_inproj_kernel, rope=rope, round_weights=round_weights)
    vec = lambda: pl.BlockSpec((1, d), lambda i, j: (0, 0))
    tab = lambda: pl.BlockSpec((tm, HEAD_DIM), lambda i, j: (i, 0))
    out = lambda: pl.BlockSpec((tm, SUB), lambda i, j: (i, j))
    seg_shape = jax.ShapeDtypeStruct((tm, SEG), BF16)
    out_specs = [out() for _ in range(6)]
    out_shape = [seg_shape] * 6
    if round_weights:
        w_specs = [pl.BlockSpec((None, d, SUB), lambda i, j, g=g: (layer, 0, g * N_SUB + j))
                   for g in range(N_SEG)]
        w_operands = [w_f32] * N_SEG
        out_specs += [pl.BlockSpec((d, SUB), lambda i, j: (0, j)) for _ in range(N_SEG)]
        out_shape += [jax.ShapeDtypeStruct((d, SEG), BF16)] * N_SEG
    else:
        w_specs = [_w_spec(first) for _, first in w_groups]
        w_operands = [w for w, _ in w_groups]
    res = pl.pallas_call(
        kern,
        grid=(1, N_SUB),
        in_specs=[
            pl.BlockSpec((tm, d), lambda i, j: (i, 0)),
            vec(), vec(), vec(),
            *w_specs,
            _convw_spec(layer),
            tab(), tab(), tab(),
        ],
        out_specs=out_specs,
        out_shape=out_shape,
        scratch_shapes=[pltpu.VMEM((tm, d), BF16)],
        compiler_params=_params(56, 2),
    )(x, norm_w, shift, scale, *w_operands, conv_w, cos, sa, sb)
    return res[:6], ([(w, 0) for w in res[6:]] if round_weights else w_groups)


def _inproj_latent(x, norm_w, shift, scale, w_groups, conv_w, layer, tables, to_round, *, tm):
    rows, d = x.shape
    nt = rows // tm
    cos, sa, sb = tables
    kern = functools.partial(_inproj_latent_kernel, tm=tm, n_cast=len(to_round))
    done = lambda i: jnp.maximum(i - 1, 0)
    vec = lambda: pl.BlockSpec((1, d), lambda i, j: (0, 0))
    tab = lambda: pl.BlockSpec((tm, HEAD_DIM), lambda i, j: (done(i), 0))
    out = lambda: pl.BlockSpec((tm, SUB), lambda i, j: (done(i), jnp.where(i == 0, 0, j)))
    seg_shape = jax.ShapeDtypeStruct((rows, SEG), BF16)
    in_specs = [
        pl.BlockSpec((tm // N_SUB, d), lambda i, j: (jnp.minimum(i, nt - 1) * N_SUB + j, 0)),
        vec(), vec(), vec(),
        *[_w_spec(first) for _, first in w_groups],
        _convw_spec(layer),
        tab(), tab(), tab(),
    ]
    operands = [x, norm_w, shift, scale, *[w for w, _ in w_groups], conv_w, cos, sa, sb]
    edge = lambda: pl.BlockSpec((1, 8, SUB), lambda i, j: (done(i), 0, jnp.where(i == 0, 0, j)))
    out_specs = [out() for _ in range(6)] + [edge(), edge()]
    out_shape = [seg_shape] * 6 + [jax.ShapeDtypeStruct((nt, 8, SEG), F32)] * 2
    n_blocks = W_CAST_BLOCKS
    assert n_blocks <= (nt + 1) * N_SUB
    blk = lambda i, j: jnp.minimum(i * N_SUB + j, n_blocks - 1)
    for w_all, which in to_round:
        n_rows, n_cols = w_all.shape[1:]
        assert n_rows % n_blocks == 0
        in_specs.append(pl.BlockSpec((None, n_rows // n_blocks, n_cols),
                                     lambda i, j, which=which: (which, blk(i, j), 0)))
        operands.append(w_all)
        out_specs.append(pl.BlockSpec((n_rows // n_blocks, n_cols), lambda i, j: (blk(i, j), 0)))
        out_shape.append(jax.ShapeDtypeStruct((n_rows, n_cols), BF16))
    res = pl.pallas_call(
        kern,
        grid=(nt + 1, N_SUB),
        in_specs=in_specs,
        out_specs=out_specs,
        out_shape=out_shape,
        scratch_shapes=[pltpu.VMEM((tm, d), BF16), pltpu.VMEM((tm, d), BF16),
                        pltpu.VMEM((N_SUB, 8, SUB), F32)],
        compiler_params=_params(56, 2),
    )(*operands)
    return res[:6], res[6:8], res[8:]


def _inproj_kv_kernel(x_ref, nw_ref, shift_ref, scale_ref, wk_ref, wv_ref, k_ref, v_ref, hx_ref):
    @pl.when(pl.program_id(1) == 0)
    def _():
        _prologue(x_ref, nw_ref, shift_ref, scale_ref, hx_ref)

    hx = hx_ref[...]
    k_ref[...] = (jnp.dot(hx, wk_ref[...], preferred_element_type=F32) * K_SCALE).astype(BF16)
    v_ref[...] = jnp.dot(hx, wv_ref[...], preferred_element_type=F32).astype(BF16)


def _inproj_kv(x, norm_w, shift, scale, w_groups, *, tm):
    rows, d = x.shape
    vec = lambda: pl.BlockSpec((1, d), lambda i, j: (0, 0))
    out = lambda: pl.BlockSpec((tm, SUB), lambda i, j: (i, j))
    seg_shape = jax.ShapeDtypeStruct((rows, SEG), BF16)
    return pl.pallas_call(
        _inproj_kv_kernel,
        grid=(rows // tm, N_SUB),
        in_specs=[
            pl.BlockSpec((tm, d), lambda i, j: (i, 0)),
            vec(), vec(), vec(),
            _w_spec(w_groups[G_K][1]), _w_spec(w_groups[G_V][1]),
        ],
        out_specs=[out(), out()],
        out_shape=[seg_shape] * 2,
        scratch_shapes=[pltpu.VMEM((tm, d), BF16)],
        compiler_params=_params(32, 2),
    )(x, norm_w, shift, scale, w_groups[G_K][0], w_groups[G_V][0])


def _states_kernel(lgf_ref, lgb_ref, kf_ref, vf_ref, kb_ref, vb_ref, s0f_ref, s0b_ref,
                   sf_ref, sb_ref, ff_ref, fb_ref, *, nc):
    t = pl.program_id(0)

    @pl.when(t == 0)
    def _():
        ff_ref[...] = s0f_ref[...]
        fb_ref[...] = s0b_ref[...]

    row = lax.broadcasted_iota(jnp.int32, (CHUNK, HEAD_DIM), 0).astype(F32)
    tn = (((0,), (0,)), ((), ()))
    for h in range(RET_HEADS):
        hs = slice(h * HEAD_DIM, (h + 1) * HEAD_DIM)
        lgf = lgf_ref[h]
        lgb = lgb_ref[h]
        kdf = jnp.exp(lgf * (CHUNK - 1.0 - row))
        kdb = jnp.exp(lgb * row)
        cdf = jnp.exp(lgf * CHUNK + 0.0 * row)
        cdb = jnp.exp(lgb * CHUNK + 0.0 * row)

        s = ff_ref[h]
        for c in range(nc):
            rs = slice(c * CHUNK, (c + 1) * CHUNK)
            sf_ref[c, h] = s.astype(BF16)
            kd = (kf_ref[rs, hs].astype(F32) * kdf).astype(BF16)
            s = cdf * s + lax.dot_general(kd, vf_ref[rs, hs], tn, preferred_element_type=F32)
        ff_ref[h] = s

        s = fb_ref[h]
        for c in reversed(range(nc)):
            rs = slice(c * CHUNK, (c + 1) * CHUNK)
            sb_ref[c, h] = s.astype(BF16)
            kd = (kb_ref[rs, hs].astype(F32) * kdb).astype(BF16)
            s = cdb * s + lax.dot_general(kd, vb_ref[rs, hs], tn, preferred_element_type=F32)
        fb_ref[h] = s


def _states(lg_f, lg_b, k, v, s0f, s0b, *, tr):
    rows = k.shape[0]
    nt = rows // tr
    nc = tr // CHUNK
    n_chunks = rows // CHUNK
    kern = functools.partial(_states_kernel, nc=nc)
    smem = lambda: pl.BlockSpec(memory_space=pltpu.SMEM)
    st = lambda: pl.BlockSpec((RET_HEADS, HEAD_DIM, HEAD_DIM), lambda t: (0, 0, 0))
    fwd = lambda: pl.BlockSpec((tr, SEG), lambda t: (t, 0))
    bwd = lambda: pl.BlockSpec((tr, SEG), lambda t: (nt - 1 - t, 0))
    seq_shape = pltpu.HBM((n_chunks, RET_HEADS, HEAD_DIM, HEAD_DIM), BF16)
    fin_shape = jax.ShapeDtypeStruct((RET_HEADS, HEAD_DIM, HEAD_DIM), F32)
    return pl.pallas_call(
        kern,
        grid=(nt,),
        in_specs=[smem(), smem(), fwd(), fwd(), bwd(), bwd(), st(), st()],
        out_specs=[
            pl.BlockSpec((nc, RET_HEADS, HEAD_DIM, HEAD_DIM), lambda t: (t, 0, 0, 0)),
            pl.BlockSpec((nc, RET_HEADS, HEAD_DIM, HEAD_DIM), lambda t: (nt - 1 - t, 0, 0, 0)),
            st(), st(),
        ],
        out_shape=[seq_shape, seq_shape, fin_shape, fin_shape],
        compiler_params=_params(56, 1),
    )(lg_f, lg_b, k, v, k, v, s0f, s0b)


def _lane_means(ts):
    parts = []
    for t in ts:
        hi = t.astype(BF16)
        lo = (t - hi.astype(F32)).astype(BF16)
        parts.append(jnp.concatenate([hi, lo], axis=1))
    rows = ts[0].shape[0]
    ones = jnp.full((2 * HEAD_DIM, HEAD_DIM), 1.0 / HEAD_DIM, BF16)
    m = jnp.dot(jnp.concatenate(parts, axis=0), ones, preferred_element_type=F32)
    return [m[n * rows:(n + 1) * rows] for n in range(len(ts))]


def _out_kernel(lgf_ref, lgb_ref,
                ycv_ref, zc_ref, q_ref, k_ref, v_ref, zr_ref, eb_ref, ec_ref,
                sf_ref, sb_ref, x_ref, gate_ref, convw_ref, cnw_ref, gnw_ref, wout_ref, fnw_ref,
                o_ref,
                mask_ref, qdf_ref, qdb_ref, ya_ref, yb_ref, *, tm, n_tiles, edge_every, final):
    s_id = pl.program_id(0)
    i = jnp.maximum(s_id - 1, 0)
    nc = tm // CHUNK
    at_edge = jnp.logical_and((i + 1) % edge_every == 0, i + 1 < n_tiles)

    @pl.when(s_id == 0)
    def _():
        r = lax.broadcasted_iota(jnp.int32, (CHUNK, CHUNK), 0).astype(F32)
        cc = lax.broadcasted_iota(jnp.int32, (CHUNK, CHUNK), 1).astype(F32)
        d = r - cc
        for h in range(RET_HEADS):
            lgf = lgf_ref[h]
            lgb = lgb_ref[h]
            mf = jnp.where(d >= 0, jnp.exp(lgf * jnp.maximum(d, 0.0)), 0.0)
            mb = jnp.where(d <= 0, jnp.exp(lgb * jnp.maximum(-d, 0.0)), 0.0)
            mask_ref[h] = mf + mb
            qdf_ref[h] = jnp.exp(lgf * (r + 1.0))
            qdb_ref[h] = jnp.exp(lgb * (CHUNK - r))

    def step(yr_old, yr_new):
        n_q = 4
        cw_ = D_CONV // n_q
        pw = D_MODEL // n_q

        if yr_old is not None:
            last_row = lax.broadcasted_iota(jnp.int32, (8, 1), 0) == 7
            cwt = convw_ref[0]
            proj_ret, y_l, ssq = [], [], None
            for n in range(n_q):
                proj_ret.append(jnp.dot(yr_old[...], wout_ref[D_CONV:, n * pw:(n + 1) * pw],
                                        preferred_element_type=F32))
                cs = slice(n * cw_, (n + 1) * cw_)
                y = ycv_ref[:, cs].astype(F32)
                below = eb_ref[0, 7:8, cs] * cwt[2:3, cs] * ec_ref[0, 0:1, cs]
                below = jnp.where(jnp.logical_and(at_edge, last_row), below, 0.0)
                y = jnp.concatenate([y[:tm - 8], y[tm - 8:] + below], axis=0)
                y_l.append(y)
                part = jnp.sum(y * y, axis=-1, keepdims=True)
                ssq = part if ssq is None else ssq + part
            rinv = lax.rsqrt(ssq * (1.0 / D_CONV) + EPS)

            y_conv = jnp.concatenate(
                [(zc_ref[:, n * cw_:(n + 1) * cw_].astype(F32)
                  * (y_l[n] * rinv * cnw_ref[:, n * cw_:(n + 1) * cw_])).astype(BF16)
                 for n in range(n_q)], axis=1)

        pending = list(range(n_q)) if yr_old is not None else []

        def finish_chunk():
            if not pending:
                return
            n = pending.pop(0)
            cols = slice(n * pw, (n + 1) * pw)
            pc = jnp.dot(y_conv, wout_ref[:D_CONV, cols], preferred_element_type=F32)
            o_ref[:, cols] = x_ref[:, cols] + gate_ref[:, cols] * (pc + proj_ret[n])

        nt = (((1,), (1,)), ((), ()))
        rs = lambda c: slice(c * CHUNK, (c + 1) * CHUNK)
        hs = lambda h: slice(h * HEAD_DIM, (h + 1) * HEAD_DIM)
        rounds = range(0, nc, RET_ROUND_CHUNKS) if yr_new is not None else []
        per_round = -(-n_q // max(len(rounds), 1) // 2)
        for c0 in rounds:
            pairs = [(c, h) for c in range(c0, c0 + RET_ROUND_CHUNKS) for h in range(RET_HEADS)]
            q_l = [q_ref[rs(c), hs(h)] for c, h in pairs]
            s_l = [lax.dot_general(q, k_ref[rs(c), hs(h)], nt, preferred_element_type=F32)
                   for q, (c, h) in zip(q_l, pairs)]
            for _ in range(per_round):
                finish_chunk()
            lhs_l = []
            for q, s, (c, h) in zip(q_l, s_l, pairs):
                qf32 = q.astype(F32)
                lhs_l.append(jnp.concatenate(
                    [(s * mask_ref[h]).astype(BF16), (qf32 * qdf_ref[h]).astype(BF16),
                     (qf32 * qdb_ref[h]).astype(BF16)], axis=1))
            o_l = [jnp.dot(lhs, jnp.concatenate([v_ref[rs(c), hs(h)], sf_ref[c, h],
                                                 sb_ref[c, h]], axis=0),
                           preferred_element_type=F32)
                   for lhs, (c, h) in zip(lhs_l, pairs)]
            for _ in range(per_round):
                finish_chunk()
            per_chunk = lambda xs: [xs[n * RET_HEADS:(n + 1) * RET_HEADS]
                                    for n in range(len(xs) // RET_HEADS)]
            mean_l = [m for grp in per_chunk(o_l) for m in _lane_means(grp)]
            oc_l = [o - m for o, m in zip(o_l, mean_l)]
            var_l = [m for grp in per_chunk([oc * oc for oc in oc_l]) for m in _lane_means(grp)]
            for oc, var, (c, h) in zip(oc_l, var_l, pairs):
                on = oc * lax.rsqrt(var + EPS) * gnw_ref[:, hs(h)]
                yr_new[rs(c), hs(h)] = (zr_ref[rs(c), hs(h)].astype(F32) * on).astype(BF16)
        while pending:
            finish_chunk()
        if final and yr_old is not None:
            xn = o_ref[...]
            ms = jnp.mean(xn * xn, axis=-1, keepdims=True)
            o_ref[...] = xn * lax.rsqrt(ms + EPS) * fnw_ref[...]

    middle = jnp.logical_and(s_id > 0, s_id < n_tiles)

    @pl.when(s_id == 0)
    def _():
        step(None, ya_ref)

    @pl.when(jnp.logical_and(middle, s_id % 2 == 0))
    def _():
        step(yb_ref, ya_ref)

    @pl.when(jnp.logical_and(middle, s_id % 2 == 1))
    def _():
        step(ya_ref, yb_ref)

    @pl.when(s_id == n_tiles)
    def _():
        step(ya_ref if n_tiles % 2 == 1 else yb_ref, None)


def _out(lg_f, lg_b, segs, edges, sf, sb, x, gate, conv_w, cnw, gnw, w_out, fnw, layer, *,
         tm, final):
    ycv, zc, q, k, v, zr = segs
    eb, ec = edges
    rows, d = x.shape
    nt = rows // tm
    nc = tm // CHUNK
    n_edge = eb.shape[0]
    edge_every = nt // n_edge
    kern = functools.partial(_out_kernel, tm=tm, n_tiles=nt, edge_every=edge_every, final=final)
    smem = lambda: pl.BlockSpec(memory_space=pltpu.SMEM)
    cur = lambda s: jnp.minimum(s, nt - 1)
    prv = lambda s: jnp.maximum(s - 1, 0)
    seg = lambda: pl.BlockSpec((tm, SEG), lambda s: (cur(s), 0))
    fin = lambda: pl.BlockSpec((tm, SEG), lambda s: (prv(s), 0))
    st = lambda: pl.BlockSpec((nc, RET_HEADS, HEAD_DIM, HEAD_DIM), lambda s: (cur(s), 0, 0, 0))
    full = lambda shape: pl.BlockSpec(shape, lambda s: (0,) * len(shape))
    return pl.pallas_call(
        kern,
        grid=(nt + 1,),
        in_specs=[
            smem(), smem(),
            fin(), fin(), seg(), seg(), seg(), seg(),
            pl.BlockSpec((1, 8, SEG), lambda s: (prv(s) // edge_every, 0, 0)),
            pl.BlockSpec((1, 8, SEG),
                         lambda s: (jnp.minimum(prv(s) // edge_every + 1, n_edge - 1), 0, 0)),
            st(), st(),
            pl.BlockSpec((tm, d), lambda s: (prv(s), 0)),
            full((1, d)),
            pl.BlockSpec((1, 3, D_CONV), lambda s: (layer, 0, 0)),
            full((1, D_CONV)), full((1, D_RET)),
            pl.BlockSpec((d, d), lambda s: (0, 0), pipeline_mode=pl.Buffered(1)),
            full((1, d)),
        ],
        out_specs=pl.BlockSpec((tm, d), lambda s: (prv(s), 0)),
        out_shape=jax.ShapeDtypeStruct((rows, d), F32),
        scratch_shapes=[
            pltpu.VMEM((RET_HEADS, CHUNK, CHUNK), F32),
            pltpu.VMEM((RET_HEADS, CHUNK, HEAD_DIM), F32),
            pltpu.VMEM((RET_HEADS, CHUNK, HEAD_DIM), F32),
            pltpu.VMEM((tm, D_RET), BF16),
            pltpu.VMEM((tm, D_RET), BF16),
        ],
        compiler_params=_params(56, 1),
    )(lg_f, lg_b, ycv, zc, q, k, v, zr, eb, ec, sf, sb, x, gate, conv_w, cnw, gnw,
      w_out, fnw)


def _rope_tables(seq):
    t = np.arange(seq)
    row = (t // GRID_W).astype(np.float64)
    col = (t % GRID_W).astype(np.float64)
    inv = ROPE_BASE ** (-np.arange(ROPE_F, dtype=np.float64) / ROPE_F)
    ar = row[:, None] * inv[None, :]
    ac = col[:, None] * inv[None, :]
    z = np.zeros_like(ar)
    cos = np.concatenate([np.cos(ar), np.cos(ar), np.cos(ac), np.cos(ac)], axis=-1)
    sa = np.concatenate([-np.sin(ar), z, -np.sin(ac), z], axis=-1)
    sb = np.concatenate([z, np.sin(ar), z, np.sin(ac)], axis=-1)
    return tuple(jnp.asarray(a, dtype=F32) for a in (cos, sa, sb))


def kernel(x, c, ctx, c_ctx, norm_w, w_mod, b_mod, w_in, conv_w, conv_norm_w, ret_norm_w,
           ret_decay_f, ret_decay_b, w_out, final_norm_w):
    batch, seq, d = x.shape
    assert batch == 1 and d == D_MODEL and seq % 1024 == 0
    depth = norm_w.shape[0]
    ctx_len = ctx.shape[1]
    xs = x[0]
    cs = ctx[0]

    cv = jnp.zeros((8, d), F32).at[0].set(c[0]).at[1].set(c_ctx)
    mod = _modulation(cv, w_mod, b_mod)
    tables = _rope_tables(seq)
    ctx_tables = tuple(jnp.zeros((ctx_len, HEAD_DIM), F32) for _ in range(3))
    zero_state = jnp.zeros((RET_HEADS, HEAD_DIM, HEAD_DIM), F32)
    zero_edge = jnp.zeros((1, 8, D_CONV), F32)
    fnw = final_norm_w.reshape(1, d)
    w_groups = None

    for layer in range(depth):
        update_ctx = layer < depth - 1
        lg_f = -jnp.exp(ret_decay_f[layer].astype(F32))
        lg_b = -jnp.exp(ret_decay_b[layer].astype(F32))
        nw = norm_w[layer].reshape(1, d)
        m = mod[layer]
        shift, scale, gate = m[0:1, 0:d], m[0:1, d:2 * d], m[0:1, 2 * d:3 * d]
        shift_c, scale_c, gate_c = m[1:2, 0:d], m[1:2, d:2 * d], m[1:2, 2 * d:3 * d]
        cnw = conv_norm_w[layer].reshape(1, D_CONV)
        gnw = ret_norm_w[layer].reshape(1, D_RET)

        if update_ctx:
            segs_c, w_groups = _inproj(cs, nw, shift_c, scale_c, w_groups, conv_w, layer,
                                       ctx_tables, rope=False,
                                       w_f32=w_in if w_groups is None else None)
            k_c, v_c = segs_c[3], segs_c[4]
        else:
            if w_groups is None:
                w_groups = _split_groups(w_in[layer].astype(BF16))
            k_c, v_c = _inproj_kv(cs, nw, shift_c, scale_c, w_groups, tm=ctx_len)
        sf_c, sb_c, s_f, s_b = _states(lg_f, lg_b, k_c, v_c, zero_state, zero_state, tr=ctx_len)

        to_round = [(w_out, layer)] + ([(w_in, layer + 1)] if layer + 1 < depth else [])
        segs, edges, rounded = _inproj_latent(xs, nw, shift, scale, w_groups, conv_w, layer,
                                              tables, to_round, tm=1024)
        w_out_b = rounded[0]
        w_groups = _split_groups(rounded[1]) if layer + 1 < depth else None
        sf, sb, _, _ = _states(lg_f, lg_b, segs[3], segs[4], s_f, s_b, tr=1024)
        xs = _out(lg_f, lg_b, segs, edges, sf, sb, xs, gate, conv_w, cnw, gnw, w_out_b, fnw, layer,
                  tm=512, final=not update_ctx)
        if update_ctx:
            cs = _out(lg_f, lg_b, segs_c, (zero_edge, zero_edge), sf_c, sb_c, cs, gate_c, conv_w,
                      cnw, gnw, w_out_b, fnw, layer, tm=ctx_len, final=False)
    return xs[None]
```

```python
import functools

import numpy as np
import jax
import jax.numpy as jnp
from jax import lax
from jax.experimental import pallas as pl
from jax.experimental.pallas import tpu as pltpu

D_MODEL = 2048
D_CONV = 1024
D_RET = 1024
RET_HEADS = 8
HEAD_DIM = 128
CHUNK = 128
SEG = 1024
N_SEG = 8
SUB = 256
N_SUB = SEG // SUB
GRID_W = 64
ROPE_BASE = 10000.0
ROPE_F = 32
EPS = 1e-6
K_SCALE = HEAD_DIM ** -0.5

F32 = jnp.float32
BF16 = jnp.bfloat16

G_H, G_B, G_C, G_ZC, G_Q, G_K, G_V, G_ZR = range(8)


def _silu(x):
    return x / (1.0 + jnp.exp(-x))


def _params(vmem_mb, n_axes, flags=None):
    return pltpu.CompilerParams(
        dimension_semantics=("arbitrary",) * n_axes,
        vmem_limit_bytes=vmem_mb * 1024 * 1024,
        flags=flags,
    )


def _mod_kernel(cv_ref, w_ref, b_ref, o_ref):
    s = _silu(cv_ref[...])
    o_ref[0] = jnp.dot(s.astype(BF16), w_ref[0].astype(BF16),
                       preferred_element_type=F32) + b_ref[0]


def _modulation(cv, w_mod, b_mod, tn=1024):
    depth, d, n = w_mod.shape
    return pl.pallas_call(
        _mod_kernel,
        grid=(depth, n // tn),
        in_specs=[
            pl.BlockSpec((8, d), lambda l, j: (0, 0)),
            pl.BlockSpec((1, d, tn), lambda l, j: (l, 0, j)),
            pl.BlockSpec((1, 1, tn), lambda l, j: (l, 0, j)),
        ],
        out_specs=pl.BlockSpec((1, 8, tn), lambda l, j: (l, 0, j)),
        out_shape=jax.ShapeDtypeStruct((depth, 8, n), F32),
        compiler_params=_params(40, 2),
    )(cv, w_mod, b_mod.reshape(depth, 1, n))


PROLOGUE_ROWS = 64
RET_ROUND_CHUNKS = 2
W_CAST_BLOCKS = 32


def _prologue(x_ref, nw_ref, shift_ref, scale_ref, hx_ref):
    gain = nw_ref[...] * (1.0 + scale_ref[...])
    shift = shift_ref[...]

    def body(r, carry):
        rows = pl.ds(pl.multiple_of(r * PROLOGUE_ROWS, PROLOGUE_ROWS), PROLOGUE_ROWS)
        x = x_ref[rows, :]
        ms = jnp.mean(x * x, axis=-1, keepdims=True)
        hx_ref[rows, :] = (x * lax.rsqrt(ms + EPS) * gain + shift).astype(BF16)
        return carry

    lax.fori_loop(0, x_ref.shape[0] // PROLOGUE_ROWS, body, 0)


def _rope_pair(acc, cos, sa, sb):
    outs = []
    for h in range(SUB // HEAD_DIM):
        a = acc[:, h * HEAD_DIM:(h + 1) * HEAD_DIM]
        outs.append(a * cos + pltpu.roll(a, HEAD_DIM - ROPE_F, 1) * sa
                    + pltpu.roll(a, ROPE_F, 1) * sb)
    return jnp.concatenate(outs, axis=1)


def _project_groups(hx, w_refs, convw_ref, table_refs, out_refs, rope, prev_row=None,
                    edge_refs=None):
    wh_ref, wb_ref, wc_ref, wzc_ref, wq_ref, wk_ref, wv_ref, wzr_ref = w_refs
    yb_ref, zc_ref, q_ref, k_ref, v_ref, zr_ref = out_refs
    tm = hx.shape[0]

    def seg(w_ref):
        return jnp.dot(hx, w_ref[...], preferred_element_type=F32)

    ch = seg(wc_ref) * seg(wh_ref)
    ridx = lax.broadcasted_iota(jnp.int32, (tm, 1), 0)
    above = pltpu.roll(ch, 1, 0)
    above = jnp.where(ridx == 0, 0.0 if prev_row is None else prev_row, above)
    below = jnp.where(ridx == tm - 1, 0.0, pltpu.roll(ch, tm - 1, 0))
    cw = convw_ref[...]
    b = seg(wb_ref)
    yb_ref[...] = (b * (above * cw[0:1, :] + ch * cw[1:2, :] + below * cw[2:3, :])).astype(BF16)
    if edge_refs is not None:
        edge_refs[0][0] = b[tm - 8:tm, :]
        edge_refs[1][0] = ch[0:8, :]
    zc_ref[...] = _silu(seg(wzc_ref)).astype(BF16)
    if rope:
        cos, sa, sb = (t[...] for t in table_refs)
        q_ref[...] = _rope_pair(seg(wq_ref), cos, sa, sb).astype(BF16)
        k_ref[...] = _rope_pair(seg(wk_ref), cos * K_SCALE, sa * K_SCALE, sb * K_SCALE).astype(BF16)
    else:
        q_ref[...] = seg(wq_ref).astype(BF16)
        k_ref[...] = (seg(wk_ref) * K_SCALE).astype(BF16)
    v_ref[...] = seg(wv_ref).astype(BF16)
    zr_ref[...] = _silu(seg(wzr_ref)).astype(BF16)
    return ch[tm - 8:tm, :]


def _inproj_kernel(x_ref, nw_ref, shift_ref, scale_ref, *refs, rope, round_weights):
    w_refs, convw_ref, table_refs = refs[:8], refs[8], refs[9:12]
    out_refs, refs = refs[12:18], refs[18:]
    if round_weights:
        for w_ref, wb_ref in zip(w_refs, refs[:8]):
            wb_ref[...] = w_ref[...].astype(BF16)
        w_refs, refs = refs[:8], refs[8:]
    hx_ref, = refs

    @pl.when(pl.program_id(1) == 0)
    def _():
        _prologue(x_ref, nw_ref, shift_ref, scale_ref, hx_ref)

    _project_groups(hx_ref[...], w_refs, convw_ref, table_refs, out_refs, rope)


def _inproj_latent_kernel(xq_ref, nw_ref, shift_ref, scale_ref, *refs, tm, n_cast):
    w_refs, convw_ref, table_refs = refs[:8], refs[8], refs[9:12]
    cast_in_refs, refs = refs[12:12 + n_cast], refs[12 + n_cast:]
    out_refs, edge_refs, refs = refs[:6], refs[6:8], refs[8:]
    cast_out_refs, refs = refs[:n_cast], refs[n_cast:]
    hxa_ref, hxb_ref, carry_ref = refs
    i = pl.program_id(0)
    j = pl.program_id(1)
    quarter = tm // N_SUB
    row0 = pl.multiple_of(j * quarter, quarter)

    def project(hx_ref):
        prev_row = carry_ref[j][7:8, :]
        carry_ref[j] = _project_groups(hx_ref[...], w_refs, convw_ref, table_refs, out_refs,
                                       True, prev_row, edge_refs)

    def normalise(hx_ref):
        for src_ref, dst_ref in zip(cast_in_refs, cast_out_refs):
            dst_ref[...] = src_ref[...].astype(BF16)
        gain = nw_ref[...] * (1.0 + scale_ref[...])
        shift = shift_ref[...]
        for r in range(0, quarter, PROLOGUE_ROWS):
            x = xq_ref[r:r + PROLOGUE_ROWS, :]
            ms = jnp.mean(x * x, axis=-1, keepdims=True)
            hx_ref[pl.ds(row0 + r, PROLOGUE_ROWS), :] = (
                x * lax.rsqrt(ms + EPS) * gain + shift).astype(BF16)

    @pl.when(i == 0)
    def _():
        normalise(hxa_ref)
        carry_ref[j] = jnp.zeros(carry_ref.shape[1:], F32)

    @pl.when(i % 2 == 1)
    def _():
        normalise(hxb_ref)
        project(hxa_ref)

    @pl.when(jnp.logical_and(i > 0, i % 2 == 0))
    def _():
        normalise(hxa_ref)
        project(hxb_ref)


def _w_spec(first_block):
    return pl.BlockSpec((D_MODEL, SUB), lambda i, j: (0, first_block + j))


def _split_groups(w_bf16):
    return [(w_bf16, g * N_SUB) for g in range(N_SEG)]


def _convw_spec(layer):
    return pl.BlockSpec((None, 3, SUB), lambda i, j: (layer, 0, j))


def _inproj(x, norm_w, shift, scale, w_groups, conv_w, layer, tables, *, rope, w_f32=None):
    tm, d = x.shape
    cos, sa, sb = tables
    round_weights = w_f32 is not None
    kern = functools.partial(_inproj_kernel, rope=rope, round_weights=round_weights)
    vec = lambda: pl.BlockSpec((1, d), lambda i, j: (0, 0))
    tab = lambda: pl.BlockSpec((tm, HEAD_DIM), lambda i, j: (i, 0))
    out = lambda: pl.BlockSpec((tm, SUB), lambda i, j: (i, j))
    seg_shape = jax.ShapeDtypeStruct((tm, SEG), BF16)
    out_specs = [out() for _ in range(6)]
    out_shape = [seg_shape] * 6
    if round_weights:
        w_specs = [pl.BlockSpec((None, d, SUB), lambda i, j, g=g: (layer, 0, g * N_SUB + j))
                   for g in range(N_SEG)]
        w_operands = [w_f32] * N_SEG
        out_specs += [pl.BlockSpec((d, SUB), lambda i, j: (0, j)) for _ in range(N_SEG)]
        out_shape += [jax.ShapeDtypeStruct((d, SEG), BF16)] * N_SEG
    else:
        w_specs = [_w_spec(first) for _, first in w_groups]
        w_operands = [w for w, _ in w_groups]
    res = pl.pallas_call(
        kern,
        grid=(1, N_SUB),
        in_specs=[
            pl.BlockSpec((tm, d), lambda i, j: (i, 0)),
            vec(), vec(), vec(),
            *w_specs,
            _convw_spec(layer),
            tab(), tab(), tab(),
        ],
        out_specs=out_specs,
        out_shape=out_shape,
        scratch_shapes=[pltpu.VMEM((tm, d), BF16)],
        compiler_params=_params(56, 2),
    )(x, norm_w, shift, scale, *w_operands, conv_w, cos, sa, sb)
    return res[:6], ([(w, 0) for w in res[6:]] if round_weights else w_groups)


def _inproj_latent(x, norm_w, shift, scale, w_groups, conv_w, layer, tables, to_round, *, tm):
    rows, d = x.shape
    nt = rows // tm
    cos, sa, sb = tables
    kern = functools.partial(_inproj_latent_kernel, tm=tm, n_cast=len(to_round))
    done = lambda i: jnp.maximum(i - 1, 0)
    vec = lambda: pl.BlockSpec((1, d), lambda i, j: (0, 0))
    tab = lambda: pl.BlockSpec((tm, HEAD_DIM), lambda i, j: (done(i), 0))
    out = lambda: pl.BlockSpec((tm, SUB), lambda i, j: (done(i), jnp.where(i == 0, 0, j)))
    seg_shape = jax.ShapeDtypeStruct((rows, SEG), BF16)
    in_specs = [
        pl.BlockSpec((tm // N_SUB, d), lambda i, j: (jnp.minimum(i, nt - 1) * N_SUB + j, 0)),
        vec(), vec(), vec(),
        *[_w_spec(first) for _, first in w_groups],
        _convw_spec(layer),
        tab(), tab(), tab(),
    ]
    operands = [x, norm_w, shift, scale, *[w for w, _ in w_groups], conv_w, cos, sa, sb]
    edge = lambda: pl.BlockSpec((1, 8, SUB), lambda i, j: (done(i), 0, jnp.where(i == 0, 0, j)))
    out_specs = [out() for _ in range(6)] + [edge(), edge()]
    out_shape = [seg_shape] * 6 + [jax.ShapeDtypeStruct((nt, 8, SEG), F32)] * 2
    n_blocks = W_CAST_BLOCKS
    assert n_blocks <= (nt + 1) * N_SUB
    blk = lambda i, j: jnp.minimum(i * N_SUB + j, n_blocks - 1)
    for w_all, which in to_round:
        n_rows, n_cols = w_all.shape[1:]
        assert n_rows % n_blocks == 0
        in_specs.append(pl.BlockSpec((None, n_rows // n_blocks, n_cols),
                                     lambda i, j, which=which: (which, blk(i, j), 0)))
        operands.append(w_all)
        out_specs.append(pl.BlockSpec((n_rows // n_blocks, n_cols), lambda i, j: (blk(i, j), 0)))
        out_shape.append(jax.ShapeDtypeStruct((n_rows, n_cols), BF16))
    res = pl.pallas_call(
        kern,
        grid=(nt + 1, N_SUB),
        in_specs=in_specs,
        out_specs=out_specs,
        out_shape=out_shape,
        scratch_shapes=[pltpu.VMEM((tm, d), BF16), pltpu.VMEM((tm, d), BF16),
                        pltpu.VMEM((N_SUB, 8, SUB), F32)],
        compiler_params=_params(56, 2),
    )(*operands)
    return res[:6], res[6:8], res[8:]


def _inproj_kv_kernel(x_ref, nw_ref, shift_ref, scale_ref, wk_ref, wv_ref, k_ref, v_ref, hx_ref):
    @pl.when(pl.program_id(1) == 0)
    def _():
        _prologue(x_ref, nw_ref, shift_ref, scale_ref, hx_ref)

    hx = hx_ref[...]
    k_ref[...] = (jnp.dot(hx, wk_ref[...], preferred_element_type=F32) * K_SCALE).astype(BF16)
    v_ref[...] = jnp.dot(hx, wv_ref[...], preferred_element_type=F32).astype(BF16)


def _inproj_kv(x, norm_w, shift, scale, w_groups, *, tm):
    rows, d = x.shape
    vec = lambda: pl.BlockSpec((1, d), lambda i, j: (0, 0))
    out = lambda: pl.BlockSpec((tm, SUB), lambda i, j: (i, j))
    seg_shape = jax.ShapeDtypeStruct((rows, SEG), BF16)
    return pl.pallas_call(
        _inproj_kv_kernel,
        grid=(rows // tm, N_SUB),
        in_specs=[
            pl.BlockSpec((tm, d), lambda i, j: (i, 0)),
            vec(), vec(), vec(),
            _w_spec(w_groups[G_K][1]), _w_spec(w_groups[G_V][1]),
        ],
        out_specs=[out(), out()],
        out_shape=[seg_shape] * 2,
        scratch_shapes=[pltpu.VMEM((tm, d), BF16)],
        compiler_params=_params(32, 2),
    )(x, norm_w, shift, scale, w_groups[G_K][0], w_groups[G_V][0])


def _states_kernel(lgf_ref, lgb_ref, k_ref, v_ref, s0f_ref, s0b_ref,
                   sf_ref, sb_ref, ff_ref, fb_ref, kvb_ref, *, nc, nt):
    t = pl.program_id(0)

    @pl.when(t == 0)
    def _():
        ff_ref[...] = s0f_ref[...]
        fb_ref[...] = s0b_ref[...]

    row = lax.broadcasted_iota(jnp.int32, (CHUNK, HEAD_DIM), 0).astype(F32)
    tn = (((0,), (0,)), ((), ()))

    @pl.when(t < nt)
    def _():
        for h in range(RET_HEADS):
            hs = slice(h * HEAD_DIM, (h + 1) * HEAD_DIM)
            kdf = jnp.exp(lgf_ref[h] * (CHUNK - 1.0 - row))
            kdb = jnp.exp(lgb_ref[h] * row)
            cdf = jnp.exp(lgf_ref[h] * CHUNK + 0.0 * row)
            s = ff_ref[h]
            for c in range(nc):
                rs = slice(c * CHUNK, (c + 1) * CHUNK)
                sf_ref[c, h] = s.astype(BF16)
                k = k_ref[rs, hs].astype(F32)
                v = v_ref[rs, hs]
                s = cdf * s + lax.dot_general((k * kdf).astype(BF16), v, tn,
                                              preferred_element_type=F32)
                kvb_ref[t * nc + c, h] = lax.dot_general((k * kdb).astype(BF16), v, tn,
                                                         preferred_element_type=F32)
            ff_ref[h] = s

    @pl.when(t >= nt)
    def _():
        tile = 2 * nt - 1 - t
        for h in range(RET_HEADS):
            cdb = jnp.exp(lgb_ref[h] * CHUNK + 0.0 * row)
            s = fb_ref[h]
            for c in reversed(range(nc)):
                sb_ref[c, h] = s.astype(BF16)
                s = cdb * s + kvb_ref[tile * nc + c, h]
            fb_ref[h] = s


def _states(lg_f, lg_b, k, v, s0f, s0b, *, tr):
    rows = k.shape[0]
    nt = rows // tr
    nc = tr // CHUNK
    n_chunks = rows // CHUNK
    kern = functools.partial(_states_kernel, nc=nc, nt=nt)
    smem = lambda: pl.BlockSpec(memory_space=pltpu.SMEM)
    st = lambda: pl.BlockSpec((RET_HEADS, HEAD_DIM, HEAD_DIM), lambda t: (0, 0, 0))
    rows_in = lambda: pl.BlockSpec((tr, SEG), lambda t: (jnp.minimum(t, nt - 1), 0))
    chunk_states = (nc, RET_HEADS, HEAD_DIM, HEAD_DIM)
    seq_shape = pltpu.HBM((n_chunks, RET_HEADS, HEAD_DIM, HEAD_DIM), BF16)
    fin_shape = jax.ShapeDtypeStruct((RET_HEADS, HEAD_DIM, HEAD_DIM), F32)
    return pl.pallas_call(
        kern,
        grid=(2 * nt,),
        in_specs=[smem(), smem(), rows_in(), rows_in(), st(), st()],
        out_specs=[
            pl.BlockSpec(chunk_states, lambda t: (jnp.minimum(t, nt - 1), 0, 0, 0)),
            pl.BlockSpec(chunk_states, lambda t: (nt - 1 - jnp.maximum(t - nt, 0), 0, 0, 0)),
            st(), st(),
        ],
        out_shape=[seq_shape, seq_shape, fin_shape, fin_shape],
        scratch_shapes=[pltpu.VMEM((n_chunks, RET_HEADS, HEAD_DIM, HEAD_DIM), F32)],
        compiler_params=_params(56, 1),
    )(lg_f, lg_b, k, v, s0f, s0b)


def _lane_means(ts):
    parts = []
    for t in ts:
        hi = t.astype(BF16)
        lo = (t - hi.astype(F32)).astype(BF16)
        parts.append(jnp.concatenate([hi, lo], axis=1))
    rows = ts[0].shape[0]
    ones = jnp.full((2 * HEAD_DIM, HEAD_DIM), 1.0 / HEAD_DIM, BF16)
    m = jnp.dot(jnp.concatenate(parts, axis=0), ones, preferred_element_type=F32)
    return [m[n * rows:(n + 1) * rows] for n in range(len(ts))]


def _out_kernel(lgf_ref, lgb_ref,
                ycv_ref, zc_ref, q_ref, k_ref, v_ref, zr_ref, eb_ref, ec_ref,
                sf_ref, sb_ref, x_ref, gate_ref, convw_ref, cnw_ref, gnw_ref, wout_ref, fnw_ref,
                o_ref,
                mask_ref, qdf_ref, qdb_ref, ya_ref, yb_ref, *, tm, n_tiles, edge_every, final):
    s_id = pl.program_id(0)
    i = jnp.maximum(s_id - 1, 0)
    nc = tm // CHUNK
    at_edge = jnp.logical_and((i + 1) % edge_every == 0, i + 1 < n_tiles)

    @pl.when(s_id == 0)
    def _():
        r = lax.broadcasted_iota(jnp.int32, (CHUNK, CHUNK), 0).astype(F32)
        cc = lax.broadcasted_iota(jnp.int32, (CHUNK, CHUNK), 1).astype(F32)
        d = r - cc
        for h in range(RET_HEADS):
            lgf = lgf_ref[h]
            lgb = lgb_ref[h]
            mf = jnp.where(d >= 0, jnp.exp(lgf * jnp.maximum(d, 0.0)), 0.0)
            mb = jnp.where(d <= 0, jnp.exp(lgb * jnp.maximum(-d, 0.0)), 0.0)
            mask_ref[h] = mf + mb
            qdf_ref[h] = jnp.exp(lgf * (r + 1.0))
            qdb_ref[h] = jnp.exp(lgb * (CHUNK - r))

    def step(yr_old, yr_new):
        n_q = 4
        cw_ = D_CONV // n_q
        pw = D_MODEL // n_q

        if yr_old is not None:
            last_row = lax.broadcasted_iota(jnp.int32, (8, 1), 0) == 7
            cwt = convw_ref[0]
            proj_ret, y_l, ssq = [], [], None
            for n in range(n_q):
                proj_ret.append(jnp.dot(yr_old[...], wout_ref[D_CONV:, n * pw:(n + 1) * pw],
                                        preferred_element_type=F32))
                cs = slice(n * cw_, (n + 1) * cw_)
                y = ycv_ref[:, cs].astype(F32)
                below = eb_ref[0, 7:8, cs] * cwt[2:3, cs] * ec_ref[0, 0:1, cs]
                below = jnp.where(jnp.logical_and(at_edge, last_row), below, 0.0)
                y = jnp.concatenate([y[:tm - 8], y[tm - 8:] + below], axis=0)
                y_l.append(y)
                part = jnp.sum(y * y, axis=-1, keepdims=True)
                ssq = part if ssq is None else ssq + part
            rinv = lax.rsqrt(ssq * (1.0 / D_CONV) + EPS)

            y_conv = jnp.concatenate(
                [(zc_ref[:, n * cw_:(n + 1) * cw_].astype(F32)
                  * (y_l[n] * rinv * cnw_ref[:, n * cw_:(n + 1) * cw_])).astype(BF16)
                 for n in range(n_q)], axis=1)

        pending = list(range(n_q)) if yr_old is not None else []

        def finish_chunk():
            if not pending:
                return
            n = pending.pop(0)
            cols = slice(n * pw, (n + 1) * pw)
            pc = jnp.dot(y_conv, wout_ref[:D_CONV, cols], preferred_element_type=F32)
            o_ref[:, cols] = x_ref[:, cols] + gate_ref[:, cols] * (pc + proj_ret[n])

        nt = (((1,), (1,)), ((), ()))
        rs = lambda c: slice(c * CHUNK, (c + 1) * CHUNK)
        hs = lambda h: slice(h * HEAD_DIM, (h + 1) * HEAD_DIM)
        rounds = range(0, nc, RET_ROUND_CHUNKS) if yr_new is not None else []
        per_round = -(-n_q // max(len(rounds), 1) // 2)
        for c0 in rounds:
            pairs = [(c, h) for c in range(c0, c0 + RET_ROUND_CHUNKS) for h in range(RET_HEADS)]
            q_l = [q_ref[rs(c), hs(h)] for c, h in pairs]
            s_l = [lax.dot_general(q, k_ref[rs(c), hs(h)], nt, preferred_element_type=F32)
                   for q, (c, h) in zip(q_l, pairs)]
            for _ in range(per_round):
                finish_chunk()
            lhs_l = []
            for q, s, (c, h) in zip(q_l, s_l, pairs):
                qf32 = q.astype(F32)
                lhs_l.append(jnp.concatenate(
                    [(s * mask_ref[h]).astype(BF16), (qf32 * qdf_ref[h]).astype(BF16),
                     (qf32 * qdb_ref[h]).astype(BF16)], axis=1))
            o_l = [jnp.dot(lhs, jnp.concatenate([v_ref[rs(c), hs(h)], sf_ref[c, h],
                                                 sb_ref[c, h]], axis=0),
                           preferred_element_type=F32)
                   for lhs, (c, h) in zip(lhs_l, pairs)]
            for _ in range(per_round):
                finish_chunk()
            per_chunk = lambda xs: [xs[n * RET_HEADS:(n + 1) * RET_HEADS]
                                    for n in range(len(xs) // RET_HEADS)]
            mean_l = [m for grp in per_chunk(o_l) for m in _lane_means(grp)]
            oc_l = [o - m for o, m in zip(o_l, mean_l)]
            var_l = [m for grp in per_chunk([oc * oc for oc in oc_l]) for m in _lane_means(grp)]
            for oc, var, (c, h) in zip(oc_l, var_l, pairs):
                on = oc * lax.rsqrt(var + EPS) * gnw_ref[:, hs(h)]
                yr_new[rs(c), hs(h)] = (zr_ref[rs(c), hs(h)].astype(F32) * on).astype(BF16)
        while pending:
            finish_chunk()
        if final and yr_old is not None:
            xn = o_ref[...]
            ms = jnp.mean(xn * xn, axis=-1, keepdims=True)
            o_ref[...] = xn * lax.rsqrt(ms + EPS) * fnw_ref[...]

    middle = jnp.logical_and(s_id > 0, s_id < n_tiles)

    @pl.when(s_id == 0)
    def _():
        step(None, ya_ref)

    @pl.when(jnp.logical_and(middle, s_id % 2 == 0))
    def _():
        step(yb_ref, ya_ref)

    @pl.when(jnp.logical_and(middle, s_id % 2 == 1))
    def _():
        step(ya_ref, yb_ref)

    @pl.when(s_id == n_tiles)
    def _():
        step(ya_ref if n_tiles % 2 == 1 else yb_ref, None)


def _out(lg_f, lg_b, segs, edges, sf, sb, x, gate, conv_w, cnw, gnw, w_out, fnw, layer, *,
         tm, final):
    ycv, zc, q, k, v, zr = segs
    eb, ec = edges
    rows, d = x.shape
    nt = rows // tm
    nc = tm // CHUNK
    n_edge = eb.shape[0]
    edge_every = nt // n_edge
    kern = functools.partial(_out_kernel, tm=tm, n_tiles=nt, edge_every=edge_every, final=final)
    smem = lambda: pl.BlockSpec(memory_space=pltpu.SMEM)
    cur = lambda s: jnp.minimum(s, nt - 1)
    prv = lambda s: jnp.maximum(s - 1, 0)
    seg = lambda: pl.BlockSpec((tm, SEG), lambda s: (cur(s), 0))
    fin = lambda: pl.BlockSpec((tm, SEG), lambda s: (prv(s), 0))
    st = lambda: pl.BlockSpec((nc, RET_HEADS, HEAD_DIM, HEAD_DIM), lambda s: (cur(s), 0, 0, 0))
    full = lambda shape: pl.BlockSpec(shape, lambda s: (0,) * len(shape))
    return pl.pallas_call(
        kern,
        grid=(nt + 1,),
        in_specs=[
            smem(), smem(),
            fin(), fin(), seg(), seg(), seg(), seg(),
            pl.BlockSpec((1, 8, SEG), lambda s: (prv(s) // edge_every, 0, 0)),
            pl.BlockSpec((1, 8, SEG),
                         lambda s: (jnp.minimum(prv(s) // edge_every + 1, n_edge - 1), 0, 0)),
            st(), st(),
            pl.BlockSpec((tm, d), lambda s: (prv(s), 0)),
            full((1, d)),
            pl.BlockSpec((1, 3, D_CONV), lambda s: (layer, 0, 0)),
            full((1, D_CONV)), full((1, D_RET)),
            pl.BlockSpec((d, d), lambda s: (0, 0), pipeline_mode=pl.Buffered(1)),
            full((1, d)),
        ],
        out_specs=pl.BlockSpec((tm, d), lambda s: (prv(s), 0)),
        out_shape=jax.ShapeDtypeStruct((rows, d), F32),
        scratch_shapes=[
            pltpu.VMEM((RET_HEADS, CHUNK, CHUNK), F32),
            pltpu.VMEM((RET_HEADS, CHUNK, HEAD_DIM), F32),
            pltpu.VMEM((RET_HEADS, CHUNK, HEAD_DIM), F32),
            pltpu.VMEM((tm, D_RET), BF16),
            pltpu.VMEM((tm, D_RET), BF16),
        ],
        compiler_params=_params(56, 1),
    )(lg_f, lg_b, ycv, zc, q, k, v, zr, eb, ec, sf, sb, x, gate, conv_w, cnw, gnw,
      w_out, fnw)


def _rope_tables(seq):
    t = np.arange(seq)
    row = (t // GRID_W).astype(np.float64)
    col = (t % GRID_W).astype(np.float64)
    inv = ROPE_BASE ** (-np.arange(ROPE_F, dtype=np.float64) / ROPE_F)
    ar = row[:, None] * inv[None, :]
    ac = col[:, None] * inv[None, :]
    z = np.zeros_like(ar)
    cos = np.concatenate([np.cos(ar), np.cos(ar), np.cos(ac), np.cos(ac)], axis=-1)
    sa = np.concatenate([-np.sin(ar), z, -np.sin(ac), z], axis=-1)
    sb = np.concatenate([z, np.sin(ar), z, np.sin(ac)], axis=-1)
    return tuple(jnp.asarray(a, dtype=F32) for a in (cos, sa, sb))


def kernel(x, c, ctx, c_ctx, norm_w, w_mod, b_mod, w_in, conv_w, conv_norm_w, ret_norm_w,
           ret_decay_f, ret_decay_b, w_out, final_norm_w):
    batch, seq, d = x.shape
    assert batch == 1 and d == D_MODEL and seq % 1024 == 0
    depth = norm_w.shape[0]
    ctx_len = ctx.shape[1]
    xs = x[0]
    cs = ctx[0]

    cv = jnp.zeros((8, d), F32).at[0].set(c[0]).at[1].set(c_ctx)
    mod = _modulation(cv, w_mod, b_mod)
    tables = _rope_tables(seq)
    ctx_tables = tuple(jnp.zeros((ctx_len, HEAD_DIM), F32) for _ in range(3))
    zero_state = jnp.zeros((RET_HEADS, HEAD_DIM, HEAD_DIM), F32)
    zero_edge = jnp.zeros((1, 8, D_CONV), F32)
    fnw = final_norm_w.reshape(1, d)
    w_groups = None

    for layer in range(depth):
        update_ctx = layer < depth - 1
        lg_f = -jnp.exp(ret_decay_f[layer].astype(F32))
        lg_b = -jnp.exp(ret_decay_b[layer].astype(F32))
        nw = norm_w[layer].reshape(1, d)
        m = mod[layer]
        shift, scale, gate = m[0:1, 0:d], m[0:1, d:2 * d], m[0:1, 2 * d:3 * d]
        shift_c, scale_c, gate_c = m[1:2, 0:d], m[1:2, d:2 * d], m[1:2, 2 * d:3 * d]
        cnw = conv_norm_w[layer].reshape(1, D_CONV)
        gnw = ret_norm_w[layer].reshape(1, D_RET)

        if update_ctx:
            segs_c, w_groups = _inproj(cs, nw, shift_c, scale_c, w_groups, conv_w, layer,
                                       ctx_tables, rope=False,
                                       w_f32=w_in if w_groups is None else None)
            k_c, v_c = segs_c[3], segs_c[4]
        else:
            if w_groups is None:
                w_groups = _split_groups(w_in[layer].astype(BF16))
            k_c, v_c = _inproj_kv(cs, nw, shift_c, scale_c, w_groups, tm=ctx_len)
        sf_c, sb_c, s_f, s_b = _states(lg_f, lg_b, k_c, v_c, zero_state, zero_state, tr=ctx_len)

        to_round = [(w_out, layer)] + ([(w_in, layer + 1)] if layer + 1 < depth else [])
        segs, edges, rounded = _inproj_latent(xs, nw, shift, scale, w_groups, conv_w, layer,
                                              tables, to_round, tm=1024)
        w_out_b = rounded[0]
        w_groups = _split_groups(rounded[1]) if layer + 1 < depth else None
        sf, sb, _, _ = _states(lg_f, lg_b, segs[3], segs[4], s_f, s_b, tr=1024)
        xs = _out(lg_f, lg_b, segs, edges, sf, sb, xs, gate, conv_w, cnw, gnw, w_out_b, fnw, layer,
                  tm=512, final=not update_ctx)
        if update_ctx:
            cs = _out(lg_f, lg_b, segs_c, (zero_edge, zero_edge), sf_c, sb_c, cs, gate_c, conv_w,
                      cnw, gnw, w_out_b, fnw, layer, tm=ctx_len, final=False)
    return xs[None]
```

```python
import functools

import numpy as np
import jax
import jax.numpy as jnp
from jax import lax
from jax.experimental import pallas as pl
from jax.experimental.pallas import tpu as pltpu

D_MODEL = 2048
D_CONV = 1024
D_RET = 1024
RET_HEADS = 8
HEAD_DIM = 128
CHUNK = 128
SEG = 1024
N_SEG = 8
SUB = 256
N_SUB = SEG // SUB
GRID_W = 64
ROPE_BASE = 10000.0
ROPE_F = 32
EPS = 1e-6
K_SCALE = HEAD_DIM ** -0.5

F32 = jnp.float32
BF16 = jnp.bfloat16

G_H, G_B, G_C, G_ZC, G_Q, G_K, G_V, G_ZR = range(8)


def _silu(x):
    return x / (1.0 + jnp.exp(-x))


INPROJ_ROWS = 1024
OUT_ROWS = 512
STATES_ROWS = 1024
MOD_COLS = 1024
PROLOGUE_ROWS = 64
RET_ROUND_CHUNKS = 2
PROJ_CHUNKS = 4
W_CAST_BLOCKS = 32
VMEM_LARGE_MB = 56
VMEM_MOD_MB = 40
VMEM_SMALL_MB = 32
SUBLANES = 8


def _params(vmem_mb, n_axes):
    return pltpu.CompilerParams(
        dimension_semantics=("arbitrary",) * n_axes,
        vmem_limit_bytes=vmem_mb * 1024 * 1024,
    )


def _mod_kernel(cv_ref, w_ref, b_ref, o_ref):
    s = _silu(cv_ref[...])
    o_ref[0] = jnp.dot(s.astype(BF16), w_ref[0].astype(BF16),
                       preferred_element_type=F32) + b_ref[0]


def _modulation(cv, w_mod, b_mod):
    depth, d, n = w_mod.shape
    tn = MOD_COLS
    return pl.pallas_call(
        _mod_kernel,
        grid=(depth, n // tn),
        in_specs=[
            pl.BlockSpec((SUBLANES, d), lambda l, j: (0, 0)),
            pl.BlockSpec((1, d, tn), lambda l, j: (l, 0, j)),
            pl.BlockSpec((1, 1, tn), lambda l, j: (l, 0, j)),
        ],
        out_specs=pl.BlockSpec((1, SUBLANES, tn), lambda l, j: (l, 0, j)),
        out_shape=jax.ShapeDtypeStruct((depth, SUBLANES, n), F32),
        compiler_params=_params(VMEM_MOD_MB, 2),
    )(cv, w_mod, b_mod.reshape(depth, 1, n))


def _prologue(x_ref, nw_ref, shift_ref, scale_ref, hx_ref):
    gain = nw_ref[...] * (1.0 + scale_ref[...])
    shift = shift_ref[...]

    def body(r, carry):
        rows = pl.ds(pl.multiple_of(r * PROLOGUE_ROWS, PROLOGUE_ROWS), PROLOGUE_ROWS)
        x = x_ref[rows, :]
        ms = jnp.mean(x * x, axis=-1, keepdims=True)
        hx_ref[rows, :] = (x * lax.rsqrt(ms + EPS) * gain + shift).astype(BF16)
        return carry

    lax.fori_loop(0, x_ref.shape[0] // PROLOGUE_ROWS, body, 0)


def _rope_pair(acc, cos, sa, sb):
    outs = []
    for h in range(SUB // HEAD_DIM):
        a = acc[:, h * HEAD_DIM:(h + 1) * HEAD_DIM]
        outs.append(a * cos + pltpu.roll(a, HEAD_DIM - ROPE_F, 1) * sa
                    + pltpu.roll(a, ROPE_F, 1) * sb)
    return jnp.concatenate(outs, axis=1)


def _project_groups(hx, w_refs, convw_ref, out_refs, table_refs=None, prev_row=None,
                    edge_refs=None):
    wh_ref, wb_ref, wc_ref, wzc_ref, wq_ref, wk_ref, wv_ref, wzr_ref = w_refs
    yb_ref, zc_ref, q_ref, k_ref, v_ref, zr_ref = out_refs
    tm = hx.shape[0]

    def seg(w_ref):
        return jnp.dot(hx, w_ref[...], preferred_element_type=F32)

    ch = seg(wc_ref) * seg(wh_ref)
    ridx = lax.broadcasted_iota(jnp.int32, (tm, 1), 0)
    above = pltpu.roll(ch, 1, 0)
    above = jnp.where(ridx == 0, 0.0 if prev_row is None else prev_row, above)
    below = jnp.where(ridx == tm - 1, 0.0, pltpu.roll(ch, tm - 1, 0))
    cw = convw_ref[...]
    b = seg(wb_ref)
    yb_ref[...] = (b * (above * cw[0:1, :] + ch * cw[1:2, :] + below * cw[2:3, :])).astype(BF16)
    if edge_refs is not None:
        edge_refs[0][0] = b[tm - SUBLANES:tm, :]
        edge_refs[1][0] = ch[0:SUBLANES, :]
    zc_ref[...] = _silu(seg(wzc_ref)).astype(BF16)
    if table_refs is not None:
        cos, sa, sb = (t[...] for t in table_refs)
        q_ref[...] = _rope_pair(seg(wq_ref), cos, sa, sb).astype(BF16)
        k_ref[...] = _rope_pair(seg(wk_ref), cos * K_SCALE, sa * K_SCALE, sb * K_SCALE).astype(BF16)
    else:
        q_ref[...] = seg(wq_ref).astype(BF16)
        k_ref[...] = (seg(wk_ref) * K_SCALE).astype(BF16)
    v_ref[...] = seg(wv_ref).astype(BF16)
    zr_ref[...] = _silu(seg(wzr_ref)).astype(BF16)
    return ch[tm - SUBLANES:tm, :]


def _inproj_kernel(x_ref, nw_ref, shift_ref, scale_ref, *refs, round_weights):
    w_refs, convw_ref = refs[:8], refs[8]
    out_refs, refs = refs[9:15], refs[15:]
    if round_weights:
        for w_ref, wb_ref in zip(w_refs, refs[:8]):
            wb_ref[...] = w_ref[...].astype(BF16)
        w_refs, refs = refs[:8], refs[8:]
    hx_ref, = refs

    @pl.when(pl.program_id(1) == 0)
    def _():
        _prologue(x_ref, nw_ref, shift_ref, scale_ref, hx_ref)

    _project_groups(hx_ref[...], w_refs, convw_ref, out_refs)


def _inproj_latent_kernel(xq_ref, nw_ref, shift_ref, scale_ref, *refs, tm, n_cast):
    w_refs, convw_ref, table_refs = refs[:8], refs[8], refs[9:12]
    cast_in_refs, refs = refs[12:12 + n_cast], refs[12 + n_cast:]
    out_refs, edge_refs, refs = refs[:6], refs[6:8], refs[8:]
    cast_out_refs, refs = refs[:n_cast], refs[n_cast:]
    hxa_ref, hxb_ref, carry_ref = refs
    i = pl.program_id(0)
    j = pl.program_id(1)
    quarter = tm // N_SUB
    row0 = pl.multiple_of(j * quarter, quarter)

    def project(hx_ref):
        prev_row = carry_ref[j][SUBLANES - 1:SUBLANES, :]
        carry_ref[j] = _project_groups(hx_ref[...], w_refs, convw_ref, out_refs, table_refs,
                                       prev_row, edge_refs)

    def normalise(hx_ref):
        for src_ref, dst_ref in zip(cast_in_refs, cast_out_refs):
            dst_ref[...] = src_ref[...].astype(BF16)
        gain = nw_ref[...] * (1.0 + scale_ref[...])
        shift = shift_ref[...]
        for r in range(0, quarter, PROLOGUE_ROWS):
            x = xq_ref[r:r + PROLOGUE_ROWS, :]
            ms = jnp.mean(x * x, axis=-1, keepdims=True)
            hx_ref[pl.ds(row0 + r, PROLOGUE_ROWS), :] = (
                x * lax.rsqrt(ms + EPS) * gain + shift).astype(BF16)

    @pl.when(i == 0)
    def _():
        normalise(hxa_ref)
        carry_ref[j] = jnp.zeros(carry_ref.shape[1:], F32)

    @pl.when(i % 2 == 1)
    def _():
        normalise(hxb_ref)
        project(hxa_ref)

    @pl.when(jnp.logical_and(i > 0, i % 2 == 0))
    def _():
        normalise(hxa_ref)
        project(hxb_ref)


def _w_spec(first_block):
    return pl.BlockSpec((D_MODEL, SUB), lambda i, j: (0, first_block + j))


def _split_groups(w_bf16):
    return [(w_bf16, g * N_SUB) for g in range(N_SEG)]


def _convw_spec(layer):
    return pl.BlockSpec((None, 3, SUB), lambda i, j: (layer, 0, j))


def _inproj(x, norm_w, shift, scale, w_groups, conv_w, layer, *, w_f32=None):
    tm, d = x.shape
    round_weights = w_f32 is not None
    kern = functools.partial(_inproj_kernel, round_weights=round_weights)
    vec = lambda: pl.BlockSpec((1, d), lambda i, j: (0, 0))
    out = lambda: pl.BlockSpec((tm, SUB), lambda i, j: (i, j))
    seg_shape = jax.ShapeDtypeStruct((tm, SEG), BF16)
    out_specs = [out() for _ in range(6)]
    out_shape = [seg_shape] * 6
    if round_weights:
        w_specs = [pl.BlockSpec((None, d, SUB), lambda i, j, g=g: (layer, 0, g * N_SUB + j))
                   for g in range(N_SEG)]
        w_operands = [w_f32] * N_SEG
        out_specs += [pl.BlockSpec((d, SUB), lambda i, j: (0, j)) for _ in range(N_SEG)]
        out_shape += [jax.ShapeDtypeStruct((d, SEG), BF16)] * N_SEG
    else:
        w_specs = [_w_spec(first) for _, first in w_groups]
        w_operands = [w for w, _ in w_groups]
    res = pl.pallas_call(
        kern,
        grid=(1, N_SUB),
        in_specs=[
            pl.BlockSpec((tm, d), lambda i, j: (i, 0)),
            vec(), vec(), vec(),
            *w_specs,
            _convw_spec(layer),
        ],
        out_specs=out_specs,
        out_shape=out_shape,
        scratch_shapes=[pltpu.VMEM((tm, d), BF16)],
        compiler_params=_params(VMEM_LARGE_MB, 2),
    )(x, norm_w, shift, scale, *w_operands, conv_w)
    return res[:6], ([(w, 0) for w in res[6:]] if round_weights else w_groups)


def _inproj_latent(x, norm_w, shift, scale, w_groups, conv_w, layer, tables, to_round, *, tm):
    rows, d = x.shape
    nt = rows // tm
    cos, sa, sb = tables
    kern = functools.partial(_inproj_latent_kernel, tm=tm, n_cast=len(to_round))
    done = lambda i: jnp.maximum(i - 1, 0)
    vec = lambda: pl.BlockSpec((1, d), lambda i, j: (0, 0))
    tab = lambda: pl.BlockSpec((tm, HEAD_DIM), lambda i, j: (done(i), 0))
    out = lambda: pl.BlockSpec((tm, SUB), lambda i, j: (done(i), jnp.where(i == 0, 0, j)))
    seg_shape = jax.ShapeDtypeStruct((rows, SEG), BF16)
    in_specs = [
        pl.BlockSpec((tm // N_SUB, d), lambda i, j: (jnp.minimum(i, nt - 1) * N_SUB + j, 0)),
        vec(), vec(), vec(),
        *[_w_spec(first) for _, first in w_groups],
        _convw_spec(layer),
        tab(), tab(), tab(),
    ]
    operands = [x, norm_w, shift, scale, *[w for w, _ in w_groups], conv_w, cos, sa, sb]
    edge = lambda: pl.BlockSpec((1, SUBLANES, SUB),
                                lambda i, j: (done(i), 0, jnp.where(i == 0, 0, j)))
    out_specs = [out() for _ in range(6)] + [edge(), edge()]
    out_shape = [seg_shape] * 6 + [jax.ShapeDtypeStruct((nt, SUBLANES, SEG), F32)] * 2
    n_blocks = W_CAST_BLOCKS
    assert n_blocks <= (nt + 1) * N_SUB
    blk = lambda i, j: jnp.minimum(i * N_SUB + j, n_blocks - 1)
    for w_all, which in to_round:
        n_rows, n_cols = w_all.shape[1:]
        assert n_rows % n_blocks == 0
        in_specs.append(pl.BlockSpec((None, n_rows // n_blocks, n_cols),
                                     lambda i, j, which=which: (which, blk(i, j), 0)))
        operands.append(w_all)
        out_specs.append(pl.BlockSpec((n_rows // n_blocks, n_cols), lambda i, j: (blk(i, j), 0)))
        out_shape.append(jax.ShapeDtypeStruct((n_rows, n_cols), BF16))
    res = pl.pallas_call(
        kern,
        grid=(nt + 1, N_SUB),
        in_specs=in_specs,
        out_specs=out_specs,
        out_shape=out_shape,
        scratch_shapes=[pltpu.VMEM((tm, d), BF16), pltpu.VMEM((tm, d), BF16),
                        pltpu.VMEM((N_SUB, SUBLANES, SUB), F32)],
        compiler_params=_params(VMEM_LARGE_MB, 2),
    )(*operands)
    return res[:6], res[6:8], res[8:]


def _inproj_kv_kernel(x_ref, nw_ref, shift_ref, scale_ref, wk_ref, wv_ref, k_ref, v_ref, hx_ref):
    @pl.when(pl.program_id(1) == 0)
    def _():
        _prologue(x_ref, nw_ref, shift_ref, scale_ref, hx_ref)

    hx = hx_ref[...]
    k_ref[...] = (jnp.dot(hx, wk_ref[...], preferred_element_type=F32) * K_SCALE).astype(BF16)
    v_ref[...] = jnp.dot(hx, wv_ref[...], preferred_element_type=F32).astype(BF16)


def _inproj_kv(x, norm_w, shift, scale, w_groups, *, tm):
    rows, d = x.shape
    vec = lambda: pl.BlockSpec((1, d), lambda i, j: (0, 0))
    out = lambda: pl.BlockSpec((tm, SUB), lambda i, j: (i, j))
    seg_shape = jax.ShapeDtypeStruct((rows, SEG), BF16)
    return pl.pallas_call(
        _inproj_kv_kernel,
        grid=(rows // tm, N_SUB),
        in_specs=[
            pl.BlockSpec((tm, d), lambda i, j: (i, 0)),
            vec(), vec(), vec(),
            _w_spec(w_groups[G_K][1]), _w_spec(w_groups[G_V][1]),
        ],
        out_specs=[out(), out()],
        out_shape=[seg_shape] * 2,
        scratch_shapes=[pltpu.VMEM((tm, d), BF16)],
        compiler_params=_params(VMEM_SMALL_MB, 2),
    )(x, norm_w, shift, scale, w_groups[G_K][0], w_groups[G_V][0])


def _states_kernel(lgf_ref, lgb_ref, k_ref, v_ref, s0f_ref, s0b_ref,
                   sf_ref, sb_ref, ff_ref, fb_ref, kvb_ref, *, nc, nt):
    t = pl.program_id(0)

    @pl.when(t == 0)
    def _():
        ff_ref[...] = s0f_ref[...]
        fb_ref[...] = s0b_ref[...]

    row = lax.broadcasted_iota(jnp.int32, (CHUNK, HEAD_DIM), 0).astype(F32)
    tn = (((0,), (0,)), ((), ()))

    @pl.when(t < nt)
    def _():
        for h in range(RET_HEADS):
            hs = slice(h * HEAD_DIM, (h + 1) * HEAD_DIM)
            kdf = jnp.exp(lgf_ref[h] * (CHUNK - 1.0 - row))
            kdb = jnp.exp(lgb_ref[h] * row)
            cdf = jnp.exp(lgf_ref[h] * CHUNK + 0.0 * row)
            s = ff_ref[h]
            for c in range(nc):
                rs = slice(c * CHUNK, (c + 1) * CHUNK)
                sf_ref[c, h] = s.astype(BF16)
                k = k_ref[rs, hs].astype(F32)
                v = v_ref[rs, hs]
                s = cdf * s + lax.dot_general((k * kdf).astype(BF16), v, tn,
                                              preferred_element_type=F32)
                kvb_ref[t * nc + c, h] = lax.dot_general((k * kdb).astype(BF16), v, tn,
                                                         preferred_element_type=F32)
            ff_ref[h] = s

    @pl.when(t >= nt)
    def _():
        tile = 2 * nt - 1 - t
        for h in range(RET_HEADS):
            cdb = jnp.exp(lgb_ref[h] * CHUNK + 0.0 * row)
            s = fb_ref[h]
            for c in reversed(range(nc)):
                sb_ref[c, h] = s.astype(BF16)
                s = cdb * s + kvb_ref[tile * nc + c, h]
            fb_ref[h] = s


def _states(lg_f, lg_b, k, v, s0f, s0b, *, tr):
    rows = k.shape[0]
    nt = rows // tr
    nc = tr // CHUNK
    n_chunks = rows // CHUNK
    kern = functools.partial(_states_kernel, nc=nc, nt=nt)
    smem = lambda: pl.BlockSpec(memory_space=pltpu.SMEM)
    st = lambda: pl.BlockSpec((RET_HEADS, HEAD_DIM, HEAD_DIM), lambda t: (0, 0, 0))
    rows_in = lambda: pl.BlockSpec((tr, SEG), lambda t: (jnp.minimum(t, nt - 1), 0))
    chunk_states = (nc, RET_HEADS, HEAD_DIM, HEAD_DIM)
    seq_shape = pltpu.HBM((n_chunks, RET_HEADS, HEAD_DIM, HEAD_DIM), BF16)
    fin_shape = jax.ShapeDtypeStruct((RET_HEADS, HEAD_DIM, HEAD_DIM), F32)
    return pl.pallas_call(
        kern,
        grid=(2 * nt,),
        in_specs=[smem(), smem(), rows_in(), rows_in(), st(), st()],
        out_specs=[
            pl.BlockSpec(chunk_states, lambda t: (jnp.minimum(t, nt - 1), 0, 0, 0)),
            pl.BlockSpec(chunk_states, lambda t: (nt - 1 - jnp.maximum(t - nt, 0), 0, 0, 0)),
            st(), st(),
        ],
        out_shape=[seq_shape, seq_shape, fin_shape, fin_shape],
        scratch_shapes=[pltpu.VMEM((n_chunks, RET_HEADS, HEAD_DIM, HEAD_DIM), F32)],
        compiler_params=_params(VMEM_LARGE_MB, 1),
    )(lg_f, lg_b, k, v, s0f, s0b)


def _lane_means(ts):
    parts = []
    for t in ts:
        hi = t.astype(BF16)
        lo = (t - hi.astype(F32)).astype(BF16)
        parts.append(jnp.concatenate([hi, lo], axis=1))
    rows = ts[0].shape[0]
    ones = jnp.full((2 * HEAD_DIM, HEAD_DIM), 1.0 / HEAD_DIM, BF16)
    m = jnp.dot(jnp.concatenate(parts, axis=0), ones, preferred_element_type=F32)
    return [m[n * rows:(n + 1) * rows] for n in range(len(ts))]


def _out_kernel(lgf_ref, lgb_ref,
                ycv_ref, zc_ref, q_ref, k_ref, v_ref, zr_ref, eb_ref, ec_ref,
                sf_ref, sb_ref, x_ref, gate_ref, convw_ref, cnw_ref, gnw_ref, wout_ref, fnw_ref,
                o_ref,
                mask_ref, qdf_ref, qdb_ref, ya_ref, yb_ref, *, tm, n_tiles, edge_every, final):
    s_id = pl.program_id(0)
    i = jnp.maximum(s_id - 1, 0)
    nc = tm // CHUNK
    at_edge = jnp.logical_and((i + 1) % edge_every == 0, i + 1 < n_tiles)

    @pl.when(s_id == 0)
    def _():
        r = lax.broadcasted_iota(jnp.int32, (CHUNK, CHUNK), 0).astype(F32)
        cc = lax.broadcasted_iota(jnp.int32, (CHUNK, CHUNK), 1).astype(F32)
        d = r - cc
        for h in range(RET_HEADS):
            lgf = lgf_ref[h]
            lgb = lgb_ref[h]
            mf = jnp.where(d >= 0, jnp.exp(lgf * jnp.maximum(d, 0.0)), 0.0)
            mb = jnp.where(d <= 0, jnp.exp(lgb * jnp.maximum(-d, 0.0)), 0.0)
            mask_ref[h] = mf + mb
            qdf_ref[h] = jnp.exp(lgf * (r + 1.0))
            qdb_ref[h] = jnp.exp(lgb * (CHUNK - r))

    def step(yr_old, yr_new):
        n_q = PROJ_CHUNKS
        cw_ = D_CONV // n_q
        pw = D_MODEL // n_q

        if yr_old is not None:
            last_row = lax.broadcasted_iota(jnp.int32, (SUBLANES, 1), 0) == SUBLANES - 1
            cwt = convw_ref[0]
            proj_ret, y_l, ssq = [], [], None
            for n in range(n_q):
                proj_ret.append(jnp.dot(yr_old[...], wout_ref[D_CONV:, n * pw:(n + 1) * pw],
                                        preferred_element_type=F32))
                cs = slice(n * cw_, (n + 1) * cw_)
                y = ycv_ref[:, cs].astype(F32)
                below = eb_ref[0, SUBLANES - 1:SUBLANES, cs] * cwt[2:3, cs] * ec_ref[0, 0:1, cs]
                below = jnp.where(jnp.logical_and(at_edge, last_row), below, 0.0)
                y = jnp.concatenate([y[:tm - SUBLANES], y[tm - SUBLANES:] + below], axis=0)
                y_l.append(y)
                part = jnp.sum(y * y, axis=-1, keepdims=True)
                ssq = part if ssq is None else ssq + part
            rinv = lax.rsqrt(ssq * (1.0 / D_CONV) + EPS)

            y_conv = jnp.concatenate(
                [(zc_ref[:, n * cw_:(n + 1) * cw_].astype(F32)
                  * (y_l[n] * rinv * cnw_ref[:, n * cw_:(n + 1) * cw_])).astype(BF16)
                 for n in range(n_q)], axis=1)

        pending = list(range(n_q)) if yr_old is not None else []

        def finish_chunk():
            if not pending:
                return
            n = pending.pop(0)
            cols = slice(n * pw, (n + 1) * pw)
            pc = jnp.dot(y_conv, wout_ref[:D_CONV, cols], preferred_element_type=F32)
            o_ref[:, cols] = x_ref[:, cols] + gate_ref[:, cols] * (pc + proj_ret[n])

        nt = (((1,), (1,)), ((), ()))
        rs = lambda c: slice(c * CHUNK, (c + 1) * CHUNK)
        hs = lambda h: slice(h * HEAD_DIM, (h + 1) * HEAD_DIM)
        rounds = range(0, nc, RET_ROUND_CHUNKS) if yr_new is not None else []
        per_round = -(-n_q // max(len(rounds), 1) // 2)
        for c0 in rounds:
            pairs = [(c, h) for c in range(c0, c0 + RET_ROUND_CHUNKS) for h in range(RET_HEADS)]
            q_l = [q_ref[rs(c), hs(h)] for c, h in pairs]
            s_l = [lax.dot_general(q, k_ref[rs(c), hs(h)], nt, preferred_element_type=F32)
                   for q, (c, h) in zip(q_l, pairs)]
            for _ in range(per_round):
                finish_chunk()
            lhs_l = []
            for q, s, (c, h) in zip(q_l, s_l, pairs):
                qf32 = q.astype(F32)
                lhs_l.append(jnp.concatenate(
                    [(s * mask_ref[h]).astype(BF16), (qf32 * qdf_ref[h]).astype(BF16),
                     (qf32 * qdb_ref[h]).astype(BF16)], axis=1))
            o_l = [jnp.dot(lhs, jnp.concatenate([v_ref[rs(c), hs(h)], sf_ref[c, h],
                                                 sb_ref[c, h]], axis=0),
                           preferred_element_type=F32)
                   for lhs, (c, h) in zip(lhs_l, pairs)]
            for _ in range(per_round):
                finish_chunk()
            per_chunk = lambda xs: [xs[n * RET_HEADS:(n + 1) * RET_HEADS]
                                    for n in range(len(xs) // RET_HEADS)]
            mean_l = [m for grp in per_chunk(o_l) for m in _lane_means(grp)]
            oc_l = [o - m for o, m in zip(o_l, mean_l)]
            var_l = [m for grp in per_chunk([oc * oc for oc in oc_l]) for m in _lane_means(grp)]
            for oc, var, (c, h) in zip(oc_l, var_l, pairs):
                on = oc * lax.rsqrt(var + EPS) * gnw_ref[:, hs(h)]
                yr_new[rs(c), hs(h)] = (zr_ref[rs(c), hs(h)].astype(F32) * on).astype(BF16)
        while pending:
            finish_chunk()
        if final and yr_old is not None:
            xn = o_ref[...]
            ms = jnp.mean(xn * xn, axis=-1, keepdims=True)
            o_ref[...] = xn * lax.rsqrt(ms + EPS) * fnw_ref[...]

    middle = jnp.logical_and(s_id > 0, s_id < n_tiles)

    @pl.when(s_id == 0)
    def _():
        step(None, ya_ref)

    @pl.when(jnp.logical_and(middle, s_id % 2 == 0))
    def _():
        step(yb_ref, ya_ref)

    @pl.when(jnp.logical_and(middle, s_id % 2 == 1))
    def _():
        step(ya_ref, yb_ref)

    @pl.when(s_id == n_tiles)
    def _():
        step(ya_ref if n_tiles % 2 == 1 else yb_ref, None)


def _out(lg_f, lg_b, segs, edges, sf, sb, x, gate, conv_w, cnw, gnw, w_out, fnw, layer, *,
         tm, final):
    ycv, zc, q, k, v, zr = segs
    eb, ec = edges
    rows, d = x.shape
    nt = rows // tm
    nc = tm // CHUNK
    n_edge = eb.shape[0]
    assert rows % tm == 0 and nt % n_edge == 0 and nc % RET_ROUND_CHUNKS == 0
    edge_every = nt // n_edge
    kern = functools.partial(_out_kernel, tm=tm, n_tiles=nt, edge_every=edge_every, final=final)
    smem = lambda: pl.BlockSpec(memory_space=pltpu.SMEM)
    cur = lambda s: jnp.minimum(s, nt - 1)
    prv = lambda s: jnp.maximum(s - 1, 0)
    seg = lambda: pl.BlockSpec((tm, SEG), lambda s: (cur(s), 0))
    fin = lambda: pl.BlockSpec((tm, SEG), lambda s: (prv(s), 0))
    st = lambda: pl.BlockSpec((nc, RET_HEADS, HEAD_DIM, HEAD_DIM), lambda s: (cur(s), 0, 0, 0))
    full = lambda shape: pl.BlockSpec(shape, lambda s: (0,) * len(shape))
    return pl.pallas_call(
        kern,
        grid=(nt + 1,),
        in_specs=[
            smem(), smem(),
            fin(), fin(), seg(), seg(), seg(), seg(),
            pl.BlockSpec((1, SUBLANES, SEG), lambda s: (prv(s) // edge_every, 0, 0)),
            pl.BlockSpec((1, SUBLANES, SEG),
                         lambda s: (jnp.minimum(prv(s) // edge_every + 1, n_edge - 1), 0, 0)),
            st(), st(),
            pl.BlockSpec((tm, d), lambda s: (prv(s), 0)),
            full((1, d)),
            pl.BlockSpec((1, 3, D_CONV), lambda s: (layer, 0, 0)),
            full((1, D_CONV)), full((1, D_RET)),
            pl.BlockSpec((d, d), lambda s: (0, 0), pipeline_mode=pl.Buffered(1)),
            full((1, d)),
        ],
        out_specs=pl.BlockSpec((tm, d), lambda s: (prv(s), 0)),
        out_shape=jax.ShapeDtypeStruct((rows, d), F32),
        scratch_shapes=[
            pltpu.VMEM((RET_HEADS, CHUNK, CHUNK), F32),
            pltpu.VMEM((RET_HEADS, CHUNK, HEAD_DIM), F32),
            pltpu.VMEM((RET_HEADS, CHUNK, HEAD_DIM), F32),
            pltpu.VMEM((tm, D_RET), BF16),
            pltpu.VMEM((tm, D_RET), BF16),
        ],
        compiler_params=_params(VMEM_LARGE_MB, 1),
    )(lg_f, lg_b, ycv, zc, q, k, v, zr, eb, ec, sf, sb, x, gate, conv_w, cnw, gnw,
      w_out, fnw)


def _rope_tables(seq):
    t = np.arange(seq)
    row = (t // GRID_W).astype(np.float64)
    col = (t % GRID_W).astype(np.float64)
    inv = ROPE_BASE ** (-np.arange(ROPE_F, dtype=np.float64) / ROPE_F)
    ar = row[:, None] * inv[None, :]
    ac = col[:, None] * inv[None, :]
    z = np.zeros_like(ar)
    cos = np.concatenate([np.cos(ar), np.cos(ar), np.cos(ac), np.cos(ac)], axis=-1)
    sa = np.concatenate([-np.sin(ar), z, -np.sin(ac), z], axis=-1)
    sb = np.concatenate([z, np.sin(ar), z, np.sin(ac)], axis=-1)
    return tuple(jnp.asarray(a, dtype=F32) for a in (cos, sa, sb))


def kernel(x, c, ctx, c_ctx, norm_w, w_mod, b_mod, w_in, conv_w, conv_norm_w, ret_norm_w,
           ret_decay_f, ret_decay_b, w_out, final_norm_w):
    batch, seq, d = x.shape
    assert batch == 1 and d == D_MODEL and seq % INPROJ_ROWS == 0 and seq % GRID_W == 0
    depth = norm_w.shape[0]
    ctx_len = ctx.shape[1]
    xs = x[0]
    cs = ctx[0]

    cv = jnp.zeros((SUBLANES, d), F32).at[0].set(c[0]).at[1].set(c_ctx)
    mod = _modulation(cv, w_mod, b_mod)
    tables = _rope_tables(seq)
    zero_state = jnp.zeros((RET_HEADS, HEAD_DIM, HEAD_DIM), F32)
    zero_edge = jnp.zeros((1, SUBLANES, D_CONV), F32)
    fnw = final_norm_w.reshape(1, d)
    w_groups = None

    for layer in range(depth):
        update_ctx = layer < depth - 1
        lg_f = -jnp.exp(ret_decay_f[layer].astype(F32))
        lg_b = -jnp.exp(ret_decay_b[layer].astype(F32))
        nw = norm_w[layer].reshape(1, d)
        m = mod[layer]
        shift, scale, gate = m[0:1, 0:d], m[0:1, d:2 * d], m[0:1, 2 * d:3 * d]
        shift_c, scale_c, gate_c = m[1:2, 0:d], m[1:2, d:2 * d], m[1:2, 2 * d:3 * d]
        cnw = conv_norm_w[layer].reshape(1, D_CONV)
        gnw = ret_norm_w[layer].reshape(1, D_RET)

        if update_ctx:
            segs_c, w_groups = _inproj(cs, nw, shift_c, scale_c, w_groups, conv_w, layer,
                                       w_f32=w_in if w_groups is None else None)
            k_c, v_c = segs_c[3], segs_c[4]
        else:
            if w_groups is None:
                w_groups = _split_groups(w_in[layer].astype(BF16))
            k_c, v_c = _inproj_kv(cs, nw, shift_c, scale_c, w_groups, tm=ctx_len)
        sf_c, sb_c, s_f, s_b = _states(lg_f, lg_b, k_c, v_c, zero_state, zero_state, tr=ctx_len)

        to_round = [(w_out, layer)] + ([(w_in, layer + 1)] if layer + 1 < depth else [])
        segs, edges, rounded = _inproj_latent(xs, nw, shift, scale, w_groups, conv_w, layer,
                                              tables, to_round, tm=INPROJ_ROWS)
        w_out_b = rounded[0]
        w_groups = _split_groups(rounded[1]) if layer + 1 < depth else None
        sf, sb, _, _ = _states(lg_f, lg_b, segs[3], segs[4], s_f, s_b, tr=STATES_ROWS)
        xs = _out(lg_f, lg_b, segs, edges, sf, sb, xs, gate, conv_w, cnw, gnw, w_out_b, fnw, layer,
                  tm=OUT_ROWS, final=not update_ctx)
        if update_ctx:
            cs = _out(lg_f, lg_b, segs_c, (zero_edge, zero_edge), sf_c, sb_c, cs, gate_c, conv_w,
                      cnw, gnw, w_out_b, fnw, layer, tm=ctx_len, final=False)
    return xs[None]
```

```python
import functools

import numpy as np
import jax
import jax.numpy as jnp
from jax import lax
from jax.experimental import pallas as pl
from jax.experimental.pallas import tpu as pltpu

D_MODEL = 2048
D_CONV = 1024
D_RET = 1024
RET_HEADS = 8
HEAD_DIM = 128
CHUNK = 128
SEG = 1024
N_SEG = 8
SUB = 256
N_SUB = SEG // SUB
GRID_W = 64
ROPE_BASE = 10000.0
ROPE_F = 32
EPS = 1e-6
K_SCALE = HEAD_DIM ** -0.5

F32 = jnp.float32
BF16 = jnp.bfloat16

G_H, G_B, G_C, G_ZC, G_Q, G_K, G_V, G_ZR = range(8)


def _silu(x):
    return x / (1.0 + jnp.exp(-x))


INPROJ_ROWS = 1024
OUT_ROWS = 512
STATES_ROWS = 1024
MOD_COLS = 1024
PROLOGUE_ROWS = 64
RET_ROUND_CHUNKS = 2
PROJ_CHUNKS = 4
W_CAST_BLOCKS = 32
VMEM_LARGE_MB = 56
VMEM_MOD_MB = 40
VMEM_SMALL_MB = 32
SUBLANES = 8


def _params(vmem_mb, n_axes):
    return pltpu.CompilerParams(
        dimension_semantics=("arbitrary",) * n_axes,
        vmem_limit_bytes=vmem_mb * 1024 * 1024,
    )


def _mod_kernel(cv_ref, w_ref, b_ref, o_ref):
    s = _silu(cv_ref[...])
    o_ref[0] = jnp.dot(s.astype(BF16), w_ref[0].astype(BF16),
                       preferred_element_type=F32) + b_ref[0]


def _modulation(cv, w_mod, b_mod):
    depth, d, n = w_mod.shape
    tn = MOD_COLS
    return pl.pallas_call(
        _mod_kernel,
        grid=(depth, n // tn),
        in_specs=[
            pl.BlockSpec((SUBLANES, d), lambda l, j: (0, 0)),
            pl.BlockSpec((1, d, tn), lambda l, j: (l, 0, j)),
            pl.BlockSpec((1, 1, tn), lambda l, j: (l, 0, j)),
        ],
        out_specs=pl.BlockSpec((1, SUBLANES, tn), lambda l, j: (l, 0, j)),
        out_shape=jax.ShapeDtypeStruct((depth, SUBLANES, n), F32),
        compiler_params=_params(VMEM_MOD_MB, 2),
    )(cv, w_mod, b_mod.reshape(depth, 1, n))


def _prologue(x_ref, nw_ref, shift_ref, scale_ref, hx_ref):
    gain = nw_ref[...] * (1.0 + scale_ref[...])
    shift = shift_ref[...]

    def body(r, carry):
        rows = pl.ds(pl.multiple_of(r * PROLOGUE_ROWS, PROLOGUE_ROWS), PROLOGUE_ROWS)
        x = x_ref[rows, :]
        ms = jnp.mean(x * x, axis=-1, keepdims=True)
        hx_ref[rows, :] = (x * lax.rsqrt(ms + EPS) * gain + shift).astype(BF16)
        return carry

    lax.fori_loop(0, x_ref.shape[0] // PROLOGUE_ROWS, body, 0)


def _rope_pair(acc, cos, sa, sb):
    outs = []
    for h in range(SUB // HEAD_DIM):
        a = acc[:, h * HEAD_DIM:(h + 1) * HEAD_DIM]
        outs.append(a * cos + pltpu.roll(a, HEAD_DIM - ROPE_F, 1) * sa
                    + pltpu.roll(a, ROPE_F, 1) * sb)
    return jnp.concatenate(outs, axis=1)


def _project_groups(hx, w_refs, convw_ref, out_refs, table_refs=None, prev_row=None,
                    edge_refs=None):
    wh_ref, wb_ref, wc_ref, wzc_ref, wq_ref, wk_ref, wv_ref, wzr_ref = w_refs
    yb_ref, zc_ref, q_ref, k_ref, v_ref, zr_ref = out_refs
    tm = hx.shape[0]

    def seg(w_ref):
        return jnp.dot(hx, w_ref[...], preferred_element_type=F32)

    ch = seg(wc_ref) * seg(wh_ref)
    ridx = lax.broadcasted_iota(jnp.int32, (tm, 1), 0)
    above = pltpu.roll(ch, 1, 0)
    above = jnp.where(ridx == 0, 0.0 if prev_row is None else prev_row, above)
    below = jnp.where(ridx == tm - 1, 0.0, pltpu.roll(ch, tm - 1, 0))
    cw = convw_ref[...]
    b = seg(wb_ref)
    yb_ref[...] = (b * (above * cw[0:1, :] + ch * cw[1:2, :] + below * cw[2:3, :])).astype(BF16)
    if edge_refs is not None:
        edge_refs[0][0] = b[tm - SUBLANES:tm, :]
        edge_refs[1][0] = ch[0:SUBLANES, :]
    zc_ref[...] = _silu(seg(wzc_ref)).astype(BF16)
    if table_refs is not None:
        cos, sa, sb = (t[...] for t in table_refs)
        q_ref[...] = _rope_pair(seg(wq_ref), cos, sa, sb).astype(BF16)
        k_ref[...] = _rope_pair(seg(wk_ref), cos * K_SCALE, sa * K_SCALE, sb * K_SCALE).astype(BF16)
    else:
        q_ref[...] = seg(wq_ref).astype(BF16)
        k_ref[...] = (seg(wk_ref) * K_SCALE).astype(BF16)
    v_ref[...] = seg(wv_ref).astype(BF16)
    zr_ref[...] = _silu(seg(wzr_ref)).astype(BF16)
    return ch[tm - SUBLANES:tm, :]


def _inproj_kernel(x_ref, nw_ref, shift_ref, scale_ref, *refs, round_weights):
    w_refs, convw_ref = refs[:8], refs[8]
    out_refs, refs = refs[9:15], refs[15:]
    if round_weights:
        for w_ref, wb_ref in zip(w_refs, refs[:8]):
            wb_ref[...] = w_ref[...].astype(BF16)
        w_refs, refs = refs[:8], refs[8:]
    hx_ref, = refs

    @pl.when(pl.program_id(1) == 0)
    def _():
        _prologue(x_ref, nw_ref, shift_ref, scale_ref, hx_ref)

    _project_groups(hx_ref[...], w_refs, convw_ref, out_refs)


def _inproj_latent_kernel(xq_ref, nw_ref, shift_ref, scale_ref, *refs, tm, n_cast):
    w_refs, convw_ref, table_refs = refs[:8], refs[8], refs[9:12]
    cast_in_refs, refs = refs[12:12 + n_cast], refs[12 + n_cast:]
    out_refs, edge_refs, refs = refs[:6], refs[6:8], refs[8:]
    cast_out_refs, refs = refs[:n_cast], refs[n_cast:]
    hxa_ref, hxb_ref, carry_ref = refs
    i = pl.program_id(0)
    j = pl.program_id(1)
    quarter = tm // N_SUB
    row0 = pl.multiple_of(j * quarter, quarter)

    def project(hx_ref):
        prev_row = carry_ref[j][SUBLANES - 1:SUBLANES, :]
        carry_ref[j] = _project_groups(hx_ref[...], w_refs, convw_ref, out_refs, table_refs,
                                       prev_row, edge_refs)

    def normalise(hx_ref):
        for src_ref, dst_ref in zip(cast_in_refs, cast_out_refs):
            dst_ref[...] = src_ref[...].astype(BF16)
        gain = nw_ref[...] * (1.0 + scale_ref[...])
        shift = shift_ref[...]
        for r in range(0, quarter, PROLOGUE_ROWS):
            x = xq_ref[r:r + PROLOGUE_ROWS, :]
            ms = jnp.mean(x * x, axis=-1, keepdims=True)
            hx_ref[pl.ds(row0 + r, PROLOGUE_ROWS), :] = (
                x * lax.rsqrt(ms + EPS) * gain + shift).astype(BF16)

    @pl.when(i == 0)
    def _():
        normalise(hxa_ref)
        carry_ref[j] = jnp.zeros(carry_ref.shape[1:], F32)

    @pl.when(i % 2 == 1)
    def _():
        normalise(hxb_ref)
        project(hxa_ref)

    @pl.when(jnp.logical_and(i > 0, i % 2 == 0))
    def _():
        normalise(hxa_ref)
        project(hxb_ref)


def _w_spec(first_block):
    return pl.BlockSpec((D_MODEL, SUB), lambda i, j: (0, first_block + j))


def _split_groups(w_bf16):
    return [(w_bf16, g * N_SUB) for g in range(N_SEG)]


def _convw_spec(layer):
    return pl.BlockSpec((None, 3, SUB), lambda i, j: (layer, 0, j))


def _inproj(x, norm_w, shift, scale, w_groups, conv_w, layer, *, w_f32=None):
    tm, d = x.shape
    round_weights = w_f32 is not None
    kern = functools.partial(_inproj_kernel, round_weights=round_weights)
    vec = lambda: pl.BlockSpec((1, d), lambda i, j: (0, 0))
    out = lambda: pl.BlockSpec((tm, SUB), lambda i, j: (i, j))
    seg_shape = jax.ShapeDtypeStruct((tm, SEG), BF16)
    out_specs = [out() for _ in range(6)]
    out_shape = [seg_shape] * 6
    if round_weights:
        w_specs = [pl.BlockSpec((None, d, SUB), lambda i, j, g=g: (layer, 0, g * N_SUB + j))
                   for g in range(N_SEG)]
        w_operands = [w_f32] * N_SEG
        out_specs += [pl.BlockSpec((d, SUB), lambda i, j: (0, j)) for _ in range(N_SEG)]
        out_shape += [jax.ShapeDtypeStruct((d, SEG), BF16)] * N_SEG
    else:
        w_specs = [_w_spec(first) for _, first in w_groups]
        w_operands = [w for w, _ in w_groups]
    res = pl.pallas_call(
        kern,
        grid=(1, N_SUB),
        in_specs=[
            pl.BlockSpec((tm, d), lambda i, j: (i, 0)),
            vec(), vec(), vec(),
            *w_specs,
            _convw_spec(layer),
        ],
        out_specs=out_specs,
        out_shape=out_shape,
        scratch_shapes=[pltpu.VMEM((tm, d), BF16)],
        compiler_params=_params(VMEM_LARGE_MB, 2),
    )(x, norm_w, shift, scale, *w_operands, conv_w)
    return res[:6], ([(w, 0) for w in res[6:]] if round_weights else w_groups)


def _inproj_latent(x, norm_w, shift, scale, w_groups, conv_w, layer, tables, to_round, *, tm):
    rows, d = x.shape
    nt = rows // tm
    cos, sa, sb = tables
    kern = functools.partial(_inproj_latent_kernel, tm=tm, n_cast=len(to_round))
    done = lambda i: jnp.maximum(i - 1, 0)
    vec = lambda: pl.BlockSpec((1, d), lambda i, j: (0, 0))
    tab = lambda: pl.BlockSpec((tm, HEAD_DIM), lambda i, j: (done(i), 0))
    out = lambda: pl.BlockSpec((tm, SUB), lambda i, j: (done(i), jnp.where(i == 0, 0, j)))
    seg_shape = jax.ShapeDtypeStruct((rows, SEG), BF16)
    in_specs = [
        pl.BlockSpec((tm // N_SUB, d), lambda i, j: (jnp.minimum(i, nt - 1) * N_SUB + j, 0)),
        vec(), vec(), vec(),
        *[_w_spec(first) for _, first in w_groups],
        _convw_spec(layer),
        tab(), tab(), tab(),
    ]
    operands = [x, norm_w, shift, scale, *[w for w, _ in w_groups], conv_w, cos, sa, sb]
    edge = lambda: pl.BlockSpec((1, SUBLANES, SUB),
                                lambda i, j: (done(i), 0, jnp.where(i == 0, 0, j)))
    out_specs = [out() for _ in range(6)] + [edge(), edge()]
    out_shape = [seg_shape] * 6 + [jax.ShapeDtypeStruct((nt, SUBLANES, SEG), F32)] * 2
    n_blocks = W_CAST_BLOCKS
    assert n_blocks <= (nt + 1) * N_SUB
    blk = lambda i, j: jnp.minimum(i * N_SUB + j, n_blocks - 1)
    for w_all, which in to_round:
        n_rows, n_cols = w_all.shape[1:]
        assert n_rows % n_blocks == 0
        in_specs.append(pl.BlockSpec((None, n_rows // n_blocks, n_cols),
                                     lambda i, j, which=which: (which, blk(i, j), 0)))
        operands.append(w_all)
        out_specs.append(pl.BlockSpec((n_rows // n_blocks, n_cols), lambda i, j: (blk(i, j), 0)))
        out_shape.append(jax.ShapeDtypeStruct((n_rows, n_cols), BF16))
    res = pl.pallas_call(
        kern,
        grid=(nt + 1, N_SUB),
        in_specs=in_specs,
        out_specs=out_specs,
        out_shape=out_shape,
        scratch_shapes=[pltpu.VMEM((tm, d), BF16), pltpu.VMEM((tm, d), BF16),
                        pltpu.VMEM((N_SUB, SUBLANES, SUB), F32)],
        compiler_params=_params(VMEM_LARGE_MB, 2),
    )(*operands)
    return res[:6], res[6:8], res[8:]


def _inproj_kv_kernel(x_ref, nw_ref, shift_ref, scale_ref, wk_ref, wv_ref, k_ref, v_ref, hx_ref):
    @pl.when(pl.program_id(1) == 0)
    def _():
        _prologue(x_ref, nw_ref, shift_ref, scale_ref, hx_ref)

    hx = hx_ref[...]
    k_ref[...] = (jnp.dot(hx, wk_ref[...], preferred_element_type=F32) * K_SCALE).astype(BF16)
    v_ref[...] = jnp.dot(hx, wv_ref[...], preferred_element_type=F32).astype(BF16)


def _inproj_kv(x, norm_w, shift, scale, w_groups, *, tm):
    rows, d = x.shape
    vec = lambda: pl.BlockSpec((1, d), lambda i, j: (0, 0))
    out = lambda: pl.BlockSpec((tm, SUB), lambda i, j: (i, j))
    seg_shape = jax.ShapeDtypeStruct((rows, SEG), BF16)
    return pl.pallas_call(
        _inproj_kv_kernel,
        grid=(rows // tm, N_SUB),
        in_specs=[
            pl.BlockSpec((tm, d), lambda i, j: (i, 0)),
            vec(), vec(), vec(),
            _w_spec(w_groups[G_K][1]), _w_spec(w_groups[G_V][1]),
        ],
        out_specs=[out(), out()],
        out_shape=[seg_shape] * 2,
        scratch_shapes=[pltpu.VMEM((tm, d), BF16)],
        compiler_params=_params(VMEM_SMALL_MB, 2),
    )(x, norm_w, shift, scale, w_groups[G_K][0], w_groups[G_V][0])


def _states_kernel(lgf_ref, lgb_ref, k_ref, v_ref, s0f_ref, s0b_ref,
                   sf_ref, sb_ref, ff_ref, fb_ref, kvb_ref, *, nc, nt):
    t = pl.program_id(0)

    @pl.when(t == 0)
    def _():
        ff_ref[...] = s0f_ref[...]
        fb_ref[...] = s0b_ref[...]

    row = lax.broadcasted_iota(jnp.int32, (CHUNK, HEAD_DIM), 0).astype(F32)
    pos = lax.broadcasted_iota(jnp.int32, (HEAD_DIM, CHUNK), 1).astype(F32)

    @pl.when(t < nt)
    def _():
        for h in range(RET_HEADS):
            hs = slice(h * HEAD_DIM, (h + 1) * HEAD_DIM)
            kdf = jnp.exp(lgf_ref[h] * (CHUNK - 1.0 - pos))
            kdb = jnp.exp(lgb_ref[h] * pos)
            cdf = jnp.exp(lgf_ref[h] * CHUNK + 0.0 * row)
            s = ff_ref[h]
            for c in range(nc):
                rs = slice(c * CHUNK, (c + 1) * CHUNK)
                sf_ref[c, h] = s.astype(BF16)
                kt = k_ref[rs, hs].astype(F32).T
                v = v_ref[rs, hs]
                s = cdf * s + jnp.dot((kt * kdf).astype(BF16), v, preferred_element_type=F32)
                kvb_ref[t * nc + c, h] = jnp.dot((kt * kdb).astype(BF16), v,
                                                 preferred_element_type=F32)
            ff_ref[h] = s

    @pl.when(t >= nt)
    def _():
        tile = 2 * nt - 1 - t
        for h in range(RET_HEADS):
            cdb = jnp.exp(lgb_ref[h] * CHUNK + 0.0 * row)
            s = fb_ref[h]
            for c in reversed(range(nc)):
                sb_ref[c, h] = s.astype(BF16)
                s = cdb * s + kvb_ref[tile * nc + c, h]
            fb_ref[h] = s


def _states(lg_f, lg_b, k, v, s0f, s0b, *, tr):
    rows = k.shape[0]
    nt = rows // tr
    nc = tr // CHUNK
    n_chunks = rows // CHUNK
    kern = functools.partial(_states_kernel, nc=nc, nt=nt)
    smem = lambda: pl.BlockSpec(memory_space=pltpu.SMEM)
    st = lambda: pl.BlockSpec((RET_HEADS, HEAD_DIM, HEAD_DIM), lambda t: (0, 0, 0))
    rows_in = lambda: pl.BlockSpec((tr, SEG), lambda t: (jnp.minimum(t, nt - 1), 0))
    chunk_states = (nc, RET_HEADS, HEAD_DIM, HEAD_DIM)
    seq_shape = pltpu.HBM((n_chunks, RET_HEADS, HEAD_DIM, HEAD_DIM), BF16)
    fin_shape = jax.ShapeDtypeStruct((RET_HEADS, HEAD_DIM, HEAD_DIM), F32)
    return pl.pallas_call(
        kern,
        grid=(2 * nt,),
        in_specs=[smem(), smem(), rows_in(), rows_in(), st(), st()],
        out_specs=[
            pl.BlockSpec(chunk_states, lambda t: (jnp.minimum(t, nt - 1), 0, 0, 0)),
            pl.BlockSpec(chunk_states, lambda t: (nt - 1 - jnp.maximum(t - nt, 0), 0, 0, 0)),
            st(), st(),
        ],
        out_shape=[seq_shape, seq_shape, fin_shape, fin_shape],
        scratch_shapes=[pltpu.VMEM((n_chunks, RET_HEADS, HEAD_DIM, HEAD_DIM), F32)],
        compiler_params=_params(VMEM_LARGE_MB, 1),
    )(lg_f, lg_b, k, v, s0f, s0b)


def _lane_means(ts):
    parts = []
    for t in ts:
        hi = t.astype(BF16)
        lo = (t - hi.astype(F32)).astype(BF16)
        parts.append(jnp.concatenate([hi, lo], axis=1))
    rows = ts[0].shape[0]
    ones = jnp.full((2 * HEAD_DIM, HEAD_DIM), 1.0 / HEAD_DIM, BF16)
    m = jnp.dot(jnp.concatenate(parts, axis=0), ones, preferred_element_type=F32)
    return [m[n * rows:(n + 1) * rows] for n in range(len(ts))]


def _out_kernel(lgf_ref, lgb_ref,
                ycv_ref, zc_ref, q_ref, k_ref, v_ref, zr_ref, eb_ref, ec_ref,
                sf_ref, sb_ref, x_ref, gate_ref, convw_ref, cnw_ref, gnw_ref, wout_ref, fnw_ref,
                o_ref,
                mask_ref, qdf_ref, qdb_ref, ya_ref, yb_ref, *, tm, n_tiles, edge_every, final):
    s_id = pl.program_id(0)
    i = jnp.maximum(s_id - 1, 0)
    nc = tm // CHUNK
    at_edge = jnp.logical_and((i + 1) % edge_every == 0, i + 1 < n_tiles)

    @pl.when(s_id == 0)
    def _():
        r = lax.broadcasted_iota(jnp.int32, (CHUNK, CHUNK), 0).astype(F32)
        cc = lax.broadcasted_iota(jnp.int32, (CHUNK, CHUNK), 1).astype(F32)
        d = r - cc
        for h in range(RET_HEADS):
            lgf = lgf_ref[h]
            lgb = lgb_ref[h]
            mf = jnp.where(d >= 0, jnp.exp(lgf * jnp.maximum(d, 0.0)), 0.0)
            mb = jnp.where(d <= 0, jnp.exp(lgb * jnp.maximum(-d, 0.0)), 0.0)
            mask_ref[h] = mf + mb
            qdf_ref[h] = jnp.exp(lgf * (r + 1.0))
            qdb_ref[h] = jnp.exp(lgb * (CHUNK - r))

    def step(yr_old, yr_new):
        n_q = PROJ_CHUNKS
        cw_ = D_CONV // n_q
        pw = D_MODEL // n_q

        if yr_old is not None:
            last_row = lax.broadcasted_iota(jnp.int32, (SUBLANES, 1), 0) == SUBLANES - 1
            cwt = convw_ref[0]
            proj_ret, y_l, ssq = [], [], None
            for n in range(n_q):
                proj_ret.append(jnp.dot(yr_old[...], wout_ref[D_CONV:, n * pw:(n + 1) * pw],
                                        preferred_element_type=F32))
                cs = slice(n * cw_, (n + 1) * cw_)
                y = ycv_ref[:, cs].astype(F32)
                below = eb_ref[0, SUBLANES - 1:SUBLANES, cs] * cwt[2:3, cs] * ec_ref[0, 0:1, cs]
                below = jnp.where(jnp.logical_and(at_edge, last_row), below, 0.0)
                y = jnp.concatenate([y[:tm - SUBLANES], y[tm - SUBLANES:] + below], axis=0)
                y_l.append(y)
                part = jnp.sum(y * y, axis=-1, keepdims=True)
                ssq = part if ssq is None else ssq + part
            rinv = lax.rsqrt(ssq * (1.0 / D_CONV) + EPS)

            y_conv = jnp.concatenate(
                [(zc_ref[:, n * cw_:(n + 1) * cw_].astype(F32)
                  * (y_l[n] * rinv * cnw_ref[:, n * cw_:(n + 1) * cw_])).astype(BF16)
                 for n in range(n_q)], axis=1)

        pending = list(range(n_q)) if yr_old is not None else []
        out_ssq = []

        def finish_chunk():
            if not pending:
                return
            n = pending.pop(0)
            cols = slice(n * pw, (n + 1) * pw)
            pc = jnp.dot(y_conv, wout_ref[:D_CONV, cols], preferred_element_type=F32)
            xn = x_ref[:, cols] + gate_ref[:, cols] * (pc + proj_ret[n])
            o_ref[:, cols] = xn
            if final:
                out_ssq.append(jnp.sum(xn * xn, axis=-1, keepdims=True))

        nt = (((1,), (1,)), ((), ()))
        rs = lambda c: slice(c * CHUNK, (c + 1) * CHUNK)
        hs = lambda h: slice(h * HEAD_DIM, (h + 1) * HEAD_DIM)
        rounds = range(0, nc, RET_ROUND_CHUNKS) if yr_new is not None else []
        per_round = -(-n_q // max(len(rounds), 1) // 2)
        for c0 in rounds:
            pairs = [(c, h) for c in range(c0, c0 + RET_ROUND_CHUNKS) for h in range(RET_HEADS)]
            q_l = [q_ref[rs(c), hs(h)] for c, h in pairs]
            s_l = [lax.dot_general(q, k_ref[rs(c), hs(h)], nt, preferred_element_type=F32)
                   for q, (c, h) in zip(q_l, pairs)]
            for _ in range(per_round):
                finish_chunk()
            lhs_l = []
            for q, s, (c, h) in zip(q_l, s_l, pairs):
                qf32 = q.astype(F32)
                lhs_l.append(jnp.concatenate(
                    [(s * mask_ref[h]).astype(BF16), (qf32 * qdf_ref[h]).astype(BF16),
                     (qf32 * qdb_ref[h]).astype(BF16)], axis=1))
            o_l = [jnp.dot(lhs, jnp.concatenate([v_ref[rs(c), hs(h)], sf_ref[c, h],
                                                 sb_ref[c, h]], axis=0),
                           preferred_element_type=F32)
                   for lhs, (c, h) in zip(lhs_l, pairs)]
            for _ in range(per_round):
                finish_chunk()
            per_chunk = lambda xs: [xs[n * RET_HEADS:(n + 1) * RET_HEADS]
                                    for n in range(len(xs) // RET_HEADS)]
            mean_l = [m for grp in per_chunk(o_l) for m in _lane_means(grp)]
            oc_l = [o - m for o, m in zip(o_l, mean_l)]
            var_l = [m for grp in per_chunk([oc * oc for oc in oc_l]) for m in _lane_means(grp)]
            for oc, var, (c, h) in zip(oc_l, var_l, pairs):
                on = oc * lax.rsqrt(var + EPS) * gnw_ref[:, hs(h)]
                yr_new[rs(c), hs(h)] = (zr_ref[rs(c), hs(h)].astype(F32) * on).astype(BF16)
        while pending:
            finish_chunk()
        if final and yr_old is not None:
            ms = functools.reduce(lambda a, b: a + b, out_ssq) * (1.0 / D_MODEL)
            o_ref[...] = o_ref[...] * lax.rsqrt(ms + EPS) * fnw_ref[...]

    middle = jnp.logical_and(s_id > 0, s_id < n_tiles)

    @pl.when(s_id == 0)
    def _():
        step(None, ya_ref)

    @pl.when(jnp.logical_and(middle, s_id % 2 == 0))
    def _():
        step(yb_ref, ya_ref)

    @pl.when(jnp.logical_and(middle, s_id % 2 == 1))
    def _():
        step(ya_ref, yb_ref)

    @pl.when(s_id == n_tiles)
    def _():
        step(ya_ref if n_tiles % 2 == 1 else yb_ref, None)


def _out(lg_f, lg_b, segs, edges, sf, sb, x, gate, conv_w, cnw, gnw, w_out, fnw, layer, *,
         tm, final):
    ycv, zc, q, k, v, zr = segs
    eb, ec = edges
    rows, d = x.shape
    nt = rows // tm
    nc = tm // CHUNK
    n_edge = eb.shape[0]
    assert rows % tm == 0 and nt % n_edge == 0 and nc % RET_ROUND_CHUNKS == 0
    edge_every = nt // n_edge
    kern = functools.partial(_out_kernel, tm=tm, n_tiles=nt, edge_every=edge_every, final=final)
    smem = lambda: pl.BlockSpec(memory_space=pltpu.SMEM)
    cur = lambda s: jnp.minimum(s, nt - 1)
    prv = lambda s: jnp.maximum(s - 1, 0)
    seg = lambda: pl.BlockSpec((tm, SEG), lambda s: (cur(s), 0))
    fin = lambda: pl.BlockSpec((tm, SEG), lambda s: (prv(s), 0))
    st = lambda: pl.BlockSpec((nc, RET_HEADS, HEAD_DIM, HEAD_DIM), lambda s: (cur(s), 0, 0, 0))
    full = lambda shape: pl.BlockSpec(shape, lambda s: (0,) * len(shape))
    return pl.pallas_call(
        kern,
        grid=(nt + 1,),
        in_specs=[
            smem(), smem(),
            fin(), fin(), seg(), seg(), seg(), seg(),
            pl.BlockSpec((1, SUBLANES, SEG), lambda s: (prv(s) // edge_every, 0, 0)),
            pl.BlockSpec((1, SUBLANES, SEG),
                         lambda s: (jnp.minimum(prv(s) // edge_every + 1, n_edge - 1), 0, 0)),
            st(), st(),
            pl.BlockSpec((tm, d), lambda s: (prv(s), 0)),
            full((1, d)),
            pl.BlockSpec((1, 3, D_CONV), lambda s: (layer, 0, 0)),
            full((1, D_CONV)), full((1, D_RET)),
            pl.BlockSpec((d, d), lambda s: (0, 0), pipeline_mode=pl.Buffered(1)),
            full((1, d)),
        ],
        out_specs=pl.BlockSpec((tm, d), lambda s: (prv(s), 0)),
        out_shape=jax.ShapeDtypeStruct((rows, d), F32),
        scratch_shapes=[
            pltpu.VMEM((RET_HEADS, CHUNK, CHUNK), F32),
            pltpu.VMEM((RET_HEADS, CHUNK, HEAD_DIM), F32),
            pltpu.VMEM((RET_HEADS, CHUNK, HEAD_DIM), F32),
            pltpu.VMEM((tm, D_RET), BF16),
            pltpu.VMEM((tm, D_RET), BF16),
        ],
        compiler_params=_params(VMEM_LARGE_MB, 1),
    )(lg_f, lg_b, ycv, zc, q, k, v, zr, eb, ec, sf, sb, x, gate, conv_w, cnw, gnw,
      w_out, fnw)


def _rope_tables(seq):
    t = np.arange(seq)
    row = (t // GRID_W).astype(np.float64)
    col = (t % GRID_W).astype(np.float64)
    inv = ROPE_BASE ** (-np.arange(ROPE_F, dtype=np.float64) / ROPE_F)
    ar = row[:, None] * inv[None, :]
    ac = col[:, None] * inv[None, :]
    z = np.zeros_like(ar)
    cos = np.concatenate([np.cos(ar), np.cos(ar), np.cos(ac), np.cos(ac)], axis=-1)
    sa = np.concatenate([-np.sin(ar), z, -np.sin(ac), z], axis=-1)
    sb = np.concatenate([z, np.sin(ar), z, np.sin(ac)], axis=-1)
    return tuple(jnp.asarray(a, dtype=F32) for a in (cos, sa, sb))


def kernel(x, c, ctx, c_ctx, norm_w, w_mod, b_mod, w_in, conv_w, conv_norm_w, ret_norm_w,
           ret_decay_f, ret_decay_b, w_out, final_norm_w):
    batch, seq, d = x.shape
    assert batch == 1 and d == D_MODEL and seq % INPROJ_ROWS == 0 and seq % GRID_W == 0
    depth = norm_w.shape[0]
    ctx_len = ctx.shape[1]
    xs = x[0]
    cs = ctx[0]

    cv = jnp.zeros((SUBLANES, d), F32).at[0].set(c[0]).at[1].set(c_ctx)
    mod = _modulation(cv, w_mod, b_mod)
    tables = _rope_tables(seq)
    zero_state = jnp.zeros((RET_HEADS, HEAD_DIM, HEAD_DIM), F32)
    zero_edge = jnp.zeros((1, SUBLANES, D_CONV), F32)
    fnw = final_norm_w.reshape(1, d)
    w_groups = None

    for layer in range(depth):
        update_ctx = layer < depth - 1
        lg_f = -jnp.exp(ret_decay_f[layer].astype(F32))
        lg_b = -jnp.exp(ret_decay_b[layer].astype(F32))
        nw = norm_w[layer].reshape(1, d)
        m = mod[layer]
        shift, scale, gate = m[0:1, 0:d], m[0:1, d:2 * d], m[0:1, 2 * d:3 * d]
        shift_c, scale_c, gate_c = m[1:2, 0:d], m[1:2, d:2 * d], m[1:2, 2 * d:3 * d]
        cnw = conv_norm_w[layer].reshape(1, D_CONV)
        gnw = ret_norm_w[layer].reshape(1, D_RET)

        if update_ctx:
            segs_c, w_groups = _inproj(cs, nw, shift_c, scale_c, w_groups, conv_w, layer,
                                       w_f32=w_in if w_groups is None else None)
            k_c, v_c = segs_c[3], segs_c[4]
        else:
            if w_groups is None:
                w_groups = _split_groups(w_in[layer].astype(BF16))
            k_c, v_c = _inproj_kv(cs, nw, shift_c, scale_c, w_groups, tm=ctx_len)
        sf_c, sb_c, s_f, s_b = _states(lg_f, lg_b, k_c, v_c, zero_state, zero_state, tr=ctx_len)

        to_round = [(w_out, layer)] + ([(w_in, layer + 1)] if layer + 1 < depth else [])
        segs, edges, rounded = _inproj_latent(xs, nw, shift, scale, w_groups, conv_w, layer,
                                              tables, to_round, tm=INPROJ_ROWS)
        w_out_b = rounded[0]
        w_groups = _split_groups(rounded[1]) if layer + 1 < depth else None
        sf, sb, _, _ = _states(lg_f, lg_b, segs[3], segs[4], s_f, s_b, tr=STATES_ROWS)
        xs = _out(lg_f, lg_b, segs, edges, sf, sb, xs, gate, conv_w, cnw, gnw, w_out_b, fnw, layer,
                  tm=OUT_ROWS, final=not update_ctx)
        if update_ctx:
            cs = _out(lg_f, lg_b, segs_c, (zero_edge, zero_edge), sf_c, sb_c, cs, gate_c, conv_w,
                      cnw, gnw, w_out_b, fnw, layer, tm=ctx_len, final=False)
    return xs[None]
```

```python
import functools

import numpy as np
import jax
import jax.numpy as jnp
from jax import lax
from jax.experimental import pallas as pl
from jax.experimental.pallas import tpu as pltpu

D_MODEL = 2048
D_CONV = 1024
D_RET = 1024
RET_HEADS = 8
HEAD_DIM = 128
CHUNK = 128
SEG = 1024
N_SEG = 8
SUB = 256
N_SUB = SEG // SUB
GRID_W = 64
ROPE_BASE = 10000.0
ROPE_F = 32
EPS = 1e-6
K_SCALE = HEAD_DIM ** -0.5

F32 = jnp.float32
BF16 = jnp.bfloat16

G_H, G_B, G_C, G_ZC, G_Q, G_K, G_V, G_ZR = range(8)


def _silu(x):
    return x / (1.0 + jnp.exp(-x))


INPROJ_ROWS = 1024
OUT_ROWS = 512
STATES_ROWS = 1024
MOD_COLS = 1024
PROLOGUE_ROWS = 64
RET_ROUND_CHUNKS = 2
PROJ_CHUNKS = 4
W_CAST_BLOCKS = 32
VMEM_LARGE_MB = 56
VMEM_MOD_MB = 40
VMEM_SMALL_MB = 32
SUBLANES = 8


def _params(vmem_mb, n_axes):
    return pltpu.CompilerParams(
        dimension_semantics=("arbitrary",) * n_axes,
        vmem_limit_bytes=vmem_mb * 1024 * 1024,
    )


def _mod_kernel(cv_ref, w_ref, b_ref, o_ref):
    s = _silu(cv_ref[...])
    o_ref[0] = jnp.dot(s.astype(BF16), w_ref[0].astype(BF16),
                       preferred_element_type=F32) + b_ref[0]


def _modulation(cv, w_mod, b_mod):
    depth, d, n = w_mod.shape
    tn = MOD_COLS
    return pl.pallas_call(
        _mod_kernel,
        grid=(depth, n // tn),
        in_specs=[
            pl.BlockSpec((SUBLANES, d), lambda l, j: (0, 0)),
            pl.BlockSpec((1, d, tn), lambda l, j: (l, 0, j)),
            pl.BlockSpec((1, 1, tn), lambda l, j: (l, 0, j)),
        ],
        out_specs=pl.BlockSpec((1, SUBLANES, tn), lambda l, j: (l, 0, j)),
        out_shape=jax.ShapeDtypeStruct((depth, SUBLANES, n), F32),
        compiler_params=_params(VMEM_MOD_MB, 2),
    )(cv, w_mod, b_mod.reshape(depth, 1, n))


def _prologue(x_ref, nw_ref, shift_ref, scale_ref, hx_ref):
    gain = nw_ref[...] * (1.0 + scale_ref[...])
    shift = shift_ref[...]

    def body(r, carry):
        rows = pl.ds(pl.multiple_of(r * PROLOGUE_ROWS, PROLOGUE_ROWS), PROLOGUE_ROWS)
        x = x_ref[rows, :]
        ms = jnp.mean(x * x, axis=-1, keepdims=True)
        hx_ref[rows, :] = (x * lax.rsqrt(ms + EPS) * gain + shift).astype(BF16)
        return carry

    lax.fori_loop(0, x_ref.shape[0] // PROLOGUE_ROWS, body, 0)


def _rope_pair(acc, cos, sa, sb):
    outs = []
    for h in range(SUB // HEAD_DIM):
        a = acc[:, h * HEAD_DIM:(h + 1) * HEAD_DIM]
        outs.append(a * cos + pltpu.roll(a, HEAD_DIM - ROPE_F, 1) * sa
                    + pltpu.roll(a, ROPE_F, 1) * sb)
    return jnp.concatenate(outs, axis=1)


def _project_groups(hx, w_refs, convw_ref, out_refs, table_refs=None, prev_row=None,
                    edge_refs=None):
    wh_ref, wb_ref, wc_ref, wzc_ref, wq_ref, wk_ref, wv_ref, wzr_ref = w_refs
    yb_ref, zc_ref, q_ref, k_ref, v_ref, zr_ref = out_refs
    tm = hx.shape[0]

    def seg(w_ref):
        return jnp.dot(hx, w_ref[...], preferred_element_type=F32)

    ch = seg(wc_ref) * seg(wh_ref)
    ridx = lax.broadcasted_iota(jnp.int32, (tm, 1), 0)
    above = pltpu.roll(ch, 1, 0)
    above = jnp.where(ridx == 0, 0.0 if prev_row is None else prev_row, above)
    below = jnp.where(ridx == tm - 1, 0.0, pltpu.roll(ch, tm - 1, 0))
    cw = convw_ref[...]
    b = seg(wb_ref)
    yb_ref[...] = (b * (above * cw[0:1, :] + ch * cw[1:2, :] + below * cw[2:3, :])).astype(BF16)
    if edge_refs is not None:
        edge_refs[0][0] = b[tm - SUBLANES:tm, :]
        edge_refs[1][0] = ch[0:SUBLANES, :]
    zc_ref[...] = _silu(seg(wzc_ref)).astype(BF16)
    if table_refs is not None:
        cos, sa, sb = (t[...] for t in table_refs)
        q_ref[...] = _rope_pair(seg(wq_ref), cos, sa, sb).astype(BF16)
        k_ref[...] = _rope_pair(seg(wk_ref), cos * K_SCALE, sa * K_SCALE, sb * K_SCALE).astype(BF16)
    else:
        q_ref[...] = seg(wq_ref).astype(BF16)
        k_ref[...] = (seg(wk_ref) * K_SCALE).astype(BF16)
    v_ref[...] = seg(wv_ref).astype(BF16)
    zr_ref[...] = _silu(seg(wzr_ref)).astype(BF16)
    return ch[tm - SUBLANES:tm, :]


def _inproj_kernel(x_ref, nw_ref, shift_ref, scale_ref, *refs, round_weights):
    w_refs, convw_ref = refs[:8], refs[8]
    out_refs, refs = refs[9:15], refs[15:]
    if round_weights:
        for w_ref, wb_ref in zip(w_refs, refs[:8]):
            wb_ref[...] = w_ref[...].astype(BF16)
        w_refs, refs = refs[:8], refs[8:]
    hx_ref, = refs

    @pl.when(pl.program_id(1) == 0)
    def _():
        _prologue(x_ref, nw_ref, shift_ref, scale_ref, hx_ref)

    _project_groups(hx_ref[...], w_refs, convw_ref, out_refs)


def _inproj_latent_kernel(xq_ref, nw_ref, shift_ref, scale_ref, *refs, tm, n_cast):
    w_refs, convw_ref, table_refs = refs[:8], refs[8], refs[9:12]
    cast_in_refs, refs = refs[12:12 + n_cast], refs[12 + n_cast:]
    out_refs, edge_refs, refs = refs[:6], refs[6:8], refs[8:]
    cast_out_refs, refs = refs[:n_cast], refs[n_cast:]
    hxa_ref, hxb_ref, carry_ref = refs
    i = pl.program_id(0)
    j = pl.program_id(1)
    quarter = tm // N_SUB
    row0 = pl.multiple_of(j * quarter, quarter)

    def project(hx_ref):
        prev_row = carry_ref[j][SUBLANES - 1:SUBLANES, :]
        carry_ref[j] = _project_groups(hx_ref[...], w_refs, convw_ref, out_refs, table_refs,
                                       prev_row, edge_refs)

    def normalise(hx_ref):
        for src_ref, dst_ref in zip(cast_in_refs, cast_out_refs):
            dst_ref[...] = src_ref[...].astype(BF16)
        gain = nw_ref[...] * (1.0 + scale_ref[...])
        shift = shift_ref[...]
        for r in range(0, quarter, PROLOGUE_ROWS):
            x = xq_ref[r:r + PROLOGUE_ROWS, :]
            ms = jnp.mean(x * x, axis=-1, keepdims=True)
            hx_ref[pl.ds(row0 + r, PROLOGUE_ROWS), :] = (
                x * lax.rsqrt(ms + EPS) * gain + shift).astype(BF16)

    @pl.when(i == 0)
    def _():
        normalise(hxa_ref)
        carry_ref[j] = jnp.zeros(carry_ref.shape[1:], F32)

    @pl.when(i % 2 == 1)
    def _():
        normalise(hxb_ref)
        project(hxa_ref)

    @pl.when(jnp.logical_and(i > 0, i % 2 == 0))
    def _():
        normalise(hxa_ref)
        project(hxb_ref)


def _w_spec(first_block, idle_first_row=False):
    if idle_first_row:
        return pl.BlockSpec((D_MODEL, SUB),
                            lambda i, j: (0, first_block + jnp.where(i == 0, 0, j)))
    return pl.BlockSpec((D_MODEL, SUB), lambda i, j: (0, first_block + j))


def _split_groups(w_bf16):
    return [(w_bf16, g * N_SUB) for g in range(N_SEG)]


def _convw_spec(layer):
    return pl.BlockSpec((None, 3, SUB), lambda i, j: (layer, 0, j))


def _inproj(x, norm_w, shift, scale, w_groups, conv_w, layer, *, w_f32=None):
    tm, d = x.shape
    round_weights = w_f32 is not None
    kern = functools.partial(_inproj_kernel, round_weights=round_weights)
    vec = lambda: pl.BlockSpec((1, d), lambda i, j: (0, 0))
    out = lambda: pl.BlockSpec((tm, SUB), lambda i, j: (i, j))
    seg_shape = jax.ShapeDtypeStruct((tm, SEG), BF16)
    out_specs = [out() for _ in range(6)]
    out_shape = [seg_shape] * 6
    if round_weights:
        w_specs = [pl.BlockSpec((None, d, SUB), lambda i, j, g=g: (layer, 0, g * N_SUB + j))
                   for g in range(N_SEG)]
        w_operands = [w_f32] * N_SEG
        out_specs += [pl.BlockSpec((d, SUB), lambda i, j: (0, j)) for _ in range(N_SEG)]
        out_shape += [jax.ShapeDtypeStruct((d, SEG), BF16)] * N_SEG
    else:
        w_specs = [_w_spec(first) for _, first in w_groups]
        w_operands = [w for w, _ in w_groups]
    res = pl.pallas_call(
        kern,
        grid=(1, N_SUB),
        in_specs=[
            pl.BlockSpec((tm, d), lambda i, j: (i, 0)),
            vec(), vec(), vec(),
            *w_specs,
            _convw_spec(layer),
        ],
        out_specs=out_specs,
        out_shape=out_shape,
        scratch_shapes=[pltpu.VMEM((tm, d), BF16)],
        compiler_params=_params(VMEM_LARGE_MB, 2),
    )(x, norm_w, shift, scale, *w_operands, conv_w)
    return res[:6], ([(w, 0) for w in res[6:]] if round_weights else w_groups)


def _inproj_latent(x, norm_w, shift, scale, w_groups, conv_w, layer, tables, to_round, *, tm):
    rows, d = x.shape
    nt = rows // tm
    cos, sa, sb = tables
    kern = functools.partial(_inproj_latent_kernel, tm=tm, n_cast=len(to_round))
    done = lambda i: jnp.maximum(i - 1, 0)
    vec = lambda: pl.BlockSpec((1, d), lambda i, j: (0, 0))
    tab = lambda: pl.BlockSpec((tm, HEAD_DIM), lambda i, j: (done(i), 0))
    out = lambda: pl.BlockSpec((tm, SUB), lambda i, j: (done(i), jnp.where(i == 0, 0, j)))
    seg_shape = jax.ShapeDtypeStruct((rows, SEG), BF16)
    in_specs = [
        pl.BlockSpec((tm // N_SUB, d), lambda i, j: (jnp.minimum(i, nt - 1) * N_SUB + j, 0)),
        vec(), vec(), vec(),
        *[_w_spec(first, idle_first_row=True) for _, first in w_groups],
        _convw_spec(layer),
        tab(), tab(), tab(),
    ]
    operands = [x, norm_w, shift, scale, *[w for w, _ in w_groups], conv_w, cos, sa, sb]
    edge = lambda: pl.BlockSpec((1, SUBLANES, SUB),
                                lambda i, j: (done(i), 0, jnp.where(i == 0, 0, j)))
    out_specs = [out() for _ in range(6)] + [edge(), edge()]
    out_shape = [seg_shape] * 6 + [jax.ShapeDtypeStruct((nt, SUBLANES, SEG), F32)] * 2
    n_blocks = W_CAST_BLOCKS
    assert n_blocks <= (nt + 1) * N_SUB
    blk = lambda i, j: jnp.minimum(i * N_SUB + j, n_blocks - 1)
    for w_all, which in to_round:
        n_rows, n_cols = w_all.shape[1:]
        assert n_rows % n_blocks == 0
        in_specs.append(pl.BlockSpec((None, n_rows // n_blocks, n_cols),
                                     lambda i, j, which=which: (which, blk(i, j), 0)))
        operands.append(w_all)
        out_specs.append(pl.BlockSpec((n_rows // n_blocks, n_cols), lambda i, j: (blk(i, j), 0)))
        out_shape.append(jax.ShapeDtypeStruct((n_rows, n_cols), BF16))
    res = pl.pallas_call(
        kern,
        grid=(nt + 1, N_SUB),
        in_specs=in_specs,
        out_specs=out_specs,
        out_shape=out_shape,
        scratch_shapes=[pltpu.VMEM((tm, d), BF16), pltpu.VMEM((tm, d), BF16),
                        pltpu.VMEM((N_SUB, SUBLANES, SUB), F32)],
        compiler_params=_params(VMEM_LARGE_MB, 2),
    )(*operands)
    return res[:6], res[6:8], res[8:]


def _inproj_kv_kernel(x_ref, nw_ref, shift_ref, scale_ref, wk_ref, wv_ref, k_ref, v_ref, hx_ref):
    @pl.when(pl.program_id(1) == 0)
    def _():
        _prologue(x_ref, nw_ref, shift_ref, scale_ref, hx_ref)

    hx = hx_ref[...]
    k_ref[...] = (jnp.dot(hx, wk_ref[...], preferred_element_type=F32) * K_SCALE).astype(BF16)
    v_ref[...] = jnp.dot(hx, wv_ref[...], preferred_element_type=F32).astype(BF16)


def _inproj_kv(x, norm_w, shift, scale, w_groups, *, tm):
    rows, d = x.shape
    vec = lambda: pl.BlockSpec((1, d), lambda i, j: (0, 0))
    out = lambda: pl.BlockSpec((tm, SUB), lambda i, j: (i, j))
    seg_shape = jax.ShapeDtypeStruct((rows, SEG), BF16)
    return pl.pallas_call(
        _inproj_kv_kernel,
        grid=(rows // tm, N_SUB),
        in_specs=[
            pl.BlockSpec((tm, d), lambda i, j: (i, 0)),
            vec(), vec(), vec(),
            _w_spec(w_groups[G_K][1]), _w_spec(w_groups[G_V][1]),
        ],
        out_specs=[out(), out()],
        out_shape=[seg_shape] * 2,
        scratch_shapes=[pltpu.VMEM((tm, d), BF16)],
        compiler_params=_params(VMEM_SMALL_MB, 2),
    )(x, norm_w, shift, scale, w_groups[G_K][0], w_groups[G_V][0])


def _states_kernel(lgf_ref, lgb_ref, k_ref, v_ref, s0f_ref, s0b_ref,
                   sf_ref, sb_ref, ff_ref, fb_ref, kvb_ref, *, nc, nt):
    t = pl.program_id(0)

    @pl.when(t == 0)
    def _():
        ff_ref[...] = s0f_ref[...]
        fb_ref[...] = s0b_ref[...]

    row = lax.broadcasted_iota(jnp.int32, (CHUNK, HEAD_DIM), 0).astype(F32)
    pos = lax.broadcasted_iota(jnp.int32, (HEAD_DIM, CHUNK), 1).astype(F32)

    @pl.when(t < nt)
    def _():
        for h in range(RET_HEADS):
            hs = slice(h * HEAD_DIM, (h + 1) * HEAD_DIM)
            kdf = jnp.exp(lgf_ref[h] * (CHUNK - 1.0 - pos))
            kdb = jnp.exp(lgb_ref[h] * pos)
            cdf = jnp.exp(lgf_ref[h] * CHUNK + 0.0 * row)
            s = ff_ref[h]
            for c in range(nc):
                rs = slice(c * CHUNK, (c + 1) * CHUNK)
                sf_ref[c, h] = s.astype(BF16)
                kt = k_ref[rs, hs].astype(F32).T
                v = v_ref[rs, hs]
                s = cdf * s + jnp.dot((kt * kdf).astype(BF16), v, preferred_element_type=F32)
                kvb_ref[t * nc + c, h] = jnp.dot((kt * kdb).astype(BF16), v,
                                                 preferred_element_type=F32)
            ff_ref[h] = s

    @pl.when(t >= nt)
    def _():
        tile = 2 * nt - 1 - t
        for h in range(RET_HEADS):
            cdb = jnp.exp(lgb_ref[h] * CHUNK + 0.0 * row)
            s = fb_ref[h]
            for c in reversed(range(nc)):
                sb_ref[c, h] = s.astype(BF16)
                s = cdb * s + kvb_ref[tile * nc + c, h]
            fb_ref[h] = s


def _states(lg_f, lg_b, k, v, s0f, s0b, *, tr):
    rows = k.shape[0]
    nt = rows // tr
    nc = tr // CHUNK
    n_chunks = rows // CHUNK
    kern = functools.partial(_states_kernel, nc=nc, nt=nt)
    smem = lambda: pl.BlockSpec(memory_space=pltpu.SMEM)
    st = lambda: pl.BlockSpec((RET_HEADS, HEAD_DIM, HEAD_DIM), lambda t: (0, 0, 0))
    rows_in = lambda: pl.BlockSpec((tr, SEG), lambda t: (jnp.minimum(t, nt - 1), 0))
    chunk_states = (nc, RET_HEADS, HEAD_DIM, HEAD_DIM)
    seq_shape = pltpu.HBM((n_chunks, RET_HEADS, HEAD_DIM, HEAD_DIM), BF16)
    fin_shape = jax.ShapeDtypeStruct((RET_HEADS, HEAD_DIM, HEAD_DIM), F32)
    return pl.pallas_call(
        kern,
        grid=(2 * nt,),
        in_specs=[smem(), smem(), rows_in(), rows_in(), st(), st()],
        out_specs=[
            pl.BlockSpec(chunk_states, lambda t: (jnp.minimum(t, nt - 1), 0, 0, 0)),
            pl.BlockSpec(chunk_states, lambda t: (nt - 1 - jnp.maximum(t - nt, 0), 0, 0, 0)),
            st(), st(),
        ],
        out_shape=[seq_shape, seq_shape, fin_shape, fin_shape],
        scratch_shapes=[pltpu.VMEM((n_chunks, RET_HEADS, HEAD_DIM, HEAD_DIM), F32)],
        compiler_params=_params(VMEM_LARGE_MB, 1),
    )(lg_f, lg_b, k, v, s0f, s0b)


def _lane_means(ts):
    parts = []
    for t in ts:
        hi = t.astype(BF16)
        lo = (t - hi.astype(F32)).astype(BF16)
        parts.append(jnp.concatenate([hi, lo], axis=1))
    rows = ts[0].shape[0]
    ones = jnp.full((2 * HEAD_DIM, HEAD_DIM), 1.0 / HEAD_DIM, BF16)
    m = jnp.dot(jnp.concatenate(parts, axis=0), ones, preferred_element_type=F32)
    return [m[n * rows:(n + 1) * rows] for n in range(len(ts))]


def _out_kernel(lgf_ref, lgb_ref,
                ycv_ref, zc_ref, q_ref, k_ref, v_ref, zr_ref, eb_ref, ec_ref,
                sf_ref, sb_ref, x_ref, gate_ref, convw_ref, cnw_ref, gnw_ref, wout_ref, fnw_ref,
                o_ref,
                mask_ref, qdf_ref, qdb_ref, ya_ref, yb_ref, *, tm, n_tiles, edge_every, final):
    s_id = pl.program_id(0)
    i = jnp.maximum(s_id - 1, 0)
    nc = tm // CHUNK
    at_edge = jnp.logical_and((i + 1) % edge_every == 0, i + 1 < n_tiles)

    @pl.when(s_id == 0)
    def _():
        r = lax.broadcasted_iota(jnp.int32, (CHUNK, CHUNK), 0).astype(F32)
        cc = lax.broadcasted_iota(jnp.int32, (CHUNK, CHUNK), 1).astype(F32)
        d = r - cc
        for h in range(RET_HEADS):
            lgf = lgf_ref[h]
            lgb = lgb_ref[h]
            mf = jnp.where(d >= 0, jnp.exp(lgf * jnp.maximum(d, 0.0)), 0.0)
            mb = jnp.where(d <= 0, jnp.exp(lgb * jnp.maximum(-d, 0.0)), 0.0)
            mask_ref[h] = mf + mb
            qdf_ref[h] = jnp.exp(lgf * (r + 1.0))
            qdb_ref[h] = jnp.exp(lgb * (CHUNK - r))

    def step(yr_old, yr_new):
        n_q = PROJ_CHUNKS
        cw_ = D_CONV // n_q
        pw = D_MODEL // n_q

        if yr_old is not None:
            last_row = lax.broadcasted_iota(jnp.int32, (SUBLANES, 1), 0) == SUBLANES - 1
            cwt = convw_ref[0]
            proj_ret, y_l, ssq = [], [], None
            for n in range(n_q):
                proj_ret.append(jnp.dot(yr_old[...], wout_ref[D_CONV:, n * pw:(n + 1) * pw],
                                        preferred_element_type=F32))
                cs = slice(n * cw_, (n + 1) * cw_)
                y = ycv_ref[:, cs].astype(F32)
                below = eb_ref[0, SUBLANES - 1:SUBLANES, cs] * cwt[2:3, cs] * ec_ref[0, 0:1, cs]
                below = jnp.where(jnp.logical_and(at_edge, last_row), below, 0.0)
                y = jnp.concatenate([y[:tm - SUBLANES], y[tm - SUBLANES:] + below], axis=0)
                y_l.append(y)
                part = jnp.sum(y * y, axis=-1, keepdims=True)
                ssq = part if ssq is None else ssq + part
            rinv = lax.rsqrt(ssq * (1.0 / D_CONV) + EPS)

            y_conv = jnp.concatenate(
                [(zc_ref[:, n * cw_:(n + 1) * cw_].astype(F32)
                  * (y_l[n] * rinv * cnw_ref[:, n * cw_:(n + 1) * cw_])).astype(BF16)
                 for n in range(n_q)], axis=1)

        pending = list(range(n_q)) if yr_old is not None else []
        out_ssq = []

        def finish_chunk():
            if not pending:
                return
            n = pending.pop(0)
            cols = slice(n * pw, (n + 1) * pw)
            pc = jnp.dot(y_conv, wout_ref[:D_CONV, cols], preferred_element_type=F32)
            xn = x_ref[:, cols] + gate_ref[:, cols] * (pc + proj_ret[n])
            o_ref[:, cols] = xn
            if final:
                out_ssq.append(jnp.sum(xn * xn, axis=-1, keepdims=True))

        nt = (((1,), (1,)), ((), ()))
        rs = lambda c: slice(c * CHUNK, (c + 1) * CHUNK)
        hs = lambda h: slice(h * HEAD_DIM, (h + 1) * HEAD_DIM)
        rounds = range(0, nc, RET_ROUND_CHUNKS) if yr_new is not None else []
        per_round = -(-n_q // max(len(rounds), 1) // 2)
        for c0 in rounds:
            pairs = [(c, h) for c in range(c0, c0 + RET_ROUND_CHUNKS) for h in range(RET_HEADS)]
            q_l = [q_ref[rs(c), hs(h)] for c, h in pairs]
            s_l = [lax.dot_general(q, k_ref[rs(c), hs(h)], nt, preferred_element_type=F32)
                   for q, (c, h) in zip(q_l, pairs)]
            for _ in range(per_round):
                finish_chunk()
            lhs_l = []
            for q, s, (c, h) in zip(q_l, s_l, pairs):
                qf32 = q.astype(F32)
                lhs_l.append(jnp.concatenate(
                    [(s * mask_ref[h]).astype(BF16), (qf32 * qdf_ref[h]).astype(BF16),
                     (qf32 * qdb_ref[h]).astype(BF16)], axis=1))
            o_l = [jnp.dot(lhs, jnp.concatenate([v_ref[rs(c), hs(h)], sf_ref[c, h],
                                                 sb_ref[c, h]], axis=0),
                           preferred_element_type=F32)
                   for lhs, (c, h) in zip(lhs_l, pairs)]
            for _ in range(per_round):
                finish_chunk()
            per_chunk = lambda xs: [xs[n * RET_HEADS:(n + 1) * RET_HEADS]
                                    for n in range(len(xs) // RET_HEADS)]
            mean_l = [m for grp in per_chunk(o_l) for m in _lane_means(grp)]
            oc_l = [o - m for o, m in zip(o_l, mean_l)]
            var_l = [m for grp in per_chunk([oc * oc for oc in oc_l]) for m in _lane_means(grp)]
            for oc, var, (c, h) in zip(oc_l, var_l, pairs):
                on = oc * lax.rsqrt(var + EPS) * gnw_ref[:, hs(h)]
                yr_new[rs(c), hs(h)] = (zr_ref[rs(c), hs(h)].astype(F32) * on).astype(BF16)
        while pending:
            finish_chunk()
        if final and yr_old is not None:
            ms = functools.reduce(lambda a, b: a + b, out_ssq) * (1.0 / D_MODEL)
            o_ref[...] = o_ref[...] * lax.rsqrt(ms + EPS) * fnw_ref[...]

    middle = jnp.logical_and(s_id > 0, s_id < n_tiles)

    @pl.when(s_id == 0)
    def _():
        step(None, ya_ref)

    @pl.when(jnp.logical_and(middle, s_id % 2 == 0))
    def _():
        step(yb_ref, ya_ref)

    @pl.when(jnp.logical_and(middle, s_id % 2 == 1))
    def _():
        step(ya_ref, yb_ref)

    @pl.when(s_id == n_tiles)
    def _():
        step(ya_ref if n_tiles % 2 == 1 else yb_ref, None)


def _out(lg_f, lg_b, segs, edges, sf, sb, x, gate, conv_w, cnw, gnw, w_out, fnw, layer, *,
         tm, final):
    ycv, zc, q, k, v, zr = segs
    eb, ec = edges
    rows, d = x.shape
    nt = rows // tm
    nc = tm // CHUNK
    n_edge = eb.shape[0]
    assert rows % tm == 0 and nt % n_edge == 0 and nc % RET_ROUND_CHUNKS == 0
    edge_every = nt // n_edge
    kern = functools.partial(_out_kernel, tm=tm, n_tiles=nt, edge_every=edge_every, final=final)
    smem = lambda: pl.BlockSpec(memory_space=pltpu.SMEM)
    cur = lambda s: jnp.minimum(s, nt - 1)
    prv = lambda s: jnp.maximum(s - 1, 0)
    seg = lambda: pl.BlockSpec((tm, SEG), lambda s: (cur(s), 0))
    fin = lambda: pl.BlockSpec((tm, SEG), lambda s: (prv(s), 0))
    st = lambda: pl.BlockSpec((nc, RET_HEADS, HEAD_DIM, HEAD_DIM), lambda s: (cur(s), 0, 0, 0))
    full = lambda shape: pl.BlockSpec(shape, lambda s: (0,) * len(shape))
    return pl.pallas_call(
        kern,
        grid=(nt + 1,),
        in_specs=[
            smem(), smem(),
            fin(), fin(), seg(), seg(), seg(), seg(),
            pl.BlockSpec((1, SUBLANES, SEG), lambda s: (prv(s) // edge_every, 0, 0)),
            pl.BlockSpec((1, SUBLANES, SEG),
                         lambda s: (jnp.minimum(prv(s) // edge_every + 1, n_edge - 1), 0, 0)),
            st(), st(),
            pl.BlockSpec((tm, d), lambda s: (prv(s), 0)),
            full((1, d)),
            pl.BlockSpec((1, 3, D_CONV), lambda s: (layer, 0, 0)),
            full((1, D_CONV)), full((1, D_RET)),
            pl.BlockSpec((d, d), lambda s: (0, 0), pipeline_mode=pl.Buffered(1)),
            full((1, d)),
        ],
        out_specs=pl.BlockSpec((tm, d), lambda s: (prv(s), 0)),
        out_shape=jax.ShapeDtypeStruct((rows, d), F32),
        scratch_shapes=[
            pltpu.VMEM((RET_HEADS, CHUNK, CHUNK), F32),
            pltpu.VMEM((RET_HEADS, CHUNK, HEAD_DIM), F32),
            pltpu.VMEM((RET_HEADS, CHUNK, HEAD_DIM), F32),
            pltpu.VMEM((tm, D_RET), BF16),
            pltpu.VMEM((tm, D_RET), BF16),
        ],
        compiler_params=_params(VMEM_LARGE_MB, 1),
    )(lg_f, lg_b, ycv, zc, q, k, v, zr, eb, ec, sf, sb, x, gate, conv_w, cnw, gnw,
      w_out, fnw)


def _rope_tables(seq):
    t = np.arange(seq)
    row = (t // GRID_W).astype(np.float64)
    col = (t % GRID_W).astype(np.float64)
    inv = ROPE_BASE ** (-np.arange(ROPE_F, dtype=np.float64) / ROPE_F)
    ar = row[:, None] * inv[None, :]
    ac = col[:, None] * inv[None, :]
    z = np.zeros_like(ar)
    cos = np.concatenate([np.cos(ar), np.cos(ar), np.cos(ac), np.cos(ac)], axis=-1)
    sa = np.concatenate([-np.sin(ar), z, -np.sin(ac), z], axis=-1)
    sb = np.concatenate([z, np.sin(ar), z, np.sin(ac)], axis=-1)
    return tuple(jnp.asarray(a, dtype=F32) for a in (cos, sa, sb))


def kernel(x, c, ctx, c_ctx, norm_w, w_mod, b_mod, w_in, conv_w, conv_norm_w, ret_norm_w,
           ret_decay_f, ret_decay_b, w_out, final_norm_w):
    batch, seq, d = x.shape
    assert batch == 1 and d == D_MODEL and seq % INPROJ_ROWS == 0 and seq % GRID_W == 0
    depth = norm_w.shape[0]
    ctx_len = ctx.shape[1]
    xs = x[0]
    cs = ctx[0]

    cv = jnp.zeros((SUBLANES, d), F32).at[0].set(c[0]).at[1].set(c_ctx)
    mod = _modulation(cv, w_mod, b_mod)
    tables = _rope_tables(seq)
    zero_state = jnp.zeros((RET_HEADS, HEAD_DIM, HEAD_DIM), F32)
    zero_edge = jnp.zeros((1, SUBLANES, D_CONV), F32)
    fnw = final_norm_w.reshape(1, d)
    w_groups = None

    for layer in range(depth):
        update_ctx = layer < depth - 1
        lg_f = -jnp.exp(ret_decay_f[layer].astype(F32))
        lg_b = -jnp.exp(ret_decay_b[layer].astype(F32))
        nw = norm_w[layer].reshape(1, d)
        m = mod[layer]
        shift, scale, gate = m[0:1, 0:d], m[0:1, d:2 * d], m[0:1, 2 * d:3 * d]
        shift_c, scale_c, gate_c = m[1:2, 0:d], m[1:2, d:2 * d], m[1:2, 2 * d:3 * d]
        cnw = conv_norm_w[layer].reshape(1, D_CONV)
        gnw = ret_norm_w[layer].reshape(1, D_RET)

        if update_ctx:
            segs_c, w_groups = _inproj(cs, nw, shift_c, scale_c, w_groups, conv_w, layer,
                                       w_f32=w_in if w_groups is None else None)
            k_c, v_c = segs_c[3], segs_c[4]
        else:
            if w_groups is None:
                w_groups = _split_groups(w_in[layer].astype(BF16))
            k_c, v_c = _inproj_kv(cs, nw, shift_c, scale_c, w_groups, tm=ctx_len)
        sf_c, sb_c, s_f, s_b = _states(lg_f, lg_b, k_c, v_c, zero_state, zero_state, tr=ctx_len)

        to_round = [(w_out, layer)] + ([(w_in, layer + 1)] if layer + 1 < depth else [])
        segs, edges, rounded = _inproj_latent(xs, nw, shift, scale, w_groups, conv_w, layer,
                                              tables, to_round, tm=INPROJ_ROWS)
        w_out_b = rounded[0]
        w_groups = _split_groups(rounded[1]) if layer + 1 < depth else None
        sf, sb, _, _ = _states(lg_f, lg_b, segs[3], segs[4], s_f, s_b, tr=STATES_ROWS)
        xs = _out(lg_f, lg_b, segs, edges, sf, sb, xs, gate, conv_w, cnw, gnw, w_out_b, fnw, layer,
                  tm=OUT_ROWS, final=not update_ctx)
        if update_ctx:
            cs = _out(lg_f, lg_b, segs_c, (zero_edge, zero_edge), sf_c, sb_c, cs, gate_c, conv_w,
                      cnw, gnw, w_out_b, fnw, layer, tm=ctx_len, final=False)
    return xs[None]
```

```python
import functools

import numpy as np
import jax
import jax.numpy as jnp
from jax import lax
from jax.experimental import pallas as pl
from jax.experimental.pallas import tpu as pltpu

D_MODEL = 2048
D_CONV = 1024
D_RET = 1024
RET_HEADS = 8
HEAD_DIM = 128
CHUNK = 128
SEG = 1024
N_SEG = 8
SUB = 256
N_SUB = SEG // SUB
GRID_W = 64
ROPE_BASE = 10000.0
ROPE_F = 32
EPS = 1e-6
K_SCALE = HEAD_DIM ** -0.5

F32 = jnp.float32
BF16 = jnp.bfloat16

G_H, G_B, G_C, G_ZC, G_Q, G_K, G_V, G_ZR = range(8)


def _silu(x):
    return x / (1.0 + jnp.exp(-x))


INPROJ_ROWS = 1024
OUT_ROWS = 512
STATES_ROWS = 1024
MOD_COLS = 1024
PROLOGUE_ROWS = 64
RET_ROUND_CHUNKS = 2
PROJ_CHUNKS = 4
W_CAST_BLOCKS = 32
VMEM_LARGE_MB = 56
VMEM_MOD_MB = 40
VMEM_SMALL_MB = 32
SUBLANES = 8


def _params(vmem_mb, n_axes):
    return pltpu.CompilerParams(
        dimension_semantics=("arbitrary",) * n_axes,
        vmem_limit_bytes=vmem_mb * 1024 * 1024,
    )


def _mod_kernel(cv_ref, w_ref, b_ref, o_ref):
    s = _silu(cv_ref[...])
    o_ref[...] = jnp.dot(s.astype(BF16), w_ref[...].astype(BF16),
                         preferred_element_type=F32) + b_ref[...]


def _modulation(cv, w_mod, b_mod):
    depth, d, n = w_mod.shape
    tn = MOD_COLS
    return pl.pallas_call(
        _mod_kernel,
        grid=(n // tn,),
        in_specs=[
            pl.BlockSpec((SUBLANES, d), lambda j: (0, 0)),
            pl.BlockSpec((None, d, tn), lambda j: (0, 0, j)),
            pl.BlockSpec((None, 1, tn), lambda j: (0, 0, j)),
        ],
        out_specs=pl.BlockSpec((SUBLANES, tn), lambda j: (0, j)),
        out_shape=jax.ShapeDtypeStruct((SUBLANES, n), F32),
        compiler_params=_params(VMEM_MOD_MB, 1),
    )(cv, w_mod, b_mod.reshape(depth, 1, n))


def _prologue(x_ref, nw_ref, shift_ref, scale_ref, hx_ref):
    gain = nw_ref[...] * (1.0 + scale_ref[...])
    shift = shift_ref[...]

    def body(r, carry):
        rows = pl.ds(pl.multiple_of(r * PROLOGUE_ROWS, PROLOGUE_ROWS), PROLOGUE_ROWS)
        x = x_ref[rows, :]
        ms = jnp.mean(x * x, axis=-1, keepdims=True)
        hx_ref[rows, :] = (x * lax.rsqrt(ms + EPS) * gain + shift).astype(BF16)
        return carry

    lax.fori_loop(0, x_ref.shape[0] // PROLOGUE_ROWS, body, 0)


def _rope_pair(acc, cos, sa, sb):
    outs = []
    for h in range(SUB // HEAD_DIM):
        a = acc[:, h * HEAD_DIM:(h + 1) * HEAD_DIM]
        outs.append(a * cos + pltpu.roll(a, HEAD_DIM - ROPE_F, 1) * sa
                    + pltpu.roll(a, ROPE_F, 1) * sb)
    return jnp.concatenate(outs, axis=1)


def _project_groups(hx, w_refs, convw_ref, out_refs, table_refs=None, prev_row=None,
                    edge_refs=None):
    wh_ref, wb_ref, wc_ref, wzc_ref, wq_ref, wk_ref, wv_ref, wzr_ref = w_refs
    yb_ref, zc_ref, q_ref, k_ref, v_ref, zr_ref = out_refs
    tm = hx.shape[0]

    def seg(w_ref):
        return jnp.dot(hx, w_ref[...], preferred_element_type=F32)

    ch = seg(wc_ref) * seg(wh_ref)
    ridx = lax.broadcasted_iota(jnp.int32, (tm, 1), 0)
    above = pltpu.roll(ch, 1, 0)
    above = jnp.where(ridx == 0, 0.0 if prev_row is None else prev_row, above)
    below = jnp.where(ridx == tm - 1, 0.0, pltpu.roll(ch, tm - 1, 0))
    cw = convw_ref[...]
    b = seg(wb_ref)
    yb_ref[...] = (b * (above * cw[0:1, :] + ch * cw[1:2, :] + below * cw[2:3, :])).astype(BF16)
    if edge_refs is not None:
        edge_refs[0][0] = b[tm - SUBLANES:tm, :]
        edge_refs[1][0] = ch[0:SUBLANES, :]
    zc_ref[...] = _silu(seg(wzc_ref)).astype(BF16)
    if table_refs is not None:
        cos, sa, sb = (t[...] for t in table_refs)
        q_ref[...] = _rope_pair(seg(wq_ref), cos, sa, sb).astype(BF16)
        k_ref[...] = _rope_pair(seg(wk_ref), cos * K_SCALE, sa * K_SCALE, sb * K_SCALE).astype(BF16)
    else:
        q_ref[...] = seg(wq_ref).astype(BF16)
        k_ref[...] = (seg(wk_ref) * K_SCALE).astype(BF16)
    v_ref[...] = seg(wv_ref).astype(BF16)
    zr_ref[...] = _silu(seg(wzr_ref)).astype(BF16)
    return ch[tm - SUBLANES:tm, :]


def _inproj_kernel(x_ref, nw_ref, shift_ref, scale_ref, *refs, round_weights):
    w_refs, convw_ref = refs[:8], refs[8]
    out_refs, refs = refs[9:15], refs[15:]
    if round_weights:
        for w_ref, wb_ref in zip(w_refs, refs[:8]):
            wb_ref[...] = w_ref[...].astype(BF16)
        w_refs, refs = refs[:8], refs[8:]
    hx_ref, = refs

    @pl.when(pl.program_id(1) == 0)
    def _():
        _prologue(x_ref, nw_ref, shift_ref, scale_ref, hx_ref)

    _project_groups(hx_ref[...], w_refs, convw_ref, out_refs)


def _inproj_latent_kernel(xq_ref, nw_ref, shift_ref, scale_ref, *refs, tm, n_cast, mod_next):
    w_refs, convw_ref, table_refs = refs[:8], refs[8], refs[9:12]
    cast_in_refs, refs = refs[12:12 + n_cast], refs[12 + n_cast:]
    if mod_next:
        (cb_ref, wm_ref, bm_ref), refs = refs[:3], refs[3:]
    out_refs, edge_refs, refs = refs[:6], refs[6:8], refs[8:]
    cast_out_refs, refs = refs[:n_cast], refs[n_cast:]
    if mod_next:
        modn_ref, refs = refs[0], refs[1:]
        hxa_ref, hxb_ref, carry_ref, macc_ref = refs
    else:
        hxa_ref, hxb_ref, carry_ref = refs
    i = pl.program_id(0)
    j = pl.program_id(1)
    quarter = tm // N_SUB
    row0 = pl.multiple_of(j * quarter, quarter)

    def modulation_rows():
        live = (i * N_SUB + j < W_CAST_BLOCKS).astype(F32)
        rows_k = wm_ref.shape[0]
        for r in range(2):
            s = _silu(cb_ref[r]) * live
            for g in range(wm_ref.shape[1] // HEAD_DIM):
                lanes = slice(g * HEAD_DIM, (g + 1) * HEAD_DIM)
                prod = wm_ref[:, lanes] * s
                part = prod[0:SUBLANES]
                for k0 in range(SUBLANES, rows_k, SUBLANES):
                    part = part + prod[k0:k0 + SUBLANES]
                macc_ref[r, :, lanes] += part
        sel = lax.broadcasted_iota(jnp.int32, (SUBLANES, 1), 0)
        row = lambda r: jnp.sum(macc_ref[r], axis=0, keepdims=True) + bm_ref[...]
        modn_ref[...] = jnp.where(sel == 0, row(0), jnp.where(sel == 1, row(1), 0.0))

    if mod_next:
        @pl.when(jnp.logical_and(i == 0, j == 0))
        def _():
            macc_ref[...] = jnp.zeros_like(macc_ref)

    def project(hx_ref):
        prev_row = carry_ref[j][SUBLANES - 1:SUBLANES, :]
        carry_ref[j] = _project_groups(hx_ref[...], w_refs, convw_ref, out_refs, table_refs,
                                       prev_row, edge_refs)

    def normalise(hx_ref):
        for src_ref, dst_ref in zip(cast_in_refs, cast_out_refs):
            dst_ref[...] = src_ref[...].astype(BF16)
        if mod_next:
            modulation_rows()
        gain = nw_ref[...] * (1.0 + scale_ref[...])
        shift = shift_ref[...]
        for r in range(0, quarter, PROLOGUE_ROWS):
            x = xq_ref[r:r + PROLOGUE_ROWS, :]
            ms = jnp.mean(x * x, axis=-1, keepdims=True)
            hx_ref[pl.ds(row0 + r, PROLOGUE_ROWS), :] = (
                x * lax.rsqrt(ms + EPS) * gain + shift).astype(BF16)

    @pl.when(i == 0)
    def _():
        normalise(hxa_ref)
        carry_ref[j] = jnp.zeros(carry_ref.shape[1:], F32)

    @pl.when(i % 2 == 1)
    def _():
        normalise(hxb_ref)
        project(hxa_ref)

    @pl.when(jnp.logical_and(i > 0, i % 2 == 0))
    def _():
        normalise(hxa_ref)
        project(hxb_ref)


def _w_spec(first_block, idle_first_row=False):
    if idle_first_row:
        return pl.BlockSpec((D_MODEL, SUB),
                            lambda i, j: (0, first_block + jnp.where(i == 0, 0, j)))
    return pl.BlockSpec((D_MODEL, SUB), lambda i, j: (0, first_block + j))


def _split_groups(w_bf16):
    return [(w_bf16, g * N_SUB) for g in range(N_SEG)]


def _convw_spec(layer):
    return pl.BlockSpec((None, 3, SUB), lambda i, j: (layer, 0, j))


def _inproj(x, norm_w, shift, scale, w_groups, conv_w, layer, *, w_f32=None):
    tm, d = x.shape
    round_weights = w_f32 is not None
    kern = functools.partial(_inproj_kernel, round_weights=round_weights)
    vec = lambda: pl.BlockSpec((1, d), lambda i, j: (0, 0))
    out = lambda: pl.BlockSpec((tm, SUB), lambda i, j: (i, j))
    seg_shape = jax.ShapeDtypeStruct((tm, SEG), BF16)
    out_specs = [out() for _ in range(6)]
    out_shape = [seg_shape] * 6
    if round_weights:
        w_specs = [pl.BlockSpec((None, d, SUB), lambda i, j, g=g: (layer, 0, g * N_SUB + j))
                   for g in range(N_SEG)]
        w_operands = [w_f32] * N_SEG
        out_specs += [pl.BlockSpec((d, SUB), lambda i, j: (0, j)) for _ in range(N_SEG)]
        out_shape += [jax.ShapeDtypeStruct((d, SEG), BF16)] * N_SEG
    else:
        w_specs = [_w_spec(first) for _, first in w_groups]
        w_operands = [w for w, _ in w_groups]
    res = pl.pallas_call(
        kern,
        grid=(1, N_SUB),
        in_specs=[
            pl.BlockSpec((tm, d), lambda i, j: (i, 0)),
            vec(), vec(), vec(),
            *w_specs,
            _convw_spec(layer),
        ],
        out_specs=out_specs,
        out_shape=out_shape,
        scratch_shapes=[pltpu.VMEM((tm, d), BF16)],
        compiler_params=_params(VMEM_LARGE_MB, 2),
    )(x, norm_w, shift, scale, *w_operands, conv_w)
    return res[:6], ([(w, 0) for w in res[6:]] if round_weights else w_groups)


def _inproj_latent(x, norm_w, shift, scale, w_groups, conv_w, layer, tables, to_round,
                   next_mod, *, tm):
    rows, d = x.shape
    nt = rows // tm
    cos, sa, sb = tables
    kern = functools.partial(_inproj_latent_kernel, tm=tm, n_cast=len(to_round),
                             mod_next=next_mod is not None)
    done = lambda i: jnp.maximum(i - 1, 0)
    vec = lambda: pl.BlockSpec((1, d), lambda i, j: (0, 0))
    tab = lambda: pl.BlockSpec((tm, HEAD_DIM), lambda i, j: (done(i), 0))
    out = lambda: pl.BlockSpec((tm, SUB), lambda i, j: (done(i), jnp.where(i == 0, 0, j)))
    seg_shape = jax.ShapeDtypeStruct((rows, SEG), BF16)
    in_specs = [
        pl.BlockSpec((tm // N_SUB, d), lambda i, j: (jnp.minimum(i, nt - 1) * N_SUB + j, 0)),
        vec(), vec(), vec(),
        *[_w_spec(first, idle_first_row=True) for _, first in w_groups],
        _convw_spec(layer),
        tab(), tab(), tab(),
    ]
    operands = [x, norm_w, shift, scale, *[w for w, _ in w_groups], conv_w, cos, sa, sb]
    edge = lambda: pl.BlockSpec((1, SUBLANES, SUB),
                                lambda i, j: (done(i), 0, jnp.where(i == 0, 0, j)))
    out_specs = [out() for _ in range(6)] + [edge(), edge()]
    out_shape = [seg_shape] * 6 + [jax.ShapeDtypeStruct((nt, SUBLANES, SEG), F32)] * 2
    n_blocks = W_CAST_BLOCKS
    assert n_blocks <= (nt + 1) * N_SUB
    blk = lambda i, j: jnp.minimum(i * N_SUB + j, n_blocks - 1)
    for w_all, which in to_round:
        n_rows, n_cols = w_all.shape[1:]
        assert n_rows % n_blocks == 0
        in_specs.append(pl.BlockSpec((None, n_rows // n_blocks, n_cols),
                                     lambda i, j, which=which: (which, blk(i, j), 0)))
        operands.append(w_all)
        out_specs.append(pl.BlockSpec((n_rows // n_blocks, n_cols), lambda i, j: (blk(i, j), 0)))
        out_shape.append(jax.ShapeDtypeStruct((n_rows, n_cols), BF16))
    scratch = [pltpu.VMEM((tm, d), BF16), pltpu.VMEM((tm, d), BF16),
               pltpu.VMEM((N_SUB, SUBLANES, SUB), F32)]
    if next_mod is not None:
        c_lanes, w_mod, b_mod = next_mod
        n_mod = w_mod.shape[2]
        assert d % n_blocks == 0
        in_specs += [
            pl.BlockSpec((2, d // n_blocks, HEAD_DIM), lambda i, j: (0, blk(i, j), 0)),
            pl.BlockSpec((None, d // n_blocks, n_mod), lambda i, j: (layer + 1, blk(i, j), 0)),
            pl.BlockSpec((None, 1, n_mod), lambda i, j: (layer + 1, 0, 0)),
        ]
        operands += [c_lanes, w_mod, b_mod.reshape(b_mod.shape[0], 1, n_mod)]
        out_specs.append(pl.BlockSpec((SUBLANES, n_mod), lambda i, j: (0, 0)))
        out_shape.append(jax.ShapeDtypeStruct((SUBLANES, n_mod), F32))
        scratch.append(pltpu.VMEM((2, SUBLANES, n_mod), F32))
    res = pl.pallas_call(
        kern,
        grid=(nt + 1, N_SUB),
        in_specs=in_specs,
        out_specs=out_specs,
        out_shape=out_shape,
        scratch_shapes=scratch,
        compiler_params=_params(VMEM_LARGE_MB, 2),
    )(*operands)
    n_round = len(to_round)
    return res[:6], res[6:8], res[8:8 + n_round], (res[8 + n_round] if next_mod else None)


def _inproj_kv_kernel(x_ref, nw_ref, shift_ref, scale_ref, wk_ref, wv_ref, k_ref, v_ref, hx_ref):
    @pl.when(pl.program_id(1) == 0)
    def _():
        _prologue(x_ref, nw_ref, shift_ref, scale_ref, hx_ref)

    hx = hx_ref[...]
    k_ref[...] = (jnp.dot(hx, wk_ref[...], preferred_element_type=F32) * K_SCALE).astype(BF16)
    v_ref[...] = jnp.dot(hx, wv_ref[...], preferred_element_type=F32).astype(BF16)


def _inproj_kv(x, norm_w, shift, scale, w_groups, *, tm):
    rows, d = x.shape
    vec = lambda: pl.BlockSpec((1, d), lambda i, j: (0, 0))
    out = lambda: pl.BlockSpec((tm, SUB), lambda i, j: (i, j))
    seg_shape = jax.ShapeDtypeStruct((rows, SEG), BF16)
    return pl.pallas_call(
        _inproj_kv_kernel,
        grid=(rows // tm, N_SUB),
        in_specs=[
            pl.BlockSpec((tm, d), lambda i, j: (i, 0)),
            vec(), vec(), vec(),
            _w_spec(w_groups[G_K][1]), _w_spec(w_groups[G_V][1]),
        ],
        out_specs=[out(), out()],
        out_shape=[seg_shape] * 2,
        scratch_shapes=[pltpu.VMEM((tm, d), BF16)],
        compiler_params=_params(VMEM_SMALL_MB, 2),
    )(x, norm_w, shift, scale, w_groups[G_K][0], w_groups[G_V][0])


def _states_kernel(lgf_ref, lgb_ref, k_ref, v_ref, s0f_ref, s0b_ref,
                   sf_ref, sb_ref, ff_ref, fb_ref, kvb_ref, *, nc, nt):
    t = pl.program_id(0)

    @pl.when(t == 0)
    def _():
        ff_ref[...] = s0f_ref[...]
        fb_ref[...] = s0b_ref[...]

    row = lax.broadcasted_iota(jnp.int32, (CHUNK, HEAD_DIM), 0).astype(F32)
    pos = lax.broadcasted_iota(jnp.int32, (HEAD_DIM, CHUNK), 1).astype(F32)

    @pl.when(t < nt)
    def _():
        for h in range(RET_HEADS):
            hs = slice(h * HEAD_DIM, (h + 1) * HEAD_DIM)
            kdf = jnp.exp(lgf_ref[h] * (CHUNK - 1.0 - pos))
            kdb = jnp.exp(lgb_ref[h] * pos)
            cdf = jnp.exp(lgf_ref[h] * CHUNK + 0.0 * row)
            s = ff_ref[h]
            for c in range(nc):
                rs = slice(c * CHUNK, (c + 1) * CHUNK)
                sf_ref[c, h] = s.astype(BF16)
                kt = k_ref[rs, hs].astype(F32).T
                v = v_ref[rs, hs]
                s = cdf * s + jnp.dot((kt * kdf).astype(BF16), v, preferred_element_type=F32)
                kvb_ref[t * nc + c, h] = jnp.dot((kt * kdb).astype(BF16), v,
                                                 preferred_element_type=F32)
            ff_ref[h] = s

    @pl.when(t >= nt)
    def _():
        tile = 2 * nt - 1 - t
        for h in range(RET_HEADS):
            cdb = jnp.exp(lgb_ref[h] * CHUNK + 0.0 * row)
            s = fb_ref[h]
            for c in reversed(range(nc)):
                sb_ref[c, h] = s.astype(BF16)
                s = cdb * s + kvb_ref[tile * nc + c, h]
            fb_ref[h] = s


def _states(lg_f, lg_b, k, v, s0f, s0b, *, tr):
    rows = k.shape[0]
    nt = rows // tr
    nc = tr // CHUNK
    n_chunks = rows // CHUNK
    kern = functools.partial(_states_kernel, nc=nc, nt=nt)
    smem = lambda: pl.BlockSpec(memory_space=pltpu.SMEM)
    st = lambda: pl.BlockSpec((RET_HEADS, HEAD_DIM, HEAD_DIM), lambda t: (0, 0, 0))
    rows_in = lambda: pl.BlockSpec((tr, SEG), lambda t: (jnp.minimum(t, nt - 1), 0))
    chunk_states = (nc, RET_HEADS, HEAD_DIM, HEAD_DIM)
    seq_shape = pltpu.HBM((n_chunks, RET_HEADS, HEAD_DIM, HEAD_DIM), BF16)
    fin_shape = jax.ShapeDtypeStruct((RET_HEADS, HEAD_DIM, HEAD_DIM), F32)
    return pl.pallas_call(
        kern,
        grid=(2 * nt,),
        in_specs=[smem(), smem(), rows_in(), rows_in(), st(), st()],
        out_specs=[
            pl.BlockSpec(chunk_states, lambda t: (jnp.minimum(t, nt - 1), 0, 0, 0)),
            pl.BlockSpec(chunk_states, lambda t: (nt - 1 - jnp.maximum(t - nt, 0), 0, 0, 0)),
            st(), st(),
        ],
        out_shape=[seq_shape, seq_shape, fin_shape, fin_shape],
        scratch_shapes=[pltpu.VMEM((n_chunks, RET_HEADS, HEAD_DIM, HEAD_DIM), F32)],
        compiler_params=_params(VMEM_LARGE_MB, 1),
    )(lg_f, lg_b, k, v, s0f, s0b)


def _lane_means(ts):
    parts = []
    for t in ts:
        hi = t.astype(BF16)
        lo = (t - hi.astype(F32)).astype(BF16)
        parts.append(jnp.concatenate([hi, lo], axis=1))
    rows = ts[0].shape[0]
    ones = jnp.full((2 * HEAD_DIM, HEAD_DIM), 1.0 / HEAD_DIM, BF16)
    m = jnp.dot(jnp.concatenate(parts, axis=0), ones, preferred_element_type=F32)
    return [m[n * rows:(n + 1) * rows] for n in range(len(ts))]


def _out_kernel(lgf_ref, lgb_ref,
                ycv_ref, zc_ref, q_ref, k_ref, v_ref, zr_ref, eb_ref, ec_ref,
                sf_ref, sb_ref, x_ref, gate_ref, convw_ref, cnw_ref, gnw_ref, wout_ref, fnw_ref,
                o_ref,
                mask_ref, qdf_ref, qdb_ref, ya_ref, yb_ref, *, tm, n_tiles, edge_every, final):
    s_id = pl.program_id(0)
    i = jnp.maximum(s_id - 1, 0)
    nc = tm // CHUNK
    at_edge = jnp.logical_and((i + 1) % edge_every == 0, i + 1 < n_tiles)

    @pl.when(s_id == 0)
    def _():
        r = lax.broadcasted_iota(jnp.int32, (CHUNK, CHUNK), 0).astype(F32)
        cc = lax.broadcasted_iota(jnp.int32, (CHUNK, CHUNK), 1).astype(F32)
        d = r - cc
        for h in range(RET_HEADS):
            lgf = lgf_ref[h]
            lgb = lgb_ref[h]
            mf = jnp.where(d >= 0, jnp.exp(lgf * jnp.maximum(d, 0.0)), 0.0)
            mb = jnp.where(d <= 0, jnp.exp(lgb * jnp.maximum(-d, 0.0)), 0.0)
            mask_ref[h] = mf + mb
            qdf_ref[h] = jnp.exp(lgf * (r + 1.0))
            qdb_ref[h] = jnp.exp(lgb * (CHUNK - r))

    def step(yr_old, yr_new):
        n_q = PROJ_CHUNKS
        cw_ = D_CONV // n_q
        pw = D_MODEL // n_q

        if yr_old is not None:
            last_row = lax.broadcasted_iota(jnp.int32, (SUBLANES, 1), 0) == SUBLANES - 1
            cwt = convw_ref[0]
            proj_ret, y_l, ssq = [], [], None
            for n in range(n_q):
                proj_ret.append(jnp.dot(yr_old[...], wout_ref[D_CONV:, n * pw:(n + 1) * pw],
                                        preferred_element_type=F32))
                cs = slice(n * cw_, (n + 1) * cw_)
                y = ycv_ref[:, cs].astype(F32)
                below = eb_ref[0, SUBLANES - 1:SUBLANES, cs] * cwt[2:3, cs] * ec_ref[0, 0:1, cs]
                below = jnp.where(jnp.logical_and(at_edge, last_row), below, 0.0)
                y = jnp.concatenate([y[:tm - SUBLANES], y[tm - SUBLANES:] + below], axis=0)
                y_l.append(y)
                part = jnp.sum(y * y, axis=-1, keepdims=True)
                ssq = part if ssq is None else ssq + part
            rinv = lax.rsqrt(ssq * (1.0 / D_CONV) + EPS)

            y_conv = jnp.concatenate(
                [(zc_ref[:, n * cw_:(n + 1) * cw_].astype(F32)
                  * (y_l[n] * rinv * cnw_ref[:, n * cw_:(n + 1) * cw_])).astype(BF16)
                 for n in range(n_q)], axis=1)

        pending = list(range(n_q)) if yr_old is not None else []
        out_ssq = []

        def finish_chunk():
            if not pending:
                return
            n = pending.pop(0)
            cols = slice(n * pw, (n + 1) * pw)
            pc = jnp.dot(y_conv, wout_ref[:D_CONV, cols], preferred_element_type=F32)
            xn = x_ref[:, cols] + gate_ref[:, cols] * (pc + proj_ret[n])
            o_ref[:, cols] = xn
            if final:
                out_ssq.append(jnp.sum(xn * xn, axis=-1, keepdims=True))

        nt = (((1,), (1,)), ((), ()))
        rs = lambda c: slice(c * CHUNK, (c + 1) * CHUNK)
        hs = lambda h: slice(h * HEAD_DIM, (h + 1) * HEAD_DIM)
        rounds = range(0, nc, RET_ROUND_CHUNKS) if yr_new is not None else []
        per_round = -(-n_q // max(len(rounds), 1) // 2)
        for c0 in rounds:
            pairs = [(c, h) for c in range(c0, c0 + RET_ROUND_CHUNKS) for h in range(RET_HEADS)]
            q_l = [q_ref[rs(c), hs(h)] for c, h in pairs]
            s_l = [lax.dot_general(q, k_ref[rs(c), hs(h)], nt, preferred_element_type=F32)
                   for q, (c, h) in zip(q_l, pairs)]
            for _ in range(per_round):
                finish_chunk()
            lhs_l = []
            for q, s, (c, h) in zip(q_l, s_l, pairs):
                qf32 = q.astype(F32)
                lhs_l.append(jnp.concatenate(
                    [(s * mask_ref[h]).astype(BF16), (qf32 * qdf_ref[h]).astype(BF16),
                     (qf32 * qdb_ref[h]).astype(BF16)], axis=1))
            o_l = [jnp.dot(lhs, jnp.concatenate([v_ref[rs(c), hs(h)], sf_ref[c, h],
                                                 sb_ref[c, h]], axis=0),
                           preferred_element_type=F32)
                   for lhs, (c, h) in zip(lhs_l, pairs)]
            for _ in range(per_round):
                finish_chunk()
            per_chunk = lambda xs: [xs[n * RET_HEADS:(n + 1) * RET_HEADS]
                                    for n in range(len(xs) // RET_HEADS)]
            mean_l = [m for grp in per_chunk(o_l) for m in _lane_means(grp)]
            oc_l = [o - m for o, m in zip(o_l, mean_l)]
            var_l = [m for grp in per_chunk([oc * oc for oc in oc_l]) for m in _lane_means(grp)]
            for oc, var, (c, h) in zip(oc_l, var_l, pairs):
                on = oc * lax.rsqrt(var + EPS) * gnw_ref[:, hs(h)]
                yr_new[rs(c), hs(h)] = (zr_ref[rs(c), hs(h)].astype(F32) * on).astype(BF16)
        while pending:
            finish_chunk()
        if final and yr_old is not None:
            ms = functools.reduce(lambda a, b: a + b, out_ssq) * (1.0 / D_MODEL)
            o_ref[...] = o_ref[...] * lax.rsqrt(ms + EPS) * fnw_ref[...]

    middle = jnp.logical_and(s_id > 0, s_id < n_tiles)

    @pl.when(s_id == 0)
    def _():
        step(None, ya_ref)

    @pl.when(jnp.logical_and(middle, s_id % 2 == 0))
    def _():
        step(yb_ref, ya_ref)

    @pl.when(jnp.logical_and(middle, s_id % 2 == 1))
    def _():
        step(ya_ref, yb_ref)

    @pl.when(s_id == n_tiles)
    def _():
        step(ya_ref if n_tiles % 2 == 1 else yb_ref, None)


def _out(lg_f, lg_b, segs, edges, sf, sb, x, gate, conv_w, cnw, gnw, w_out, fnw, layer, *,
         tm, final):
    ycv, zc, q, k, v, zr = segs
    eb, ec = edges
    rows, d = x.shape
    nt = rows // tm
    nc = tm // CHUNK
    n_edge = eb.shape[0]
    assert rows % tm == 0 and nt % n_edge == 0 and nc % RET_ROUND_CHUNKS == 0
    edge_every = nt // n_edge
    kern = functools.partial(_out_kernel, tm=tm, n_tiles=nt, edge_every=edge_every, final=final)
    smem = lambda: pl.BlockSpec(memory_space=pltpu.SMEM)
    cur = lambda s: jnp.minimum(s, nt - 1)
    prv = lambda s: jnp.maximum(s - 1, 0)
    seg = lambda: pl.BlockSpec((tm, SEG), lambda s: (cur(s), 0))
    fin = lambda: pl.BlockSpec((tm, SEG), lambda s: (prv(s), 0))
    st = lambda: pl.BlockSpec((nc, RET_HEADS, HEAD_DIM, HEAD_DIM), lambda s: (cur(s), 0, 0, 0))
    full = lambda shape: pl.BlockSpec(shape, lambda s: (0,) * len(shape))
    return pl.pallas_call(
        kern,
        grid=(nt + 1,),
        in_specs=[
            smem(), smem(),
            fin(), fin(), seg(), seg(), seg(), seg(),
            pl.BlockSpec((1, SUBLANES, SEG), lambda s: (prv(s) // edge_every, 0, 0)),
            pl.BlockSpec((1, SUBLANES, SEG),
                         lambda s: (jnp.minimum(prv(s) // edge_every + 1, n_edge - 1), 0, 0)),
            st(), st(),
            pl.BlockSpec((tm, d), lambda s: (prv(s), 0)),
            full((1, d)),
            pl.BlockSpec((1, 3, D_CONV), lambda s: (layer, 0, 0)),
            full((1, D_CONV)), full((1, D_RET)),
            pl.BlockSpec((d, d), lambda s: (0, 0), pipeline_mode=pl.Buffered(1)),
            full((1, d)),
        ],
        out_specs=pl.BlockSpec((tm, d), lambda s: (prv(s), 0)),
        out_shape=jax.ShapeDtypeStruct((rows, d), F32),
        scratch_shapes=[
            pltpu.VMEM((RET_HEADS, CHUNK, CHUNK), F32),
            pltpu.VMEM((RET_HEADS, CHUNK, HEAD_DIM), F32),
            pltpu.VMEM((RET_HEADS, CHUNK, HEAD_DIM), F32),
            pltpu.VMEM((tm, D_RET), BF16),
            pltpu.VMEM((tm, D_RET), BF16),
        ],
        compiler_params=_params(VMEM_LARGE_MB, 1),
    )(lg_f, lg_b, ycv, zc, q, k, v, zr, eb, ec, sf, sb, x, gate, conv_w, cnw, gnw,
      w_out, fnw)


def _rope_tables(seq):
    t = np.arange(seq)
    row = (t // GRID_W).astype(np.float64)
    col = (t % GRID_W).astype(np.float64)
    inv = ROPE_BASE ** (-np.arange(ROPE_F, dtype=np.float64) / ROPE_F)
    ar = row[:, None] * inv[None, :]
    ac = col[:, None] * inv[None, :]
    z = np.zeros_like(ar)
    cos = np.concatenate([np.cos(ar), np.cos(ar), np.cos(ac), np.cos(ac)], axis=-1)
    sa = np.concatenate([-np.sin(ar), z, -np.sin(ac), z], axis=-1)
    sb = np.concatenate([z, np.sin(ar), z, np.sin(ac)], axis=-1)
    return tuple(jnp.asarray(a, dtype=F32) for a in (cos, sa, sb))


def kernel(x, c, ctx, c_ctx, norm_w, w_mod, b_mod, w_in, conv_w, conv_norm_w, ret_norm_w,
           ret_decay_f, ret_decay_b, w_out, final_norm_w):
    batch, seq, d = x.shape
    assert batch == 1 and d == D_MODEL and seq % INPROJ_ROWS == 0 and seq % GRID_W == 0
    depth = norm_w.shape[0]
    ctx_len = ctx.shape[1]
    xs = x[0]
    cs = ctx[0]

    cv = jnp.zeros((SUBLANES, d), F32).at[0].set(c[0]).at[1].set(c_ctx)
    m = _modulation(cv, w_mod, b_mod)
    c_lanes = jnp.broadcast_to(cv[0:2, :, None], (2, d, HEAD_DIM))
    tables = _rope_tables(seq)
    zero_state = jnp.zeros((RET_HEADS, HEAD_DIM, HEAD_DIM), F32)
    zero_edge = jnp.zeros((1, SUBLANES, D_CONV), F32)
    fnw = final_norm_w.reshape(1, d)
    w_groups = None

    for layer in range(depth):
        update_ctx = layer < depth - 1
        lg_f = -jnp.exp(ret_decay_f[layer].astype(F32))
        lg_b = -jnp.exp(ret_decay_b[layer].astype(F32))
        nw = norm_w[layer].reshape(1, d)
        shift, scale, gate = m[0:1, 0:d], m[0:1, d:2 * d], m[0:1, 2 * d:3 * d]
        shift_c, scale_c, gate_c = m[1:2, 0:d], m[1:2, d:2 * d], m[1:2, 2 * d:3 * d]
        cnw = conv_norm_w[layer].reshape(1, D_CONV)
        gnw = ret_norm_w[layer].reshape(1, D_RET)

        if update_ctx:
            segs_c, w_groups = _inproj(cs, nw, shift_c, scale_c, w_groups, conv_w, layer,
                                       w_f32=w_in if w_groups is None else None)
            k_c, v_c = segs_c[3], segs_c[4]
        else:
            if w_groups is None:
                w_groups = _split_groups(w_in[layer].astype(BF16))
            k_c, v_c = _inproj_kv(cs, nw, shift_c, scale_c, w_groups, tm=ctx_len)
        sf_c, sb_c, s_f, s_b = _states(lg_f, lg_b, k_c, v_c, zero_state, zero_state, tr=ctx_len)

        more = layer + 1 < depth
        to_round = [(w_out, layer)] + ([(w_in, layer + 1)] if more else [])
        segs, edges, rounded, m_next = _inproj_latent(
            xs, nw, shift, scale, w_groups, conv_w, layer, tables, to_round,
            (c_lanes, w_mod, b_mod) if more else None, tm=INPROJ_ROWS)
        w_out_b = rounded[0]
        w_groups = _split_groups(rounded[1]) if more else None
        sf, sb, _, _ = _states(lg_f, lg_b, segs[3], segs[4], s_f, s_b, tr=STATES_ROWS)
        xs = _out(lg_f, lg_b, segs, edges, sf, sb, xs, gate, conv_w, cnw, gnw, w_out_b, fnw, layer,
                  tm=OUT_ROWS, final=not update_ctx)
        if update_ctx:
            cs = _out(lg_f, lg_b, segs_c, (zero_edge, zero_edge), sf_c, sb_c, cs, gate_c, conv_w,
                      cnw, gnw, w_out_b, fnw, layer, tm=ctx_len, final=False)
        m = m_next
    return xs[None]
```

```python
import functools

import numpy as np
import jax
import jax.numpy as jnp
from jax import lax
from jax.experimental import pallas as pl
from jax.experimental.pallas import tpu as pltpu

D_MODEL = 2048
D_CONV = 1024
D_RET = 1024
RET_HEADS = 8
HEAD_DIM = 128
CHUNK = 128
SEG = 1024
N_SEG = 8
SUB = 256
N_SUB = SEG // SUB
GRID_W = 64
ROPE_BASE = 10000.0
ROPE_F = 32
EPS = 1e-6
K_SCALE = HEAD_DIM ** -0.5

F32 = jnp.float32
BF16 = jnp.bfloat16

G_H, G_B, G_C, G_ZC, G_Q, G_K, G_V, G_ZR = range(8)


def _silu(x):
    return x / (1.0 + jnp.exp(-x))


INPROJ_ROWS = 1024
OUT_ROWS = 512
STATES_ROWS = 1024
MOD_COLS = 1024
PROLOGUE_ROWS = 64
RET_ROUND_CHUNKS = 2
PROJ_CHUNKS = 4
W_CAST_BLOCKS = 32
VMEM_LARGE_MB = 56
VMEM_MOD_MB = 40
VMEM_SMALL_MB = 32
SUBLANES = 8


def _params(vmem_mb, n_axes):
    return pltpu.CompilerParams(
        dimension_semantics=("arbitrary",) * n_axes,
        vmem_limit_bytes=vmem_mb * 1024 * 1024,
    )


def _mod_kernel(cv_ref, w_ref, b_ref, o_ref):
    s = _silu(cv_ref[...])
    o_ref[...] = jnp.dot(s.astype(BF16), w_ref[...].astype(BF16),
                         preferred_element_type=F32) + b_ref[...]


def _modulation(cv, w_mod, b_mod):
    depth, d, n = w_mod.shape
    tn = MOD_COLS
    return pl.pallas_call(
        _mod_kernel,
        grid=(n // tn,),
        in_specs=[
            pl.BlockSpec((SUBLANES, d), lambda j: (0, 0)),
            pl.BlockSpec((None, d, tn), lambda j: (0, 0, j)),
            pl.BlockSpec((None, 1, tn), lambda j: (0, 0, j)),
        ],
        out_specs=pl.BlockSpec((SUBLANES, tn), lambda j: (0, j)),
        out_shape=jax.ShapeDtypeStruct((SUBLANES, n), F32),
        compiler_params=_params(VMEM_MOD_MB, 1),
    )(cv, w_mod, b_mod.reshape(depth, 1, n))


def _prologue(x_ref, nw_ref, shift_ref, scale_ref, hx_ref):
    gain = nw_ref[...] * (1.0 + scale_ref[...])
    shift = shift_ref[...]

    def body(r, carry):
        rows = pl.ds(pl.multiple_of(r * PROLOGUE_ROWS, PROLOGUE_ROWS), PROLOGUE_ROWS)
        x = x_ref[rows, :]
        ms = jnp.mean(x * x, axis=-1, keepdims=True)
        hx_ref[rows, :] = (x * lax.rsqrt(ms + EPS) * gain + shift).astype(BF16)
        return carry

    lax.fori_loop(0, x_ref.shape[0] // PROLOGUE_ROWS, body, 0)


def _rope_pair(acc, cos, sa, sb):
    outs = []
    for h in range(SUB // HEAD_DIM):
        a = acc[:, h * HEAD_DIM:(h + 1) * HEAD_DIM]
        outs.append(a * cos + pltpu.roll(a, HEAD_DIM - ROPE_F, 1) * sa
                    + pltpu.roll(a, ROPE_F, 1) * sb)
    return jnp.concatenate(outs, axis=1)


def _project_groups(hx, w_refs, convw_ref, out_refs, table_refs=None, prev_row=None,
                    edge_refs=None):
    wh_ref, wb_ref, wc_ref, wzc_ref, wq_ref, wk_ref, wv_ref, wzr_ref = w_refs
    yb_ref, zc_ref, q_ref, k_ref, v_ref, zr_ref = out_refs
    tm = hx.shape[0]

    def seg(w_ref):
        return jnp.dot(hx, w_ref[...], preferred_element_type=F32)

    ch = seg(wc_ref) * seg(wh_ref)
    ridx = lax.broadcasted_iota(jnp.int32, (tm, 1), 0)
    above = pltpu.roll(ch, 1, 0)
    above = jnp.where(ridx == 0, 0.0 if prev_row is None else prev_row, above)
    below = jnp.where(ridx == tm - 1, 0.0, pltpu.roll(ch, tm - 1, 0))
    cw = convw_ref[...]
    b = seg(wb_ref)
    yb_ref[...] = (b * (above * cw[0:1, :] + ch * cw[1:2, :] + below * cw[2:3, :])).astype(BF16)
    if edge_refs is not None:
        edge_refs[0][0] = b[tm - SUBLANES:tm, :]
        edge_refs[1][0] = ch[0:SUBLANES, :]
    zc_ref[...] = _silu(seg(wzc_ref)).astype(BF16)
    if table_refs is not None:
        cos, sa, sb = (t[...] for t in table_refs)
        q_ref[...] = _rope_pair(seg(wq_ref), cos, sa, sb).astype(BF16)
        k_ref[...] = _rope_pair(seg(wk_ref), cos * K_SCALE, sa * K_SCALE, sb * K_SCALE).astype(BF16)
    else:
        q_ref[...] = seg(wq_ref).astype(BF16)
        k_ref[...] = (seg(wk_ref) * K_SCALE).astype(BF16)
    v_ref[...] = seg(wv_ref).astype(BF16)
    zr_ref[...] = _silu(seg(wzr_ref)).astype(BF16)
    return ch[tm - SUBLANES:tm, :]


def _inproj_kernel(x_ref, nw_ref, shift_ref, scale_ref, *refs, round_weights):
    w_refs, convw_ref = refs[:8], refs[8]
    out_refs, refs = refs[9:15], refs[15:]
    if round_weights:
        for w_ref, wb_ref in zip(w_refs, refs[:8]):
            wb_ref[...] = w_ref[...].astype(BF16)
        w_refs, refs = refs[:8], refs[8:]
    hx_ref, = refs

    @pl.when(pl.program_id(1) == 0)
    def _():
        _prologue(x_ref, nw_ref, shift_ref, scale_ref, hx_ref)

    _project_groups(hx_ref[...], w_refs, convw_ref, out_refs)


def _inproj_latent_kernel(xq_ref, nw_ref, shift_ref, scale_ref, *refs, tm, n_cast, mod_next):
    w_refs, convw_ref, table_refs = refs[:8], refs[8], refs[9:12]
    cast_in_refs, refs = refs[12:12 + n_cast], refs[12 + n_cast:]
    if mod_next:
        (cb_ref, wm_ref, bm_ref), refs = refs[:3], refs[3:]
    out_refs, edge_refs, refs = refs[:6], refs[6:8], refs[8:]
    cast_out_refs, refs = refs[:n_cast], refs[n_cast:]
    if mod_next:
        modn_ref, refs = refs[0], refs[1:]
        hxa_ref, hxb_ref, carry_ref, macc_ref = refs
    else:
        hxa_ref, hxb_ref, carry_ref = refs
    i = pl.program_id(0)
    j = pl.program_id(1)
    quarter = tm // N_SUB
    row0 = pl.multiple_of(j * quarter, quarter)

    def modulation_rows():
        live = (i * N_SUB + j < W_CAST_BLOCKS).astype(F32)
        rows_k = wm_ref.shape[0]
        for r in range(2):
            s = _silu(cb_ref[r]) * live
            for g in range(wm_ref.shape[1] // HEAD_DIM):
                lanes = slice(g * HEAD_DIM, (g + 1) * HEAD_DIM)
                prod = wm_ref[:, lanes] * s
                part = prod[0:SUBLANES]
                for k0 in range(SUBLANES, rows_k, SUBLANES):
                    part = part + prod[k0:k0 + SUBLANES]
                macc_ref[r, :, lanes] += part
        sel = lax.broadcasted_iota(jnp.int32, (SUBLANES, 1), 0)
        row = lambda r: jnp.sum(macc_ref[r], axis=0, keepdims=True) + bm_ref[...]
        modn_ref[...] = jnp.where(sel == 0, row(0), jnp.where(sel == 1, row(1), 0.0))

    if mod_next:
        @pl.when(jnp.logical_and(i == 0, j == 0))
        def _():
            macc_ref[...] = jnp.zeros_like(macc_ref)

    def project(hx_ref):
        prev_row = carry_ref[j][SUBLANES - 1:SUBLANES, :]
        carry_ref[j] = _project_groups(hx_ref[...], w_refs, convw_ref, out_refs, table_refs,
                                       prev_row, edge_refs)

    def normalise(hx_ref):
        for src_ref, dst_ref in zip(cast_in_refs, cast_out_refs):
            dst_ref[...] = src_ref[...].astype(BF16)
        if mod_next:
            modulation_rows()
        gain = nw_ref[...] * (1.0 + scale_ref[...])
        shift = shift_ref[...]
        for r in range(0, quarter, PROLOGUE_ROWS):
            x = xq_ref[r:r + PROLOGUE_ROWS, :]
            ms = jnp.mean(x * x, axis=-1, keepdims=True)
            hx_ref[pl.ds(row0 + r, PROLOGUE_ROWS), :] = (
                x * lax.rsqrt(ms + EPS) * gain + shift).astype(BF16)

    @pl.when(i == 0)
    def _():
        normalise(hxa_ref)
        carry_ref[j] = jnp.zeros(carry_ref.shape[1:], F32)

    @pl.when(i % 2 == 1)
    def _():
        normalise(hxb_ref)
        project(hxa_ref)

    @pl.when(jnp.logical_and(i > 0, i % 2 == 0))
    def _():
        normalise(hxa_ref)
        project(hxb_ref)


def _w_spec(first_block, idle_first_row=False):
    if idle_first_row:
        return pl.BlockSpec((D_MODEL, SUB),
                            lambda i, j: (0, first_block + jnp.where(i == 0, 0, j)))
    return pl.BlockSpec((D_MODEL, SUB), lambda i, j: (0, first_block + j))


def _split_groups(w_bf16):
    return [(w_bf16, g * N_SUB) for g in range(N_SEG)]


def _convw_spec(layer):
    return pl.BlockSpec((None, 3, SUB), lambda i, j: (layer, 0, j))


def _inproj(x, norm_w, shift, scale, w_groups, conv_w, layer, *, w_f32=None):
    tm, d = x.shape
    round_weights = w_f32 is not None
    kern = functools.partial(_inproj_kernel, round_weights=round_weights)
    vec = lambda: pl.BlockSpec((1, d), lambda i, j: (0, 0))
    out = lambda: pl.BlockSpec((tm, SUB), lambda i, j: (i, j))
    seg_shape = jax.ShapeDtypeStruct((tm, SEG), BF16)
    out_specs = [out() for _ in range(6)]
    out_shape = [seg_shape] * 6
    if round_weights:
        w_specs = [pl.BlockSpec((None, d, SUB), lambda i, j, g=g: (layer, 0, g * N_SUB + j))
                   for g in range(N_SEG)]
        w_operands = [w_f32] * N_SEG
        out_specs += [pl.BlockSpec((d, SUB), lambda i, j: (0, j)) for _ in range(N_SEG)]
        out_shape += [jax.ShapeDtypeStruct((d, SEG), BF16)] * N_SEG
    else:
        w_specs = [_w_spec(first) for _, first in w_groups]
        w_operands = [w for w, _ in w_groups]
    res = pl.pallas_call(
        kern,
        grid=(1, N_SUB),
        in_specs=[
            pl.BlockSpec((tm, d), lambda i, j: (i, 0)),
            vec(), vec(), vec(),
            *w_specs,
            _convw_spec(layer),
        ],
        out_specs=out_specs,
        out_shape=out_shape,
        scratch_shapes=[pltpu.VMEM((tm, d), BF16)],
        compiler_params=_params(VMEM_LARGE_MB, 2),
    )(x, norm_w, shift, scale, *w_operands, conv_w)
    return res[:6], ([(w, 0) for w in res[6:]] if round_weights else w_groups)


def _inproj_latent(x, norm_w, shift, scale, w_groups, conv_w, layer, tables, to_round,
                   next_mod, *, tm):
    rows, d = x.shape
    nt = rows // tm
    cos, sa, sb = tables
    kern = functools.partial(_inproj_latent_kernel, tm=tm, n_cast=len(to_round),
                             mod_next=next_mod is not None)
    done = lambda i: jnp.maximum(i - 1, 0)
    vec = lambda: pl.BlockSpec((1, d), lambda i, j: (0, 0))
    tab = lambda: pl.BlockSpec((tm, HEAD_DIM), lambda i, j: (done(i), 0))
    out = lambda: pl.BlockSpec((tm, SUB), lambda i, j: (done(i), jnp.where(i == 0, 0, j)))
    seg_shape = jax.ShapeDtypeStruct((rows, SEG), BF16)
    in_specs = [
        pl.BlockSpec((tm // N_SUB, d), lambda i, j: (jnp.minimum(i, nt - 1) * N_SUB + j, 0)),
        vec(), vec(), vec(),
        *[_w_spec(first, idle_first_row=True) for _, first in w_groups],
        _convw_spec(layer),
        tab(), tab(), tab(),
    ]
    operands = [x, norm_w, shift, scale, *[w for w, _ in w_groups], conv_w, cos, sa, sb]
    edge = lambda: pl.BlockSpec((1, SUBLANES, SUB),
                                lambda i, j: (done(i), 0, jnp.where(i == 0, 0, j)))
    out_specs = [out() for _ in range(6)] + [edge(), edge()]
    out_shape = [seg_shape] * 6 + [jax.ShapeDtypeStruct((nt, SUBLANES, SEG), F32)] * 2
    n_blocks = W_CAST_BLOCKS
    assert n_blocks <= (nt + 1) * N_SUB
    blk = lambda i, j: jnp.minimum(i * N_SUB + j, n_blocks - 1)
    for w_all, which in to_round:
        n_rows, n_cols = w_all.shape[1:]
        assert n_rows % n_blocks == 0
        in_specs.append(pl.BlockSpec((None, n_rows // n_blocks, n_cols),
                                     lambda i, j, which=which: (which, blk(i, j), 0)))
        operands.append(w_all)
        out_specs.append(pl.BlockSpec((n_rows // n_blocks, n_cols), lambda i, j: (blk(i, j), 0)))
        out_shape.append(jax.ShapeDtypeStruct((n_rows, n_cols), BF16))
    scratch = [pltpu.VMEM((tm, d), BF16), pltpu.VMEM((tm, d), BF16),
               pltpu.VMEM((N_SUB, SUBLANES, SUB), F32)]
    if next_mod is not None:
        c_lanes, w_mod, b_mod = next_mod
        n_mod = w_mod.shape[2]
        assert d % n_blocks == 0
        in_specs += [
            pl.BlockSpec((2, d // n_blocks, HEAD_DIM), lambda i, j: (0, blk(i, j), 0)),
            pl.BlockSpec((None, d // n_blocks, n_mod), lambda i, j: (layer + 1, blk(i, j), 0)),
            pl.BlockSpec((None, 1, n_mod), lambda i, j: (layer + 1, 0, 0)),
        ]
        operands += [c_lanes, w_mod, b_mod.reshape(b_mod.shape[0], 1, n_mod)]
        out_specs.append(pl.BlockSpec((SUBLANES, n_mod), lambda i, j: (0, 0)))
        out_shape.append(jax.ShapeDtypeStruct((SUBLANES, n_mod), F32))
        scratch.append(pltpu.VMEM((2, SUBLANES, n_mod), F32))
    res = pl.pallas_call(
        kern,
        grid=(nt + 1, N_SUB),
        in_specs=in_specs,
        out_specs=out_specs,
        out_shape=out_shape,
        scratch_shapes=scratch,
        compiler_params=_params(VMEM_LARGE_MB, 2),
    )(*operands)
    n_round = len(to_round)
    return res[:6], res[6:8], res[8:8 + n_round], (res[8 + n_round] if next_mod else None)


def _inproj_kv_kernel(x_ref, nw_ref, shift_ref, scale_ref, wk_ref, wv_ref, k_ref, v_ref, hx_ref):
    @pl.when(pl.program_id(1) == 0)
    def _():
        _prologue(x_ref, nw_ref, shift_ref, scale_ref, hx_ref)

    hx = hx_ref[...]
    k_ref[...] = (jnp.dot(hx, wk_ref[...], preferred_element_type=F32) * K_SCALE).astype(BF16)
    v_ref[...] = jnp.dot(hx, wv_ref[...], preferred_element_type=F32).astype(BF16)


def _inproj_kv(x, norm_w, shift, scale, w_groups, *, tm):
    rows, d = x.shape
    vec = lambda: pl.BlockSpec((1, d), lambda i, j: (0, 0))
    out = lambda: pl.BlockSpec((tm, SUB), lambda i, j: (i, j))
    seg_shape = jax.ShapeDtypeStruct((rows, SEG), BF16)
    return pl.pallas_call(
        _inproj_kv_kernel,
        grid=(rows // tm, N_SUB),
        in_specs=[
            pl.BlockSpec((tm, d), lambda i, j: (i, 0)),
            vec(), vec(), vec(),
            _w_spec(w_groups[G_K][1]), _w_spec(w_groups[G_V][1]),
        ],
        out_specs=[out(), out()],
        out_shape=[seg_shape] * 2,
        scratch_shapes=[pltpu.VMEM((tm, d), BF16)],
        compiler_params=_params(VMEM_SMALL_MB, 2),
    )(x, norm_w, shift, scale, w_groups[G_K][0], w_groups[G_V][0])


def _states_kernel(lg_ref, k_ref, v_ref, s0f_ref, s0b_ref,
                   sf_ref, sb_ref, ff_ref, fb_ref, kvb_ref, *, nc, nt, lg_rows):
    t = pl.program_id(0)
    row_f, row_b = lg_rows

    @pl.when(t == 0)
    def _():
        ff_ref[...] = s0f_ref[...]
        fb_ref[...] = s0b_ref[...]

    row = lax.broadcasted_iota(jnp.int32, (CHUNK, HEAD_DIM), 0).astype(F32)
    pos = lax.broadcasted_iota(jnp.int32, (HEAD_DIM, CHUNK), 1).astype(F32)

    @pl.when(t < nt)
    def _():
        for h in range(RET_HEADS):
            hs = slice(h * HEAD_DIM, (h + 1) * HEAD_DIM)
            kdf = jnp.exp(lg_ref[row_f, h] * (CHUNK - 1.0 - pos))
            kdb = jnp.exp(lg_ref[row_b, h] * pos)
            cdf = jnp.exp(lg_ref[row_f, h] * CHUNK + 0.0 * row)
            s = ff_ref[h]
            for c in range(nc):
                rs = slice(c * CHUNK, (c + 1) * CHUNK)
                sf_ref[c, h] = s.astype(BF16)
                kt = k_ref[rs, hs].astype(F32).T
                v = v_ref[rs, hs]
                s = cdf * s + jnp.dot((kt * kdf).astype(BF16), v, preferred_element_type=F32)
                kvb_ref[t * nc + c, h] = jnp.dot((kt * kdb).astype(BF16), v,
                                                 preferred_element_type=F32)
            ff_ref[h] = s

    @pl.when(t >= nt)
    def _():
        tile = 2 * nt - 1 - t
        for h in range(RET_HEADS):
            cdb = jnp.exp(lg_ref[row_b, h] * CHUNK + 0.0 * row)
            s = fb_ref[h]
            for c in reversed(range(nc)):
                sb_ref[c, h] = s.astype(BF16)
                s = cdb * s + kvb_ref[tile * nc + c, h]
            fb_ref[h] = s


def _states(lg, lg_rows, k, v, s0f, s0b, *, tr):
    rows = k.shape[0]
    nt = rows // tr
    nc = tr // CHUNK
    n_chunks = rows // CHUNK
    kern = functools.partial(_states_kernel, nc=nc, nt=nt, lg_rows=lg_rows)
    smem = lambda: pl.BlockSpec(memory_space=pltpu.SMEM)
    st = lambda: pl.BlockSpec((RET_HEADS, HEAD_DIM, HEAD_DIM), lambda t: (0, 0, 0))
    rows_in = lambda: pl.BlockSpec((tr, SEG), lambda t: (jnp.minimum(t, nt - 1), 0))
    chunk_states = (nc, RET_HEADS, HEAD_DIM, HEAD_DIM)
    seq_shape = pltpu.HBM((n_chunks, RET_HEADS, HEAD_DIM, HEAD_DIM), BF16)
    fin_shape = jax.ShapeDtypeStruct((RET_HEADS, HEAD_DIM, HEAD_DIM), F32)
    return pl.pallas_call(
        kern,
        grid=(2 * nt,),
        in_specs=[smem(), rows_in(), rows_in(), st(), st()],
        out_specs=[
            pl.BlockSpec(chunk_states, lambda t: (jnp.minimum(t, nt - 1), 0, 0, 0)),
            pl.BlockSpec(chunk_states, lambda t: (nt - 1 - jnp.maximum(t - nt, 0), 0, 0, 0)),
            st(), st(),
        ],
        out_shape=[seq_shape, seq_shape, fin_shape, fin_shape],
        scratch_shapes=[pltpu.VMEM((n_chunks, RET_HEADS, HEAD_DIM, HEAD_DIM), F32)],
        compiler_params=_params(VMEM_LARGE_MB, 1),
    )(lg, k, v, s0f, s0b)


def _lane_means(ts):
    parts = []
    for t in ts:
        hi = t.astype(BF16)
        lo = (t - hi.astype(F32)).astype(BF16)
        parts.append(jnp.concatenate([hi, lo], axis=1))
    rows = ts[0].shape[0]
    ones = jnp.full((2 * HEAD_DIM, HEAD_DIM), 1.0 / HEAD_DIM, BF16)
    m = jnp.dot(jnp.concatenate(parts, axis=0), ones, preferred_element_type=F32)
    return [m[n * rows:(n + 1) * rows] for n in range(len(ts))]


def _out_kernel(lg_ref,
                ycv_ref, zc_ref, q_ref, k_ref, v_ref, zr_ref, eb_ref, ec_ref,
                sf_ref, sb_ref, x_ref, gate_ref, convw_ref, cnw_ref, gnw_ref, wout_ref, fnw_ref,
                o_ref,
                mask_ref, qdf_ref, qdb_ref, ya_ref, yb_ref, *, tm, n_tiles, edge_every, final,
                lg_rows):
    s_id = pl.program_id(0)
    i = jnp.maximum(s_id - 1, 0)
    nc = tm // CHUNK
    at_edge = jnp.logical_and((i + 1) % edge_every == 0, i + 1 < n_tiles)

    @pl.when(s_id == 0)
    def _():
        r = lax.broadcasted_iota(jnp.int32, (CHUNK, CHUNK), 0).astype(F32)
        cc = lax.broadcasted_iota(jnp.int32, (CHUNK, CHUNK), 1).astype(F32)
        d = r - cc
        for h in range(RET_HEADS):
            lgf = lg_ref[lg_rows[0], h]
            lgb = lg_ref[lg_rows[1], h]
            mf = jnp.where(d >= 0, jnp.exp(lgf * jnp.maximum(d, 0.0)), 0.0)
            mb = jnp.where(d <= 0, jnp.exp(lgb * jnp.maximum(-d, 0.0)), 0.0)
            mask_ref[h] = mf + mb
            qdf_ref[h] = jnp.exp(lgf * (r + 1.0))
            qdb_ref[h] = jnp.exp(lgb * (CHUNK - r))

    def step(yr_old, yr_new):
        n_q = PROJ_CHUNKS
        cw_ = D_CONV // n_q
        pw = D_MODEL // n_q

        if yr_old is not None:
            last_row = lax.broadcasted_iota(jnp.int32, (SUBLANES, 1), 0) == SUBLANES - 1
            cwt = convw_ref[0]
            proj_ret, y_l, ssq = [], [], None
            for n in range(n_q):
                proj_ret.append(jnp.dot(yr_old[...], wout_ref[D_CONV:, n * pw:(n + 1) * pw],
                                        preferred_element_type=F32))
                cs = slice(n * cw_, (n + 1) * cw_)
                y = ycv_ref[:, cs].astype(F32)
                below = eb_ref[0, SUBLANES - 1:SUBLANES, cs] * cwt[2:3, cs] * ec_ref[0, 0:1, cs]
                below = jnp.where(jnp.logical_and(at_edge, last_row), below, 0.0)
                y = jnp.concatenate([y[:tm - SUBLANES], y[tm - SUBLANES:] + below], axis=0)
                y_l.append(y)
                part = jnp.sum(y * y, axis=-1, keepdims=True)
                ssq = part if ssq is None else ssq + part
            rinv = lax.rsqrt(ssq * (1.0 / D_CONV) + EPS)

            y_conv = jnp.concatenate(
                [(zc_ref[:, n * cw_:(n + 1) * cw_].astype(F32)
                  * (y_l[n] * rinv * cnw_ref[:, n * cw_:(n + 1) * cw_])).astype(BF16)
                 for n in range(n_q)], axis=1)

        pending = list(range(n_q)) if yr_old is not None else []
        out_ssq = []

        def finish_chunk():
            if not pending:
                return
            n = pending.pop(0)
            cols = slice(n * pw, (n + 1) * pw)
            pc = jnp.dot(y_conv, wout_ref[:D_CONV, cols], preferred_element_type=F32)
            xn = x_ref[:, cols] + gate_ref[:, cols] * (pc + proj_ret[n])
            o_ref[:, cols] = xn
            if final:
                out_ssq.append(jnp.sum(xn * xn, axis=-1, keepdims=True))

        nt = (((1,), (1,)), ((), ()))
        rs = lambda c: slice(c * CHUNK, (c + 1) * CHUNK)
        hs = lambda h: slice(h * HEAD_DIM, (h + 1) * HEAD_DIM)
        rounds = range(0, nc, RET_ROUND_CHUNKS) if yr_new is not None else []
        per_round = -(-n_q // max(len(rounds), 1) // 2)
        for c0 in rounds:
            pairs = [(c, h) for c in range(c0, c0 + RET_ROUND_CHUNKS) for h in range(RET_HEADS)]
            q_l = [q_ref[rs(c), hs(h)] for c, h in pairs]
            s_l = [lax.dot_general(q, k_ref[rs(c), hs(h)], nt, preferred_element_type=F32)
                   for q, (c, h) in zip(q_l, pairs)]
            for _ in range(per_round):
                finish_chunk()
            lhs_l = []
            for q, s, (c, h) in zip(q_l, s_l, pairs):
                qf32 = q.astype(F32)
                lhs_l.append(jnp.concatenate(
                    [(s * mask_ref[h]).astype(BF16), (qf32 * qdf_ref[h]).astype(BF16),
                     (qf32 * qdb_ref[h]).astype(BF16)], axis=1))
            o_l = [jnp.dot(lhs, jnp.concatenate([v_ref[rs(c), hs(h)], sf_ref[c, h],
                                                 sb_ref[c, h]], axis=0),
                           preferred_element_type=F32)
                   for lhs, (c, h) in zip(lhs_l, pairs)]
            for _ in range(per_round):
                finish_chunk()
            per_chunk = lambda xs: [xs[n * RET_HEADS:(n + 1) * RET_HEADS]
                                    for n in range(len(xs) // RET_HEADS)]
            mean_l = [m for grp in per_chunk(o_l) for m in _lane_means(grp)]
            oc_l = [o - m for o, m in zip(o_l, mean_l)]
            var_l = [m for grp in per_chunk([oc * oc for oc in oc_l]) for m in _lane_means(grp)]
            for oc, var, (c, h) in zip(oc_l, var_l, pairs):
                on = oc * lax.rsqrt(var + EPS) * gnw_ref[:, hs(h)]
                yr_new[rs(c), hs(h)] = (zr_ref[rs(c), hs(h)].astype(F32) * on).astype(BF16)
        while pending:
            finish_chunk()
        if final and yr_old is not None:
            ms = functools.reduce(lambda a, b: a + b, out_ssq) * (1.0 / D_MODEL)
            o_ref[...] = o_ref[...] * lax.rsqrt(ms + EPS) * fnw_ref[...]

    middle = jnp.logical_and(s_id > 0, s_id < n_tiles)

    @pl.when(s_id == 0)
    def _():
        step(None, ya_ref)

    @pl.when(jnp.logical_and(middle, s_id % 2 == 0))
    def _():
        step(yb_ref, ya_ref)

    @pl.when(jnp.logical_and(middle, s_id % 2 == 1))
    def _():
        step(ya_ref, yb_ref)

    @pl.when(s_id == n_tiles)
    def _():
        step(ya_ref if n_tiles % 2 == 1 else yb_ref, None)


def _out(lg, lg_rows, segs, edges, sf, sb, x, gate, conv_w, cnw, gnw, w_out, fnw, layer, *,
         tm, final):
    ycv, zc, q, k, v, zr = segs
    eb, ec = edges
    rows, d = x.shape
    nt = rows // tm
    nc = tm // CHUNK
    n_edge = eb.shape[0]
    assert rows % tm == 0 and nt % n_edge == 0 and nc % RET_ROUND_CHUNKS == 0
    edge_every = nt // n_edge
    kern = functools.partial(_out_kernel, tm=tm, n_tiles=nt, edge_every=edge_every, final=final,
                             lg_rows=lg_rows)
    smem = lambda: pl.BlockSpec(memory_space=pltpu.SMEM)
    cur = lambda s: jnp.minimum(s, nt - 1)
    prv = lambda s: jnp.maximum(s - 1, 0)
    seg = lambda: pl.BlockSpec((tm, SEG), lambda s: (cur(s), 0))
    fin = lambda: pl.BlockSpec((tm, SEG), lambda s: (prv(s), 0))
    st = lambda: pl.BlockSpec((nc, RET_HEADS, HEAD_DIM, HEAD_DIM), lambda s: (cur(s), 0, 0, 0))
    full = lambda shape: pl.BlockSpec(shape, lambda s: (0,) * len(shape))
    return pl.pallas_call(
        kern,
        grid=(nt + 1,),
        in_specs=[
            smem(),
            fin(), fin(), seg(), seg(), seg(), seg(),
            pl.BlockSpec((1, SUBLANES, SEG), lambda s: (prv(s) // edge_every, 0, 0)),
            pl.BlockSpec((1, SUBLANES, SEG),
                         lambda s: (jnp.minimum(prv(s) // edge_every + 1, n_edge - 1), 0, 0)),
            st(), st(),
            pl.BlockSpec((tm, d), lambda s: (prv(s), 0)),
            full((1, d)),
            pl.BlockSpec((1, 3, D_CONV), lambda s: (layer, 0, 0)),
            full((1, D_CONV)), full((1, D_RET)),
            pl.BlockSpec((d, d), lambda s: (0, 0), pipeline_mode=pl.Buffered(1)),
            full((1, d)),
        ],
        out_specs=pl.BlockSpec((tm, d), lambda s: (prv(s), 0)),
        out_shape=jax.ShapeDtypeStruct((rows, d), F32),
        scratch_shapes=[
            pltpu.VMEM((RET_HEADS, CHUNK, CHUNK), F32),
            pltpu.VMEM((RET_HEADS, CHUNK, HEAD_DIM), F32),
            pltpu.VMEM((RET_HEADS, CHUNK, HEAD_DIM), F32),
            pltpu.VMEM((tm, D_RET), BF16),
            pltpu.VMEM((tm, D_RET), BF16),
        ],
        compiler_params=_params(VMEM_LARGE_MB, 1),
    )(lg, ycv, zc, q, k, v, zr, eb, ec, sf, sb, x, gate, conv_w, cnw, gnw,
      w_out, fnw)


def _rope_tables(seq):
    t = np.arange(seq)
    row = (t // GRID_W).astype(np.float64)
    col = (t % GRID_W).astype(np.float64)
    inv = ROPE_BASE ** (-np.arange(ROPE_F, dtype=np.float64) / ROPE_F)
    ar = row[:, None] * inv[None, :]
    ac = col[:, None] * inv[None, :]
    z = np.zeros_like(ar)
    cos = np.concatenate([np.cos(ar), np.cos(ar), np.cos(ac), np.cos(ac)], axis=-1)
    sa = np.concatenate([-np.sin(ar), z, -np.sin(ac), z], axis=-1)
    sb = np.concatenate([z, np.sin(ar), z, np.sin(ac)], axis=-1)
    return tuple(jnp.asarray(a, dtype=F32) for a in (cos, sa, sb))


def kernel(x, c, ctx, c_ctx, norm_w, w_mod, b_mod, w_in, conv_w, conv_norm_w, ret_norm_w,
           ret_decay_f, ret_decay_b, w_out, final_norm_w):
    batch, seq, d = x.shape
    assert batch == 1 and d == D_MODEL and seq % INPROJ_ROWS == 0 and seq % GRID_W == 0
    depth = norm_w.shape[0]
    ctx_len = ctx.shape[1]
    xs = x[0]
    cs = ctx[0]

    cv = jnp.concatenate([c[0:1], c_ctx[None, :], jnp.zeros((SUBLANES - 2, d), F32)], axis=0)
    lg = -jnp.exp(jnp.concatenate([ret_decay_f, ret_decay_b], axis=0).astype(F32))
    m = _modulation(cv, w_mod, b_mod)
    c_lanes = jnp.broadcast_to(cv[0:2, :, None], (2, d, HEAD_DIM))
    tables = _rope_tables(seq)
    zero_state = jnp.zeros((RET_HEADS, HEAD_DIM, HEAD_DIM), F32)
    zero_edge = jnp.zeros((1, SUBLANES, D_CONV), F32)
    fnw = final_norm_w.reshape(1, d)
    w_groups = None

    for layer in range(depth):
        update_ctx = layer < depth - 1
        lg_rows = (layer, depth + layer)
        nw = norm_w[layer].reshape(1, d)
        shift, scale, gate = m[0:1, 0:d], m[0:1, d:2 * d], m[0:1, 2 * d:3 * d]
        shift_c, scale_c, gate_c = m[1:2, 0:d], m[1:2, d:2 * d], m[1:2, 2 * d:3 * d]
        cnw = conv_norm_w[layer].reshape(1, D_CONV)
        gnw = ret_norm_w[layer].reshape(1, D_RET)

        if update_ctx:
            segs_c, w_groups = _inproj(cs, nw, shift_c, scale_c, w_groups, conv_w, layer,
                                       w_f32=w_in if w_groups is None else None)
            k_c, v_c = segs_c[3], segs_c[4]
        else:
            if w_groups is None:
                w_groups = _split_groups(w_in[layer].astype(BF16))
            k_c, v_c = _inproj_kv(cs, nw, shift_c, scale_c, w_groups, tm=ctx_len)
        sf_c, sb_c, s_f, s_b = _states(lg, lg_rows, k_c, v_c, zero_state, zero_state, tr=ctx_len)

        more = layer + 1 < depth
        to_round = [(w_out, layer)] + ([(w_in, layer + 1)] if more else [])
        segs, edges, rounded, m_next = _inproj_latent(
            xs, nw, shift, scale, w_groups, conv_w, layer, tables, to_round,
            (c_lanes, w_mod, b_mod) if more else None, tm=INPROJ_ROWS)
        w_out_b = rounded[0]
        w_groups = _split_groups(rounded[1]) if more else None
        sf, sb, _, _ = _states(lg, lg_rows, segs[3], segs[4], s_f, s_b, tr=STATES_ROWS)
        xs = _out(lg, lg_rows, segs, edges, sf, sb, xs, gate, conv_w, cnw, gnw, w_out_b, fnw, layer,
                  tm=OUT_ROWS, final=not update_ctx)
        if update_ctx:
            cs = _out(lg, lg_rows, segs_c, (zero_edge, zero_edge), sf_c, sb_c, cs, gate_c, conv_w,
                      cnw, gnw, w_out_b, fnw, layer, tm=ctx_len, final=False)
        m = m_next
    return xs[None]
```

```python
import functools

import numpy as np
import jax
import jax.numpy as jnp
from jax import lax
from jax.experimental import pallas as pl
from jax.experimental.pallas import tpu as pltpu

D_MODEL = 2048
D_CONV = 1024
D_RET = 1024
RET_HEADS = 8
HEAD_DIM = 128
CHUNK = 128
SEG = 1024
N_SEG = 8
SUB = 256
N_SUB = SEG // SUB
GRID_W = 64
ROPE_BASE = 10000.0
ROPE_F = 32
EPS = 1e-6
K_SCALE = HEAD_DIM ** -0.5

F32 = jnp.float32
BF16 = jnp.bfloat16

G_H, G_B, G_C, G_ZC, G_Q, G_K, G_V, G_ZR = range(8)


def _silu(x):
    return x / (1.0 + jnp.exp(-x))


INPROJ_ROWS = 1024
OUT_ROWS = 512
STATES_ROWS = 1024
MOD_COLS = 1024
PROLOGUE_ROWS = 64
RET_ROUND_CHUNKS = 2
PROJ_CHUNKS = 4
W_CAST_BLOCKS = 32
VMEM_LARGE_MB = 56
VMEM_MOD_MB = 40
VMEM_SMALL_MB = 32
SUBLANES = 8


def _params(vmem_mb, n_axes):
    return pltpu.CompilerParams(
        dimension_semantics=("arbitrary",) * n_axes,
        vmem_limit_bytes=vmem_mb * 1024 * 1024,
    )


def _mod_kernel(cv_ref, w_ref, b_ref, o_ref):
    s = _silu(cv_ref[...])
    o_ref[...] = jnp.dot(s.astype(BF16), w_ref[...].astype(BF16),
                         preferred_element_type=F32) + b_ref[...]


def _modulation(cv, w_mod, b_mod, n):
    depth, d, n_all = w_mod.shape
    tn = MOD_COLS
    return pl.pallas_call(
        _mod_kernel,
        grid=(n // tn,),
        in_specs=[
            pl.BlockSpec((SUBLANES, d), lambda j: (0, 0)),
            pl.BlockSpec((None, d, tn), lambda j: (0, 0, j)),
            pl.BlockSpec((None, 1, tn), lambda j: (0, 0, j)),
        ],
        out_specs=pl.BlockSpec((SUBLANES, tn), lambda j: (0, j)),
        out_shape=jax.ShapeDtypeStruct((SUBLANES, n), F32),
        compiler_params=_params(VMEM_MOD_MB, 1),
    )(cv, w_mod, b_mod.reshape(depth, 1, n_all))


def _prologue(x_ref, nw_ref, shift_ref, scale_ref, hx_ref):
    gain = nw_ref[...] * (1.0 + scale_ref[...])
    shift = shift_ref[...]

    def body(r, carry):
        rows = pl.ds(pl.multiple_of(r * PROLOGUE_ROWS, PROLOGUE_ROWS), PROLOGUE_ROWS)
        x = x_ref[rows, :]
        ms = jnp.mean(x * x, axis=-1, keepdims=True)
        hx_ref[rows, :] = (x * lax.rsqrt(ms + EPS) * gain + shift).astype(BF16)
        return carry

    lax.fori_loop(0, x_ref.shape[0] // PROLOGUE_ROWS, body, 0)


def _rope_pair(acc, cos, sa, sb):
    outs = []
    for h in range(SUB // HEAD_DIM):
        a = acc[:, h * HEAD_DIM:(h + 1) * HEAD_DIM]
        outs.append(a * cos + pltpu.roll(a, HEAD_DIM - ROPE_F, 1) * sa
                    + pltpu.roll(a, ROPE_F, 1) * sb)
    return jnp.concatenate(outs, axis=1)


def _project_groups(hx, w_refs, convw_ref, out_refs, table_refs=None, prev_row=None,
                    edge_refs=None):
    wh_ref, wb_ref, wc_ref, wzc_ref, wq_ref, wk_ref, wv_ref, wzr_ref = w_refs
    yb_ref, zc_ref, q_ref, k_ref, v_ref, zr_ref = out_refs
    tm = hx.shape[0]

    def seg(w_ref):
        return jnp.dot(hx, w_ref[...], preferred_element_type=F32)

    ch = seg(wc_ref) * seg(wh_ref)
    ridx = lax.broadcasted_iota(jnp.int32, (tm, 1), 0)
    above = pltpu.roll(ch, 1, 0)
    above = jnp.where(ridx == 0, 0.0 if prev_row is None else prev_row, above)
    below = jnp.where(ridx == tm - 1, 0.0, pltpu.roll(ch, tm - 1, 0))
    cw = convw_ref[...]
    b = seg(wb_ref)
    yb_ref[...] = (b * (above * cw[0:1, :] + ch * cw[1:2, :] + below * cw[2:3, :])).astype(BF16)
    if edge_refs is not None:
        edge_refs[0][0] = b[tm - SUBLANES:tm, :]
        edge_refs[1][0] = ch[0:SUBLANES, :]
    zc_ref[...] = _silu(seg(wzc_ref)).astype(BF16)
    if table_refs is not None:
        cos, sa, sb = (t[...] for t in table_refs)
        q_ref[...] = _rope_pair(seg(wq_ref), cos, sa, sb).astype(BF16)
        k_ref[...] = _rope_pair(seg(wk_ref), cos * K_SCALE, sa * K_SCALE, sb * K_SCALE).astype(BF16)
    else:
        q_ref[...] = seg(wq_ref).astype(BF16)
        k_ref[...] = (seg(wk_ref) * K_SCALE).astype(BF16)
    v_ref[...] = seg(wv_ref).astype(BF16)
    zr_ref[...] = _silu(seg(wzr_ref)).astype(BF16)
    return ch[tm - SUBLANES:tm, :]


def _inproj_kernel(x_ref, nw_ref, shift_ref, scale_ref, *refs, round_weights):
    w_refs, convw_ref = refs[:8], refs[8]
    out_refs, refs = refs[9:15], refs[15:]
    if round_weights:
        for w_ref, wb_ref in zip(w_refs, refs[:8]):
            wb_ref[...] = w_ref[...].astype(BF16)
        w_refs, refs = refs[:8], refs[8:]
    hx_ref, = refs

    @pl.when(pl.program_id(1) == 0)
    def _():
        _prologue(x_ref, nw_ref, shift_ref, scale_ref, hx_ref)

    _project_groups(hx_ref[...], w_refs, convw_ref, out_refs)


def _inproj_latent_kernel(xq_ref, nw_ref, shift_ref, scale_ref, *refs, tm, n_cast, mod_next):
    w_refs, convw_ref, table_refs = refs[:8], refs[8], refs[9:12]
    cast_in_refs, refs = refs[12:12 + n_cast], refs[12 + n_cast:]
    if mod_next:
        (cb_ref, wm0_ref, wm1_ref, bm0_ref, bm1_ref), refs = refs[:5], refs[5:]
    out_refs, edge_refs, refs = refs[:6], refs[6:8], refs[8:]
    cast_out_refs, refs = refs[:n_cast], refs[n_cast:]
    if mod_next:
        modn_ref, refs = refs[0], refs[1:]
        hxa_ref, hxb_ref, carry_ref, macc_ref = refs
    else:
        hxa_ref, hxb_ref, carry_ref = refs
    i = pl.program_id(0)
    j = pl.program_id(1)
    quarter = tm // N_SUB
    row0 = pl.multiple_of(j * quarter, quarter)

    def modulation_rows():
        live = (i * N_SUB + j < W_CAST_BLOCKS).astype(F32)
        rows_k = wm0_ref.shape[0]
        sel = lax.broadcasted_iota(jnp.int32, (SUBLANES, 1), 0)
        col0 = 0
        for wm_ref, bm_ref in ((wm0_ref, bm0_ref), (wm1_ref, bm1_ref)):
            for r in range(2):
                s = _silu(cb_ref[r]) * live
                for g in range(wm_ref.shape[1] // HEAD_DIM):
                    src = slice(g * HEAD_DIM, (g + 1) * HEAD_DIM)
                    dst = slice(col0 + g * HEAD_DIM, col0 + (g + 1) * HEAD_DIM)
                    prod = wm_ref[:, src] * s
                    part = prod[0:SUBLANES]
                    for k0 in range(SUBLANES, rows_k, SUBLANES):
                        part = part + prod[k0:k0 + SUBLANES]
                    macc_ref[r, :, dst] += part
            cols = slice(col0, col0 + wm_ref.shape[1])
            row = lambda r: jnp.sum(macc_ref[r, :, cols], axis=0, keepdims=True) + bm_ref[...]
            modn_ref[:, cols] = jnp.where(sel == 0, row(0), jnp.where(sel == 1, row(1), 0.0))
            col0 += wm_ref.shape[1]

    if mod_next:
        @pl.when(jnp.logical_and(i == 0, j == 0))
        def _():
            macc_ref[...] = jnp.zeros_like(macc_ref)

    def project(hx_ref):
        prev_row = carry_ref[j][SUBLANES - 1:SUBLANES, :]
        carry_ref[j] = _project_groups(hx_ref[...], w_refs, convw_ref, out_refs, table_refs,
                                       prev_row, edge_refs)

    def normalise(hx_ref):
        for src_ref, dst_ref in zip(cast_in_refs, cast_out_refs):
            dst_ref[...] = src_ref[...].astype(BF16)
        if mod_next:
            modulation_rows()
        gain = nw_ref[...] * (1.0 + scale_ref[...])
        shift = shift_ref[...]
        for r in range(0, quarter, PROLOGUE_ROWS):
            x = xq_ref[r:r + PROLOGUE_ROWS, :]
            ms = jnp.mean(x * x, axis=-1, keepdims=True)
            hx_ref[pl.ds(row0 + r, PROLOGUE_ROWS), :] = (
                x * lax.rsqrt(ms + EPS) * gain + shift).astype(BF16)

    @pl.when(i == 0)
    def _():
        normalise(hxa_ref)
        carry_ref[j] = jnp.zeros(carry_ref.shape[1:], F32)

    @pl.when(i % 2 == 1)
    def _():
        normalise(hxb_ref)
        project(hxa_ref)

    @pl.when(jnp.logical_and(i > 0, i % 2 == 0))
    def _():
        normalise(hxa_ref)
        project(hxb_ref)


def _w_spec(first_block, idle_first_row=False):
    if idle_first_row:
        return pl.BlockSpec((D_MODEL, SUB),
                            lambda i, j: (0, first_block + jnp.where(i == 0, 0, j)))
    return pl.BlockSpec((D_MODEL, SUB), lambda i, j: (0, first_block + j))


def _split_groups(w_bf16):
    return [(w_bf16, g * N_SUB) for g in range(N_SEG)]


def _convw_spec(layer):
    return pl.BlockSpec((None, 3, SUB), lambda i, j: (layer, 0, j))


def _inproj(x, norm_w, shift, scale, w_groups, conv_w, layer, *, w_f32=None):
    tm, d = x.shape
    round_weights = w_f32 is not None
    kern = functools.partial(_inproj_kernel, round_weights=round_weights)
    vec = lambda: pl.BlockSpec((1, d), lambda i, j: (0, 0))
    out = lambda: pl.BlockSpec((tm, SUB), lambda i, j: (i, j))
    seg_shape = jax.ShapeDtypeStruct((tm, SEG), BF16)
    out_specs = [out() for _ in range(6)]
    out_shape = [seg_shape] * 6
    if round_weights:
        w_specs = [pl.BlockSpec((None, d, SUB), lambda i, j, g=g: (layer, 0, g * N_SUB + j))
                   for g in range(N_SEG)]
        w_operands = [w_f32] * N_SEG
        out_specs += [pl.BlockSpec((d, SUB), lambda i, j: (0, j)) for _ in range(N_SEG)]
        out_shape += [jax.ShapeDtypeStruct((d, SEG), BF16)] * N_SEG
    else:
        w_specs = [_w_spec(first) for _, first in w_groups]
        w_operands = [w for w, _ in w_groups]
    res = pl.pallas_call(
        kern,
        grid=(1, N_SUB),
        in_specs=[
            pl.BlockSpec((tm, d), lambda i, j: (i, 0)),
            vec(), vec(), vec(),
            *w_specs,
            _convw_spec(layer),
        ],
        out_specs=out_specs,
        out_shape=out_shape,
        scratch_shapes=[pltpu.VMEM((tm, d), BF16)],
        compiler_params=_params(VMEM_LARGE_MB, 2),
    )(x, norm_w, shift, scale, *w_operands, conv_w)
    return res[:6], ([(w, 0) for w in res[6:]] if round_weights else w_groups)


def _inproj_latent(x, norm_w, shift, scale, w_groups, conv_w, layer, tables, to_round,
                   next_mod, *, tm):
    rows, d = x.shape
    nt = rows // tm
    cos, sa, sb = tables
    kern = functools.partial(_inproj_latent_kernel, tm=tm, n_cast=len(to_round),
                             mod_next=next_mod is not None)
    done = lambda i: jnp.maximum(i - 1, 0)
    vec = lambda: pl.BlockSpec((1, d), lambda i, j: (0, 0))
    tab = lambda: pl.BlockSpec((tm, HEAD_DIM), lambda i, j: (done(i), 0))
    out = lambda: pl.BlockSpec((tm, SUB), lambda i, j: (done(i), jnp.where(i == 0, 0, j)))
    seg_shape = jax.ShapeDtypeStruct((rows, SEG), BF16)
    in_specs = [
        pl.BlockSpec((tm // N_SUB, d), lambda i, j: (jnp.minimum(i, nt - 1) * N_SUB + j, 0)),
        vec(), vec(), vec(),
        *[_w_spec(first, idle_first_row=True) for _, first in w_groups],
        _convw_spec(layer),
        tab(), tab(), tab(),
    ]
    operands = [x, norm_w, shift, scale, *[w for w, _ in w_groups], conv_w, cos, sa, sb]
    edge = lambda: pl.BlockSpec((1, SUBLANES, SUB),
                                lambda i, j: (done(i), 0, jnp.where(i == 0, 0, j)))
    out_specs = [out() for _ in range(6)] + [edge(), edge()]
    out_shape = [seg_shape] * 6 + [jax.ShapeDtypeStruct((nt, SUBLANES, SEG), F32)] * 2
    n_blocks = W_CAST_BLOCKS
    assert n_blocks <= (nt + 1) * N_SUB
    blk = lambda i, j: jnp.minimum(i * N_SUB + j, n_blocks - 1)
    for w_all, which in to_round:
        n_rows, n_cols = w_all.shape[1:]
        assert n_rows % n_blocks == 0
        in_specs.append(pl.BlockSpec((None, n_rows // n_blocks, n_cols),
                                     lambda i, j, which=which: (which, blk(i, j), 0)))
        operands.append(w_all)
        out_specs.append(pl.BlockSpec((n_rows // n_blocks, n_cols), lambda i, j: (blk(i, j), 0)))
        out_shape.append(jax.ShapeDtypeStruct((n_rows, n_cols), BF16))
    scratch = [pltpu.VMEM((tm, d), BF16), pltpu.VMEM((tm, d), BF16),
               pltpu.VMEM((N_SUB, SUBLANES, SUB), F32)]
    if next_mod is not None:
        c_lanes, w_mod, b_mod = next_mod
        n_mod = w_mod.shape[2]
        n_gate = n_mod // 3
        assert d % n_blocks == 0
        b_mod3 = b_mod.reshape(b_mod.shape[0], 1, n_mod)
        in_specs += [
            pl.BlockSpec((2, d // n_blocks, HEAD_DIM), lambda i, j: (0, blk(i, j), 0)),
            pl.BlockSpec((None, d // n_blocks, n_gate), lambda i, j: (layer, blk(i, j), 2)),
            pl.BlockSpec((None, d // n_blocks, n_mod), lambda i, j: (layer + 1, blk(i, j), 0)),
            pl.BlockSpec((None, 1, n_gate), lambda i, j: (layer, 0, 2)),
            pl.BlockSpec((None, 1, n_mod), lambda i, j: (layer + 1, 0, 0)),
        ]
        operands += [c_lanes, w_mod, w_mod, b_mod3, b_mod3]
        out_specs.append(pl.BlockSpec((SUBLANES, n_gate + n_mod), lambda i, j: (0, 0)))
        out_shape.append(jax.ShapeDtypeStruct((SUBLANES, n_gate + n_mod), F32))
        scratch.append(pltpu.VMEM((2, SUBLANES, n_gate + n_mod), F32))
    res = pl.pallas_call(
        kern,
        grid=(nt + 1, N_SUB),
        in_specs=in_specs,
        out_specs=out_specs,
        out_shape=out_shape,
        scratch_shapes=scratch,
        compiler_params=_params(VMEM_LARGE_MB, 2),
    )(*operands)
    n_round = len(to_round)
    return res[:6], res[6:8], res[8:8 + n_round], (res[8 + n_round] if next_mod else None)


def _inproj_kv_kernel(x_ref, nw_ref, shift_ref, scale_ref, wk_ref, wv_ref, k_ref, v_ref, hx_ref):
    @pl.when(pl.program_id(1) == 0)
    def _():
        _prologue(x_ref, nw_ref, shift_ref, scale_ref, hx_ref)

    hx = hx_ref[...]
    k_ref[...] = (jnp.dot(hx, wk_ref[...], preferred_element_type=F32) * K_SCALE).astype(BF16)
    v_ref[...] = jnp.dot(hx, wv_ref[...], preferred_element_type=F32).astype(BF16)


def _inproj_kv(x, norm_w, shift, scale, w_groups, *, tm):
    rows, d = x.shape
    vec = lambda: pl.BlockSpec((1, d), lambda i, j: (0, 0))
    out = lambda: pl.BlockSpec((tm, SUB), lambda i, j: (i, j))
    seg_shape = jax.ShapeDtypeStruct((rows, SEG), BF16)
    return pl.pallas_call(
        _inproj_kv_kernel,
        grid=(rows // tm, N_SUB),
        in_specs=[
            pl.BlockSpec((tm, d), lambda i, j: (i, 0)),
            vec(), vec(), vec(),
            _w_spec(w_groups[G_K][1]), _w_spec(w_groups[G_V][1]),
        ],
        out_specs=[out(), out()],
        out_shape=[seg_shape] * 2,
        scratch_shapes=[pltpu.VMEM((tm, d), BF16)],
        compiler_params=_params(VMEM_SMALL_MB, 2),
    )(x, norm_w, shift, scale, w_groups[G_K][0], w_groups[G_V][0])


def _states_kernel(lg_ref, k_ref, v_ref, s0f_ref, s0b_ref,
                   sf_ref, sb_ref, ff_ref, fb_ref, kvb_ref, *, nc, nt, lg_rows):
    t = pl.program_id(0)
    row_f, row_b = lg_rows

    @pl.when(t == 0)
    def _():
        ff_ref[...] = s0f_ref[...]
        fb_ref[...] = s0b_ref[...]

    row = lax.broadcasted_iota(jnp.int32, (CHUNK, HEAD_DIM), 0).astype(F32)
    pos = lax.broadcasted_iota(jnp.int32, (HEAD_DIM, CHUNK), 1).astype(F32)

    @pl.when(t < nt)
    def _():
        for h in range(RET_HEADS):
            hs = slice(h * HEAD_DIM, (h + 1) * HEAD_DIM)
            kdf = jnp.exp(lg_ref[row_f, h] * (CHUNK - 1.0 - pos))
            kdb = jnp.exp(lg_ref[row_b, h] * pos)
            cdf = jnp.exp(lg_ref[row_f, h] * CHUNK + 0.0 * row)
            s = ff_ref[h]
            for c in range(nc):
                rs = slice(c * CHUNK, (c + 1) * CHUNK)
                sf_ref[c, h] = s.astype(BF16)
                kt = k_ref[rs, hs].astype(F32).T
                v = v_ref[rs, hs]
                s = cdf * s + jnp.dot((kt * kdf).astype(BF16), v, preferred_element_type=F32)
                kvb_ref[t * nc + c, h] = jnp.dot((kt * kdb).astype(BF16), v,
                                                 preferred_element_type=F32)
            ff_ref[h] = s

    @pl.when(t >= nt)
    def _():
        tile = 2 * nt - 1 - t
        for h in range(RET_HEADS):
            cdb = jnp.exp(lg_ref[row_b, h] * CHUNK + 0.0 * row)
            s = fb_ref[h]
            for c in reversed(range(nc)):
                sb_ref[c, h] = s.astype(BF16)
                s = cdb * s + kvb_ref[tile * nc + c, h]
            fb_ref[h] = s


def _states(lg, lg_rows, k, v, s0f, s0b, *, tr):
    rows = k.shape[0]
    nt = rows // tr
    nc = tr // CHUNK
    n_chunks = rows // CHUNK
    kern = functools.partial(_states_kernel, nc=nc, nt=nt, lg_rows=lg_rows)
    smem = lambda: pl.BlockSpec(memory_space=pltpu.SMEM)
    st = lambda: pl.BlockSpec((RET_HEADS, HEAD_DIM, HEAD_DIM), lambda t: (0, 0, 0))
    rows_in = lambda: pl.BlockSpec((tr, SEG), lambda t: (jnp.minimum(t, nt - 1), 0))
    chunk_states = (nc, RET_HEADS, HEAD_DIM, HEAD_DIM)
    seq_shape = pltpu.HBM((n_chunks, RET_HEADS, HEAD_DIM, HEAD_DIM), BF16)
    fin_shape = jax.ShapeDtypeStruct((RET_HEADS, HEAD_DIM, HEAD_DIM), F32)
    return pl.pallas_call(
        kern,
        grid=(2 * nt,),
        in_specs=[smem(), rows_in(), rows_in(), st(), st()],
        out_specs=[
            pl.BlockSpec(chunk_states, lambda t: (jnp.minimum(t, nt - 1), 0, 0, 0)),
            pl.BlockSpec(chunk_states, lambda t: (nt - 1 - jnp.maximum(t - nt, 0), 0, 0, 0)),
            st(), st(),
        ],
        out_shape=[seq_shape, seq_shape, fin_shape, fin_shape],
        scratch_shapes=[pltpu.VMEM((n_chunks, RET_HEADS, HEAD_DIM, HEAD_DIM), F32)],
        compiler_params=_params(VMEM_LARGE_MB, 1),
    )(lg, k, v, s0f, s0b)


def _lane_means(ts):
    parts = []
    for t in ts:
        hi = t.astype(BF16)
        lo = (t - hi.astype(F32)).astype(BF16)
        parts.append(jnp.concatenate([hi, lo], axis=1))
    rows = ts[0].shape[0]
    ones = jnp.full((2 * HEAD_DIM, HEAD_DIM), 1.0 / HEAD_DIM, BF16)
    m = jnp.dot(jnp.concatenate(parts, axis=0), ones, preferred_element_type=F32)
    return [m[n * rows:(n + 1) * rows] for n in range(len(ts))]


def _out_kernel(lg_ref,
                ycv_ref, zc_ref, q_ref, k_ref, v_ref, zr_ref, eb_ref, ec_ref,
                sf_ref, sb_ref, x_ref, gate_ref, convw_ref, cnw_ref, gnw_ref, wout_ref, fnw_ref,
                o_ref,
                mask_ref, qdf_ref, qdb_ref, ya_ref, yb_ref, *, tm, n_tiles, edge_every, final,
                lg_rows):
    s_id = pl.program_id(0)
    i = jnp.maximum(s_id - 1, 0)
    nc = tm // CHUNK
    at_edge = jnp.logical_and((i + 1) % edge_every == 0, i + 1 < n_tiles)

    @pl.when(s_id == 0)
    def _():
        r = lax.broadcasted_iota(jnp.int32, (CHUNK, CHUNK), 0).astype(F32)
        cc = lax.broadcasted_iota(jnp.int32, (CHUNK, CHUNK), 1).astype(F32)
        d = r - cc
        for h in range(RET_HEADS):
            lgf = lg_ref[lg_rows[0], h]
            lgb = lg_ref[lg_rows[1], h]
            mf = jnp.where(d >= 0, jnp.exp(lgf * jnp.maximum(d, 0.0)), 0.0)
            mb = jnp.where(d <= 0, jnp.exp(lgb * jnp.maximum(-d, 0.0)), 0.0)
            mask_ref[h] = mf + mb
            qdf_ref[h] = jnp.exp(lgf * (r + 1.0))
            qdb_ref[h] = jnp.exp(lgb * (CHUNK - r))

    def step(yr_old, yr_new):
        n_q = PROJ_CHUNKS
        cw_ = D_CONV // n_q
        pw = D_MODEL // n_q

        if yr_old is not None:
            last_row = lax.broadcasted_iota(jnp.int32, (SUBLANES, 1), 0) == SUBLANES - 1
            cwt = convw_ref[0]
            proj_ret, y_l, ssq = [], [], None
            for n in range(n_q):
                proj_ret.append(jnp.dot(yr_old[...], wout_ref[D_CONV:, n * pw:(n + 1) * pw],
                                        preferred_element_type=F32))
                cs = slice(n * cw_, (n + 1) * cw_)
                y = ycv_ref[:, cs].astype(F32)
                below = eb_ref[0, SUBLANES - 1:SUBLANES, cs] * cwt[2:3, cs] * ec_ref[0, 0:1, cs]
                below = jnp.where(jnp.logical_and(at_edge, last_row), below, 0.0)
                y = jnp.concatenate([y[:tm - SUBLANES], y[tm - SUBLANES:] + below], axis=0)
                y_l.append(y)
                part = jnp.sum(y * y, axis=-1, keepdims=True)
                ssq = part if ssq is None else ssq + part
            rinv = lax.rsqrt(ssq * (1.0 / D_CONV) + EPS)

            y_conv = jnp.concatenate(
                [(zc_ref[:, n * cw_:(n + 1) * cw_].astype(F32)
                  * (y_l[n] * rinv * cnw_ref[:, n * cw_:(n + 1) * cw_])).astype(BF16)
                 for n in range(n_q)], axis=1)

        pending = list(range(n_q)) if yr_old is not None else []
        out_ssq = []

        def finish_chunk():
            if not pending:
                return
            n = pending.pop(0)
            cols = slice(n * pw, (n + 1) * pw)
            pc = jnp.dot(y_conv, wout_ref[:D_CONV, cols], preferred_element_type=F32)
            xn = x_ref[:, cols] + gate_ref[:, cols] * (pc + proj_ret[n])
            o_ref[:, cols] = xn
            if final:
                out_ssq.append(jnp.sum(xn * xn, axis=-1, keepdims=True))

        nt = (((1,), (1,)), ((), ()))
        rs = lambda c: slice(c * CHUNK, (c + 1) * CHUNK)
        hs = lambda h: slice(h * HEAD_DIM, (h + 1) * HEAD_DIM)
        rounds = range(0, nc, RET_ROUND_CHUNKS) if yr_new is not None else []
        per_round = -(-n_q // max(len(rounds), 1) // 2)
        for c0 in rounds:
            pairs = [(c, h) for c in range(c0, c0 + RET_ROUND_CHUNKS) for h in range(RET_HEADS)]
            q_l = [q_ref[rs(c), hs(h)] for c, h in pairs]
            s_l = [lax.dot_general(q, k_ref[rs(c), hs(h)], nt, preferred_element_type=F32)
                   for q, (c, h) in zip(q_l, pairs)]
            for _ in range(per_round):
                finish_chunk()
            lhs_l = []
            for q, s, (c, h) in zip(q_l, s_l, pairs):
                qf32 = q.astype(F32)
                lhs_l.append(jnp.concatenate(
                    [(s * mask_ref[h]).astype(BF16), (qf32 * qdf_ref[h]).astype(BF16),
                     (qf32 * qdb_ref[h]).astype(BF16)], axis=1))
            o_l = [jnp.dot(lhs, jnp.concatenate([v_ref[rs(c), hs(h)], sf_ref[c, h],
                                                 sb_ref[c, h]], axis=0),
                           preferred_element_type=F32)
                   for lhs, (c, h) in zip(lhs_l, pairs)]
            for _ in range(per_round):
                finish_chunk()
            per_chunk = lambda xs: [xs[n * RET_HEADS:(n + 1) * RET_HEADS]
                                    for n in range(len(xs) // RET_HEADS)]
            mean_l = [m for grp in per_chunk(o_l) for m in _lane_means(grp)]
            oc_l = [o - m for o, m in zip(o_l, mean_l)]
            var_l = [m for grp in per_chunk([oc * oc for oc in oc_l]) for m in _lane_means(grp)]
            for oc, var, (c, h) in zip(oc_l, var_l, pairs):
                on = oc * lax.rsqrt(var + EPS) * gnw_ref[:, hs(h)]
                yr_new[rs(c), hs(h)] = (zr_ref[rs(c), hs(h)].astype(F32) * on).astype(BF16)
        while pending:
            finish_chunk()
        if final and yr_old is not None:
            ms = functools.reduce(lambda a, b: a + b, out_ssq) * (1.0 / D_MODEL)
            o_ref[...] = o_ref[...] * lax.rsqrt(ms + EPS) * fnw_ref[...]

    middle = jnp.logical_and(s_id > 0, s_id < n_tiles)

    @pl.when(s_id == 0)
    def _():
        step(None, ya_ref)

    @pl.when(jnp.logical_and(middle, s_id % 2 == 0))
    def _():
        step(yb_ref, ya_ref)

    @pl.when(jnp.logical_and(middle, s_id % 2 == 1))
    def _():
        step(ya_ref, yb_ref)

    @pl.when(s_id == n_tiles)
    def _():
        step(ya_ref if n_tiles % 2 == 1 else yb_ref, None)


def _out(lg, lg_rows, segs, edges, sf, sb, x, gate, conv_w, cnw, gnw, w_out, fnw, layer, *,
         tm, final):
    ycv, zc, q, k, v, zr = segs
    eb, ec = edges
    rows, d = x.shape
    nt = rows // tm
    nc = tm // CHUNK
    n_edge = eb.shape[0]
    assert rows % tm == 0 and nt % n_edge == 0 and nc % RET_ROUND_CHUNKS == 0
    edge_every = nt // n_edge
    kern = functools.partial(_out_kernel, tm=tm, n_tiles=nt, edge_every=edge_every, final=final,
                             lg_rows=lg_rows)
    smem = lambda: pl.BlockSpec(memory_space=pltpu.SMEM)
    cur = lambda s: jnp.minimum(s, nt - 1)
    prv = lambda s: jnp.maximum(s - 1, 0)
    seg = lambda: pl.BlockSpec((tm, SEG), lambda s: (cur(s), 0))
    fin = lambda: pl.BlockSpec((tm, SEG), lambda s: (prv(s), 0))
    st = lambda: pl.BlockSpec((nc, RET_HEADS, HEAD_DIM, HEAD_DIM), lambda s: (cur(s), 0, 0, 0))
    full = lambda shape: pl.BlockSpec(shape, lambda s: (0,) * len(shape))
    return pl.pallas_call(
        kern,
        grid=(nt + 1,),
        in_specs=[
            smem(),
            fin(), fin(), seg(), seg(), seg(), seg(),
            pl.BlockSpec((1, SUBLANES, SEG), lambda s: (prv(s) // edge_every, 0, 0)),
            pl.BlockSpec((1, SUBLANES, SEG),
                         lambda s: (jnp.minimum(prv(s) // edge_every + 1, n_edge - 1), 0, 0)),
            st(), st(),
            pl.BlockSpec((tm, d), lambda s: (prv(s), 0)),
            full((1, d)),
            pl.BlockSpec((1, 3, D_CONV), lambda s: (layer, 0, 0)),
            full((1, D_CONV)), full((1, D_RET)),
            pl.BlockSpec((d, d), lambda s: (0, 0), pipeline_mode=pl.Buffered(1)),
            full((1, d)),
        ],
        out_specs=pl.BlockSpec((tm, d), lambda s: (prv(s), 0)),
        out_shape=jax.ShapeDtypeStruct((rows, d), F32),
        scratch_shapes=[
            pltpu.VMEM((RET_HEADS, CHUNK, CHUNK), F32),
            pltpu.VMEM((RET_HEADS, CHUNK, HEAD_DIM), F32),
            pltpu.VMEM((RET_HEADS, CHUNK, HEAD_DIM), F32),
            pltpu.VMEM((tm, D_RET), BF16),
            pltpu.VMEM((tm, D_RET), BF16),
        ],
        compiler_params=_params(VMEM_LARGE_MB, 1),
    )(lg, ycv, zc, q, k, v, zr, eb, ec, sf, sb, x, gate, conv_w, cnw, gnw,
      w_out, fnw)


def _rope_tables(seq):
    t = np.arange(seq)
    row = (t // GRID_W).astype(np.float64)
    col = (t % GRID_W).astype(np.float64)
    inv = ROPE_BASE ** (-np.arange(ROPE_F, dtype=np.float64) / ROPE_F)
    ar = row[:, None] * inv[None, :]
    ac = col[:, None] * inv[None, :]
    z = np.zeros_like(ar)
    cos = np.concatenate([np.cos(ar), np.cos(ar), np.cos(ac), np.cos(ac)], axis=-1)
    sa = np.concatenate([-np.sin(ar), z, -np.sin(ac), z], axis=-1)
    sb = np.concatenate([z, np.sin(ar), z, np.sin(ac)], axis=-1)
    return tuple(jnp.asarray(a, dtype=F32) for a in (cos, sa, sb))


def kernel(x, c, ctx, c_ctx, norm_w, w_mod, b_mod, w_in, conv_w, conv_norm_w, ret_norm_w,
           ret_decay_f, ret_decay_b, w_out, final_norm_w):
    batch, seq, d = x.shape
    assert batch == 1 and d == D_MODEL and seq % INPROJ_ROWS == 0 and seq % GRID_W == 0
    depth = norm_w.shape[0]
    ctx_len = ctx.shape[1]
    xs = x[0]
    cs = ctx[0]

    cv = jnp.concatenate([c[0:1], c_ctx[None, :], jnp.zeros((SUBLANES - 2, d), F32)], axis=0)
    lg = -jnp.exp(jnp.concatenate([ret_decay_f, ret_decay_b], axis=0).astype(F32))
    m = _modulation(cv, w_mod, b_mod, 2 * d if depth > 1 else 3 * d)
    c_lanes = jnp.broadcast_to(cv[0:2, :, None], (2, d, HEAD_DIM))
    tables = _rope_tables(seq)
    zero_state = jnp.zeros((RET_HEADS, HEAD_DIM, HEAD_DIM), F32)
    zero_edge = jnp.zeros((1, SUBLANES, D_CONV), F32)
    fnw = final_norm_w.reshape(1, d)
    w_groups = None

    for layer in range(depth):
        update_ctx = layer < depth - 1
        lg_rows = (layer, depth + layer)
        nw = norm_w[layer].reshape(1, d)
        shift, scale = m[0:1, 0:d], m[0:1, d:2 * d]
        shift_c, scale_c = m[1:2, 0:d], m[1:2, d:2 * d]
        cnw = conv_norm_w[layer].reshape(1, D_CONV)
        gnw = ret_norm_w[layer].reshape(1, D_RET)

        if update_ctx:
            segs_c, w_groups = _inproj(cs, nw, shift_c, scale_c, w_groups, conv_w, layer,
                                       w_f32=w_in if w_groups is None else None)
            k_c, v_c = segs_c[3], segs_c[4]
        else:
            if w_groups is None:
                w_groups = _split_groups(w_in[layer].astype(BF16))
            k_c, v_c = _inproj_kv(cs, nw, shift_c, scale_c, w_groups, tm=ctx_len)
        sf_c, sb_c, s_f, s_b = _states(lg, lg_rows, k_c, v_c, zero_state, zero_state, tr=ctx_len)

        more = layer + 1 < depth
        to_round = [(w_out, layer)] + ([(w_in, layer + 1)] if more else [])
        segs, edges, rounded, side_mod = _inproj_latent(
            xs, nw, shift, scale, w_groups, conv_w, layer, tables, to_round,
            (c_lanes, w_mod, b_mod) if more else None, tm=INPROJ_ROWS)
        gates = m[:, 2 * d:3 * d] if m.shape[1] == 3 * d else side_mod[:, 0:d]
        gate, gate_c = gates[0:1], gates[1:2]
        m_next = side_mod[:, d:] if more else None
        w_out_b = rounded[0]
        w_groups = _split_groups(rounded[1]) if more else None
        sf, sb, _, _ = _states(lg, lg_rows, segs[3], segs[4], s_f, s_b, tr=STATES_ROWS)
        xs = _out(lg, lg_rows, segs, edges, sf, sb, xs, gate, conv_w, cnw, gnw, w_out_b, fnw, layer,
                  tm=OUT_ROWS, final=not update_ctx)
        if update_ctx:
            cs = _out(lg, lg_rows, segs_c, (zero_edge, zero_edge), sf_c, sb_c, cs, gate_c, conv_w,
                      cnw, gnw, w_out_b, fnw, layer, tm=ctx_len, final=False)
        m = m_next
    return xs[None]
```

```python
import functools

import numpy as np
import jax
import jax.numpy as jnp
from jax import lax
from jax.experimental import pallas as pl
from jax.experimental.pallas import tpu as pltpu

D_MODEL = 2048
D_CONV = 1024
D_RET = 1024
RET_HEADS = 8
HEAD_DIM = 128
CHUNK = 128
SEG = 1024
N_SEG = 8
SUB = 256
N_SUB = SEG // SUB
GRID_W = 64
ROPE_BASE = 10000.0
ROPE_F = 32
EPS = 1e-6
K_SCALE = HEAD_DIM ** -0.5

F32 = jnp.float32
BF16 = jnp.bfloat16

G_H, G_B, G_C, G_ZC, G_Q, G_K, G_V, G_ZR = range(8)


def _silu(x):
    return x / (1.0 + jnp.exp(-x))


INPROJ_ROWS = 1024
OUT_ROWS = 512
STATES_ROWS = 1024
MOD_COLS = 1024
PROLOGUE_ROWS = 64
RET_ROUND_CHUNKS = 2
PROJ_CHUNKS = 4
W_CAST_BLOCKS = 32
VMEM_LARGE_MB = 56
VMEM_MOD_MB = 40
VMEM_SMALL_MB = 32
SUBLANES = 8


def _params(vmem_mb, n_axes):
    return pltpu.CompilerParams(
        dimension_semantics=("arbitrary",) * n_axes,
        vmem_limit_bytes=vmem_mb * 1024 * 1024,
    )


def _mod_kernel(cv_ref, w_ref, b_ref, o_ref):
    s = _silu(cv_ref[...])
    o_ref[...] = jnp.dot(s.astype(BF16), w_ref[...].astype(BF16),
                         preferred_element_type=F32) + b_ref[...]


def _modulation(cv, w_mod, b_mod, n):
    depth, d, n_all = w_mod.shape
    tn = MOD_COLS
    return pl.pallas_call(
        _mod_kernel,
        grid=(n // tn,),
        in_specs=[
            pl.BlockSpec((SUBLANES, d), lambda j: (0, 0)),
            pl.BlockSpec((None, d, tn), lambda j: (0, 0, j)),
            pl.BlockSpec((None, 1, tn), lambda j: (0, 0, j)),
        ],
        out_specs=pl.BlockSpec((SUBLANES, tn), lambda j: (0, j)),
        out_shape=jax.ShapeDtypeStruct((SUBLANES, n), F32),
        compiler_params=_params(VMEM_MOD_MB, 1),
    )(cv, w_mod, b_mod.reshape(depth, 1, n_all))


def _prologue(x_ref, nw_ref, shift_ref, scale_ref, hx_ref):
    gain = nw_ref[...] * (1.0 + scale_ref[...])
    shift = shift_ref[...]

    def body(r, carry):
        rows = pl.ds(pl.multiple_of(r * PROLOGUE_ROWS, PROLOGUE_ROWS), PROLOGUE_ROWS)
        x = x_ref[rows, :]
        ms = jnp.mean(x * x, axis=-1, keepdims=True)
        hx_ref[rows, :] = (x * lax.rsqrt(ms + EPS) * gain + shift).astype(BF16)
        return carry

    lax.fori_loop(0, x_ref.shape[0] // PROLOGUE_ROWS, body, 0)


def _rope_pair(acc, cos, sa, sb):
    outs = []
    for h in range(SUB // HEAD_DIM):
        a = acc[:, h * HEAD_DIM:(h + 1) * HEAD_DIM]
        outs.append(a * cos + pltpu.roll(a, HEAD_DIM - ROPE_F, 1) * sa
                    + pltpu.roll(a, ROPE_F, 1) * sb)
    return jnp.concatenate(outs, axis=1)


def _project_groups(hx, w_refs, convw_ref, out_refs, table_refs=None, prev_row=None,
                    edge_refs=None):
    wh_ref, wb_ref, wc_ref, wzc_ref, wq_ref, wk_ref, wv_ref, wzr_ref = w_refs
    yb_ref, zc_ref, q_ref, k_ref, v_ref, zr_ref = out_refs
    tm = hx.shape[0]

    def seg(w_ref):
        return jnp.dot(hx, w_ref[...], preferred_element_type=F32)

    ch = seg(wc_ref) * seg(wh_ref)
    ridx = lax.broadcasted_iota(jnp.int32, (tm, 1), 0)
    above = pltpu.roll(ch, 1, 0)
    above = jnp.where(ridx == 0, 0.0 if prev_row is None else prev_row, above)
    below = jnp.where(ridx == tm - 1, 0.0, pltpu.roll(ch, tm - 1, 0))
    cw = convw_ref[...]
    b = seg(wb_ref)
    yb_ref[...] = (b * (above * cw[0:1, :] + ch * cw[1:2, :] + below * cw[2:3, :])).astype(BF16)
    if edge_refs is not None:
        edge_refs[0][0] = b[tm - SUBLANES:tm, :]
        edge_refs[1][0] = ch[0:SUBLANES, :]
    zc_ref[...] = _silu(seg(wzc_ref)).astype(BF16)
    if table_refs is not None:
        cos, sa, sb = (t[...] for t in table_refs)
        q_ref[...] = _rope_pair(seg(wq_ref), cos, sa, sb).astype(BF16)
        k_ref[...] = _rope_pair(seg(wk_ref), cos * K_SCALE, sa * K_SCALE, sb * K_SCALE).astype(BF16)
    else:
        q_ref[...] = seg(wq_ref).astype(BF16)
        k_ref[...] = (seg(wk_ref) * K_SCALE).astype(BF16)
    v_ref[...] = seg(wv_ref).astype(BF16)
    zr_ref[...] = _silu(seg(wzr_ref)).astype(BF16)
    return ch[tm - SUBLANES:tm, :]


def _inproj_kernel(x_ref, nw_ref, shift_ref, scale_ref, *refs, round_weights):
    w_refs, convw_ref = refs[:8], refs[8]
    out_refs, refs = refs[9:15], refs[15:]
    if round_weights:
        for w_ref, wb_ref in zip(w_refs, refs[:8]):
            wb_ref[...] = w_ref[...].astype(BF16)
        w_refs, refs = refs[:8], refs[8:]
    hx_ref, = refs

    @pl.when(pl.program_id(1) == 0)
    def _():
        _prologue(x_ref, nw_ref, shift_ref, scale_ref, hx_ref)

    _project_groups(hx_ref[...], w_refs, convw_ref, out_refs)


def _inproj_latent_kernel(xq_ref, nw_ref, shift_ref, scale_ref, *refs, tm, n_cast, mod_next):
    w_refs, convw_ref, table_refs = refs[:8], refs[8], refs[9:12]
    cast_in_refs, refs = refs[12:12 + n_cast], refs[12 + n_cast:]
    if mod_next:
        (cb_ref, wm0_ref, wm1_ref, bm0_ref, bm1_ref), refs = refs[:5], refs[5:]
    out_refs, edge_refs, refs = refs[:6], refs[6:8], refs[8:]
    cast_out_refs, refs = refs[:n_cast], refs[n_cast:]
    if mod_next:
        modn_ref, refs = refs[0], refs[1:]
        hxa_ref, hxb_ref, carry_ref, macc_ref = refs
    else:
        hxa_ref, hxb_ref, carry_ref = refs
    i = pl.program_id(0)
    j = pl.program_id(1)
    quarter = tm // N_SUB
    row0 = pl.multiple_of(j * quarter, quarter)

    def modulation_rows():
        live = (i * N_SUB + j < W_CAST_BLOCKS).astype(F32)
        rows_k = wm0_ref.shape[0]
        sel = lax.broadcasted_iota(jnp.int32, (SUBLANES, 1), 0)
        col0 = 0
        for wm_ref, bm_ref in ((wm0_ref, bm0_ref), (wm1_ref, bm1_ref)):
            for r in range(2):
                s = _silu(cb_ref[r]) * live
                for g in range(wm_ref.shape[1] // HEAD_DIM):
                    src = slice(g * HEAD_DIM, (g + 1) * HEAD_DIM)
                    dst = slice(col0 + g * HEAD_DIM, col0 + (g + 1) * HEAD_DIM)
                    prod = wm_ref[:, src] * s
                    part = prod[0:SUBLANES]
                    for k0 in range(SUBLANES, rows_k, SUBLANES):
                        part = part + prod[k0:k0 + SUBLANES]
                    macc_ref[r, :, dst] += part
            cols = slice(col0, col0 + wm_ref.shape[1])
            row = lambda r: jnp.sum(macc_ref[r, :, cols], axis=0, keepdims=True) + bm_ref[...]
            modn_ref[:, cols] = jnp.where(sel == 0, row(0), jnp.where(sel == 1, row(1), 0.0))
            col0 += wm_ref.shape[1]

    if mod_next:
        @pl.when(jnp.logical_and(i == 0, j == 0))
        def _():
            macc_ref[...] = jnp.zeros_like(macc_ref)

    def project(hx_ref):
        prev_row = carry_ref[j][SUBLANES - 1:SUBLANES, :]
        carry_ref[j] = _project_groups(hx_ref[...], w_refs, convw_ref, out_refs, table_refs,
                                       prev_row, edge_refs)

    def normalise(hx_ref):
        for src_ref, dst_ref in zip(cast_in_refs, cast_out_refs):
            dst_ref[...] = src_ref[...].astype(BF16)
        if mod_next:
            modulation_rows()
        gain = nw_ref[...] * (1.0 + scale_ref[...])
        shift = shift_ref[...]
        for r in range(0, quarter, PROLOGUE_ROWS):
            x = xq_ref[r:r + PROLOGUE_ROWS, :]
            ms = jnp.mean(x * x, axis=-1, keepdims=True)
            hx_ref[pl.ds(row0 + r, PROLOGUE_ROWS), :] = (
                x * lax.rsqrt(ms + EPS) * gain + shift).astype(BF16)

    @pl.when(i == 0)
    def _():
        normalise(hxa_ref)
        carry_ref[j] = jnp.zeros(carry_ref.shape[1:], F32)

    @pl.when(i % 2 == 1)
    def _():
        normalise(hxb_ref)
        project(hxa_ref)

    @pl.when(jnp.logical_and(i > 0, i % 2 == 0))
    def _():
        normalise(hxa_ref)
        project(hxb_ref)


def _w_spec(first_block, idle_first_row=False):
    if idle_first_row:
        return pl.BlockSpec((D_MODEL, SUB),
                            lambda i, j: (0, first_block + jnp.where(i == 0, 0, j)))
    return pl.BlockSpec((D_MODEL, SUB), lambda i, j: (0, first_block + j))


def _split_groups(w_bf16):
    return [(w_bf16, g * N_SUB) for g in range(N_SEG)]


def _convw_spec(layer):
    return pl.BlockSpec((None, 3, SUB), lambda i, j: (layer, 0, j))


def _inproj(x, norm_w, shift, scale, w_groups, conv_w, layer, *, w_f32=None):
    tm, d = x.shape
    round_weights = w_f32 is not None
    kern = functools.partial(_inproj_kernel, round_weights=round_weights)
    vec = lambda: pl.BlockSpec((1, d), lambda i, j: (0, 0))
    out = lambda: pl.BlockSpec((tm, SUB), lambda i, j: (i, j))
    seg_shape = jax.ShapeDtypeStruct((tm, SEG), BF16)
    out_specs = [out() for _ in range(6)]
    out_shape = [seg_shape] * 6
    if round_weights:
        w_specs = [pl.BlockSpec((None, d, SUB), lambda i, j, g=g: (layer, 0, g * N_SUB + j))
                   for g in range(N_SEG)]
        w_operands = [w_f32] * N_SEG
        out_specs += [pl.BlockSpec((d, SUB), lambda i, j: (0, j)) for _ in range(N_SEG)]
        out_shape += [jax.ShapeDtypeStruct((d, SEG), BF16)] * N_SEG
    else:
        w_specs = [_w_spec(first) for _, first in w_groups]
        w_operands = [w for w, _ in w_groups]
    res = pl.pallas_call(
        kern,
        grid=(1, N_SUB),
        in_specs=[
            pl.BlockSpec((tm, d), lambda i, j: (i, 0)),
            vec(), vec(), vec(),
            *w_specs,
            _convw_spec(layer),
        ],
        out_specs=out_specs,
        out_shape=out_shape,
        scratch_shapes=[pltpu.VMEM((tm, d), BF16)],
        compiler_params=_params(VMEM_LARGE_MB, 2),
    )(x, norm_w, shift, scale, *w_operands, conv_w)
    return res[:6], ([(w, 0) for w in res[6:]] if round_weights else w_groups)


def _inproj_latent(x, norm_w, shift, scale, w_groups, conv_w, layer, tables, to_round,
                   next_mod, *, tm):
    rows, d = x.shape
    nt = rows // tm
    cos, sa, sb = tables
    kern = functools.partial(_inproj_latent_kernel, tm=tm, n_cast=len(to_round),
                             mod_next=next_mod is not None)
    done = lambda i: jnp.maximum(i - 1, 0)
    vec = lambda: pl.BlockSpec((1, d), lambda i, j: (0, 0))
    tab = lambda: pl.BlockSpec((tm, HEAD_DIM), lambda i, j: (done(i), 0))
    out = lambda: pl.BlockSpec((tm, SUB), lambda i, j: (done(i), jnp.where(i == 0, 0, j)))
    seg_shape = jax.ShapeDtypeStruct((rows, SEG), BF16)
    in_specs = [
        pl.BlockSpec((tm // N_SUB, d), lambda i, j: (jnp.minimum(i, nt - 1) * N_SUB + j, 0)),
        vec(), vec(), vec(),
        *[_w_spec(first, idle_first_row=True) for _, first in w_groups],
        _convw_spec(layer),
        tab(), tab(), tab(),
    ]
    operands = [x, norm_w, shift, scale, *[w for w, _ in w_groups], conv_w, cos, sa, sb]
    edge = lambda: pl.BlockSpec((1, SUBLANES, SUB),
                                lambda i, j: (done(i), 0, jnp.where(i == 0, 0, j)))
    out_specs = [out() for _ in range(6)] + [edge(), edge()]
    out_shape = [seg_shape] * 6 + [jax.ShapeDtypeStruct((nt, SUBLANES, SEG), F32)] * 2
    n_blocks = W_CAST_BLOCKS
    assert n_blocks <= (nt + 1) * N_SUB
    blk = lambda i, j: jnp.minimum(i * N_SUB + j, n_blocks - 1)
    for w_all, which in to_round:
        n_rows, n_cols = w_all.shape[1:]
        assert n_rows % n_blocks == 0
        in_specs.append(pl.BlockSpec((None, n_rows // n_blocks, n_cols),
                                     lambda i, j, which=which: (which, blk(i, j), 0)))
        operands.append(w_all)
        out_specs.append(pl.BlockSpec((n_rows // n_blocks, n_cols), lambda i, j: (blk(i, j), 0)))
        out_shape.append(jax.ShapeDtypeStruct((n_rows, n_cols), BF16))
    scratch = [pltpu.VMEM((tm, d), BF16), pltpu.VMEM((tm, d), BF16),
               pltpu.VMEM((N_SUB, SUBLANES, SUB), F32)]
    if next_mod is not None:
        c_lanes, w_mod, b_mod = next_mod
        n_mod = w_mod.shape[2]
        n_gate = n_mod // 3
        assert d % n_blocks == 0
        b_mod3 = b_mod.reshape(b_mod.shape[0], 1, n_mod)
        in_specs += [
            pl.BlockSpec((2, d // n_blocks, HEAD_DIM), lambda i, j: (0, blk(i, j), 0)),
            pl.BlockSpec((None, d // n_blocks, n_gate), lambda i, j: (layer, blk(i, j), 2)),
            pl.BlockSpec((None, d // n_blocks, n_mod), lambda i, j: (layer + 1, blk(i, j), 0)),
            pl.BlockSpec((None, 1, n_gate), lambda i, j: (layer, 0, 2)),
            pl.BlockSpec((None, 1, n_mod), lambda i, j: (layer + 1, 0, 0)),
        ]
        operands += [c_lanes, w_mod, w_mod, b_mod3, b_mod3]
        out_specs.append(pl.BlockSpec((SUBLANES, n_gate + n_mod), lambda i, j: (0, 0)))
        out_shape.append(jax.ShapeDtypeStruct((SUBLANES, n_gate + n_mod), F32))
        scratch.append(pltpu.VMEM((2, SUBLANES, n_gate + n_mod), F32))
    res = pl.pallas_call(
        kern,
        grid=(nt + 1, N_SUB),
        in_specs=in_specs,
        out_specs=out_specs,
        out_shape=out_shape,
        scratch_shapes=scratch,
        compiler_params=_params(VMEM_LARGE_MB, 2),
    )(*operands)
    n_round = len(to_round)
    return res[:6], res[6:8], res[8:8 + n_round], (res[8 + n_round] if next_mod else None)


def _inproj_kv_kernel(x_ref, nw_ref, shift_ref, scale_ref, wk_ref, wv_ref, k_ref, v_ref, hx_ref):
    @pl.when(pl.program_id(1) == 0)
    def _():
        _prologue(x_ref, nw_ref, shift_ref, scale_ref, hx_ref)

    hx = hx_ref[...]
    k_ref[...] = (jnp.dot(hx, wk_ref[...], preferred_element_type=F32) * K_SCALE).astype(BF16)
    v_ref[...] = jnp.dot(hx, wv_ref[...], preferred_element_type=F32).astype(BF16)


def _inproj_kv(x, norm_w, shift, scale, w_groups, *, tm):
    rows, d = x.shape
    vec = lambda: pl.BlockSpec((1, d), lambda i, j: (0, 0))
    out = lambda: pl.BlockSpec((tm, SUB), lambda i, j: (i, j))
    seg_shape = jax.ShapeDtypeStruct((rows, SEG), BF16)
    return pl.pallas_call(
        _inproj_kv_kernel,
        grid=(rows // tm, N_SUB),
        in_specs=[
            pl.BlockSpec((tm, d), lambda i, j: (i, 0)),
            vec(), vec(), vec(),
            _w_spec(w_groups[G_K][1]), _w_spec(w_groups[G_V][1]),
        ],
        out_specs=[out(), out()],
        out_shape=[seg_shape] * 2,
        scratch_shapes=[pltpu.VMEM((tm, d), BF16)],
        compiler_params=_params(VMEM_SMALL_MB, 2),
    )(x, norm_w, shift, scale, w_groups[G_K][0], w_groups[G_V][0])


def _states_kernel(lg_ref, k_ref, v_ref, kc_ref, vc_ref,
                   sf_ref, sb_ref, sfc_ref, sbc_ref, ff_ref, fb_ref, kvb_ref, *,
                   nc, nt, ncc, lg_rows):
    t = pl.program_id(0)
    row_f, row_b = lg_rows
    row = lax.broadcasted_iota(jnp.int32, (CHUNK, HEAD_DIM), 0).astype(F32)
    pos = lax.broadcasted_iota(jnp.int32, (HEAD_DIM, CHUNK), 1).astype(F32)

    def decays(h):
        kdf = jnp.exp(lg_ref[row_f, h] * (CHUNK - 1.0 - pos))
        kdb = jnp.exp(lg_ref[row_b, h] * pos)
        cdf = jnp.exp(lg_ref[row_f, h] * CHUNK + 0.0 * row)
        cdb = jnp.exp(lg_ref[row_b, h] * CHUNK + 0.0 * row)
        return kdf, kdb, cdf, cdb

    def updates(k_blk, v_blk, kdf, kdb):
        kt = k_blk.astype(F32).T
        return (jnp.dot((kt * kdf).astype(BF16), v_blk, preferred_element_type=F32),
                jnp.dot((kt * kdb).astype(BF16), v_blk, preferred_element_type=F32))

    @pl.when(t == 0)
    def _():
        for h in range(RET_HEADS):
            hs = slice(h * HEAD_DIM, (h + 1) * HEAD_DIM)
            kdf, kdb, cdf, cdb = decays(h)
            kv = [updates(kc_ref[c * CHUNK:(c + 1) * CHUNK, hs],
                          vc_ref[c * CHUNK:(c + 1) * CHUNK, hs], kdf, kdb) for c in range(ncc)]
            s = jnp.zeros((HEAD_DIM, HEAD_DIM), F32)
            for c in range(ncc):
                sfc_ref[c, h] = s.astype(BF16)
                s = cdf * s + kv[c][0]
            ff_ref[h] = s
            s = jnp.zeros((HEAD_DIM, HEAD_DIM), F32)
            for c in reversed(range(ncc)):
                sbc_ref[c, h] = s.astype(BF16)
                s = cdb * s + kv[c][1]
            fb_ref[h] = s

    @pl.when(t < nt)
    def _():
        for h in range(RET_HEADS):
            hs = slice(h * HEAD_DIM, (h + 1) * HEAD_DIM)
            kdf, kdb, cdf, _ = decays(h)
            s = ff_ref[h]
            for c in range(nc):
                rs = slice(c * CHUNK, (c + 1) * CHUNK)
                sf_ref[c, h] = s.astype(BF16)
                kvf, kvb = updates(k_ref[rs, hs], v_ref[rs, hs], kdf, kdb)
                s = cdf * s + kvf
                kvb_ref[t * nc + c, h] = kvb
            ff_ref[h] = s

    @pl.when(t >= nt)
    def _():
        tile = 2 * nt - 1 - t
        for h in range(RET_HEADS):
            cdb = decays(h)[3]
            s = fb_ref[h]
            for c in reversed(range(nc)):
                sb_ref[c, h] = s.astype(BF16)
                s = cdb * s + kvb_ref[tile * nc + c, h]
            fb_ref[h] = s


def _states(lg, lg_rows, k, v, k_c, v_c, *, tr):
    rows = k.shape[0]
    nt = rows // tr
    nc = tr // CHUNK
    n_chunks = rows // CHUNK
    ncc = k_c.shape[0] // CHUNK
    kern = functools.partial(_states_kernel, nc=nc, nt=nt, ncc=ncc, lg_rows=lg_rows)
    smem = lambda: pl.BlockSpec(memory_space=pltpu.SMEM)
    ctx_in = lambda: pl.BlockSpec(k_c.shape, lambda t: (0, 0))
    ctx_out = lambda: pl.BlockSpec((ncc, RET_HEADS, HEAD_DIM, HEAD_DIM), lambda t: (0, 0, 0, 0))
    rows_in = lambda: pl.BlockSpec((tr, SEG), lambda t: (jnp.minimum(t, nt - 1), 0))
    chunk_states = (nc, RET_HEADS, HEAD_DIM, HEAD_DIM)
    seq_shape = pltpu.HBM((n_chunks, RET_HEADS, HEAD_DIM, HEAD_DIM), BF16)
    ctx_shape = jax.ShapeDtypeStruct((ncc, RET_HEADS, HEAD_DIM, HEAD_DIM), BF16)
    state = pltpu.VMEM((RET_HEADS, HEAD_DIM, HEAD_DIM), F32)
    return pl.pallas_call(
        kern,
        grid=(2 * nt,),
        in_specs=[smem(), rows_in(), rows_in(), ctx_in(), ctx_in()],
        out_specs=[
            pl.BlockSpec(chunk_states, lambda t: (jnp.minimum(t, nt - 1), 0, 0, 0)),
            pl.BlockSpec(chunk_states, lambda t: (nt - 1 - jnp.maximum(t - nt, 0), 0, 0, 0)),
            ctx_out(), ctx_out(),
        ],
        out_shape=[seq_shape, seq_shape, ctx_shape, ctx_shape],
        scratch_shapes=[state, state,
                        pltpu.VMEM((n_chunks, RET_HEADS, HEAD_DIM, HEAD_DIM), F32)],
        compiler_params=_params(VMEM_LARGE_MB, 1),
    )(lg, k, v, k_c, v_c)


def _lane_means(ts):
    parts = []
    for t in ts:
        hi = t.astype(BF16)
        lo = (t - hi.astype(F32)).astype(BF16)
        parts.append(jnp.concatenate([hi, lo], axis=1))
    rows = ts[0].shape[0]
    ones = jnp.full((2 * HEAD_DIM, HEAD_DIM), 1.0 / HEAD_DIM, BF16)
    m = jnp.dot(jnp.concatenate(parts, axis=0), ones, preferred_element_type=F32)
    return [m[n * rows:(n + 1) * rows] for n in range(len(ts))]


def _out_kernel(lg_ref,
                ycv_ref, zc_ref, q_ref, k_ref, v_ref, zr_ref, eb_ref, ec_ref,
                sf_ref, sb_ref, x_ref, gate_ref, convw_ref, cnw_ref, gnw_ref, wout_ref, fnw_ref,
                o_ref,
                mask_ref, qdf_ref, qdb_ref, ya_ref, yb_ref, *, tm, n_tiles, edge_every, final,
                lg_rows):
    s_id = pl.program_id(0)
    i = jnp.maximum(s_id - 1, 0)
    nc = tm // CHUNK
    at_edge = jnp.logical_and((i + 1) % edge_every == 0, i + 1 < n_tiles)

    @pl.when(s_id == 0)
    def _():
        r = lax.broadcasted_iota(jnp.int32, (CHUNK, CHUNK), 0).astype(F32)
        cc = lax.broadcasted_iota(jnp.int32, (CHUNK, CHUNK), 1).astype(F32)
        d = r - cc
        for h in range(RET_HEADS):
            lgf = lg_ref[lg_rows[0], h]
            lgb = lg_ref[lg_rows[1], h]
            mf = jnp.where(d >= 0, jnp.exp(lgf * jnp.maximum(d, 0.0)), 0.0)
            mb = jnp.where(d <= 0, jnp.exp(lgb * jnp.maximum(-d, 0.0)), 0.0)
            mask_ref[h] = mf + mb
            qdf_ref[h] = jnp.exp(lgf * (r + 1.0))
            qdb_ref[h] = jnp.exp(lgb * (CHUNK - r))

    def step(yr_old, yr_new):
        n_q = PROJ_CHUNKS
        cw_ = D_CONV // n_q
        pw = D_MODEL // n_q

        if yr_old is not None:
            last_row = lax.broadcasted_iota(jnp.int32, (SUBLANES, 1), 0) == SUBLANES - 1
            cwt = convw_ref[0]
            proj_ret, y_l, ssq = [], [], None
            for n in range(n_q):
                proj_ret.append(jnp.dot(yr_old[...], wout_ref[D_CONV:, n * pw:(n + 1) * pw],
                                        preferred_element_type=F32))
                cs = slice(n * cw_, (n + 1) * cw_)
                y = ycv_ref[:, cs].astype(F32)
                below = eb_ref[0, SUBLANES - 1:SUBLANES, cs] * cwt[2:3, cs] * ec_ref[0, 0:1, cs]
                below = jnp.where(jnp.logical_and(at_edge, last_row), below, 0.0)
                y = jnp.concatenate([y[:tm - SUBLANES], y[tm - SUBLANES:] + below], axis=0)
                y_l.append(y)
                part = jnp.sum(y * y, axis=-1, keepdims=True)
                ssq = part if ssq is None else ssq + part
            rinv = lax.rsqrt(ssq * (1.0 / D_CONV) + EPS)

            y_conv = jnp.concatenate(
                [(zc_ref[:, n * cw_:(n + 1) * cw_].astype(F32)
                  * (y_l[n] * rinv * cnw_ref[:, n * cw_:(n + 1) * cw_])).astype(BF16)
                 for n in range(n_q)], axis=1)

        pending = list(range(n_q)) if yr_old is not None else []
        out_ssq = []

        def finish_chunk():
            if not pending:
                return
            n = pending.pop(0)
            cols = slice(n * pw, (n + 1) * pw)
            pc = jnp.dot(y_conv, wout_ref[:D_CONV, cols], preferred_element_type=F32)
            xn = x_ref[:, cols] + gate_ref[:, cols] * (pc + proj_ret[n])
            o_ref[:, cols] = xn
            if final:
                out_ssq.append(jnp.sum(xn * xn, axis=-1, keepdims=True))

        nt = (((1,), (1,)), ((), ()))
        rs = lambda c: slice(c * CHUNK, (c + 1) * CHUNK)
        hs = lambda h: slice(h * HEAD_DIM, (h + 1) * HEAD_DIM)
        rounds = range(0, nc, RET_ROUND_CHUNKS) if yr_new is not None else []
        per_round = -(-n_q // max(len(rounds), 1) // 2)
        for c0 in rounds:
            pairs = [(c, h) for c in range(c0, c0 + RET_ROUND_CHUNKS) for h in range(RET_HEADS)]
            q_l = [q_ref[rs(c), hs(h)] for c, h in pairs]
            s_l = [lax.dot_general(q, k_ref[rs(c), hs(h)], nt, preferred_element_type=F32)
                   for q, (c, h) in zip(q_l, pairs)]
            for _ in range(per_round):
                finish_chunk()
            lhs_l = []
            for q, s, (c, h) in zip(q_l, s_l, pairs):
                qf32 = q.astype(F32)
                lhs_l.append(jnp.concatenate(
                    [(s * mask_ref[h]).astype(BF16), (qf32 * qdf_ref[h]).astype(BF16),
                     (qf32 * qdb_ref[h]).astype(BF16)], axis=1))
            o_l = [jnp.dot(lhs, jnp.concatenate([v_ref[rs(c), hs(h)], sf_ref[c, h],
                                                 sb_ref[c, h]], axis=0),
                           preferred_element_type=F32)
                   for lhs, (c, h) in zip(lhs_l, pairs)]
            for _ in range(per_round):
                finish_chunk()
            per_chunk = lambda xs: [xs[n * RET_HEADS:(n + 1) * RET_HEADS]
                                    for n in range(len(xs) // RET_HEADS)]
            mean_l = [m for grp in per_chunk(o_l) for m in _lane_means(grp)]
            oc_l = [o - m for o, m in zip(o_l, mean_l)]
            var_l = [m for grp in per_chunk([oc * oc for oc in oc_l]) for m in _lane_means(grp)]
            for oc, var, (c, h) in zip(oc_l, var_l, pairs):
                on = oc * lax.rsqrt(var + EPS) * gnw_ref[:, hs(h)]
                yr_new[rs(c), hs(h)] = (zr_ref[rs(c), hs(h)].astype(F32) * on).astype(BF16)
        while pending:
            finish_chunk()
        if final and yr_old is not None:
            ms = functools.reduce(lambda a, b: a + b, out_ssq) * (1.0 / D_MODEL)
            o_ref[...] = o_ref[...] * lax.rsqrt(ms + EPS) * fnw_ref[...]

    middle = jnp.logical_and(s_id > 0, s_id < n_tiles)

    @pl.when(s_id == 0)
    def _():
        step(None, ya_ref)

    @pl.when(jnp.logical_and(middle, s_id % 2 == 0))
    def _():
        step(yb_ref, ya_ref)

    @pl.when(jnp.logical_and(middle, s_id % 2 == 1))
    def _():
        step(ya_ref, yb_ref)

    @pl.when(s_id == n_tiles)
    def _():
        step(ya_ref if n_tiles % 2 == 1 else yb_ref, None)


def _out(lg, lg_rows, segs, edges, sf, sb, x, gate, conv_w, cnw, gnw, w_out, fnw, layer, *,
         tm, final):
    ycv, zc, q, k, v, zr = segs
    eb, ec = edges
    rows, d = x.shape
    nt = rows // tm
    nc = tm // CHUNK
    n_edge = eb.shape[0]
    assert rows % tm == 0 and nt % n_edge == 0 and nc % RET_ROUND_CHUNKS == 0
    edge_every = nt // n_edge
    kern = functools.partial(_out_kernel, tm=tm, n_tiles=nt, edge_every=edge_every, final=final,
                             lg_rows=lg_rows)
    smem = lambda: pl.BlockSpec(memory_space=pltpu.SMEM)
    cur = lambda s: jnp.minimum(s, nt - 1)
    prv = lambda s: jnp.maximum(s - 1, 0)
    seg = lambda: pl.BlockSpec((tm, SEG), lambda s: (cur(s), 0))
    fin = lambda: pl.BlockSpec((tm, SEG), lambda s: (prv(s), 0))
    st = lambda: pl.BlockSpec((nc, RET_HEADS, HEAD_DIM, HEAD_DIM), lambda s: (cur(s), 0, 0, 0))
    full = lambda shape: pl.BlockSpec(shape, lambda s: (0,) * len(shape))
    return pl.pallas_call(
        kern,
        grid=(nt + 1,),
        in_specs=[
            smem(),
            fin(), fin(), seg(), seg(), seg(), seg(),
            pl.BlockSpec((1, SUBLANES, SEG), lambda s: (prv(s) // edge_every, 0, 0)),
            pl.BlockSpec((1, SUBLANES, SEG),
                         lambda s: (jnp.minimum(prv(s) // edge_every + 1, n_edge - 1), 0, 0)),
            st(), st(),
            pl.BlockSpec((tm, d), lambda s: (prv(s), 0)),
            full((1, d)),
            pl.BlockSpec((1, 3, D_CONV), lambda s: (layer, 0, 0)),
            full((1, D_CONV)), full((1, D_RET)),
            pl.BlockSpec((d, d), lambda s: (0, 0), pipeline_mode=pl.Buffered(1)),
            full((1, d)),
        ],
        out_specs=pl.BlockSpec((tm, d), lambda s: (prv(s), 0)),
        out_shape=jax.ShapeDtypeStruct((rows, d), F32),
        scratch_shapes=[
            pltpu.VMEM((RET_HEADS, CHUNK, CHUNK), F32),
            pltpu.VMEM((RET_HEADS, CHUNK, HEAD_DIM), F32),
            pltpu.VMEM((RET_HEADS, CHUNK, HEAD_DIM), F32),
            pltpu.VMEM((tm, D_RET), BF16),
            pltpu.VMEM((tm, D_RET), BF16),
        ],
        compiler_params=_params(VMEM_LARGE_MB, 1),
    )(lg, ycv, zc, q, k, v, zr, eb, ec, sf, sb, x, gate, conv_w, cnw, gnw,
      w_out, fnw)


def _rope_tables(seq):
    t = np.arange(seq)
    row = (t // GRID_W).astype(np.float64)
    col = (t % GRID_W).astype(np.float64)
    inv = ROPE_BASE ** (-np.arange(ROPE_F, dtype=np.float64) / ROPE_F)
    ar = row[:, None] * inv[None, :]
    ac = col[:, None] * inv[None, :]
    z = np.zeros_like(ar)
    cos = np.concatenate([np.cos(ar), np.cos(ar), np.cos(ac), np.cos(ac)], axis=-1)
    sa = np.concatenate([-np.sin(ar), z, -np.sin(ac), z], axis=-1)
    sb = np.concatenate([z, np.sin(ar), z, np.sin(ac)], axis=-1)
    return tuple(jnp.asarray(a, dtype=F32) for a in (cos, sa, sb))


def kernel(x, c, ctx, c_ctx, norm_w, w_mod, b_mod, w_in, conv_w, conv_norm_w, ret_norm_w,
           ret_decay_f, ret_decay_b, w_out, final_norm_w):
    batch, seq, d = x.shape
    assert batch == 1 and d == D_MODEL and seq % INPROJ_ROWS == 0 and seq % GRID_W == 0
    depth = norm_w.shape[0]
    ctx_len = ctx.shape[1]
    xs = x[0]
    cs = ctx[0]

    cv = jnp.concatenate([c[0:1], c_ctx[None, :], jnp.zeros((SUBLANES - 2, d), F32)], axis=0)
    lg = -jnp.exp(jnp.concatenate([ret_decay_f, ret_decay_b], axis=0).astype(F32))
    m = _modulation(cv, w_mod, b_mod, 2 * d if depth > 1 else 3 * d)
    c_lanes = jnp.broadcast_to(cv[0:2, :, None], (2, d, HEAD_DIM))
    tables = _rope_tables(seq)
    zero_edge = jnp.zeros((1, SUBLANES, D_CONV), F32)
    fnw = final_norm_w.reshape(1, d)
    w_groups = None

    for layer in range(depth):
        update_ctx = layer < depth - 1
        lg_rows = (layer, depth + layer)
        nw = norm_w[layer].reshape(1, d)
        shift, scale = m[0:1, 0:d], m[0:1, d:2 * d]
        shift_c, scale_c = m[1:2, 0:d], m[1:2, d:2 * d]
        cnw = conv_norm_w[layer].reshape(1, D_CONV)
        gnw = ret_norm_w[layer].reshape(1, D_RET)

        if update_ctx:
            segs_c, w_groups = _inproj(cs, nw, shift_c, scale_c, w_groups, conv_w, layer,
                                       w_f32=w_in if w_groups is None else None)
            k_c, v_c = segs_c[3], segs_c[4]
        else:
            if w_groups is None:
                w_groups = _split_groups(w_in[layer].astype(BF16))
            k_c, v_c = _inproj_kv(cs, nw, shift_c, scale_c, w_groups, tm=ctx_len)

        more = layer + 1 < depth
        to_round = [(w_out, layer)] + ([(w_in, layer + 1)] if more else [])
        segs, edges, rounded, side_mod = _inproj_latent(
            xs, nw, shift, scale, w_groups, conv_w, layer, tables, to_round,
            (c_lanes, w_mod, b_mod) if more else None, tm=INPROJ_ROWS)
        gates = m[:, 2 * d:3 * d] if m.shape[1] == 3 * d else side_mod[:, 0:d]
        gate, gate_c = gates[0:1], gates[1:2]
        m_next = side_mod[:, d:] if more else None
        w_out_b = rounded[0]
        w_groups = _split_groups(rounded[1]) if more else None
        sf, sb, sf_c, sb_c = _states(lg, lg_rows, segs[3], segs[4], k_c, v_c, tr=STATES_ROWS)
        xs = _out(lg, lg_rows, segs, edges, sf, sb, xs, gate, conv_w, cnw, gnw, w_out_b, fnw, layer,
                  tm=OUT_ROWS, final=not update_ctx)
        if update_ctx:
            cs = _out(lg, lg_rows, segs_c, (zero_edge, zero_edge), sf_c, sb_c, cs, gate_c, conv_w,
                      cnw, gnw, w_out_b, fnw, layer, tm=ctx_len, final=False)
        m = m_next
    return xs[None]
```
